```python
import math
import jax, jax.numpy as jnp
from jax import lax
import numpy as np


D_MODEL = 2048
BATCH = 2
SEQ = 4096
DEPTH = 2

CHUNK = 64
HEAD_DIM = 64
MIX_WIDTH = D_MODEL
RWKV_WIDTH = MIX_WIDTH // 2
RWKV_HEADS = RWKV_WIDTH // HEAD_DIM
SB_WIDTH = MIX_WIDTH - RWKV_WIDTH
SB_HEADS = SB_WIDTH // HEAD_DIM
LORA_RANK = 64
SB_BLOCK = 128
S5_WIDTH = D_MODEL
S5_GROUP = 16
S5_GROUPS = S5_WIDTH // S5_GROUP
S5_STATE = 64
RMS_EPS = 1e-6
GN_EPS = 64e-5
DECAY_SCALE = math.exp(-0.5)
DT_MIN = 0.001
DT_MAX = 0.1

RWKV_PROJ = 3 * RWKV_WIDTH + 2 * LORA_RANK
SB_PROJ = 3 * SB_WIDTH
L0_IN = RWKV_PROJ + SB_PROJ + MIX_WIDTH
L1_IN = 2 * S5_WIDTH
N_EVEN = (DEPTH + 1) // 2
N_ODD = DEPTH // 2

kernel_name = 'hybrid_rwkv7_stickbreak_s5_encoder'

F32 = jnp.float32


def rms_norm(x, g):
    xf = x.astype(F32)
    y = xf * lax.rsqrt(jnp.mean(xf * xf, axis=-1, keepdims=True) + RMS_EPS)
    return (y * g.astype(F32)).astype(x.dtype)


def token_shift(p):
    return jnp.pad(p, ((0, 0), (1, 0), (0, 0)))[:, :-1]


def split_heads(t, n_heads):
    b, l, _ = t.shape
    return t.reshape(b, l, n_heads, HEAD_DIM).astype(F32)


def rwkv7_time_mix(p, shift_mix, w_up, w0, a_up, a0, k_k, k_a, r_k, gn_w, gn_b):
    b, l, _ = p.shape
    p = p.astype(F32)
    p = p + (token_shift(p) - p) * shift_mix.astype(F32)
    r, k, v, xw, xa = jnp.split(p, [RWKV_WIDTH, 2 * RWKV_WIDTH, 3 * RWKV_WIDTH, 3 * RWKV_WIDTH + LORA_RANK], axis=-1)
    z_w = w0 + jnp.tanh(xw) @ w_up
    w = jnp.exp(-DECAY_SCALE * jax.nn.sigmoid(z_w.astype(F32)))
    a = jax.nn.sigmoid((a0 + xa @ a_up).astype(F32))
    hp = lambda t: t.astype(F32).reshape(RWKV_HEADS, HEAD_DIM)
    r, k, v, w, a = [split_heads(t, RWKV_HEADS) for t in (r, k, v, w, a)]
    kk = k * hp(k_k)
    kk = kk * lax.rsqrt(jnp.maximum(jnp.sum(kk * kk, axis=-1, keepdims=True), 1e-24))
    k = k * (1.0 + (a - 1.0) * hp(k_a))
    ab = kk * a

    def step(S, inp):
        r_t, w_t, k_t, v_t, kk_t, ab_t = inp
        sa = jnp.einsum('bhvk,bhk->bhv', S, -kk_t)
        S = S * w_t[:, :, None, :] + sa[..., None] * ab_t[:, :, None, :] + v_t[..., None] * k_t[:, :, None, :]
        return S, jnp.einsum('bhvk,bhk->bhv', S, r_t)

    S0 = jnp.zeros((b, RWKV_HEADS, HEAD_DIM, HEAD_DIM), F32)
    seqs = tuple(jnp.swapaxes(t, 0, 1) for t in (r, w, k, v, kk, ab))
    _, y = lax.scan(step, S0, seqs)
    y = jnp.swapaxes(y, 0, 1)
    mu = jnp.mean(y, axis=-1, keepdims=True)
    var = jnp.mean(jnp.square(y - mu), axis=-1, keepdims=True)
    y = (y - mu) * lax.rsqrt(var + GN_EPS) * hp(gn_w) + hp(gn_b)
    y = y + jnp.sum(r * k * hp(r_k), axis=-1, keepdims=True) * v
    return y.reshape(b, l, RWKV_WIDTH)


def stick_breaking_attention(q, k, v):
    b, l, _ = q.shape
    th = lambda t: split_heads(t, SB_HEADS).transpose(0, 2, 1, 3)
    q, k, v = th(q), th(k), th(v)
    n_blk = l // SB_BLOCK
    q_blocks = q.reshape(b, SB_HEADS, n_blk, SB_BLOCK, HEAD_DIM).transpose(2, 0, 1, 3, 4)
    starts = jnp.arange(n_blk) * SB_BLOCK
    key_pos = jnp.arange(l)
    scale = HEAD_DIM ** -0.5

    def block(args):
        q_blk, start = args
        z = jnp.einsum('bhqd,bhkd->bhqk', q_blk, k) * scale
        q_pos = start + jnp.arange(SB_BLOCK)
        mask = key_pos[None, :] < q_pos[:, None]
        log_keep = jnp.where(mask, jax.nn.log_sigmoid(-z), 0.0)
        after = lax.cumsum(log_keep, axis=3, reverse=True) - log_keep
        attn = jnp.where(mask, jnp.exp(jax.nn.log_sigmoid(z) + after), 0.0)
        return jnp.einsum('bhqk,bhkd->bhqd', attn, v)

    o = lax.map(block, (q_blocks, starts))
    return o.transpose(1, 0, 3, 2, 4).reshape(b, l, SB_WIDTH)


def rwkv_stickbreak_layer(x, norm_g, w_in, shift_mix, w_up, w0, a_up, a0, k_k, k_a, r_k, gn_w, gn_b, w_out):
    xn = rms_norm(x, norm_g)
    p = xn @ w_in
    p_rwkv, q, k, v, gate = jnp.split(p, [RWKV_PROJ, RWKV_PROJ + SB_WIDTH, RWKV_PROJ + 2 * SB_WIDTH, RWKV_PROJ + 3 * SB_WIDTH], axis=-1)
    y_a = rwkv7_time_mix(p_rwkv, shift_mix, w_up, w0, a_up, a0, k_k, k_a, r_k, gn_w, gn_b)
    y_b = stick_breaking_attention(q, k, v)
    y = jnp.concatenate([y_a, y_b], axis=-1) * jax.nn.silu(gate.astype(F32))
    return x + (y @ w_out).astype(x.dtype)


def s5_layer(x, norm_g, w_in, lam_re, lam_im, log_dt, b_re, b_im, c_re, c_im, d_skip, w_glu, b_glu, w_out):
    xn = rms_norm(x, norm_g)
    p = xn @ w_in
    u, gate = jnp.split(p, [S5_WIDTH], axis=-1)
    b, l, _ = u.shape
    uf = u.astype(F32)
    lam = lax.complex(lam_re.astype(F32), lam_im.astype(F32))
    dt = jnp.exp(log_dt.astype(F32))[:, None]
    lam_bar = jnp.exp(lam * dt)
    b_bar = ((lam_bar - 1.0) / lam)[..., None] * lax.complex(b_re.astype(F32), b_im.astype(F32))
    bu = jnp.einsum('gph,blgh->blgp', b_bar, uf.reshape(b, l, S5_GROUPS, S5_GROUP).astype(jnp.complex64))
    a_elems = jnp.broadcast_to(lam_bar, (1, l, S5_GROUPS, S5_STATE))

    def combine(e1, e2):
        a1, x1 = e1
        a2, x2 = e2
        return a1 * a2, a2 * x1 + x2

    _, states = lax.associative_scan(combine, (a_elems, bu), axis=1)
    c = lax.complex(c_re.astype(F32), c_im.astype(F32))
    y = jnp.einsum('ghp,blgp->blgh', c, states).real.reshape(b, l, S5_WIDTH)
    y = jax.nn.gelu(y + d_skip.astype(F32) * uf)
    y = y * jax.nn.sigmoid(y @ w_glu + b_glu)
    y = y * jax.nn.silu(gate.astype(F32))
    return x + (y @ w_out).astype(x.dtype)


def setup_inputs(seed: int = 0) -> dict:
    key = jax.random.key(seed)
    ks = iter(jax.random.split(key, 32))
    nrm = lambda shape, s: jax.random.normal(next(ks), shape, F32) * s
    unif = lambda shape, lo, hi: jax.random.uniform(next(ks), shape, F32, lo, hi)
    x = nrm((BATCH, SEQ, D_MODEL), 1.0)
    norm_g = 1.0 + nrm((DEPTH, D_MODEL), 0.02)
    final_g = 1.0 + nrm((D_MODEL,), 0.02)
    ab_w_in = nrm((N_EVEN, D_MODEL, L0_IN), D_MODEL ** -0.5)
    rwkv_shift_mix = unif((N_EVEN, RWKV_PROJ), 0.0, 1.0)
    rwkv_w_up = nrm((N_EVEN, LORA_RANK, RWKV_WIDTH), LORA_RANK ** -0.5)
    rwkv_w0 = nrm((N_EVEN, RWKV_WIDTH), 0.5) + 0.5
    rwkv_a_up = nrm((N_EVEN, LORA_RANK, RWKV_WIDTH), LORA_RANK ** -0.5)
    rwkv_a0 = nrm((N_EVEN, RWKV_WIDTH), 0.1)
    rwkv_k_k = 0.85 + nrm((N_EVEN, RWKV_WIDTH), 0.05)
    rwkv_k_a = 1.0 + nrm((N_EVEN, RWKV_WIDTH), 0.05)
    rwkv_r_k = nrm((N_EVEN, RWKV_WIDTH), 0.1)
    rwkv_gn_w = 1.0 + nrm((N_EVEN, RWKV_WIDTH), 0.02)
    rwkv_gn_b = nrm((N_EVEN, RWKV_WIDTH), 0.02)
    ab_w_out = nrm((N_EVEN, MIX_WIDTH, D_MODEL), MIX_WIDTH ** -0.5)
    s5_w_in = nrm((N_ODD, D_MODEL, L1_IN), D_MODEL ** -0.5)
    s5_lam_re = -0.5 + nrm((N_ODD, S5_GROUPS, S5_STATE), 0.01)
    s5_lam_im = math.pi * jnp.arange(S5_STATE, dtype=F32)[None, None, :] + nrm((N_ODD, S5_GROUPS, S5_STATE), 0.01)
    s5_log_dt = unif((N_ODD, S5_GROUPS), math.log(DT_MIN), math.log(DT_MAX))
    s5_b_re = nrm((N_ODD, S5_GROUPS, S5_STATE, S5_GROUP), (2 * S5_GROUP) ** -0.5)
    s5_b_im = nrm((N_ODD, S5_GROUPS, S5_STATE, S5_GROUP), (2 * S5_GROUP) ** -0.5)
    s5_c_re = nrm((N_ODD, S5_GROUPS, S5_GROUP, S5_STATE), (2 * S5_STATE) ** -0.5)
    s5_c_im = nrm((N_ODD, S5_GROUPS, S5_GROUP, S5_STATE), (2 * S5_STATE) ** -0.5)
    s5_d = nrm((N_ODD, S5_WIDTH), 1.0)
    s5_w_glu = nrm((N_ODD, S5_WIDTH, S5_WIDTH), S5_WIDTH ** -0.5)
    s5_b_glu = nrm((N_ODD, S5_WIDTH), 0.01)
    s5_w_out = nrm((N_ODD, S5_WIDTH, D_MODEL), S5_WIDTH ** -0.5)
    return {'x': x, 'norm_g': norm_g, 'final_g': final_g, 'ab_w_in': ab_w_in, 'rwkv_shift_mix': rwkv_shift_mix,
            'rwkv_w_up': rwkv_w_up, 'rwkv_w0': rwkv_w0, 'rwkv_a_up': rwkv_a_up, 'rwkv_a0': rwkv_a0,
            'rwkv_k_k': rwkv_k_k, 'rwkv_k_a': rwkv_k_a, 'rwkv_r_k': rwkv_r_k, 'rwkv_gn_w': rwkv_gn_w,
            'rwkv_gn_b': rwkv_gn_b, 'ab_w_out': ab_w_out, 's5_w_in': s5_w_in, 's5_lam_re': s5_lam_re,
            's5_lam_im': s5_lam_im, 's5_log_dt': s5_log_dt, 's5_b_re': s5_b_re, 's5_b_im': s5_b_im,
            's5_c_re': s5_c_re, 's5_c_im': s5_c_im, 's5_d': s5_d, 's5_w_glu': s5_w_glu, 's5_b_glu': s5_b_glu,
            's5_w_out': s5_w_out}


def reference(x, norm_g, final_g, ab_w_in, rwkv_shift_mix, rwkv_w_up, rwkv_w0, rwkv_a_up, rwkv_a0,
              rwkv_k_k, rwkv_k_a, rwkv_r_k, rwkv_gn_w, rwkv_gn_b, ab_w_out, s5_w_in, s5_lam_re,
              s5_lam_im, s5_log_dt, s5_b_re, s5_b_im, s5_c_re, s5_c_im, s5_d, s5_w_glu, s5_b_glu, s5_w_out):
    h = x
    for layer in range(DEPTH):
        i = layer // 2
        if layer % 2 == 0:
            h = rwkv_stickbreak_layer(h, norm_g[layer], ab_w_in[i], rwkv_shift_mix[i], rwkv_w_up[i], rwkv_w0[i],
                                      rwkv_a_up[i], rwkv_a0[i], rwkv_k_k[i], rwkv_k_a[i], rwkv_r_k[i],
                                      rwkv_gn_w[i], rwkv_gn_b[i], ab_w_out[i])
        else:
            h = s5_layer(h, norm_g[layer], s5_w_in[i], s5_lam_re[i], s5_lam_im[i], s5_log_dt[i], s5_b_re[i],
                         s5_b_im[i], s5_c_re[i], s5_c_im[i], s5_d[i], s5_w_glu[i], s5_b_glu[i], s5_w_out[i])
    return rms_norm(h, final_g)
```

```python
import functools
import math

import jax
import jax.numpy as jnp
from jax import lax
from jax.experimental import pallas as pl
from jax.experimental.pallas import tpu as pltpu

F32 = jnp.float32
BF16 = jnp.bfloat16

HEAD_DIM = 64
LANES = 128
SUBLANES = 8
LORA_RANK = 64
S5_GROUP = 16
S5_STATE = 64
RMS_EPS = 1e-6
GN_EPS = 64e-5
DECAY_SCALE = math.exp(-0.5)

RWKV_CHUNK = 64
SB_BLOCK = 128
S5_TIME = 256
S5_GB = 16
VMEM_LIMIT = 56 * 1024 * 1024

NN = (((1,), (0,)), ((), ()))
NT = (((1,), (1,)), ((), ()))
TN = (((0,), (0,)), ((), ()))


def _cparams(sem):
    return pltpu.CompilerParams(dimension_semantics=sem, vmem_limit_bytes=VMEM_LIMIT)


def _split(x):
    hi = x.astype(BF16)
    lo = (x - hi.astype(F32)).astype(BF16)
    return hi, lo


def _dg(a, b, dn):
    return lax.dot_general(a, b, dn, preferred_element_type=F32)


def _dot(a, b, dn=NN, passes=3):
    if passes == 1:
        return _dg(a.astype(BF16), b.astype(BF16), dn)
    ah, al = _split(a)
    bh, bl = _split(b)
    return _dg(ah, bh, dn) + (_dg(ah, bl, dn) + _dg(al, bh, dn))


def _dot_rhs_exact(a, b01, passes=3):
    ah = a.astype(BF16)
    out = _dg(ah, b01, NN)
    rem = a - ah.astype(F32)
    for _ in range(passes - 1):
        rh = rem.astype(BF16)
        out = out + _dg(rh, b01, NN)
        rem = rem - rh.astype(F32)
    return out


def _dot_lhs_exact(a01, b, passes=3):
    bh = b.astype(BF16)
    out = _dg(a01, bh, NN)
    rem = b - bh.astype(F32)
    for _ in range(passes - 1):
        rh = rem.astype(BF16)
        out = out + _dg(a01, rh, NN)
        rem = rem - rh.astype(F32)
    return out


def _rmsnorm_kernel(x_ref, g_ref, o_ref):
    x = x_ref[...]
    ms = jnp.mean(x * x, axis=-1, keepdims=True)
    o_ref[...] = ((x * lax.rsqrt(ms + RMS_EPS)) * g_ref[...]).astype(o_ref.dtype)


def rmsnorm(x, g, out_dtype, tm=512):
    m, d = x.shape
    tm = min(tm, m)
    return pl.pallas_call(
        _rmsnorm_kernel,
        grid=(m // tm,),
        in_specs=[pl.BlockSpec((tm, d), lambda i: (i, 0)), pl.BlockSpec((1, d), lambda i: (0, 0))],
        out_specs=pl.BlockSpec((tm, d), lambda i: (i, 0)),
        out_shape=jax.ShapeDtypeStruct((m, d), out_dtype),
        compiler_params=_cparams(("parallel",)),
        name="rmsnorm",
    )(x, g.reshape(1, d))


def _mm_kernel(a_ref, w_ref, o_ref):
    o_ref[...] = jnp.dot(a_ref[...], w_ref[...], preferred_element_type=F32).astype(o_ref.dtype)


def matmul(a, w, out_dtype, tm=512, tn=512, name="matmul"):
    m, k = a.shape
    n = w.shape[1]
    tm = min(tm, m)
    assert m % tm == 0 and n % tn == 0
    return pl.pallas_call(
        _mm_kernel,
        grid=(m // tm, n // tn),
        in_specs=[pl.BlockSpec((tm, k), lambda i, j: (i, 0)), pl.BlockSpec((k, tn), lambda i, j: (0, j))],
        out_specs=pl.BlockSpec((tm, tn), lambda i, j: (i, j)),
        out_shape=jax.ShapeDtypeStruct((m, n), out_dtype),
        compiler_params=_cparams(("parallel", "arbitrary")),
        name=name,
    )(a, w)


def _rwkv_kernel(r_ref, k_ref, v_ref, lo_ref, mr_ref, mk_ref, mv_ref, mlo_ref, wup_ref, aup_ref,
                 w0_ref, a0_ref, kk_ref, ka_ref, rk_ref, gnw_ref, gnb_ref,
                 seg_ref, tri_ref, strict_ref, incl_ref, eye_ref,
                 y_ref, s_ref, prev_ref):
    c = pl.program_id(2)
    ch = RWKV_CHUNK

    @pl.when(c == 0)
    def _():
        s_ref[...] = jnp.zeros_like(s_ref)
        prev_ref[...] = jnp.zeros_like(prev_ref)

    row = lax.broadcasted_iota(jnp.int32, (ch, LANES), 0)
    lane = lax.broadcasted_iota(jnp.int32, (ch, LANES), 1)
    head0 = lane < HEAD_DIM

    def token_shift(x, idx, mix):
        prev = prev_ref[idx:idx + 1, :]
        shifted = jnp.where(row == 0, prev, pltpu.roll(x, 1, 0))
        return x + (shifted - x) * mix

    r_in, k_in, v_in, lo_in = r_ref[...], k_ref[...], v_ref[...], lo_ref[...]
    r = token_shift(r_in, 0, mr_ref[...])
    k = token_shift(k_in, 1, mk_ref[...])
    v = token_shift(v_in, 2, mv_ref[...])
    lo = token_shift(lo_in, 3, mlo_ref[...])
    prev_ref[0:1, :] = r_in[ch - 1:ch, :]
    prev_ref[1:2, :] = k_in[ch - 1:ch, :]
    prev_ref[2:3, :] = v_in[ch - 1:ch, :]
    prev_ref[3:4, :] = lo_in[ch - 1:ch, :]

    z_w = w0_ref[...] + _dot(jnp.tanh(lo), wup_ref[...])
    logw = -DECAY_SCALE * jax.nn.sigmoid(z_w)
    a = jax.nn.sigmoid(a0_ref[...] + _dot(lo, aup_ref[...]))

    seg = seg_ref[...]

    def head_sum(x):
        return _dot_rhs_exact(x, seg)

    kk = k * kk_ref[...]
    kk = kk * lax.rsqrt(jnp.maximum(head_sum(kk * kk), 1e-24))
    k2 = k * (1.0 + (a - 1.0) * ka_ref[...])
    ab = kk * a

    cum = _dot_lhs_exact(tri_ref[...], logw)
    cum_last = cum[ch - 1:ch, :]
    e_cum = jnp.exp(cum)
    e_ncum = jnp.exp(-cum)
    e_tail = jnp.exp(cum_last - cum)
    rt = r * e_cum
    kt = k2 * e_ncum
    bt = ab * e_ncum
    at = -kk * jnp.exp(cum - logw)
    khat = k2 * e_tail
    bhat = ab * e_tail
    p_last = e_cum[ch - 1:ch, :]

    def stack(x):
        return jnp.concatenate([jnp.where(head0, x, 0.0), jnp.where(head0, 0.0, x)], axis=0)

    at2, rt2, bt2, kt2, v2, khat2, bhat2 = (stack(t) for t in (at, rt, bt, kt, v, khat, bhat))
    strict = strict_ref[...] > 0.5
    incl = incl_ref[...] > 0.5
    a_ab = jnp.where(strict, _dot(at2, bt2, NT), 0.0)
    a_ak = jnp.where(strict, _dot(at2, kt2, NT), 0.0)
    a_rb = jnp.where(incl, _dot(rt2, bt2, NT), 0.0)
    a_rk = jnp.where(incl, _dot(rt2, kt2, NT), 0.0)

    tinv = eye_ref[...] + a_ab
    pw = a_ab
    for _ in range(int(math.log2(ch)) - 1):
        pw = _dot(pw, pw, passes=1)
        tinv = tinv + _dot(tinv, pw, passes=1)

    s = s_ref[...]
    w2 = _dot(tinv, at2)
    uv2 = _dot(tinv, _dot(a_ak, v2))
    u2 = _dot(w2, s, NT) + uv2
    y2 = _dot(rt2, s, NT) + _dot(a_rb, u2) + _dot(a_rk, v2)
    s_ref[...] = s * p_last + _dot(u2, bhat2, TN) + _dot(v2, khat2, TN)
    y = y2[:ch, :] + y2[ch:, :]

    inv_n = 1.0 / HEAD_DIM
    mu = head_sum(y) * inv_n
    d = y - mu
    var = head_sum(d * d) * inv_n
    yn = d * lax.rsqrt(var + GN_EPS) * gnw_ref[...] + gnb_ref[...]
    y_ref[...] = yn + head_sum(r * k2 * rk_ref[...]) * v


def rwkv7(p, batch, seqlen, shift_mix, w_up, w0, a_up, a0, k_k, k_a, r_k, gn_w, gn_b):
    width = w0.shape[0]
    npair = width // LANES
    ch = RWKV_CHUNK
    nchunk = seqlen // ch
    assert seqlen % ch == 0 and 2 * LORA_RANK == LANES and 2 * HEAD_DIM == LANES
    zeros = jnp.zeros((LORA_RANK, width), F32)
    wup_pad = jnp.concatenate([w_up, zeros], axis=0)
    aup_pad = jnp.concatenate([zeros, a_up], axis=0)
    idx = jnp.arange(LANES)
    seg = (idx[:, None] // HEAD_DIM == idx[None, :] // HEAD_DIM).astype(BF16)
    t = jnp.arange(ch)
    tri = (t[None, :] <= t[:, None]).astype(BF16)
    i2 = jnp.arange(2 * ch)
    same = (i2[:, None] // ch) == (i2[None, :] // ch)
    strict = (same & ((i2[None, :] % ch) < (i2[:, None] % ch))).astype(F32)
    incl = (same & ((i2[None, :] % ch) <= (i2[:, None] % ch))).astype(F32)
    eye = jnp.eye(2 * ch, dtype=F32)
    row2 = lambda x: x.reshape(1, -1)

    def tok(off):
        return pl.BlockSpec((ch, LANES), lambda b, h, c: (b * nchunk + c, off + h))

    def par(off):
        return pl.BlockSpec((1, LANES), lambda b, h, c: (0, off + h))

    def const(shape):
        return pl.BlockSpec(shape, lambda b, h, c: (0, 0))

    up = pl.BlockSpec((LANES, LANES), lambda b, h, c: (0, h))
    lora_tok = pl.BlockSpec((ch, LANES), lambda b, h, c: (b * nchunk + c, 3 * npair))
    lora_par = pl.BlockSpec((1, LANES), lambda b, h, c: (0, 3 * npair))
    return pl.pallas_call(
        _rwkv_kernel,
        grid=(batch, npair, nchunk),
        in_specs=[tok(0), tok(npair), tok(2 * npair), lora_tok,
                  par(0), par(npair), par(2 * npair), lora_par, up, up,
                  par(0), par(0), par(0), par(0), par(0), par(0), par(0),
                  const((LANES, LANES)), const((ch, ch)), const((2 * ch, 2 * ch)),
                  const((2 * ch, 2 * ch)), const((2 * ch, 2 * ch))],
        out_specs=pl.BlockSpec((ch, LANES), lambda b, h, c: (b * nchunk + c, h)),
        out_shape=jax.ShapeDtypeStruct((batch * seqlen, width), F32),
        scratch_shapes=[pltpu.VMEM((LANES, LANES), F32), pltpu.VMEM((SUBLANES, LANES), F32)],
        compiler_params=_cparams(("parallel", "parallel", "arbitrary")),
        name="rwkv7",
    )(p, p, p, p, row2(shift_mix), row2(shift_mix), row2(shift_mix), row2(shift_mix), wup_pad, aup_pad,
      row2(w0), row2(a0), row2(k_k), row2(k_a), row2(r_k), row2(gn_w), row2(gn_b),
      seg, tri, strict, incl, eye)


def _sb_kernel(q_ref, k_ref, v_ref, uo_ref, o_ref, acc_ref, carry_ref):
    qi = pl.program_id(2)
    blk = SB_BLOCK
    lane = lax.broadcasted_iota(jnp.int32, (blk, LANES), 1)
    head0 = lane < HEAD_DIM
    q = q_ref[...].astype(F32)
    qh = (jnp.where(head0, q, 0.0).astype(BF16), jnp.where(head0, 0.0, q).astype(BF16))
    uo = uo_ref[...]
    acc_ref[...] = jnp.zeros_like(acc_ref)
    carry_ref[...] = jnp.zeros_like(carry_ref)
    tpos = lax.broadcasted_iota(jnp.int32, (blk, blk), 0)
    spos = lax.broadcasted_iota(jnp.int32, (blk, blk), 1)
    causal = spos < tpos

    def sweep(kb, diagonal):
        start = pl.multiple_of(kb * blk, blk)
        kblk = k_ref[pl.ds(start, blk), :]
        vblk = v_ref[pl.ds(start, blk), :]
        for h in range(2):
            z = _dg(qh[h], kblk, NT)
            log_keep = -(jnp.maximum(z, 0.0) + jnp.log(1.0 + jnp.exp(-jnp.abs(z))))
            log_beta = log_keep + z
            if diagonal:
                log_keep = jnp.where(causal, log_keep, 0.0)
            sums = _dot_rhs_exact(log_keep, uo, passes=2)
            carry = carry_ref[h]
            attn = jnp.exp(log_beta + carry + sums[:, :blk])
            if diagonal:
                attn = jnp.where(causal, attn, 0.0)
            carry_ref[h] = carry + sums[:, blk:]
            acc_ref[h] = acc_ref[h] + _dg(attn.astype(BF16), vblk, NN)

    sweep(qi, True)

    def body(i, carry):
        sweep(qi - 1 - i, False)
        return carry

    lax.fori_loop(0, qi, body, 0)
    o_ref[...] = jnp.where(head0, acc_ref[0], acc_ref[1])


def stick_breaking(p, batch, seqlen, width):
    npair = width // LANES
    blk = SB_BLOCK
    nq = seqlen // blk
    j = jnp.arange(blk)
    later = (j[:, None] > j[None, :]).astype(BF16)
    uo = jnp.concatenate([later, jnp.ones((blk, blk), BF16)], axis=1)
    return pl.pallas_call(
        _sb_kernel,
        grid=(batch, npair, nq),
        in_specs=[pl.BlockSpec((blk, LANES), lambda b, h, i: (b * nq + i, h)),
                  pl.BlockSpec((seqlen, LANES), lambda b, h, i: (b, npair + h)),
                  pl.BlockSpec((seqlen, LANES), lambda b, h, i: (b, 2 * npair + h)),
                  pl.BlockSpec((blk, 2 * blk), lambda b, h, i: (0, 0))],
        out_specs=pl.BlockSpec((blk, LANES), lambda b, h, i: (b * nq + i, h)),
        out_shape=jax.ShapeDtypeStruct((batch * seqlen, width), F32),
        scratch_shapes=[pltpu.VMEM((2, blk, LANES), F32), pltpu.VMEM((2, blk, blk), F32)],
        compiler_params=_cparams(("parallel", "parallel", "arbitrary")),
        name="stick_breaking",
    )(p, p, p, uo)


def _gate_out_kernel(ya_ref, yb_ref, g_ref, x_ref, w_ref, ng_ref, h_ref, hn_ref):
    gate = g_ref[...]
    y = jnp.concatenate([ya_ref[...], yb_ref[...]], axis=-1) * (gate * jax.nn.sigmoid(gate))
    h = x_ref[...] + jnp.dot(y.astype(BF16), w_ref[...], preferred_element_type=F32)
    h_ref[...] = h
    ms = jnp.mean(h * h, axis=-1, keepdims=True)
    hn_ref[...] = ((h * lax.rsqrt(ms + RMS_EPS)) * ng_ref[...]).astype(hn_ref.dtype)


def gate_out(ya, yb, gate, x, w, next_g, tm=256):
    m, d = x.shape
    half = ya.shape[1]
    tm = min(tm, m)
    row = lambda n: pl.BlockSpec((tm, n), lambda i: (i, 0))
    return pl.pallas_call(
        _gate_out_kernel,
        grid=(m // tm,),
        in_specs=[row(half), row(half), row(d), row(d),
                  pl.BlockSpec((d, d), lambda i: (0, 0)), pl.BlockSpec((1, d), lambda i: (0, 0))],
        out_specs=[row(d), row(d)],
        out_shape=[jax.ShapeDtypeStruct((m, d), F32), jax.ShapeDtypeStruct((m, d), BF16)],
        compiler_params=_cparams(("parallel",)),
        name="gate_out",
    )(ya, yb, gate, x, w, next_g.reshape(1, d))


def _s5_prep_kernel(lre_ref, lim_ref, ldt_ref, cre_ref, cim_ref, pre_ref, pim_ref, ore_ref, oim_ref):
    lre, lim = lre_ref[...], lim_ref[...]
    dt = jnp.exp(ldt_ref[...])
    mag = jnp.exp(lre * dt)
    bre = mag * jnp.cos(lim * dt)
    bim = mag * jnp.sin(lim * dt)
    den = lre * lre + lim * lim
    nre, nim = bre - 1.0, bim
    fre = (nre * lre + nim * lim) / den
    fim = (nim * lre - nre * lim) / den
    cre, cim = cre_ref[...], cim_ref[...]
    ore_ref[...] = cre * fre - cim * fim
    oim_ref[...] = cre * fim + cim * fre
    pr, pi = bre, bim
    for j in range(SUBLANES):
        pre_ref[:, j:j + 1, :] = pr
        pim_ref[:, j:j + 1, :] = pi
        pr, pi = pr * bre - pi * bim, pr * bim + pi * bre


def _s5_kernel(u_ref, bh_ref, bl_ref, ch_ref, cl_ref, cst_ref, d_ref, y_ref, st_ref, carry_ref):
    tstep = pl.program_id(2)
    half = S5_GB * S5_STATE
    ntile = half // LANES

    @pl.when(tstep == 0)
    def _():
        carry_ref[...] = jnp.zeros_like(carry_ref)

    u = u_ref[...]
    uh, ul = _split(u)
    st_ref[...] = _dg(uh, bh_ref[...], NN) + (_dg(uh, bl_ref[...], NN) + _dg(ul, bh_ref[...], NN))

    def cma(xr, xi, ar, ai, sr, si):
        return xr + (ar * sr - ai * si), xi + (ar * si + ai * sr)

    def block(rb, carry):
        r0 = pl.multiple_of(rb * SUBLANES, SUBLANES)
        new_carry = []
        for t in range(ntile):
            lr = slice(t * LANES, (t + 1) * LANES)
            li = slice(half + t * LANES, half + (t + 1) * LANES)
            xr = st_ref[pl.ds(r0, SUBLANES), lr]
            xi = st_ref[pl.ds(r0, SUBLANES), li]
            for lvl, sh in enumerate((1, 2, 4)):
                ar = cst_ref[lvl * SUBLANES:(lvl + 1) * SUBLANES, lr]
                ai = cst_ref[lvl * SUBLANES:(lvl + 1) * SUBLANES, li]
                xr, xi = cma(xr, xi, ar, ai, pltpu.roll(xr, sh, 0), pltpu.roll(xi, sh, 0))
            ar = cst_ref[3 * SUBLANES:4 * SUBLANES, lr]
            ai = cst_ref[3 * SUBLANES:4 * SUBLANES, li]
            cr, ci = carry[2 * t], carry[2 * t + 1]
            xr, xi = cma(xr, xi, ar, ai, cr, ci)
            st_ref[pl.ds(r0, SUBLANES), lr] = xr
            st_ref[pl.ds(r0, SUBLANES), li] = xi
            new_carry.append(jnp.broadcast_to(xr[SUBLANES - 1:SUBLANES, :], (SUBLANES, LANES)))
            new_carry.append(jnp.broadcast_to(xi[SUBLANES - 1:SUBLANES, :], (SUBLANES, LANES)))
        return tuple(new_carry)

    carry0 = []
    for t in range(ntile):
        carry0.append(carry_ref[:, t * LANES:(t + 1) * LANES])
        carry0.append(carry_ref[:, half + t * LANES:half + (t + 1) * LANES])
    carry = lax.fori_loop(0, S5_TIME // SUBLANES, block, tuple(carry0))
    for t in range(ntile):
        carry_ref[:, t * LANES:(t + 1) * LANES] = carry[2 * t]
        carry_ref[:, half + t * LANES:half + (t + 1) * LANES] = carry[2 * t + 1]

    xh, xl = _split(st_ref[...])
    y = _dg(xh, ch_ref[...], NN) + (_dg(xh, cl_ref[...], NN) + _dg(xl, ch_ref[...], NN))
    y = y + d_ref[...] * u
    y_ref[...] = jax.nn.gelu(y)


def s5_ssm(p, batch, seqlen, lam_re, lam_im, log_dt, b_re, b_im, c_re, c_im, d_skip):
    ngroup, nstate = lam_re.shape
    width = ngroup * S5_GROUP
    gb = S5_GB
    nblk = ngroup // gb
    half = gb * nstate
    assert nstate == S5_STATE and seqlen % S5_TIME == 0 and ngroup % gb == 0

    g3 = lambda x: x.reshape(ngroup, 1, nstate)
    spec1 = pl.BlockSpec((ngroup, 1, nstate), lambda i: (0, 0, 0))
    spec_c = pl.BlockSpec((ngroup, S5_GROUP, nstate), lambda i: (0, 0, 0))
    spec_p = pl.BlockSpec((ngroup, SUBLANES, nstate), lambda i: (0, 0, 0))
    pw_re, pw_im, cf_re, cf_im = pl.pallas_call(
        _s5_prep_kernel,
        grid=(1,),
        in_specs=[spec1, spec1, spec1, spec_c, spec_c],
        out_specs=[spec_p, spec_p, spec_c, spec_c],
        out_shape=[jax.ShapeDtypeStruct((ngroup, SUBLANES, nstate), F32)] * 2
        + [jax.ShapeDtypeStruct((ngroup, S5_GROUP, nstate), F32)] * 2,
        name="s5_prep",
    )(g3(lam_re), g3(lam_im), jnp.broadcast_to(log_dt[:, None, None], (ngroup, 1, nstate)), c_re, c_im)

    eye = jnp.eye(gb, dtype=F32)

    def bmat(b):
        bt = b.reshape(nblk, gb, nstate, S5_GROUP)
        return jnp.einsum("ngph,gk->nghkp", bt, eye).reshape(nblk, gb * S5_GROUP, half)

    def cmat(c):
        ct = c.reshape(nblk, gb, S5_GROUP, nstate)
        return jnp.einsum("nghp,gk->ngpkh", ct, eye).reshape(nblk, half, gb * S5_GROUP)

    b_full = jnp.concatenate([bmat(b_re), bmat(b_im)], axis=2)
    c_full = jnp.concatenate([cmat(cf_re), -cmat(cf_im)], axis=1)
    bh, bl = _split(b_full)
    chh, cl = _split(c_full)

    def lanes(x):
        return x.reshape(nblk, gb, SUBLANES, nstate).transpose(0, 2, 1, 3).reshape(nblk, SUBLANES, half)

    pr, pi = lanes(pw_re), lanes(pw_im)
    rows = jnp.arange(SUBLANES)[None, :, None]
    levels = []
    for sh in (1, 2, 4):
        keep = rows >= sh
        levels.append(jnp.concatenate([jnp.where(keep, pr[:, sh - 1:sh, :], 0.0),
                                       jnp.where(keep, pi[:, sh - 1:sh, :], 0.0)], axis=2))
    levels.append(jnp.concatenate([pr, pi], axis=2))
    cst = jnp.concatenate(levels, axis=1)

    nt = seqlen // S5_TIME
    wu = gb * S5_GROUP
    return pl.pallas_call(
        _s5_kernel,
        grid=(batch, nblk, nt),
        in_specs=[pl.BlockSpec((S5_TIME, wu), lambda b, g, t: (b * nt + t, g)),
                  pl.BlockSpec((None, wu, 2 * half), lambda b, g, t: (g, 0, 0)),
                  pl.BlockSpec((None, wu, 2 * half), lambda b, g, t: (g, 0, 0)),
                  pl.BlockSpec((None, 2 * half, wu), lambda b, g, t: (g, 0, 0)),
                  pl.BlockSpec((None, 2 * half, wu), lambda b, g, t: (g, 0, 0)),
                  pl.BlockSpec((None, 4 * SUBLANES, 2 * half), lambda b, g, t: (g, 0, 0)),
                  pl.BlockSpec((1, wu), lambda b, g, t: (0, g))],
        out_specs=pl.BlockSpec((S5_TIME, wu), lambda b, g, t: (b * nt + t, g)),
        out_shape=jax.ShapeDtypeStruct((batch * seqlen, width), F32),
        scratch_shapes=[pltpu.VMEM((S5_TIME, 2 * half), F32), pltpu.VMEM((SUBLANES, 2 * half), F32)],
        compiler_params=_cparams(("parallel", "parallel", "arbitrary")),
        name="s5_ssm",
    )(p, bh, bl, chh, cl, cst, d_skip.reshape(1, width))


def _glu_kernel(y_ref, yj_ref, gj_ref, w_ref, b_ref, o_ref, ybf_ref):
    @pl.when(pl.program_id(1) == 0)
    def _():
        ybf_ref[...] = y_ref[...].astype(BF16)

    z = jnp.dot(ybf_ref[...], w_ref[...], preferred_element_type=F32) + b_ref[...]
    gate = gj_ref[...]
    o_ref[...] = (yj_ref[...] * jax.nn.sigmoid(z) * (gate * jax.nn.sigmoid(gate))).astype(o_ref.dtype)


def glu_gate(y, p, w, b, tm=512, tn=512):
    m, d = y.shape
    tm = min(tm, m)
    goff = d // tn
    return pl.pallas_call(
        _glu_kernel,
        grid=(m // tm, d // tn),
        in_specs=[pl.BlockSpec((tm, d), lambda i, j: (i, 0)),
                  pl.BlockSpec((tm, tn), lambda i, j: (i, j)),
                  pl.BlockSpec((tm, tn), lambda i, j: (i, goff + j)),
                  pl.BlockSpec((d, tn), lambda i, j: (0, j)),
                  pl.BlockSpec((1, tn), lambda i, j: (0, j))],
        out_specs=pl.BlockSpec((tm, tn), lambda i, j: (i, j)),
        out_shape=jax.ShapeDtypeStruct((m, d), BF16),
        scratch_shapes=[pltpu.VMEM((tm, d), BF16)],
        compiler_params=_cparams(("parallel", "arbitrary")),
        name="glu_gate",
    )(y, y, p, w, b.reshape(1, d))


def _final_kernel(a_ref, h_ref, w_ref, g_ref, o_ref):
    h = h_ref[...] + jnp.dot(a_ref[...], w_ref[...], preferred_element_type=F32)
    ms = jnp.mean(h * h, axis=-1, keepdims=True)
    o_ref[...] = (h * lax.rsqrt(ms + RMS_EPS)) * g_ref[...]


def final_out(a, h, w, g, tm=256):
    m, d = h.shape
    tm = min(tm, m)
    row = pl.BlockSpec((tm, d), lambda i: (i, 0))
    return pl.pallas_call(
        _final_kernel,
        grid=(m // tm,),
        in_specs=[row, row, pl.BlockSpec((d, d), lambda i: (0, 0)), pl.BlockSpec((1, d), lambda i: (0, 0))],
        out_specs=row,
        out_shape=jax.ShapeDtypeStruct((m, d), F32),
        compiler_params=_cparams(("parallel",)),
        name="final_out",
    )(a, h, w, g.reshape(1, d))


def kernel(x, norm_g, final_g, ab_w_in, rwkv_shift_mix, rwkv_w_up, rwkv_w0, rwkv_a_up, rwkv_a0, rwkv_k_k, rwkv_k_a, rwkv_r_k, rwkv_gn_w, rwkv_gn_b, ab_w_out, s5_w_in, s5_lam_re, s5_lam_im, s5_log_dt, s5_b_re, s5_b_im, s5_c_re, s5_c_im, s5_d, s5_w_glu, s5_b_glu, s5_w_out):
    batch, seqlen, d = x.shape
    m = batch * seqlen
    rwkv_w = rwkv_w0.shape[1]
    rwkv_proj = 3 * rwkv_w + 2 * LORA_RANK
    sb_w = (ab_w_in.shape[2] - rwkv_proj - d) // 3
    x2 = x.reshape(m, d)

    w_in = ab_w_in[0]
    scale = HEAD_DIM ** -0.5
    col_scale = jnp.concatenate([jnp.full((sb_w,), scale, F32), jnp.ones((2 * sb_w,), F32)])
    w_rwkv = w_in[:, :rwkv_proj].astype(BF16)
    w_sb = (w_in[:, rwkv_proj:rwkv_proj + 3 * sb_w] * col_scale).astype(BF16)
    w_gate = w_in[:, rwkv_proj + 3 * sb_w:].astype(BF16)
    xn = rmsnorm(x2, norm_g[0], BF16)
    p_rwkv = matmul(xn, w_rwkv, F32, tn=640, name="proj_rwkv")
    p_sb = matmul(xn, w_sb, BF16, tn=512, name="proj_sb")
    gate0 = matmul(xn, w_gate, F32, tn=512, name="proj_gate")
    y_a = rwkv7(p_rwkv, batch, seqlen, rwkv_shift_mix[0], rwkv_w_up[0], rwkv_w0[0], rwkv_a_up[0], rwkv_a0[0],
                rwkv_k_k[0], rwkv_k_a[0], rwkv_r_k[0], rwkv_gn_w[0], rwkv_gn_b[0])
    y_b = stick_breaking(p_sb, batch, seqlen, sb_w)
    h1, hn1 = gate_out(y_a, y_b, gate0, x2, ab_w_out[0].astype(BF16), norm_g[1])

    p1 = matmul(hn1, s5_w_in[0].astype(BF16), F32, tn=512, name="proj_s5")
    y_s5 = s5_ssm(p1, batch, seqlen, s5_lam_re[0], s5_lam_im[0], s5_log_dt[0], s5_b_re[0], s5_b_im[0],
                  s5_c_re[0], s5_c_im[0], s5_d[0])
    act = glu_gate(y_s5, p1, s5_w_glu[0].astype(BF16), s5_b_glu[0])
    out = final_out(act, h1, s5_w_out[0].astype(BF16), final_g)
    return out.reshape(batch, seqlen, d)
```

```python
import functools
import math

import jax
import jax.numpy as jnp
from jax import lax
from jax.experimental import pallas as pl
from jax.experimental.pallas import tpu as pltpu

F32 = jnp.float32
BF16 = jnp.bfloat16

HEAD_DIM = 64
LANES = 128
SUBLANES = 8
LORA_RANK = 64
S5_GROUP = 16
S5_STATE = 64
RMS_EPS = 1e-6
GN_EPS = 64e-5
DECAY_SCALE = math.exp(-0.5)

RWKV_CHUNK = 64
RWKV_PAIRS = 4
SB_BLOCK = 128
SB_QBLOCK = 256
S5_TIME = 256
S5_GB = 16
VMEM_LIMIT = 56 * 1024 * 1024

NN = (((1,), (0,)), ((), ()))
NT = (((1,), (1,)), ((), ()))
TN = (((0,), (0,)), ((), ()))


def _cparams(sem):
    return pltpu.CompilerParams(dimension_semantics=sem, vmem_limit_bytes=VMEM_LIMIT)


def _split(x):
    hi = x.astype(BF16)
    lo = (x - hi.astype(F32)).astype(BF16)
    return hi, lo


def _dg(a, b, dn):
    return lax.dot_general(a, b, dn, preferred_element_type=F32)


def _dot(a, b, dn=NN, passes=3):
    if passes == 1:
        return _dg(a.astype(BF16), b.astype(BF16), dn)
    ah, al = _split(a)
    bh, bl = _split(b)
    return _dg(ah, bh, dn) + (_dg(ah, bl, dn) + _dg(al, bh, dn))


def _dot_rhs_exact(a, b01, passes=3):
    ah = a.astype(BF16)
    out = _dg(ah, b01, NN)
    rem = a - ah.astype(F32)
    for _ in range(passes - 1):
        rh = rem.astype(BF16)
        out = out + _dg(rh, b01, NN)
        rem = rem - rh.astype(F32)
    return out


def _dot_lhs_exact(a01, b, passes=3):
    bh = b.astype(BF16)
    out = _dg(a01, bh, NN)
    rem = b - bh.astype(F32)
    for _ in range(passes - 1):
        rh = rem.astype(BF16)
        out = out + _dg(a01, rh, NN)
        rem = rem - rh.astype(F32)
    return out


def _rmsnorm_kernel(x_ref, g_ref, o_ref):
    x = x_ref[...]
    ms = jnp.mean(x * x, axis=-1, keepdims=True)
    o_ref[...] = ((x * lax.rsqrt(ms + RMS_EPS)) * g_ref[...]).astype(o_ref.dtype)


def rmsnorm(x, g, out_dtype, tm=512):
    m, d = x.shape
    tm = min(tm, m)
    return pl.pallas_call(
        _rmsnorm_kernel,
        grid=(m // tm,),
        in_specs=[pl.BlockSpec((tm, d), lambda i: (i, 0)), pl.BlockSpec((1, d), lambda i: (0, 0))],
        out_specs=pl.BlockSpec((tm, d), lambda i: (i, 0)),
        out_shape=jax.ShapeDtypeStruct((m, d), out_dtype),
        compiler_params=_cparams(("parallel",)),
        name="rmsnorm",
    )(x, g.reshape(1, d))


def _mm_kernel(a_ref, w_ref, o_ref):
    o_ref[...] = jnp.dot(a_ref[...], w_ref[...], preferred_element_type=F32).astype(o_ref.dtype)


def matmul(a, w, out_dtype, tm=512, tn=512, name="matmul"):
    m, k = a.shape
    n = w.shape[1]
    tm = min(tm, m)
    assert m % tm == 0 and n % tn == 0
    return pl.pallas_call(
        _mm_kernel,
        grid=(m // tm, n // tn),
        in_specs=[pl.BlockSpec((tm, k), lambda i, j: (i, 0)), pl.BlockSpec((k, tn), lambda i, j: (0, j))],
        out_specs=pl.BlockSpec((tm, tn), lambda i, j: (i, j)),
        out_shape=jax.ShapeDtypeStruct((m, n), out_dtype),
        compiler_params=_cparams(("parallel", "arbitrary")),
        name=name,
    )(a, w)


def _rwkv_kernel(r_ref, k_ref, v_ref, lo_ref, mr_ref, mk_ref, mv_ref, mlo_ref, wup_ref, aup_ref,
                 w0_ref, a0_ref, kk_ref, ka_ref, rk_ref, gnw_ref, gnb_ref,
                 seg_ref, tri_ref, strict_ref, incl_ref, eye_ref,
                 y_ref, s_ref, prev_ref):
    c = pl.program_id(2)
    ch = RWKV_CHUNK
    npair = r_ref.shape[1] // LANES

    @pl.when(c == 0)
    def _():
        s_ref[...] = jnp.zeros_like(s_ref)
        prev_ref[...] = jnp.zeros_like(prev_ref)

    def token_shift(x, idx, mix):
        row = lax.broadcasted_iota(jnp.int32, x.shape, 0)
        prev = prev_ref[idx:idx + 1, 0:x.shape[1]]
        shifted = jnp.where(row == 0, prev, pltpu.roll(x, 1, 0))
        return x + (shifted - x) * mix

    r_in, k_in, v_in, lo_in = r_ref[...], k_ref[...], v_ref[...], lo_ref[...]
    r = token_shift(r_in, 0, mr_ref[...])
    k = token_shift(k_in, 1, mk_ref[...])
    v = token_shift(v_in, 2, mv_ref[...])
    lo = token_shift(lo_in, 3, mlo_ref[...])
    prev_ref[0:1, :] = r_in[ch - 1:ch, :]
    prev_ref[1:2, :] = k_in[ch - 1:ch, :]
    prev_ref[2:3, :] = v_in[ch - 1:ch, :]
    prev_ref[3:4, 0:LANES] = lo_in[ch - 1:ch, :]

    z_w = w0_ref[...] + _dot(jnp.tanh(lo), wup_ref[...])
    logw = -DECAY_SCALE * jax.nn.sigmoid(z_w)
    a = jax.nn.sigmoid(a0_ref[...] + _dot(lo, aup_ref[...]))

    seg2 = seg_ref[...]

    def head_sum(x):
        tiles = []
        for t in range(npair):
            hi, lo_ = _split(x[:, t * LANES:(t + 1) * LANES])
            tiles.append(_dg(jnp.concatenate([hi, lo_], axis=1), seg2, NN))
        return jnp.concatenate(tiles, axis=1)

    kk = k * kk_ref[...]
    kk = kk * lax.rsqrt(jnp.maximum(head_sum(kk * kk), 1e-24))
    k2 = k * (1.0 + (a - 1.0) * ka_ref[...])
    ab = kk * a

    l_hi = logw.astype(BF16)
    rem = logw - l_hi.astype(F32)
    l_mid = rem.astype(BF16)
    l_lo = (rem - l_mid.astype(F32)).astype(BF16)
    cum = _dg(tri_ref[...], jnp.concatenate([l_hi, l_mid, l_lo], axis=0), NN)
    cum_last = cum[ch - 1:ch, :]
    e_cum = jnp.exp(cum)
    e_ncum = jnp.exp(-cum)
    e_tail = jnp.exp(cum_last - cum)
    rt = r * e_cum
    kt = k2 * e_ncum
    bt = ab * e_ncum
    at = -kk * jnp.exp(cum - logw)
    khat = k2 * e_tail
    bhat = ab * e_tail
    p_last = e_cum[ch - 1:ch, :]

    lane = lax.broadcasted_iota(jnp.int32, (ch, LANES), 1)
    head0 = lane < HEAD_DIM
    strict = strict_ref[...] > 0.5
    incl = incl_ref[...] > 0.5
    eye = eye_ref[...]

    def stack(x):
        return jnp.concatenate([jnp.where(head0, x, 0.0), jnp.where(head0, 0.0, x)], axis=0)

    def mm(x, w, dn=NN):
        return _dg(x.astype(BF16), w.astype(BF16), dn)

    pairs = range(npair)
    rows = 2 * ch
    sl = [slice(p * LANES, (p + 1) * LANES) for p in pairs]
    at2, rt2, bt2, kt2, v2, khat2, bhat2 = ([stack(t[:, sl[p]]) for p in pairs]
                                            for t in (at, rt, bt, kt, v, khat, bhat))
    gram = []
    for p in pairs:
        lh, ll = _split(jnp.concatenate([at2[p], rt2[p]], axis=0))
        rh, rl = _split(jnp.concatenate([bt2[p], kt2[p]], axis=0))
        gram.append(_dg(jnp.concatenate([lh, lh, ll], axis=1), jnp.concatenate([rh, rl, rh], axis=1), NT))
    a_ab = [jnp.where(strict, g[:rows, :rows], 0.0) for g in gram]
    a_ak = [jnp.where(strict, g[:rows, rows:], 0.0) for g in gram]
    a_rb = [jnp.where(incl, g[rows:, :rows], 0.0) for g in gram]
    a_rk = [jnp.where(incl, g[rows:, rows:], 0.0) for g in gram]
    akv = [mm(a_ak[p], v2[p]) for p in pairs]

    tinv = [eye + a for a in a_ab]
    pw = a_ab
    for _ in range(int(math.log2(ch)) - 1):
        pw = [mm(x, x) for x in pw]
        tinv = [tinv[p] + mm(tinv[p], pw[p]) for p in pairs]

    s = [s_ref[p] for p in pairs]
    tw = [mm(tinv[p], jnp.concatenate([at2[p], akv[p]], axis=1)) for p in pairs]
    ws = [mm(jnp.concatenate([tw[p][:, :LANES], rt2[p]], axis=0), s[p], NT) for p in pairs]
    uv = [jnp.concatenate([ws[p][:rows] + tw[p][:, LANES:], v2[p]], axis=0) for p in pairs]
    y2 = [ws[p][rows:] + mm(jnp.concatenate([a_rb[p], a_rk[p]], axis=1), uv[p]) for p in pairs]
    for p in pairs:
        s_ref[p] = s[p] * p_last[:, sl[p]] + mm(uv[p], jnp.concatenate([bhat2[p], khat2[p]], axis=0), TN)
    y = jnp.concatenate([t[:ch, :] + t[ch:, :] for t in y2], axis=1)

    inv_n = 1.0 / HEAD_DIM
    mu = head_sum(y) * inv_n
    d = y - mu
    var = head_sum(d * d) * inv_n
    yn = d * lax.rsqrt(var + GN_EPS) * gnw_ref[...] + gnb_ref[...]
    y_ref[...] = yn + head_sum(r * k2 * rk_ref[...]) * v


def rwkv7(p, batch, seqlen, shift_mix, w_up, w0, a_up, a0, k_k, k_a, r_k, gn_w, gn_b):
    width = w0.shape[0]
    npair = width // LANES
    ch = RWKV_CHUNK
    nchunk = seqlen // ch
    assert seqlen % ch == 0 and 2 * LORA_RANK == LANES and 2 * HEAD_DIM == LANES
    zeros = jnp.zeros((LORA_RANK, width), F32)
    wup_pad = jnp.concatenate([w_up, zeros], axis=0)
    aup_pad = jnp.concatenate([zeros, a_up], axis=0)
    hb = RWKV_PAIRS
    assert npair % hb == 0
    idx = jnp.arange(LANES)
    seg = (idx[:, None] // HEAD_DIM == idx[None, :] // HEAD_DIM).astype(BF16)
    seg = jnp.concatenate([seg, seg], axis=0)
    t = jnp.arange(ch)
    tri = (t[None, :] <= t[:, None]).astype(BF16)
    tri = jnp.concatenate([tri, tri, tri], axis=1)
    i2 = jnp.arange(2 * ch)
    same = (i2[:, None] // ch) == (i2[None, :] // ch)
    strict = (same & ((i2[None, :] % ch) < (i2[:, None] % ch))).astype(F32)
    incl = (same & ((i2[None, :] % ch) <= (i2[:, None] % ch))).astype(F32)
    eye = jnp.eye(2 * ch, dtype=F32)
    row2 = lambda x: x.reshape(1, -1)

    wd = hb * LANES
    ngrp = npair // hb

    def tok(off):
        return pl.BlockSpec((ch, wd), lambda b, h, c: (b * nchunk + c, off + h))

    def par(off):
        return pl.BlockSpec((1, wd), lambda b, h, c: (0, off + h))

    def const(shape):
        return pl.BlockSpec(shape, lambda b, h, c: (0, 0))

    up = pl.BlockSpec((LANES, wd), lambda b, h, c: (0, h))
    lora_tok = pl.BlockSpec((ch, LANES), lambda b, h, c: (b * nchunk + c, 3 * npair))
    lora_par = pl.BlockSpec((1, LANES), lambda b, h, c: (0, 3 * npair))
    return pl.pallas_call(
        _rwkv_kernel,
        grid=(batch, ngrp, nchunk),
        in_specs=[tok(0), tok(ngrp), tok(2 * ngrp), lora_tok,
                  par(0), par(ngrp), par(2 * ngrp), lora_par, up, up,
                  par(0), par(0), par(0), par(0), par(0), par(0), par(0),
                  const((2 * LANES, LANES)), const((ch, 3 * ch)), const((2 * ch, 2 * ch)),
                  const((2 * ch, 2 * ch)), const((2 * ch, 2 * ch))],
        out_specs=pl.BlockSpec((ch, wd), lambda b, h, c: (b * nchunk + c, h)),
        out_shape=jax.ShapeDtypeStruct((batch * seqlen, width), F32),
        scratch_shapes=[pltpu.VMEM((hb, LANES, LANES), F32), pltpu.VMEM((SUBLANES, wd), F32)],
        compiler_params=_cparams(("parallel", "parallel", "arbitrary")),
        name="rwkv7",
    )(p, p, p, p, row2(shift_mix), row2(shift_mix), row2(shift_mix), row2(shift_mix), wup_pad, aup_pad,
      row2(w0), row2(a0), row2(k_k), row2(k_a), row2(r_k), row2(gn_w), row2(gn_b),
      seg, tri, strict, incl, eye)


def _sb_kernel(q_ref, k_ref, v_ref, uo_ref, o_ref, k2_ref, v2_ref, acc_ref, carry_ref):
    qi = pl.program_id(2)
    kb, qb = SB_BLOCK, SB_QBLOCK
    nsub = qb // kb

    @pl.when(qi == 0)
    def _():
        lane = lax.broadcasted_iota(jnp.int32, (kb, LANES), 1)
        head0 = lane < HEAD_DIM

        def fill(i, c):
            start = pl.multiple_of(i * kb, kb)
            for src, dst in ((k_ref, k2_ref), (v_ref, v2_ref)):
                t = src[pl.ds(start, kb), :].astype(F32)
                dst[i, 0:kb, :] = jnp.where(head0, t, 0.0).astype(BF16)
                dst[i, kb:2 * kb, :] = jnp.where(head0, 0.0, t).astype(BF16)
            return c

        lax.fori_loop(0, k_ref.shape[0] // kb, fill, 0)

    q = q_ref[...]
    uo = uo_ref[...]
    acc_ref[...] = jnp.zeros_like(acc_ref)
    carry_ref[...] = jnp.zeros_like(carry_ref)
    tpos = lax.broadcasted_iota(jnp.int32, (qb, kb), 0)
    spos = lax.broadcasted_iota(jnp.int32, (qb, kb), 1)

    def sweep(sb, diagonal):
        base = pl.multiple_of(sb * nsub, nsub)
        keys = k2_ref[pl.ds(base, nsub)].reshape(nsub * 2 * kb, LANES)
        z_all = _dg(q, keys, NT)
        log_beta, sums, causal = {}, {}, {}
        for j in reversed(range(nsub)):
            causal[j] = (spos + j * kb) < tpos if diagonal else None
            for h in range(2):
                z = z_all[:, (2 * j + h) * kb:(2 * j + h + 1) * kb]
                log_keep = -(jnp.maximum(z, 0.0) + jnp.log(1.0 + jnp.exp(-jnp.abs(z))))
                log_beta[j, h] = log_keep + z
                if diagonal:
                    log_keep = jnp.where(causal[j], log_keep, 0.0)
                hi = log_keep.astype(BF16)
                lo = (log_keep - hi.astype(F32)).astype(BF16)
                sums[j, h] = _dg(jnp.concatenate([hi, lo], axis=1), uo, NN)
        attn = {}
        for h in range(2):
            carry = carry_ref[h]
            for j in reversed(range(nsub)):
                a = jnp.exp(log_beta[j, h] + carry + sums[j, h][:, :kb])
                if diagonal:
                    a = jnp.where(causal[j], a, 0.0)
                attn[j, h] = a.astype(BF16)
                carry = carry + sums[j, h][:, kb:]
            carry_ref[h] = carry
        weights = jnp.concatenate([attn[j, h] for j in range(nsub) for h in range(2)], axis=1)
        values = v2_ref[pl.ds(base, nsub)].reshape(nsub * 2 * kb, LANES)
        acc_ref[...] += _dg(weights, values, NN)

    sweep(qi, True)

    def body(i, c):
        sweep(qi - 1 - i, False)
        return c

    lax.fori_loop(0, qi, body, 0)
    o_ref[...] = acc_ref[...]


def stick_breaking(p, batch, seqlen, width):
    npair = width // LANES
    kb, qb = SB_BLOCK, SB_QBLOCK
    nq = seqlen // qb
    j = jnp.arange(kb)
    later = (j[:, None] > j[None, :]).astype(BF16)
    uo = jnp.concatenate([later, jnp.ones((kb, kb), BF16)], axis=1)
    uo = jnp.concatenate([uo, uo], axis=0)
    return pl.pallas_call(
        _sb_kernel,
        grid=(batch, npair, nq),
        in_specs=[pl.BlockSpec((qb, LANES), lambda b, h, i: (b * nq + i, h)),
                  pl.BlockSpec((seqlen, LANES), lambda b, h, i: (b, npair + h)),
                  pl.BlockSpec((seqlen, LANES), lambda b, h, i: (b, 2 * npair + h)),
                  pl.BlockSpec((2 * kb, 2 * kb), lambda b, h, i: (0, 0))],
        out_specs=pl.BlockSpec((qb, LANES), lambda b, h, i: (b * nq + i, h)),
        out_shape=jax.ShapeDtypeStruct((batch * seqlen, width), F32),
        scratch_shapes=[pltpu.VMEM((seqlen // kb, 2 * kb, LANES), BF16),
                        pltpu.VMEM((seqlen // kb, 2 * kb, LANES), BF16),
                        pltpu.VMEM((qb, LANES), F32), pltpu.VMEM((2, qb, kb), F32)],
        compiler_params=_cparams(("parallel", "parallel", "arbitrary")),
        name="stick_breaking",
    )(p, p, p, uo)


def _gate_out_kernel(ya_ref, yb_ref, g_ref, x_ref, w_ref, ng_ref, h_ref, hn_ref):
    gate = g_ref[...]
    y = jnp.concatenate([ya_ref[...], yb_ref[...]], axis=-1) * (gate * jax.nn.sigmoid(gate))
    h = x_ref[...] + jnp.dot(y.astype(BF16), w_ref[...], preferred_element_type=F32)
    h_ref[...] = h
    ms = jnp.mean(h * h, axis=-1, keepdims=True)
    hn_ref[...] = ((h * lax.rsqrt(ms + RMS_EPS)) * ng_ref[...]).astype(hn_ref.dtype)


def gate_out(ya, yb, gate, x, w, next_g, tm=256):
    m, d = x.shape
    half = ya.shape[1]
    tm = min(tm, m)
    row = lambda n: pl.BlockSpec((tm, n), lambda i: (i, 0))
    return pl.pallas_call(
        _gate_out_kernel,
        grid=(m // tm,),
        in_specs=[row(half), row(half), row(d), row(d),
                  pl.BlockSpec((d, d), lambda i: (0, 0)), pl.BlockSpec((1, d), lambda i: (0, 0))],
        out_specs=[row(d), row(d)],
        out_shape=[jax.ShapeDtypeStruct((m, d), F32), jax.ShapeDtypeStruct((m, d), BF16)],
        compiler_params=_cparams(("parallel",)),
        name="gate_out",
    )(ya, yb, gate, x, w, next_g.reshape(1, d))


def _s5_prep_kernel(lre_ref, lim_ref, ldt_ref, cre_ref, cim_ref, pre_ref, pim_ref, ore_ref, oim_ref):
    lre, lim = lre_ref[...], lim_ref[...]
    dt = jnp.exp(ldt_ref[...])
    mag = jnp.exp(lre * dt)
    bre = mag * jnp.cos(lim * dt)
    bim = mag * jnp.sin(lim * dt)
    den = lre * lre + lim * lim
    nre, nim = bre - 1.0, bim
    fre = (nre * lre + nim * lim) / den
    fim = (nim * lre - nre * lim) / den
    cre, cim = cre_ref[...], cim_ref[...]
    ore_ref[...] = cre * fre - cim * fim
    oim_ref[...] = cre * fim + cim * fre
    pr, pi = bre, bim
    for j in range(SUBLANES):
        pre_ref[:, j:j + 1, :] = pr
        pim_ref[:, j:j + 1, :] = pi
        pr, pi = pr * bre - pi * bim, pr * bim + pi * bre


def _s5_kernel(u_ref, b_ref, c_ref, cst_ref, d_ref, y_ref, st_ref, carry_ref):
    tstep = pl.program_id(2)
    half = S5_GB * S5_STATE
    ntile = half // LANES

    @pl.when(tstep == 0)
    def _():
        carry_ref[...] = jnp.zeros_like(carry_ref)

    u = u_ref[...]
    st_ref[...] = _dg(u.astype(BF16), b_ref[...], NN)

    def cma(xr, xi, ar, ai, sr, si):
        return xr + (ar * sr - ai * si), xi + (ar * si + ai * sr)

    def block(rb, carry):
        r0 = pl.multiple_of(rb * SUBLANES, SUBLANES)
        new_carry = []
        for t in range(ntile):
            lr = slice(t * LANES, (t + 1) * LANES)
            li = slice(half + t * LANES, half + (t + 1) * LANES)
            xr = st_ref[pl.ds(r0, SUBLANES), lr]
            xi = st_ref[pl.ds(r0, SUBLANES), li]
            for lvl, sh in enumerate((1, 2, 4)):
                ar = cst_ref[lvl * SUBLANES:(lvl + 1) * SUBLANES, lr]
                ai = cst_ref[lvl * SUBLANES:(lvl + 1) * SUBLANES, li]
                xr, xi = cma(xr, xi, ar, ai, pltpu.roll(xr, sh, 0), pltpu.roll(xi, sh, 0))
            ar = cst_ref[3 * SUBLANES:4 * SUBLANES, lr]
            ai = cst_ref[3 * SUBLANES:4 * SUBLANES, li]
            cr, ci = carry[2 * t], carry[2 * t + 1]
            xr, xi = cma(xr, xi, ar, ai, cr, ci)
            st_ref[pl.ds(r0, SUBLANES), lr] = xr
            st_ref[pl.ds(r0, SUBLANES), li] = xi
            new_carry.append(jnp.broadcast_to(xr[SUBLANES - 1:SUBLANES, :], (SUBLANES, LANES)))
            new_carry.append(jnp.broadcast_to(xi[SUBLANES - 1:SUBLANES, :], (SUBLANES, LANES)))
        return tuple(new_carry)

    carry0 = []
    for t in range(ntile):
        carry0.append(carry_ref[:, t * LANES:(t + 1) * LANES])
        carry0.append(carry_ref[:, half + t * LANES:half + (t + 1) * LANES])
    carry = lax.fori_loop(0, S5_TIME // SUBLANES, block, tuple(carry0))
    for t in range(ntile):
        carry_ref[:, t * LANES:(t + 1) * LANES] = carry[2 * t]
        carry_ref[:, half + t * LANES:half + (t + 1) * LANES] = carry[2 * t + 1]

    y = _dg(st_ref[...].astype(BF16), c_ref[...], NN) + d_ref[...] * u
    y_ref[...] = jax.nn.gelu(y)


def s5_ssm(p, batch, seqlen, lam_re, lam_im, log_dt, b_re, b_im, c_re, c_im, d_skip):
    ngroup, nstate = lam_re.shape
    width = ngroup * S5_GROUP
    gb = S5_GB
    nblk = ngroup // gb
    half = gb * nstate
    assert nstate == S5_STATE and seqlen % S5_TIME == 0 and ngroup % gb == 0

    g3 = lambda x: x.reshape(ngroup, 1, nstate)
    spec1 = pl.BlockSpec((ngroup, 1, nstate), lambda i: (0, 0, 0))
    spec_c = pl.BlockSpec((ngroup, S5_GROUP, nstate), lambda i: (0, 0, 0))
    spec_p = pl.BlockSpec((ngroup, SUBLANES, nstate), lambda i: (0, 0, 0))
    pw_re, pw_im, cf_re, cf_im = pl.pallas_call(
        _s5_prep_kernel,
        grid=(1,),
        in_specs=[spec1, spec1, spec1, spec_c, spec_c],
        out_specs=[spec_p, spec_p, spec_c, spec_c],
        out_shape=[jax.ShapeDtypeStruct((ngroup, SUBLANES, nstate), F32)] * 2
        + [jax.ShapeDtypeStruct((ngroup, S5_GROUP, nstate), F32)] * 2,
        name="s5_prep",
    )(g3(lam_re), g3(lam_im), jnp.broadcast_to(log_dt[:, None, None], (ngroup, 1, nstate)), c_re, c_im)

    eye = jnp.eye(gb, dtype=F32)

    def bmat(b):
        bt = b.reshape(nblk, gb, nstate, S5_GROUP)
        return jnp.einsum("ngph,gk->nghkp", bt, eye).reshape(nblk, gb * S5_GROUP, half)

    def cmat(c):
        ct = c.reshape(nblk, gb, S5_GROUP, nstate)
        return jnp.einsum("nghp,gk->ngpkh", ct, eye).reshape(nblk, half, gb * S5_GROUP)

    b_full = jnp.concatenate([bmat(b_re), bmat(b_im)], axis=2).astype(BF16)
    c_full = jnp.concatenate([cmat(cf_re), -cmat(cf_im)], axis=1).astype(BF16)

    def lanes(x):
        return x.reshape(nblk, gb, SUBLANES, nstate).transpose(0, 2, 1, 3).reshape(nblk, SUBLANES, half)

    pr, pi = lanes(pw_re), lanes(pw_im)
    rows = jnp.arange(SUBLANES)[None, :, None]
    levels = []
    for sh in (1, 2, 4):
        keep = rows >= sh
        levels.append(jnp.concatenate([jnp.where(keep, pr[:, sh - 1:sh, :], 0.0),
                                       jnp.where(keep, pi[:, sh - 1:sh, :], 0.0)], axis=2))
    levels.append(jnp.concatenate([pr, pi], axis=2))
    cst = jnp.concatenate(levels, axis=1)

    nt = seqlen // S5_TIME
    wu = gb * S5_GROUP
    return pl.pallas_call(
        _s5_kernel,
        grid=(batch, nblk, nt),
        in_specs=[pl.BlockSpec((S5_TIME, wu), lambda b, g, t: (b * nt + t, g)),
                  pl.BlockSpec((None, wu, 2 * half), lambda b, g, t: (g, 0, 0)),
                  pl.BlockSpec((None, 2 * half, wu), lambda b, g, t: (g, 0, 0)),
                  pl.BlockSpec((None, 4 * SUBLANES, 2 * half), lambda b, g, t: (g, 0, 0)),
                  pl.BlockSpec((1, wu), lambda b, g, t: (0, g))],
        out_specs=pl.BlockSpec((S5_TIME, wu), lambda b, g, t: (b * nt + t, g)),
        out_shape=jax.ShapeDtypeStruct((batch * seqlen, width), F32),
        scratch_shapes=[pltpu.VMEM((S5_TIME, 2 * half), F32), pltpu.VMEM((SUBLANES, 2 * half), F32)],
        compiler_params=_cparams(("parallel", "parallel", "arbitrary")),
        name="s5_ssm",
    )(p, b_full, c_full, cst, d_skip.reshape(1, width))


def _glu_kernel(y_ref, yj_ref, gj_ref, w_ref, b_ref, o_ref, ybf_ref):
    @pl.when(pl.program_id(1) == 0)
    def _():
        ybf_ref[...] = y_ref[...].astype(BF16)

    z = jnp.dot(ybf_ref[...], w_ref[...], preferred_element_type=F32) + b_ref[...]
    gate = gj_ref[...]
    o_ref[...] = (yj_ref[...] * jax.nn.sigmoid(z) * (gate * jax.nn.sigmoid(gate))).astype(o_ref.dtype)


def glu_gate(y, p, w, b, tm=512, tn=512):
    m, d = y.shape
    tm = min(tm, m)
    goff = d // tn
    return pl.pallas_call(
        _glu_kernel,
        grid=(m // tm, d // tn),
        in_specs=[pl.BlockSpec((tm, d), lambda i, j: (i, 0)),
                  pl.BlockSpec((tm, tn), lambda i, j: (i, j)),
                  pl.BlockSpec((tm, tn), lambda i, j: (i, goff + j)),
                  pl.BlockSpec((d, tn), lambda i, j: (0, j)),
                  pl.BlockSpec((1, tn), lambda i, j: (0, j))],
        out_specs=pl.BlockSpec((tm, tn), lambda i, j: (i, j)),
        out_shape=jax.ShapeDtypeStruct((m, d), BF16),
        scratch_shapes=[pltpu.VMEM((tm, d), BF16)],
        compiler_params=_cparams(("parallel", "arbitrary")),
        name="glu_gate",
    )(y, y, p, w, b.reshape(1, d))


def _final_kernel(a_ref, h_ref, w_ref, g_ref, o_ref):
    h = h_ref[...] + jnp.dot(a_ref[...], w_ref[...], preferred_element_type=F32)
    ms = jnp.mean(h * h, axis=-1, keepdims=True)
    o_ref[...] = (h * lax.rsqrt(ms + RMS_EPS)) * g_ref[...]


def final_out(a, h, w, g, tm=256):
    m, d = h.shape
    tm = min(tm, m)
    row = pl.BlockSpec((tm, d), lambda i: (i, 0))
    return pl.pallas_call(
        _final_kernel,
        grid=(m // tm,),
        in_specs=[row, row, pl.BlockSpec((d, d), lambda i: (0, 0)), pl.BlockSpec((1, d), lambda i: (0, 0))],
        out_specs=row,
        out_shape=jax.ShapeDtypeStruct((m, d), F32),
        compiler_params=_cparams(("parallel",)),
        name="final_out",
    )(a, h, w, g.reshape(1, d))


def kernel(x, norm_g, final_g, ab_w_in, rwkv_shift_mix, rwkv_w_up, rwkv_w0, rwkv_a_up, rwkv_a0, rwkv_k_k, rwkv_k_a, rwkv_r_k, rwkv_gn_w, rwkv_gn_b, ab_w_out, s5_w_in, s5_lam_re, s5_lam_im, s5_log_dt, s5_b_re, s5_b_im, s5_c_re, s5_c_im, s5_d, s5_w_glu, s5_b_glu, s5_w_out):
    batch, seqlen, d = x.shape
    m = batch * seqlen
    rwkv_w = rwkv_w0.shape[1]
    rwkv_proj = 3 * rwkv_w + 2 * LORA_RANK
    sb_w = (ab_w_in.shape[2] - rwkv_proj - d) // 3
    x2 = x.reshape(m, d)

    w_in = ab_w_in[0]
    scale = HEAD_DIM ** -0.5
    col_scale = jnp.concatenate([jnp.full((sb_w,), scale, F32), jnp.ones((2 * sb_w,), F32)])
    w_rwkv = w_in[:, :rwkv_proj].astype(BF16)
    w_sb = (w_in[:, rwkv_proj:rwkv_proj + 3 * sb_w] * col_scale).astype(BF16)
    w_gate = w_in[:, rwkv_proj + 3 * sb_w:].astype(BF16)
    xn = rmsnorm(x2, norm_g[0], BF16)
    p_rwkv = matmul(xn, w_rwkv, F32, tn=640, name="proj_rwkv")
    p_sb = matmul(xn, w_sb, BF16, tn=512, name="proj_sb")
    gate0 = matmul(xn, w_gate, F32, tn=512, name="proj_gate")
    y_a = rwkv7(p_rwkv, batch, seqlen, rwkv_shift_mix[0], rwkv_w_up[0], rwkv_w0[0], rwkv_a_up[0], rwkv_a0[0],
                rwkv_k_k[0], rwkv_k_a[0], rwkv_r_k[0], rwkv_gn_w[0], rwkv_gn_b[0])
    y_b = stick_breaking(p_sb, batch, seqlen, sb_w)
    h1, hn1 = gate_out(y_a, y_b, gate0, x2, ab_w_out[0].astype(BF16), norm_g[1])

    p1 = matmul(hn1, s5_w_in[0].astype(BF16), F32, tn=512, name="proj_s5")
    y_s5 = s5_ssm(p1, batch, seqlen, s5_lam_re[0], s5_lam_im[0], s5_log_dt[0], s5_b_re[0], s5_b_im[0],
                  s5_c_re[0], s5_c_im[0], s5_d[0])
    act = glu_gate(y_s5, p1, s5_w_glu[0].astype(BF16), s5_b_glu[0])
    out = final_out(act, h1, s5_w_out[0].astype(BF16), final_g)
    return out.reshape(batch, seqlen, d)
```

```python
import functools
import math

import jax
import jax.numpy as jnp
from jax import lax
from jax.experimental import pallas as pl
from jax.experimental.pallas import tpu as pltpu

F32 = jnp.float32
BF16 = jnp.bfloat16

HEAD_DIM = 64
LANES = 128
SUBLANES = 8
LORA_RANK = 64
S5_GROUP = 16
S5_STATE = 64
RMS_EPS = 1e-6
GN_EPS = 64e-5
DECAY_SCALE = math.exp(-0.5)

RWKV_CHUNK = 64
RWKV_PAIRS = 8
SB_BLOCK = 128
SB_QBLOCK = 256
S5_TIME = 256
S5_GB = 16
VMEM_LIMIT = 56 * 1024 * 1024

NN = (((1,), (0,)), ((), ()))
NT = (((1,), (1,)), ((), ()))
TN = (((0,), (0,)), ((), ()))


def _cparams(sem):
    return pltpu.CompilerParams(dimension_semantics=sem, vmem_limit_bytes=VMEM_LIMIT)


def _split(x):
    hi = x.astype(BF16)
    lo = (x - hi.astype(F32)).astype(BF16)
    return hi, lo


def _dg(a, b, dn):
    return lax.dot_general(a, b, dn, preferred_element_type=F32)


def _dot(a, b, dn=NN, passes=3):
    if passes == 1:
        return _dg(a.astype(BF16), b.astype(BF16), dn)
    ah, al = _split(a)
    bh, bl = _split(b)
    return _dg(ah, bh, dn) + (_dg(ah, bl, dn) + _dg(al, bh, dn))


def _dot_rhs_exact(a, b01, passes=3):
    ah = a.astype(BF16)
    out = _dg(ah, b01, NN)
    rem = a - ah.astype(F32)
    for _ in range(passes - 1):
        rh = rem.astype(BF16)
        out = out + _dg(rh, b01, NN)
        rem = rem - rh.astype(F32)
    return out


def _dot_lhs_exact(a01, b, passes=3):
    bh = b.astype(BF16)
    out = _dg(a01, bh, NN)
    rem = b - bh.astype(F32)
    for _ in range(passes - 1):
        rh = rem.astype(BF16)
        out = out + _dg(a01, rh, NN)
        rem = rem - rh.astype(F32)
    return out


def _rmsnorm_kernel(x_ref, g_ref, o_ref):
    x = x_ref[...]
    ms = jnp.mean(x * x, axis=-1, keepdims=True)
    o_ref[...] = ((x * lax.rsqrt(ms + RMS_EPS)) * g_ref[...]).astype(o_ref.dtype)


def rmsnorm(x, g, out_dtype, tm=512):
    m, d = x.shape
    tm = min(tm, m)
    return pl.pallas_call(
        _rmsnorm_kernel,
        grid=(m // tm,),
        in_specs=[pl.BlockSpec((tm, d), lambda i: (i, 0)), pl.BlockSpec((1, d), lambda i: (0, 0))],
        out_specs=pl.BlockSpec((tm, d), lambda i: (i, 0)),
        out_shape=jax.ShapeDtypeStruct((m, d), out_dtype),
        compiler_params=_cparams(("parallel",)),
        name="rmsnorm",
    )(x, g.reshape(1, d))


def _mm_kernel(a_ref, w_ref, o_ref):
    o_ref[...] = jnp.dot(a_ref[...], w_ref[...], preferred_element_type=F32).astype(o_ref.dtype)


def matmul(a, w, out_dtype, tm=1024, tn=1024, name="matmul"):
    m, k = a.shape
    n = w.shape[1]
    tm = min(tm, m)
    assert m % tm == 0 and n % tn == 0
    return pl.pallas_call(
        _mm_kernel,
        grid=(m // tm, n // tn),
        in_specs=[pl.BlockSpec((tm, k), lambda i, j: (i, 0)), pl.BlockSpec((k, tn), lambda i, j: (0, j))],
        out_specs=pl.BlockSpec((tm, tn), lambda i, j: (i, j)),
        out_shape=jax.ShapeDtypeStruct((m, n), out_dtype),
        compiler_params=_cparams(("parallel", "arbitrary")),
        name=name,
    )(a, w)


def _rwkv_kernel(r_ref, k_ref, v_ref, lo_ref, mr_ref, mk_ref, mv_ref, mlo_ref, wup_ref, aup_ref,
                 w0_ref, a0_ref, kk_ref, ka_ref, rk_ref, gnw_ref, gnb_ref,
                 seg_ref, tri_ref, strict_ref, incl_ref, eye_ref,
                 y_ref, s_ref, prev_ref):
    c = pl.program_id(2)
    ch = RWKV_CHUNK
    npair = r_ref.shape[1] // LANES

    @pl.when(c == 0)
    def _():
        s_ref[...] = jnp.zeros_like(s_ref)
        prev_ref[...] = jnp.zeros_like(prev_ref)

    def token_shift(x, idx, mix):
        row = lax.broadcasted_iota(jnp.int32, x.shape, 0)
        prev = prev_ref[idx:idx + 1, 0:x.shape[1]]
        shifted = jnp.where(row == 0, prev, pltpu.roll(x, 1, 0))
        return x + (shifted - x) * mix

    r_in, k_in, v_in, lo_in = r_ref[...], k_ref[...], v_ref[...], lo_ref[...]
    r = token_shift(r_in, 0, mr_ref[...])
    k = token_shift(k_in, 1, mk_ref[...])
    v = token_shift(v_in, 2, mv_ref[...])
    lo = token_shift(lo_in, 3, mlo_ref[...])
    prev_ref[0:1, :] = r_in[ch - 1:ch, :]
    prev_ref[1:2, :] = k_in[ch - 1:ch, :]
    prev_ref[2:3, :] = v_in[ch - 1:ch, :]
    prev_ref[3:4, 0:LANES] = lo_in[ch - 1:ch, :]

    z_w = w0_ref[...] + _dot(jnp.tanh(lo), wup_ref[...])
    logw = -DECAY_SCALE * jax.nn.sigmoid(z_w)
    a = jax.nn.sigmoid(a0_ref[...] + _dot(lo, aup_ref[...]))

    seg2 = seg_ref[...]

    def head_sum(x):
        tiles = []
        for t in range(npair):
            hi, lo_ = _split(x[:, t * LANES:(t + 1) * LANES])
            tiles.append(_dg(jnp.concatenate([hi, lo_], axis=1), seg2, NN))
        return jnp.concatenate(tiles, axis=1)

    kk = k * kk_ref[...]
    kk = kk * lax.rsqrt(jnp.maximum(head_sum(kk * kk), 1e-24))
    k2 = k * (1.0 + (a - 1.0) * ka_ref[...])
    ab = kk * a

    l_hi = logw.astype(BF16)
    rem = logw - l_hi.astype(F32)
    l_mid = rem.astype(BF16)
    l_lo = (rem - l_mid.astype(F32)).astype(BF16)
    cum = _dg(tri_ref[...], jnp.concatenate([l_hi, l_mid, l_lo], axis=0), NN)
    cum_last = cum[ch - 1:ch, :]
    e_cum = jnp.exp(cum)
    e_ncum = jnp.exp(-cum)
    e_tail = jnp.exp(cum_last - cum)
    rt = r * e_cum
    kt = k2 * e_ncum
    bt = ab * e_ncum
    at = -kk * jnp.exp(cum - logw)
    khat = k2 * e_tail
    bhat = ab * e_tail
    p_last = e_cum[ch - 1:ch, :]

    lane = lax.broadcasted_iota(jnp.int32, (ch, LANES), 1)
    head0 = lane < HEAD_DIM
    strict = strict_ref[...] > 0.5
    incl = incl_ref[...] > 0.5
    eye = eye_ref[...]

    def stack(x):
        return jnp.concatenate([jnp.where(head0, x, 0.0), jnp.where(head0, 0.0, x)], axis=0)

    def mm(x, w, dn=NN):
        return _dg(x.astype(BF16), w.astype(BF16), dn)

    pairs = range(npair)
    rows = 2 * ch
    sl = [slice(p * LANES, (p + 1) * LANES) for p in pairs]
    at2, rt2, bt2, kt2, v2, khat2, bhat2 = ([stack(t[:, sl[p]]) for p in pairs]
                                            for t in (at, rt, bt, kt, v, khat, bhat))
    gram = []
    for p in pairs:
        lh, ll = _split(jnp.concatenate([at2[p], rt2[p]], axis=0))
        rh, rl = _split(jnp.concatenate([bt2[p], kt2[p]], axis=0))
        gram.append(_dg(jnp.concatenate([lh, lh, ll], axis=1), jnp.concatenate([rh, rl, rh], axis=1), NT))
    a_ab = [jnp.where(strict, g[:rows, :rows], 0.0) for g in gram]
    a_ak = [jnp.where(strict, g[:rows, rows:], 0.0) for g in gram]
    a_rb = [jnp.where(incl, g[rows:, :rows], 0.0) for g in gram]
    a_rk = [jnp.where(incl, g[rows:, rows:], 0.0) for g in gram]
    akv = [mm(a_ak[p], v2[p]) for p in pairs]

    tinv = [eye + a for a in a_ab]
    pw = a_ab
    for _ in range(int(math.log2(ch)) - 1):
        pw = [mm(x, x) for x in pw]
        tinv = [tinv[p] + mm(tinv[p], pw[p]) for p in pairs]

    s = [s_ref[p] for p in pairs]
    tw = [mm(tinv[p], jnp.concatenate([at2[p], akv[p]], axis=1)) for p in pairs]
    ws = [mm(jnp.concatenate([tw[p][:, :LANES], rt2[p]], axis=0), s[p], NT) for p in pairs]
    uv = [jnp.concatenate([ws[p][:rows] + tw[p][:, LANES:], v2[p]], axis=0) for p in pairs]
    y2 = [ws[p][rows:] + mm(jnp.concatenate([a_rb[p], a_rk[p]], axis=1), uv[p]) for p in pairs]
    for p in pairs:
        s_ref[p] = s[p] * p_last[:, sl[p]] + mm(uv[p], jnp.concatenate([bhat2[p], khat2[p]], axis=0), TN)
    y = jnp.concatenate([t[:ch, :] + t[ch:, :] for t in y2], axis=1)

    inv_n = 1.0 / HEAD_DIM
    mu = head_sum(y) * inv_n
    d = y - mu
    var = head_sum(d * d) * inv_n
    yn = d * lax.rsqrt(var + GN_EPS) * gnw_ref[...] + gnb_ref[...]
    y_ref[...] = yn + head_sum(r * k2 * rk_ref[...]) * v


def rwkv7(p, batch, seqlen, shift_mix, w_up, w0, a_up, a0, k_k, k_a, r_k, gn_w, gn_b):
    width = w0.shape[0]
    npair = width // LANES
    ch = RWKV_CHUNK
    nchunk = seqlen // ch
    assert seqlen % ch == 0 and 2 * LORA_RANK == LANES and 2 * HEAD_DIM == LANES
    zeros = jnp.zeros((LORA_RANK, width), F32)
    wup_pad = jnp.concatenate([w_up, zeros], axis=0)
    aup_pad = jnp.concatenate([zeros, a_up], axis=0)
    hb = RWKV_PAIRS
    assert npair % hb == 0
    idx = jnp.arange(LANES)
    seg = (idx[:, None] // HEAD_DIM == idx[None, :] // HEAD_DIM).astype(BF16)
    seg = jnp.concatenate([seg, seg], axis=0)
    t = jnp.arange(ch)
    tri = (t[None, :] <= t[:, None]).astype(BF16)
    tri = jnp.concatenate([tri, tri, tri], axis=1)
    i2 = jnp.arange(2 * ch)
    same = (i2[:, None] // ch) == (i2[None, :] // ch)
    strict = (same & ((i2[None, :] % ch) < (i2[:, None] % ch))).astype(F32)
    incl = (same & ((i2[None, :] % ch) <= (i2[:, None] % ch))).astype(F32)
    eye = jnp.eye(2 * ch, dtype=F32)
    row2 = lambda x: x.reshape(1, -1)

    wd = hb * LANES
    ngrp = npair // hb

    def tok(off):
        return pl.BlockSpec((ch, wd), lambda b, h, c: (b * nchunk + c, off + h))

    def par(off):
        return pl.BlockSpec((1, wd), lambda b, h, c: (0, off + h))

    def const(shape):
        return pl.BlockSpec(shape, lambda b, h, c: (0, 0))

    up = pl.BlockSpec((LANES, wd), lambda b, h, c: (0, h))
    lora_tok = pl.BlockSpec((ch, LANES), lambda b, h, c: (b * nchunk + c, 3 * npair))
    lora_par = pl.BlockSpec((1, LANES), lambda b, h, c: (0, 3 * npair))
    return pl.pallas_call(
        _rwkv_kernel,
        grid=(batch, ngrp, nchunk),
        in_specs=[tok(0), tok(ngrp), tok(2 * ngrp), lora_tok,
                  par(0), par(ngrp), par(2 * ngrp), lora_par, up, up,
                  par(0), par(0), par(0), par(0), par(0), par(0), par(0),
                  const((2 * LANES, LANES)), const((ch, 3 * ch)), const((2 * ch, 2 * ch)),
                  const((2 * ch, 2 * ch)), const((2 * ch, 2 * ch))],
        out_specs=pl.BlockSpec((ch, wd), lambda b, h, c: (b * nchunk + c, h)),
        out_shape=jax.ShapeDtypeStruct((batch * seqlen, width), F32),
        scratch_shapes=[pltpu.VMEM((hb, LANES, LANES), F32), pltpu.VMEM((SUBLANES, wd), F32)],
        compiler_params=_cparams(("parallel", "parallel", "arbitrary")),
        name="rwkv7",
    )(p, p, p, p, row2(shift_mix), row2(shift_mix), row2(shift_mix), row2(shift_mix), wup_pad, aup_pad,
      row2(w0), row2(a0), row2(k_k), row2(k_a), row2(r_k), row2(gn_w), row2(gn_b),
      seg, tri, strict, incl, eye)


def _sb_kernel(q_ref, k_ref, v_ref, uo_ref, o_ref, k2_ref, v2_ref, acc_ref, carry_ref, lb_ref, sums_ref):
    qi = pl.program_id(2)
    kb, qb = SB_BLOCK, SB_QBLOCK
    nsub = qb // kb

    @pl.when(qi == 0)
    def _():
        lane = lax.broadcasted_iota(jnp.int32, (kb, LANES), 1)
        head0 = lane < HEAD_DIM

        def fill(i, c):
            start = pl.multiple_of(i * kb, kb)
            for src, dst in ((k_ref, k2_ref), (v_ref, v2_ref)):
                t = src[pl.ds(start, kb), :].astype(F32)
                dst[i, 0:kb, :] = jnp.where(head0, t, 0.0).astype(BF16)
                dst[i, kb:2 * kb, :] = jnp.where(head0, 0.0, t).astype(BF16)
            return c

        lax.fori_loop(0, k_ref.shape[0] // kb, fill, 0)

    q = q_ref[...]
    uo = uo_ref[...]
    acc_ref[...] = jnp.zeros_like(acc_ref)
    carry_ref[...] = jnp.zeros_like(carry_ref)
    tpos = lax.broadcasted_iota(jnp.int32, (qb, kb), 0)
    spos = lax.broadcasted_iota(jnp.int32, (qb, kb), 1)

    def scores(sb, slot, diagonal):
        base = pl.multiple_of(sb * nsub, nsub)
        keys = k2_ref[pl.ds(base, nsub)].reshape(nsub * 2 * kb, LANES)
        z_all = _dg(q, keys, NT)
        for j in range(nsub):
            for h in range(2):
                z = z_all[:, (2 * j + h) * kb:(2 * j + h + 1) * kb]
                log_keep = -(jnp.maximum(z, 0.0) + jnp.log(1.0 + jnp.exp(-jnp.abs(z))))
                log_beta = log_keep + z
                if diagonal:
                    causal = (spos + j * kb) < tpos
                    log_keep = jnp.where(causal, log_keep, 0.0)
                    log_beta = jnp.where(causal, log_beta, -jnp.inf)
                hi = log_keep.astype(BF16)
                lo = (log_keep - hi.astype(F32)).astype(BF16)
                lb_ref[slot, 2 * j + h] = log_beta
                sums_ref[slot, 2 * j + h] = _dg(jnp.concatenate([hi, lo], axis=1), uo, NN)

    def accumulate(sb, slot):
        base = pl.multiple_of(sb * nsub, nsub)
        attn = {}
        for h in range(2):
            carry = carry_ref[h]
            for j in reversed(range(nsub)):
                s = sums_ref[slot, 2 * j + h]
                attn[j, h] = jnp.exp(lb_ref[slot, 2 * j + h] + carry + s[:, :kb]).astype(BF16)
                carry = carry + s[:, kb:]
            carry_ref[h] = carry
        weights = jnp.concatenate([attn[j, h] for j in range(nsub) for h in range(2)], axis=1)
        values = v2_ref[pl.ds(base, nsub)].reshape(nsub * 2 * kb, LANES)
        acc_ref[...] += _dg(weights, values, NN)

    scores(qi, 0, True)
    pairs = qi // 2

    def body(i, c):
        sb = qi - 1 - 2 * i
        scores(sb, 1, False)
        accumulate(sb + 1, 0)
        scores(sb - 1, 0, False)
        accumulate(sb, 1)
        return c

    lax.fori_loop(0, pairs, body, 0)
    odd = qi - 2 * pairs == 1

    @pl.when(odd)
    def _():
        scores(0, 1, False)
        accumulate(1, 0)
        accumulate(0, 1)

    @pl.when(jnp.logical_not(odd))
    def _():
        accumulate(0, 0)

    o_ref[...] = acc_ref[...]


def stick_breaking(p, batch, seqlen, width):
    npair = width // LANES
    kb, qb = SB_BLOCK, SB_QBLOCK
    nq = seqlen // qb
    j = jnp.arange(kb)
    later = (j[:, None] > j[None, :]).astype(BF16)
    uo = jnp.concatenate([later, jnp.ones((kb, kb), BF16)], axis=1)
    uo = jnp.concatenate([uo, uo], axis=0)
    return pl.pallas_call(
        _sb_kernel,
        grid=(batch, npair, nq),
        in_specs=[pl.BlockSpec((qb, LANES), lambda b, h, i: (b * nq + i, h)),
                  pl.BlockSpec((seqlen, LANES), lambda b, h, i: (b, npair + h)),
                  pl.BlockSpec((seqlen, LANES), lambda b, h, i: (b, 2 * npair + h)),
                  pl.BlockSpec((2 * kb, 2 * kb), lambda b, h, i: (0, 0))],
        out_specs=pl.BlockSpec((qb, LANES), lambda b, h, i: (b * nq + i, h)),
        out_shape=jax.ShapeDtypeStruct((batch * seqlen, width), F32),
        scratch_shapes=[pltpu.VMEM((seqlen // kb, 2 * kb, LANES), BF16),
                        pltpu.VMEM((seqlen // kb, 2 * kb, LANES), BF16),
                        pltpu.VMEM((qb, LANES), F32), pltpu.VMEM((2, qb, kb), F32),
                        pltpu.VMEM((2, 2 * qb // kb, qb, kb), F32), pltpu.VMEM((2, 2 * qb // kb, qb, 2 * kb), F32)],
        compiler_params=_cparams(("parallel", "parallel", "arbitrary")),
        name="stick_breaking",
    )(p, p, p, uo)


def _gate_out_kernel(ya_ref, yb_ref, g_ref, x_ref, w_ref, ng_ref, h_ref, hn_ref):
    gate = g_ref[...].astype(F32)
    y = jnp.concatenate([ya_ref[...], yb_ref[...]], axis=-1) * (gate * jax.nn.sigmoid(gate))
    h = x_ref[...] + jnp.dot(y.astype(BF16), w_ref[...], preferred_element_type=F32)
    h_ref[...] = h
    ms = jnp.mean(h * h, axis=-1, keepdims=True)
    hn_ref[...] = ((h * lax.rsqrt(ms + RMS_EPS)) * ng_ref[...]).astype(hn_ref.dtype)


def gate_out(ya, yb, gate, x, w, next_g, tm=256):
    m, d = x.shape
    half = ya.shape[1]
    tm = min(tm, m)
    row = lambda n: pl.BlockSpec((tm, n), lambda i: (i, 0))
    return pl.pallas_call(
        _gate_out_kernel,
        grid=(m // tm,),
        in_specs=[row(half), row(half), row(d), row(d),
                  pl.BlockSpec((d, d), lambda i: (0, 0)), pl.BlockSpec((1, d), lambda i: (0, 0))],
        out_specs=[row(d), row(d)],
        out_shape=[jax.ShapeDtypeStruct((m, d), F32), jax.ShapeDtypeStruct((m, d), BF16)],
        compiler_params=_cparams(("parallel",)),
        name="gate_out",
    )(ya, yb, gate, x, w, next_g.reshape(1, d))


def _s5_prep_kernel(lre_ref, lim_ref, ldt_ref, cre_ref, cim_ref, pre_ref, pim_ref, ore_ref, oim_ref):
    lre, lim = lre_ref[...], lim_ref[...]
    dt = jnp.exp(ldt_ref[...])
    mag = jnp.exp(lre * dt)
    bre = mag * jnp.cos(lim * dt)
    bim = mag * jnp.sin(lim * dt)
    den = lre * lre + lim * lim
    nre, nim = bre - 1.0, bim
    fre = (nre * lre + nim * lim) / den
    fim = (nim * lre - nre * lim) / den
    cre, cim = cre_ref[...], cim_ref[...]
    ore_ref[...] = cre * fre - cim * fim
    oim_ref[...] = cre * fim + cim * fre
    pr, pi = bre, bim
    for j in range(SUBLANES):
        pre_ref[:, j:j + 1, :] = pr
        pim_ref[:, j:j + 1, :] = pi
        pr, pi = pr * bre - pi * bim, pr * bim + pi * bre


def _s5_kernel(u_ref, b_ref, c_ref, cst_ref, d_ref, y_ref, st_ref, carry_ref):
    tstep = pl.program_id(2)
    half = S5_GB * S5_STATE
    ntile = half // LANES

    @pl.when(tstep == 0)
    def _():
        carry_ref[...] = jnp.zeros_like(carry_ref)

    u = u_ref[...]
    st_ref[...] = _dg(u, b_ref[...], NN)

    def cma(xr, xi, ar, ai, sr, si):
        return xr + (ar * sr - ai * si), xi + (ar * si + ai * sr)

    def block(rb, carry):
        r0 = pl.multiple_of(rb * SUBLANES, SUBLANES)
        new_carry = []
        for t in range(ntile):
            lr = slice(t * LANES, (t + 1) * LANES)
            li = slice(half + t * LANES, half + (t + 1) * LANES)
            xr = st_ref[pl.ds(r0, SUBLANES), lr]
            xi = st_ref[pl.ds(r0, SUBLANES), li]
            for lvl, sh in enumerate((1, 2, 4)):
                ar = cst_ref[lvl * SUBLANES:(lvl + 1) * SUBLANES, lr]
                ai = cst_ref[lvl * SUBLANES:(lvl + 1) * SUBLANES, li]
                xr, xi = cma(xr, xi, ar, ai, pltpu.roll(xr, sh, 0), pltpu.roll(xi, sh, 0))
            ar = cst_ref[3 * SUBLANES:4 * SUBLANES, lr]
            ai = cst_ref[3 * SUBLANES:4 * SUBLANES, li]
            cr, ci = carry[2 * t], carry[2 * t + 1]
            xr, xi = cma(xr, xi, ar, ai, cr, ci)
            st_ref[pl.ds(r0, SUBLANES), lr] = xr
            st_ref[pl.ds(r0, SUBLANES), li] = xi
            new_carry.append(jnp.broadcast_to(xr[SUBLANES - 1:SUBLANES, :], (SUBLANES, LANES)))
            new_carry.append(jnp.broadcast_to(xi[SUBLANES - 1:SUBLANES, :], (SUBLANES, LANES)))
        return tuple(new_carry)

    carry0 = []
    for t in range(ntile):
        carry0.append(carry_ref[:, t * LANES:(t + 1) * LANES])
        carry0.append(carry_ref[:, half + t * LANES:half + (t + 1) * LANES])
    carry = lax.fori_loop(0, S5_TIME // SUBLANES, block, tuple(carry0))
    for t in range(ntile):
        carry_ref[:, t * LANES:(t + 1) * LANES] = carry[2 * t]
        carry_ref[:, half + t * LANES:half + (t + 1) * LANES] = carry[2 * t + 1]

    y = _dg(st_ref[...].astype(BF16), c_ref[...], NN) + d_ref[...] * u.astype(F32)
    y_ref[...] = jax.nn.gelu(y).astype(y_ref.dtype)


def s5_ssm(p, batch, seqlen, lam_re, lam_im, log_dt, b_re, b_im, c_re, c_im, d_skip):
    ngroup, nstate = lam_re.shape
    width = ngroup * S5_GROUP
    gb = S5_GB
    nblk = ngroup // gb
    half = gb * nstate
    assert nstate == S5_STATE and seqlen % S5_TIME == 0 and ngroup % gb == 0

    g3 = lambda x: x.reshape(ngroup, 1, nstate)
    spec1 = pl.BlockSpec((ngroup, 1, nstate), lambda i: (0, 0, 0))
    spec_c = pl.BlockSpec((ngroup, S5_GROUP, nstate), lambda i: (0, 0, 0))
    spec_p = pl.BlockSpec((ngroup, SUBLANES, nstate), lambda i: (0, 0, 0))
    pw_re, pw_im, cf_re, cf_im = pl.pallas_call(
        _s5_prep_kernel,
        grid=(1,),
        in_specs=[spec1, spec1, spec1, spec_c, spec_c],
        out_specs=[spec_p, spec_p, spec_c, spec_c],
        out_shape=[jax.ShapeDtypeStruct((ngroup, SUBLANES, nstate), F32)] * 2
        + [jax.ShapeDtypeStruct((ngroup, S5_GROUP, nstate), F32)] * 2,
        name="s5_prep",
    )(g3(lam_re), g3(lam_im), jnp.broadcast_to(log_dt[:, None, None], (ngroup, 1, nstate)), c_re, c_im)

    eye = jnp.eye(gb, dtype=F32)

    def bmat(b):
        bt = b.reshape(nblk, gb, nstate, S5_GROUP)
        return jnp.einsum("ngph,gk->nghkp", bt, eye).reshape(nblk, gb * S5_GROUP, half)

    def cmat(c):
        ct = c.reshape(nblk, gb, S5_GROUP, nstate)
        return jnp.einsum("nghp,gk->ngpkh", ct, eye).reshape(nblk, half, gb * S5_GROUP)

    b_full = jnp.concatenate([bmat(b_re), bmat(b_im)], axis=2).astype(BF16)
    c_full = jnp.concatenate([cmat(cf_re), -cmat(cf_im)], axis=1).astype(BF16)

    def lanes(x):
        return x.reshape(nblk, gb, SUBLANES, nstate).transpose(0, 2, 1, 3).reshape(nblk, SUBLANES, half)

    pr, pi = lanes(pw_re), lanes(pw_im)
    rows = jnp.arange(SUBLANES)[None, :, None]
    levels = []
    for sh in (1, 2, 4):
        keep = rows >= sh
        levels.append(jnp.concatenate([jnp.where(keep, pr[:, sh - 1:sh, :], 0.0),
                                       jnp.where(keep, pi[:, sh - 1:sh, :], 0.0)], axis=2))
    levels.append(jnp.concatenate([pr, pi], axis=2))
    cst = jnp.concatenate(levels, axis=1)

    nt = seqlen // S5_TIME
    wu = gb * S5_GROUP
    return pl.pallas_call(
        _s5_kernel,
        grid=(batch, nblk, nt),
        in_specs=[pl.BlockSpec((S5_TIME, wu), lambda b, g, t: (b * nt + t, g)),
                  pl.BlockSpec((None, wu, 2 * half), lambda b, g, t: (g, 0, 0)),
                  pl.BlockSpec((None, 2 * half, wu), lambda b, g, t: (g, 0, 0)),
                  pl.BlockSpec((None, 4 * SUBLANES, 2 * half), lambda b, g, t: (g, 0, 0)),
                  pl.BlockSpec((1, wu), lambda b, g, t: (0, g))],
        out_specs=pl.BlockSpec((S5_TIME, wu), lambda b, g, t: (b * nt + t, g)),
        out_shape=jax.ShapeDtypeStruct((batch * seqlen, width), BF16),
        scratch_shapes=[pltpu.VMEM((S5_TIME, 2 * half), F32), pltpu.VMEM((SUBLANES, 2 * half), F32)],
        compiler_params=_cparams(("parallel", "parallel", "arbitrary")),
        name="s5_ssm",
    )(p, b_full, c_full, cst, d_skip.reshape(1, width))


def _glu_kernel(y_ref, yj_ref, gj_ref, w_ref, b_ref, o_ref):
    z = jnp.dot(y_ref[...], w_ref[...], preferred_element_type=F32) + b_ref[...]
    gate = gj_ref[...].astype(F32)
    yj = yj_ref[...].astype(F32)
    o_ref[...] = (yj * jax.nn.sigmoid(z) * (gate * jax.nn.sigmoid(gate))).astype(o_ref.dtype)


def glu_gate(y, p, w, b, tm=1024, tn=1024):
    m, d = y.shape
    tm = min(tm, m)
    goff = d // tn
    return pl.pallas_call(
        _glu_kernel,
        grid=(m // tm, d // tn),
        in_specs=[pl.BlockSpec((tm, d), lambda i, j: (i, 0)),
                  pl.BlockSpec((tm, tn), lambda i, j: (i, j)),
                  pl.BlockSpec((tm, tn), lambda i, j: (i, goff + j)),
                  pl.BlockSpec((d, tn), lambda i, j: (0, j)),
                  pl.BlockSpec((1, tn), lambda i, j: (0, j))],
        out_specs=pl.BlockSpec((tm, tn), lambda i, j: (i, j)),
        out_shape=jax.ShapeDtypeStruct((m, d), BF16),
        compiler_params=_cparams(("parallel", "arbitrary")),
        name="glu_gate",
    )(y, y, p, w, b.reshape(1, d))


def _final_kernel(a_ref, h_ref, w_ref, g_ref, o_ref):
    h = h_ref[...] + jnp.dot(a_ref[...], w_ref[...], preferred_element_type=F32)
    ms = jnp.mean(h * h, axis=-1, keepdims=True)
    o_ref[...] = (h * lax.rsqrt(ms + RMS_EPS)) * g_ref[...]


def final_out(a, h, w, g, tm=256):
    m, d = h.shape
    tm = min(tm, m)
    row = pl.BlockSpec((tm, d), lambda i: (i, 0))
    return pl.pallas_call(
        _final_kernel,
        grid=(m // tm,),
        in_specs=[row, row, pl.BlockSpec((d, d), lambda i: (0, 0)), pl.BlockSpec((1, d), lambda i: (0, 0))],
        out_specs=row,
        out_shape=jax.ShapeDtypeStruct((m, d), F32),
        compiler_params=_cparams(("parallel",)),
        name="final_out",
    )(a, h, w, g.reshape(1, d))


def kernel(x, norm_g, final_g, ab_w_in, rwkv_shift_mix, rwkv_w_up, rwkv_w0, rwkv_a_up, rwkv_a0, rwkv_k_k, rwkv_k_a, rwkv_r_k, rwkv_gn_w, rwkv_gn_b, ab_w_out, s5_w_in, s5_lam_re, s5_lam_im, s5_log_dt, s5_b_re, s5_b_im, s5_c_re, s5_c_im, s5_d, s5_w_glu, s5_b_glu, s5_w_out):
    batch, seqlen, d = x.shape
    m = batch * seqlen
    rwkv_w = rwkv_w0.shape[1]
    rwkv_proj = 3 * rwkv_w + 2 * LORA_RANK
    sb_w = (ab_w_in.shape[2] - rwkv_proj - d) // 3
    x2 = x.reshape(m, d)

    w_in = ab_w_in[0]
    scale = HEAD_DIM ** -0.5
    col_scale = jnp.concatenate([jnp.full((sb_w,), scale, F32), jnp.ones((2 * sb_w,), F32)])
    w_rwkv = w_in[:, :rwkv_proj].astype(BF16)
    w_sb = (w_in[:, rwkv_proj:rwkv_proj + 3 * sb_w] * col_scale).astype(BF16)
    w_gate = w_in[:, rwkv_proj + 3 * sb_w:].astype(BF16)
    xn = rmsnorm(x2, norm_g[0], BF16)
    p_rwkv = matmul(xn, w_rwkv, F32, tn=640, name="proj_rwkv")
    p_sb = matmul(xn, w_sb, BF16, name="proj_sb")
    gate0 = matmul(xn, w_gate, BF16, name="proj_gate")
    y_a = rwkv7(p_rwkv, batch, seqlen, rwkv_shift_mix[0], rwkv_w_up[0], rwkv_w0[0], rwkv_a_up[0], rwkv_a0[0],
                rwkv_k_k[0], rwkv_k_a[0], rwkv_r_k[0], rwkv_gn_w[0], rwkv_gn_b[0])
    y_b = stick_breaking(p_sb, batch, seqlen, sb_w)
    h1, hn1 = gate_out(y_a, y_b, gate0, x2, ab_w_out[0].astype(BF16), norm_g[1])

    p1 = matmul(hn1, s5_w_in[0].astype(BF16), BF16, name="proj_s5")
    y_s5 = s5_ssm(p1, batch, seqlen, s5_lam_re[0], s5_lam_im[0], s5_log_dt[0], s5_b_re[0], s5_b_im[0],
                  s5_c_re[0], s5_c_im[0], s5_d[0])
    act = glu_gate(y_s5, p1, s5_w_glu[0].astype(BF16), s5_b_glu[0])
    out = final_out(act, h1, s5_w_out[0].astype(BF16), final_g)
    return out.reshape(batch, seqlen, d)
```

```python
import functools
import math

import jax
import jax.numpy as jnp
from jax import lax
from jax.experimental import pallas as pl
from jax.experimental.pallas import tpu as pltpu

F32 = jnp.float32
BF16 = jnp.bfloat16

HEAD_DIM = 64
LANES = 128
SUBLANES = 8
LORA_RANK = 64
S5_GROUP = 16
S5_STATE = 64
RMS_EPS = 1e-6
GN_EPS = 64e-5
DECAY_SCALE = math.exp(-0.5)

RWKV_CHUNK = 64
RWKV_PAIRS = 8
SB_BLOCK = 128
SB_QBLOCK = 256
S5_BLOCK = 16
S5_PAIRS = 4
S5_PREP_GROUPS = 16
VMEM_LIMIT = 56 * 1024 * 1024

NN = (((1,), (0,)), ((), ()))
NT = (((1,), (1,)), ((), ()))
TN = (((0,), (0,)), ((), ()))


def _cparams(sem):
    return pltpu.CompilerParams(dimension_semantics=sem, vmem_limit_bytes=VMEM_LIMIT)


def _split(x):
    hi = x.astype(BF16)
    lo = (x - hi.astype(F32)).astype(BF16)
    return hi, lo


def _dg(a, b, dn):
    return lax.dot_general(a, b, dn, preferred_element_type=F32)


def _dot(a, b, dn=NN, passes=3):
    if passes == 1:
        return _dg(a.astype(BF16), b.astype(BF16), dn)
    ah, al = _split(a)
    bh, bl = _split(b)
    return _dg(ah, bh, dn) + (_dg(ah, bl, dn) + _dg(al, bh, dn))


def _dot_rhs_exact(a, b01, passes=3):
    ah = a.astype(BF16)
    out = _dg(ah, b01, NN)
    rem = a - ah.astype(F32)
    for _ in range(passes - 1):
        rh = rem.astype(BF16)
        out = out + _dg(rh, b01, NN)
        rem = rem - rh.astype(F32)
    return out


def _dot_lhs_exact(a01, b, passes=3):
    bh = b.astype(BF16)
    out = _dg(a01, bh, NN)
    rem = b - bh.astype(F32)
    for _ in range(passes - 1):
        rh = rem.astype(BF16)
        out = out + _dg(a01, rh, NN)
        rem = rem - rh.astype(F32)
    return out


def _rmsnorm_kernel(x_ref, g_ref, o_ref):
    x = x_ref[...]
    ms = jnp.mean(x * x, axis=-1, keepdims=True)
    o_ref[...] = ((x * lax.rsqrt(ms + RMS_EPS)) * g_ref[...]).astype(o_ref.dtype)


def rmsnorm(x, g, out_dtype, tm=512):
    m, d = x.shape
    tm = min(tm, m)
    return pl.pallas_call(
        _rmsnorm_kernel,
        grid=(m // tm,),
        in_specs=[pl.BlockSpec((tm, d), lambda i: (i, 0)), pl.BlockSpec((1, d), lambda i: (0, 0))],
        out_specs=pl.BlockSpec((tm, d), lambda i: (i, 0)),
        out_shape=jax.ShapeDtypeStruct((m, d), out_dtype),
        compiler_params=_cparams(("parallel",)),
        name="rmsnorm",
    )(x, g.reshape(1, d))


def _mm_kernel(a_ref, w_ref, o_ref):
    o_ref[...] = jnp.dot(a_ref[...], w_ref[...], preferred_element_type=F32).astype(o_ref.dtype)


def matmul(a, w, out_dtype, tm=1024, tn=1024, name="matmul"):
    m, k = a.shape
    n = w.shape[1]
    tm = min(tm, m)
    assert m % tm == 0 and n % tn == 0
    return pl.pallas_call(
        _mm_kernel,
        grid=(m // tm, n // tn),
        in_specs=[pl.BlockSpec((tm, k), lambda i, j: (i, 0)), pl.BlockSpec((k, tn), lambda i, j: (0, j))],
        out_specs=pl.BlockSpec((tm, tn), lambda i, j: (i, j)),
        out_shape=jax.ShapeDtypeStruct((m, n), out_dtype),
        compiler_params=_cparams(("parallel", "arbitrary")),
        name=name,
    )(a, w)


def _rwkv_kernel(r_ref, k_ref, v_ref, lo_ref, mr_ref, mk_ref, mv_ref, mlo_ref, wup_ref, aup_ref,
                 w0_ref, a0_ref, kk_ref, ka_ref, rk_ref, gnw_ref, gnb_ref,
                 seg_ref, tri_ref, strict_ref, incl_ref, eye_ref,
                 y_ref, s_ref, prev_ref):
    c = pl.program_id(2)
    ch = RWKV_CHUNK
    npair = r_ref.shape[1] // LANES

    @pl.when(c == 0)
    def _():
        s_ref[...] = jnp.zeros_like(s_ref)
        prev_ref[...] = jnp.zeros_like(prev_ref)

    def token_shift(x, idx, mix):
        row = lax.broadcasted_iota(jnp.int32, x.shape, 0)
        prev = prev_ref[idx:idx + 1, 0:x.shape[1]]
        shifted = jnp.where(row == 0, prev, pltpu.roll(x, 1, 0))
        return x + (shifted - x) * mix

    r_in, k_in, v_in, lo_in = r_ref[...], k_ref[...], v_ref[...], lo_ref[...]
    r = token_shift(r_in, 0, mr_ref[...])
    k = token_shift(k_in, 1, mk_ref[...])
    v = token_shift(v_in, 2, mv_ref[...])
    lo = token_shift(lo_in, 3, mlo_ref[...])
    prev_ref[0:1, :] = r_in[ch - 1:ch, :]
    prev_ref[1:2, :] = k_in[ch - 1:ch, :]
    prev_ref[2:3, :] = v_in[ch - 1:ch, :]
    prev_ref[3:4, 0:LANES] = lo_in[ch - 1:ch, :]

    z_w = w0_ref[...] + _dot(jnp.tanh(lo), wup_ref[...])
    logw = -DECAY_SCALE * jax.nn.sigmoid(z_w)
    a = jax.nn.sigmoid(a0_ref[...] + _dot(lo, aup_ref[...]))

    seg2 = seg_ref[...]

    def head_sum(x):
        tiles = []
        for t in range(npair):
            hi, lo_ = _split(x[:, t * LANES:(t + 1) * LANES])
            tiles.append(_dg(jnp.concatenate([hi, lo_], axis=1), seg2, NN))
        return jnp.concatenate(tiles, axis=1)

    kk = k * kk_ref[...]
    kk = kk * lax.rsqrt(jnp.maximum(head_sum(kk * kk), 1e-24))
    k2 = k * (1.0 + (a - 1.0) * ka_ref[...])
    ab = kk * a

    l_hi = logw.astype(BF16)
    rem = logw - l_hi.astype(F32)
    l_mid = rem.astype(BF16)
    l_lo = (rem - l_mid.astype(F32)).astype(BF16)
    cum = _dg(tri_ref[...], jnp.concatenate([l_hi, l_mid, l_lo], axis=0), NN)
    cum_last = cum[ch - 1:ch, :]
    e_cum = jnp.exp(cum)
    e_ncum = jnp.exp(-cum)
    e_tail = jnp.exp(cum_last - cum)
    rt = r * e_cum
    kt = k2 * e_ncum
    bt = ab * e_ncum
    at = -kk * jnp.exp(cum - logw)
    khat = k2 * e_tail
    bhat = ab * e_tail
    p_last = e_cum[ch - 1:ch, :]

    lane = lax.broadcasted_iota(jnp.int32, (ch, LANES), 1)
    head0 = lane < HEAD_DIM
    strict = strict_ref[...] > 0.5
    incl = incl_ref[...] > 0.5
    eye = eye_ref[...]

    def stack(x):
        return jnp.concatenate([jnp.where(head0, x, 0.0), jnp.where(head0, 0.0, x)], axis=0)

    def mm(x, w, dn=NN):
        return _dg(x.astype(BF16), w.astype(BF16), dn)

    pairs = range(npair)
    rows = 2 * ch
    sl = [slice(p * LANES, (p + 1) * LANES) for p in pairs]
    at2, rt2, bt2, kt2, v2, khat2, bhat2 = ([stack(t[:, sl[p]]) for p in pairs]
                                            for t in (at, rt, bt, kt, v, khat, bhat))
    gram = []
    for p in pairs:
        lh, ll = _split(jnp.concatenate([at2[p], rt2[p]], axis=0))
        rh, rl = _split(jnp.concatenate([bt2[p], kt2[p]], axis=0))
        gram.append(_dg(jnp.concatenate([lh, lh, ll], axis=1), jnp.concatenate([rh, rl, rh], axis=1), NT))
    a_ab = [jnp.where(strict, g[:rows, :rows], 0.0) for g in gram]
    a_ak = [jnp.where(strict, g[:rows, rows:], 0.0) for g in gram]
    a_rb = [jnp.where(incl, g[rows:, :rows], 0.0) for g in gram]
    a_rk = [jnp.where(incl, g[rows:, rows:], 0.0) for g in gram]
    akv = [mm(a_ak[p], v2[p]) for p in pairs]

    tinv = [eye + a for a in a_ab]
    pw = a_ab
    for _ in range(int(math.log2(ch)) - 1):
        pw = [mm(x, x) for x in pw]
        tinv = [tinv[p] + mm(tinv[p], pw[p]) for p in pairs]

    s = [s_ref[p] for p in pairs]
    tw = [mm(tinv[p], jnp.concatenate([at2[p], akv[p]], axis=1)) for p in pairs]
    ws = [mm(jnp.concatenate([tw[p][:, :LANES], rt2[p]], axis=0), s[p], NT) for p in pairs]
    uv = [jnp.concatenate([ws[p][:rows] + tw[p][:, LANES:], v2[p]], axis=0) for p in pairs]
    y2 = [ws[p][rows:] + mm(jnp.concatenate([a_rb[p], a_rk[p]], axis=1), uv[p]) for p in pairs]
    for p in pairs:
        s_ref[p] = s[p] * p_last[:, sl[p]] + mm(uv[p], jnp.concatenate([bhat2[p], khat2[p]], axis=0), TN)
    y = jnp.concatenate([t[:ch, :] + t[ch:, :] for t in y2], axis=1)

    inv_n = 1.0 / HEAD_DIM
    mu = head_sum(y) * inv_n
    d = y - mu
    var = head_sum(d * d) * inv_n
    yn = d * lax.rsqrt(var + GN_EPS) * gnw_ref[...] + gnb_ref[...]
    y_ref[...] = yn + head_sum(r * k2 * rk_ref[...]) * v


def rwkv7(p, batch, seqlen, shift_mix, w_up, w0, a_up, a0, k_k, k_a, r_k, gn_w, gn_b):
    width = w0.shape[0]
    npair = width // LANES
    ch = RWKV_CHUNK
    nchunk = seqlen // ch
    assert seqlen % ch == 0 and 2 * LORA_RANK == LANES and 2 * HEAD_DIM == LANES
    zeros = jnp.zeros((LORA_RANK, width), F32)
    wup_pad = jnp.concatenate([w_up, zeros], axis=0)
    aup_pad = jnp.concatenate([zeros, a_up], axis=0)
    hb = RWKV_PAIRS
    assert npair % hb == 0
    idx = jnp.arange(LANES)
    seg = (idx[:, None] // HEAD_DIM == idx[None, :] // HEAD_DIM).astype(BF16)
    seg = jnp.concatenate([seg, seg], axis=0)
    t = jnp.arange(ch)
    tri = (t[None, :] <= t[:, None]).astype(BF16)
    tri = jnp.concatenate([tri, tri, tri], axis=1)
    i2 = jnp.arange(2 * ch)
    same = (i2[:, None] // ch) == (i2[None, :] // ch)
    strict = (same & ((i2[None, :] % ch) < (i2[:, None] % ch))).astype(F32)
    incl = (same & ((i2[None, :] % ch) <= (i2[:, None] % ch))).astype(F32)
    eye = jnp.eye(2 * ch, dtype=F32)
    row2 = lambda x: x.reshape(1, -1)

    wd = hb * LANES
    ngrp = npair // hb

    def tok(off):
        return pl.BlockSpec((ch, wd), lambda b, h, c: (b * nchunk + c, off + h))

    def par(off):
        return pl.BlockSpec((1, wd), lambda b, h, c: (0, off + h))

    def const(shape):
        return pl.BlockSpec(shape, lambda b, h, c: (0, 0))

    up = pl.BlockSpec((LANES, wd), lambda b, h, c: (0, h))
    lora_tok = pl.BlockSpec((ch, LANES), lambda b, h, c: (b * nchunk + c, 3 * npair))
    lora_par = pl.BlockSpec((1, LANES), lambda b, h, c: (0, 3 * npair))
    return pl.pallas_call(
        _rwkv_kernel,
        grid=(batch, ngrp, nchunk),
        in_specs=[tok(0), tok(ngrp), tok(2 * ngrp), lora_tok,
                  par(0), par(ngrp), par(2 * ngrp), lora_par, up, up,
                  par(0), par(0), par(0), par(0), par(0), par(0), par(0),
                  const((2 * LANES, LANES)), const((ch, 3 * ch)), const((2 * ch, 2 * ch)),
                  const((2 * ch, 2 * ch)), const((2 * ch, 2 * ch))],
        out_specs=pl.BlockSpec((ch, wd), lambda b, h, c: (b * nchunk + c, h)),
        out_shape=jax.ShapeDtypeStruct((batch * seqlen, width), F32),
        scratch_shapes=[pltpu.VMEM((hb, LANES, LANES), F32), pltpu.VMEM((SUBLANES, wd), F32)],
        compiler_params=_cparams(("parallel", "parallel", "arbitrary")),
        name="rwkv7",
    )(p, p, p, p, row2(shift_mix), row2(shift_mix), row2(shift_mix), row2(shift_mix), wup_pad, aup_pad,
      row2(w0), row2(a0), row2(k_k), row2(k_a), row2(r_k), row2(gn_w), row2(gn_b),
      seg, tri, strict, incl, eye)


def _sb_kernel(q_ref, k_ref, v_ref, uo_ref, o_ref, k2_ref, v2_ref, acc_ref, carry_ref, lb_ref, sums_ref):
    qi = pl.program_id(2)
    kb, qb = SB_BLOCK, SB_QBLOCK
    nsub = qb // kb

    @pl.when(qi == 0)
    def _():
        lane = lax.broadcasted_iota(jnp.int32, (kb, LANES), 1)
        head0 = lane < HEAD_DIM

        def fill(i, c):
            start = pl.multiple_of(i * kb, kb)
            for src, dst in ((k_ref, k2_ref), (v_ref, v2_ref)):
                t = src[pl.ds(start, kb), :].astype(F32)
                dst[i, 0:kb, :] = jnp.where(head0, t, 0.0).astype(BF16)
                dst[i, kb:2 * kb, :] = jnp.where(head0, 0.0, t).astype(BF16)
            return c

        lax.fori_loop(0, k_ref.shape[0] // kb, fill, 0)

    q = q_ref[...]
    uo = uo_ref[...]
    acc_ref[...] = jnp.zeros_like(acc_ref)
    carry_ref[...] = jnp.zeros_like(carry_ref)
    tpos = lax.broadcasted_iota(jnp.int32, (qb, kb), 0)
    spos = lax.broadcasted_iota(jnp.int32, (qb, kb), 1)

    def scores(sb, slot, diagonal):
        base = pl.multiple_of(sb * nsub, nsub)
        keys = k2_ref[pl.ds(base, nsub)].reshape(nsub * 2 * kb, LANES)
        z_all = _dg(q, keys, NT)
        for j in range(nsub):
            for h in range(2):
                z = z_all[:, (2 * j + h) * kb:(2 * j + h + 1) * kb]
                log_keep = -(jnp.maximum(z, 0.0) + jnp.log(1.0 + jnp.exp(-jnp.abs(z))))
                log_beta = log_keep + z
                if diagonal:
                    causal = (spos + j * kb) < tpos
                    log_keep = jnp.where(causal, log_keep, 0.0)
                    log_beta = jnp.where(causal, log_beta, -jnp.inf)
                hi = log_keep.astype(BF16)
                lo = (log_keep - hi.astype(F32)).astype(BF16)
                lb_ref[slot, 2 * j + h] = log_beta
                sums_ref[slot, 2 * j + h] = _dg(jnp.concatenate([hi, lo], axis=1), uo, NN)

    def accumulate(sb, slot):
        base = pl.multiple_of(sb * nsub, nsub)
        attn = {}
        for h in range(2):
            carry = carry_ref[h]
            for j in reversed(range(nsub)):
                s = sums_ref[slot, 2 * j + h]
                attn[j, h] = jnp.exp(lb_ref[slot, 2 * j + h] + carry + s[:, :kb]).astype(BF16)
                carry = carry + s[:, kb:]
            carry_ref[h] = carry
        weights = jnp.concatenate([attn[j, h] for j in range(nsub) for h in range(2)], axis=1)
        values = v2_ref[pl.ds(base, nsub)].reshape(nsub * 2 * kb, LANES)
        acc_ref[...] += _dg(weights, values, NN)

    scores(qi, 0, True)
    pairs = qi // 2

    def body(i, c):
        sb = qi - 1 - 2 * i
        scores(sb, 1, False)
        accumulate(sb + 1, 0)
        scores(sb - 1, 0, False)
        accumulate(sb, 1)
        return c

    lax.fori_loop(0, pairs, body, 0)
    odd = qi - 2 * pairs == 1

    @pl.when(odd)
    def _():
        scores(0, 1, False)
        accumulate(1, 0)
        accumulate(0, 1)

    @pl.when(jnp.logical_not(odd))
    def _():
        accumulate(0, 0)

    o_ref[...] = acc_ref[...]


def stick_breaking(p, batch, seqlen, width):
    npair = width // LANES
    kb, qb = SB_BLOCK, SB_QBLOCK
    nq = seqlen // qb
    j = jnp.arange(kb)
    later = (j[:, None] > j[None, :]).astype(BF16)
    uo = jnp.concatenate([later, jnp.ones((kb, kb), BF16)], axis=1)
    uo = jnp.concatenate([uo, uo], axis=0)
    return pl.pallas_call(
        _sb_kernel,
        grid=(batch, npair, nq),
        in_specs=[pl.BlockSpec((qb, LANES), lambda b, h, i: (b * nq + i, h)),
                  pl.BlockSpec((seqlen, LANES), lambda b, h, i: (b, npair + h)),
                  pl.BlockSpec((seqlen, LANES), lambda b, h, i: (b, 2 * npair + h)),
                  pl.BlockSpec((2 * kb, 2 * kb), lambda b, h, i: (0, 0))],
        out_specs=pl.BlockSpec((qb, LANES), lambda b, h, i: (b * nq + i, h)),
        out_shape=jax.ShapeDtypeStruct((batch * seqlen, width), F32),
        scratch_shapes=[pltpu.VMEM((seqlen // kb, 2 * kb, LANES), BF16),
                        pltpu.VMEM((seqlen // kb, 2 * kb, LANES), BF16),
                        pltpu.VMEM((qb, LANES), F32), pltpu.VMEM((2, qb, kb), F32),
                        pltpu.VMEM((2, 2 * qb // kb, qb, kb), F32), pltpu.VMEM((2, 2 * qb // kb, qb, 2 * kb), F32)],
        compiler_params=_cparams(("parallel", "parallel", "arbitrary")),
        name="stick_breaking",
    )(p, p, p, uo)


def _gate_out_kernel(ya_ref, yb_ref, g_ref, x_ref, w_ref, ng_ref, h_ref, hn_ref):
    gate = g_ref[...].astype(F32)
    y = jnp.concatenate([ya_ref[...], yb_ref[...]], axis=-1) * (gate * jax.nn.sigmoid(gate))
    h = x_ref[...] + jnp.dot(y.astype(BF16), w_ref[...], preferred_element_type=F32)
    h_ref[...] = h
    ms = jnp.mean(h * h, axis=-1, keepdims=True)
    hn_ref[...] = ((h * lax.rsqrt(ms + RMS_EPS)) * ng_ref[...]).astype(hn_ref.dtype)


def gate_out(ya, yb, gate, x, w, next_g, tm=256):
    m, d = x.shape
    half = ya.shape[1]
    tm = min(tm, m)
    row = lambda n: pl.BlockSpec((tm, n), lambda i: (i, 0))
    return pl.pallas_call(
        _gate_out_kernel,
        grid=(m // tm,),
        in_specs=[row(half), row(half), row(d), row(d),
                  pl.BlockSpec((d, d), lambda i: (0, 0)), pl.BlockSpec((1, d), lambda i: (0, 0))],
        out_specs=[row(d), row(d)],
        out_shape=[jax.ShapeDtypeStruct((m, d), F32), jax.ShapeDtypeStruct((m, d), BF16)],
        compiler_params=_cparams(("parallel",)),
        name="gate_out",
    )(ya, yb, gate, x, w, next_g.reshape(1, d))


def _s5_prep_kernel(lre_ref, lim_ref, ldt_ref, cre_ref, cim_ref, btre_ref, btim_ref,
                    kt_ref, wre_ref, wim_ref, vre_ref, vim_ref, qre_ref, qim_ref, ckre_ref, ckim_ref):
    blk = S5_BLOCK
    h = S5_GROUP
    lre, lim = lre_ref[...], lim_ref[...]
    dt = jnp.exp(ldt_ref[...])
    mag = jnp.exp(lre * dt)
    bre = mag * jnp.cos(lim * dt)
    bim = mag * jnp.sin(lim * dt)
    den = lre * lre + lim * lim
    nre, nim = bre - 1.0, bim
    fre = (nre * lre + nim * lim) / den
    fim = (nim * lre - nre * lim) / den
    cre, cim = cre_ref[...], cim_ref[...]
    ckr, cki = cre * fre - cim * fim, cre * fim + cim * fre
    btre, btim = btre_ref[...], btim_ref[...]
    pr, pi = jnp.ones_like(bre), jnp.zeros_like(bre)
    powers = []
    for tau in range(blk + 1):
        ckre_ref[:, tau * h:(tau + 1) * h, :] = ckr
        ckim_ref[:, tau * h:(tau + 1) * h, :] = cki
        powers.append((pr, pi))
        ckr, cki = ckr * bre - cki * bim, ckr * bim + cki * bre
        pr, pi = pr * bre - pi * bim, pr * bim + pi * bre
    for i in range(blk):
        pr, pi = powers[blk - 1 - i]
        wre_ref[:, i * h:(i + 1) * h, :] = pr * btre - pi * btim
        wim_ref[:, i * h:(i + 1) * h, :] = pr * btim + pi * btre
    vre_ref[...] = ckre_ref[:, h:(blk + 1) * h, :]
    vim_ref[...] = -ckim_ref[:, h:(blk + 1) * h, :]
    qr, qi = powers[blk]
    pr, pi = qr, qi
    for r in range(SUBLANES):
        qre_ref[:, r:r + 1, :] = pr
        qim_ref[:, r:r + 1, :] = pi
        pr, pi = pr * qr - pi * qi, pr * qi + pi * qr
    for g in range(kt_ref.shape[0]):
        kt_ref[g] = (_dot(btre[g], ckre_ref[g, 0:blk * h, :], NT) - _dot(btim[g], ckim_ref[g, 0:blk * h, :], NT))


def _s5_kernel(nbatch, u_ref, kt_ref, wre_ref, wim_ref, vt_ref, cre_ref, cim_ref, d_ref, y_ref,
               gre_ref, gim_ref, xre_ref, xim_ref, yp_ref):
    npair = u_ref.shape[0] // 2
    rows = u_ref.shape[1]
    wlane = u_ref.shape[2]
    lane = lax.broadcasted_iota(jnp.int32, (S5_GROUP, wlane), 1)

    for g in range(2 * npair):
        kt = kt_ref[g]
        blocks = [kt] + [jnp.where(lane >= i * S5_GROUP, pltpu.roll(kt, i * S5_GROUP, 1), 0.0)
                         for i in range(1, S5_BLOCK)]
        kmat = jnp.concatenate(blocks, axis=0).astype(BF16)
        yp_ref[g] = _dg(u_ref[g], kmat, NN)
    for p in range(npair):
        ucat = jnp.concatenate([u_ref[2 * p], u_ref[2 * p + 1]], axis=1)
        gre_ref[p] = _dg(ucat, wre_ref[p], NN)
        gim_ref[p] = _dg(ucat, wim_ref[p], NN)

    def cma(xr, xi, ar, ai, sr, si):
        return xr + (ar * sr - ai * si), xi + (ar * si + ai * sr)

    row = lax.broadcasted_iota(jnp.int32, (SUBLANES, LANES), 0)
    per_batch = rows // nbatch

    def tile(t, carry, base):
        r0 = pl.multiple_of(base + t * SUBLANES, SUBLANES)
        new_carry = []
        for p in range(npair):
            xr = gre_ref[p, pl.ds(r0, SUBLANES), :]
            xi = gim_ref[p, pl.ds(r0, SUBLANES), :]
            for lvl, sh in enumerate((1, 2, 4)):
                ar = cre_ref[p, lvl * SUBLANES:(lvl + 1) * SUBLANES, :]
                ai = cim_ref[p, lvl * SUBLANES:(lvl + 1) * SUBLANES, :]
                xr, xi = cma(xr, xi, ar, ai, pltpu.roll(xr, sh, 0), pltpu.roll(xi, sh, 0))
            ar = cre_ref[p, 3 * SUBLANES:4 * SUBLANES, :]
            ai = cim_ref[p, 3 * SUBLANES:4 * SUBLANES, :]
            cr, ci = carry[2 * p], carry[2 * p + 1]
            xr, xi = cma(xr, xi, ar, ai, cr, ci)
            xre_ref[p, pl.ds(r0, SUBLANES), :] = jnp.where(row == 0, cr, pltpu.roll(xr, 1, 0))
            xim_ref[p, pl.ds(r0, SUBLANES), :] = jnp.where(row == 0, ci, pltpu.roll(xi, 1, 0))
            new_carry.append(jnp.broadcast_to(xr[SUBLANES - 1:SUBLANES, :], (SUBLANES, LANES)))
            new_carry.append(jnp.broadcast_to(xi[SUBLANES - 1:SUBLANES, :], (SUBLANES, LANES)))
        return tuple(new_carry)

    zero = jnp.zeros((SUBLANES, LANES), F32)
    for b in range(nbatch):
        lax.fori_loop(0, per_batch // SUBLANES, functools.partial(tile, base=b * per_batch), (zero,) * (2 * npair))

    for p in range(npair):
        xs = jnp.concatenate([xre_ref[p], xim_ref[p]], axis=1).astype(BF16)
        corr = _dg(xs, vt_ref[p], NT)
        for k in range(2):
            g = 2 * p + k
            y = yp_ref[g] + corr[:, k * wlane:(k + 1) * wlane] + d_ref[g] * u_ref[g].astype(F32)
            y_ref[g] = jax.nn.gelu(y).astype(y_ref.dtype)


def s5_ssm(p, batch, seqlen, lam_re, lam_im, log_dt, b_re, b_im, c_re, c_im, d_skip):
    ngroup, nstate = lam_re.shape
    h, blk = S5_GROUP, S5_BLOCK
    width = ngroup * h
    wl = blk * h
    nb = batch * seqlen // blk
    gp = S5_PREP_GROUPS
    assert nstate == S5_STATE and 2 * nstate == LANES
    assert seqlen % (blk * SUBLANES) == 0 and ngroup % gp == 0 and ngroup % (2 * S5_PAIRS) == 0

    g3 = lambda x: x.reshape(ngroup, 1, nstate)
    bt = lambda x: jnp.swapaxes(x, 1, 2)
    spec1 = pl.BlockSpec((gp, 1, nstate), lambda i: (i, 0, 0))
    spec_c = pl.BlockSpec((gp, h, nstate), lambda i: (i, 0, 0))
    spec_w = pl.BlockSpec((gp, wl, nstate), lambda i: (i, 0, 0))
    spec_q = pl.BlockSpec((gp, SUBLANES, nstate), lambda i: (i, 0, 0))
    spec_k = pl.BlockSpec((gp, h, wl), lambda i: (i, 0, 0))
    f = lambda *s: jax.ShapeDtypeStruct(s, F32)
    kt, w_re, w_im, v_re, v_im, q_re, q_im = pl.pallas_call(
        _s5_prep_kernel,
        grid=(ngroup // gp,),
        in_specs=[spec1, spec1, spec1, spec_c, spec_c, spec_c, spec_c],
        out_specs=[spec_k, spec_w, spec_w, spec_w, spec_w, spec_q, spec_q],
        out_shape=[f(ngroup, h, wl), f(ngroup, wl, nstate), f(ngroup, wl, nstate), f(ngroup, wl, nstate),
                   f(ngroup, wl, nstate), f(ngroup, SUBLANES, nstate), f(ngroup, SUBLANES, nstate)],
        scratch_shapes=[pltpu.VMEM((gp, (blk + 1) * h, nstate), F32), pltpu.VMEM((gp, (blk + 1) * h, nstate), F32)],
        compiler_params=_cparams(("parallel",)),
        name="s5_prep",
    )(g3(lam_re), g3(lam_im), jnp.broadcast_to(log_dt[:, None, None], (ngroup, 1, nstate)), c_re, c_im,
      bt(b_re), bt(b_im))

    npairs = ngroup // 2

    def pair_rows(x):
        x = x.reshape(npairs, 2, x.shape[1], nstate)
        z = jnp.zeros_like(x[:, 0])
        return jnp.concatenate([jnp.concatenate([x[:, 0], z], axis=2), jnp.concatenate([z, x[:, 1]], axis=2)], axis=1)

    w_re2 = pair_rows(w_re).astype(BF16)
    w_im2 = pair_rows(w_im).astype(BF16)
    vt2 = jnp.concatenate([pair_rows(v_re), pair_rows(v_im)], axis=2).astype(BF16)

    def pair_lanes(x):
        x = x.reshape(npairs, 2, SUBLANES, nstate)
        return jnp.concatenate([x[:, 0], x[:, 1]], axis=2)

    rows = jnp.arange(SUBLANES)[None, :, None]

    def scan_consts(q):
        q = pair_lanes(q)
        levels = [jnp.where(rows >= sh, q[:, sh - 1:sh, :], 0.0) for sh in (1, 2, 4)]
        return jnp.concatenate(levels + [q], axis=1)

    c_re2, c_im2 = scan_consts(q_re), scan_consts(q_im)
    d_row = jnp.tile(d_skip.reshape(ngroup, 1, h), (1, 1, blk))

    u = p[:, :width].reshape(nb, blk, ngroup, h).transpose(2, 0, 1, 3).reshape(ngroup, nb, wl)

    np_ = S5_PAIRS
    gs = 2 * np_
    grp = lambda r, c: pl.BlockSpec((gs, r, c), lambda i: (i, 0, 0))
    par = lambda r, c: pl.BlockSpec((np_, r, c), lambda i: (i, 0, 0))
    y = pl.pallas_call(
        functools.partial(_s5_kernel, batch),
        grid=(ngroup // gs,),
        in_specs=[grp(nb, wl), grp(h, wl), par(2 * wl, LANES), par(2 * wl, LANES), par(2 * wl, 2 * LANES),
                  par(4 * SUBLANES, LANES), par(4 * SUBLANES, LANES), grp(1, wl)],
        out_specs=grp(nb, wl),
        out_shape=jax.ShapeDtypeStruct((ngroup, nb, wl), BF16),
        scratch_shapes=[pltpu.VMEM((np_, nb, LANES), F32)] * 4 + [pltpu.VMEM((gs, nb, wl), F32)],
        compiler_params=_cparams(("parallel",)),
        name="s5_ssm",
    )(u, kt, w_re2, w_im2, vt2, c_re2, c_im2, d_row)
    return y.reshape(ngroup, nb, blk, h).transpose(1, 2, 0, 3).reshape(batch * seqlen, width)


def _glu_kernel(y_ref, yj_ref, gj_ref, w_ref, b_ref, o_ref):
    z = jnp.dot(y_ref[...], w_ref[...], preferred_element_type=F32) + b_ref[...]
    gate = gj_ref[...].astype(F32)
    yj = yj_ref[...].astype(F32)
    o_ref[...] = (yj * jax.nn.sigmoid(z) * (gate * jax.nn.sigmoid(gate))).astype(o_ref.dtype)


def glu_gate(y, p, w, b, tm=1024, tn=1024):
    m, d = y.shape
    tm = min(tm, m)
    goff = d // tn
    return pl.pallas_call(
        _glu_kernel,
        grid=(m // tm, d // tn),
        in_specs=[pl.BlockSpec((tm, d), lambda i, j: (i, 0)),
                  pl.BlockSpec((tm, tn), lambda i, j: (i, j)),
                  pl.BlockSpec((tm, tn), lambda i, j: (i, goff + j)),
                  pl.BlockSpec((d, tn), lambda i, j: (0, j)),
                  pl.BlockSpec((1, tn), lambda i, j: (0, j))],
        out_specs=pl.BlockSpec((tm, tn), lambda i, j: (i, j)),
        out_shape=jax.ShapeDtypeStruct((m, d), BF16),
        compiler_params=_cparams(("parallel", "arbitrary")),
        name="glu_gate",
    )(y, y, p, w, b.reshape(1, d))


def _final_kernel(a_ref, h_ref, w_ref, g_ref, o_ref):
    h = h_ref[...] + jnp.dot(a_ref[...], w_ref[...], preferred_element_type=F32)
    ms = jnp.mean(h * h, axis=-1, keepdims=True)
    o_ref[...] = (h * lax.rsqrt(ms + RMS_EPS)) * g_ref[...]


def final_out(a, h, w, g, tm=256):
    m, d = h.shape
    tm = min(tm, m)
    row = pl.BlockSpec((tm, d), lambda i: (i, 0))
    return pl.pallas_call(
        _final_kernel,
        grid=(m // tm,),
        in_specs=[row, row, pl.BlockSpec((d, d), lambda i: (0, 0)), pl.BlockSpec((1, d), lambda i: (0, 0))],
        out_specs=row,
        out_shape=jax.ShapeDtypeStruct((m, d), F32),
        compiler_params=_cparams(("parallel",)),
        name="final_out",
    )(a, h, w, g.reshape(1, d))


def kernel(x, norm_g, final_g, ab_w_in, rwkv_shift_mix, rwkv_w_up, rwkv_w0, rwkv_a_up, rwkv_a0, rwkv_k_k, rwkv_k_a, rwkv_r_k, rwkv_gn_w, rwkv_gn_b, ab_w_out, s5_w_in, s5_lam_re, s5_lam_im, s5_log_dt, s5_b_re, s5_b_im, s5_c_re, s5_c_im, s5_d, s5_w_glu, s5_b_glu, s5_w_out):
    batch, seqlen, d = x.shape
    m = batch * seqlen
    rwkv_w = rwkv_w0.shape[1]
    rwkv_proj = 3 * rwkv_w + 2 * LORA_RANK
    sb_w = (ab_w_in.shape[2] - rwkv_proj - d) // 3
    x2 = x.reshape(m, d)

    w_in = ab_w_in[0]
    scale = HEAD_DIM ** -0.5
    col_scale = jnp.concatenate([jnp.full((sb_w,), scale, F32), jnp.ones((2 * sb_w,), F32)])
    w_rwkv = w_in[:, :rwkv_proj].astype(BF16)
    w_sb = (w_in[:, rwkv_proj:rwkv_proj + 3 * sb_w] * col_scale).astype(BF16)
    w_gate = w_in[:, rwkv_proj + 3 * sb_w:].astype(BF16)
    xn = rmsnorm(x2, norm_g[0], BF16)
    p_rwkv = matmul(xn, w_rwkv, F32, tn=640, name="proj_rwkv")
    p_sb = matmul(xn, w_sb, BF16, name="proj_sb")
    gate0 = matmul(xn, w_gate, BF16, name="proj_gate")
    y_a = rwkv7(p_rwkv, batch, seqlen, rwkv_shift_mix[0], rwkv_w_up[0], rwkv_w0[0], rwkv_a_up[0], rwkv_a0[0],
                rwkv_k_k[0], rwkv_k_a[0], rwkv_r_k[0], rwkv_gn_w[0], rwkv_gn_b[0])
    y_b = stick_breaking(p_sb, batch, seqlen, sb_w)
    h1, hn1 = gate_out(y_a, y_b, gate0, x2, ab_w_out[0].astype(BF16), norm_g[1])

    p1 = matmul(hn1, s5_w_in[0].astype(BF16), BF16, name="proj_s5")
    y_s5 = s5_ssm(p1, batch, seqlen, s5_lam_re[0], s5_lam_im[0], s5_log_dt[0], s5_b_re[0], s5_b_im[0],
                  s5_c_re[0], s5_c_im[0], s5_d[0])
    act = glu_gate(y_s5, p1, s5_w_glu[0].astype(BF16), s5_b_glu[0])
    out = final_out(act, h1, s5_w_out[0].astype(BF16), final_g)
    return out.reshape(batch, seqlen, d)
```

```python
import functools
import math

import jax
import jax.numpy as jnp
from jax import lax
from jax.experimental import pallas as pl
from jax.experimental.pallas import tpu as pltpu

F32 = jnp.float32
BF16 = jnp.bfloat16

HEAD_DIM = 64
LANES = 128
SUBLANES = 8
LORA_RANK = 64
S5_GROUP = 16
S5_STATE = 64
RMS_EPS = 1e-6
GN_EPS = 64e-5
DECAY_SCALE = math.exp(-0.5)

RWKV_CHUNK = 64
RWKV_PAIRS = 8
SB_BLOCK = 128
SB_QBLOCK = 256
S5_BLOCK = 16
S5_PAIRS = 4
S5_PREP_GROUPS = 16
S5_RELAYOUT_ROWS = 64
VMEM_LIMIT = 56 * 1024 * 1024

NN = (((1,), (0,)), ((), ()))
NT = (((1,), (1,)), ((), ()))
TN = (((0,), (0,)), ((), ()))


def _cparams(sem):
    return pltpu.CompilerParams(dimension_semantics=sem, vmem_limit_bytes=VMEM_LIMIT)


def _split(x):
    hi = x.astype(BF16)
    lo = (x - hi.astype(F32)).astype(BF16)
    return hi, lo


def _dg(a, b, dn):
    return lax.dot_general(a, b, dn, preferred_element_type=F32)


def _dot(a, b, dn=NN, passes=3):
    if passes == 1:
        return _dg(a.astype(BF16), b.astype(BF16), dn)
    ah, al = _split(a)
    bh, bl = _split(b)
    return _dg(ah, bh, dn) + (_dg(ah, bl, dn) + _dg(al, bh, dn))


def _dot_rhs_exact(a, b01, passes=3):
    ah = a.astype(BF16)
    out = _dg(ah, b01, NN)
    rem = a - ah.astype(F32)
    for _ in range(passes - 1):
        rh = rem.astype(BF16)
        out = out + _dg(rh, b01, NN)
        rem = rem - rh.astype(F32)
    return out


def _dot_lhs_exact(a01, b, passes=3):
    bh = b.astype(BF16)
    out = _dg(a01, bh, NN)
    rem = b - bh.astype(F32)
    for _ in range(passes - 1):
        rh = rem.astype(BF16)
        out = out + _dg(a01, rh, NN)
        rem = rem - rh.astype(F32)
    return out


def _rmsnorm_kernel(x_ref, g_ref, o_ref):
    x = x_ref[...]
    ms = jnp.mean(x * x, axis=-1, keepdims=True)
    o_ref[...] = ((x * lax.rsqrt(ms + RMS_EPS)) * g_ref[...]).astype(o_ref.dtype)


def rmsnorm(x, g, out_dtype, tm=512):
    m, d = x.shape
    tm = min(tm, m)
    return pl.pallas_call(
        _rmsnorm_kernel,
        grid=(m // tm,),
        in_specs=[pl.BlockSpec((tm, d), lambda i: (i, 0)), pl.BlockSpec((1, d), lambda i: (0, 0))],
        out_specs=pl.BlockSpec((tm, d), lambda i: (i, 0)),
        out_shape=jax.ShapeDtypeStruct((m, d), out_dtype),
        compiler_params=_cparams(("parallel",)),
        name="rmsnorm",
    )(x, g.reshape(1, d))


def _mm_kernel(a_ref, w_ref, o_ref):
    o_ref[...] = jnp.dot(a_ref[...], w_ref[...], preferred_element_type=F32).astype(o_ref.dtype)


def matmul(a, w, out_dtype, tm=1024, tn=1024, name="matmul"):
    m, k = a.shape
    n = w.shape[1]
    tm = min(tm, m)
    assert m % tm == 0 and n % tn == 0
    return pl.pallas_call(
        _mm_kernel,
        grid=(m // tm, n // tn),
        in_specs=[pl.BlockSpec((tm, k), lambda i, j: (i, 0)), pl.BlockSpec((k, tn), lambda i, j: (0, j))],
        out_specs=pl.BlockSpec((tm, tn), lambda i, j: (i, j)),
        out_shape=jax.ShapeDtypeStruct((m, n), out_dtype),
        compiler_params=_cparams(("parallel", "arbitrary")),
        name=name,
    )(a, w)


def _rwkv_kernel(r_ref, k_ref, v_ref, lo_ref, mr_ref, mk_ref, mv_ref, mlo_ref, wup_ref, aup_ref,
                 w0_ref, a0_ref, kk_ref, ka_ref, rk_ref, gnw_ref, gnb_ref,
                 seg_ref, tri_ref, strict_ref, incl_ref, eye_ref,
                 y_ref, s_ref, prev_ref):
    c = pl.program_id(2)
    ch = RWKV_CHUNK
    npair = r_ref.shape[1] // LANES

    @pl.when(c == 0)
    def _():
        s_ref[...] = jnp.zeros_like(s_ref)
        prev_ref[...] = jnp.zeros_like(prev_ref)

    def token_shift(x, idx, mix):
        row = lax.broadcasted_iota(jnp.int32, x.shape, 0)
        prev = prev_ref[idx:idx + 1, 0:x.shape[1]]
        shifted = jnp.where(row == 0, prev, pltpu.roll(x, 1, 0))
        return x + (shifted - x) * mix

    r_in, k_in, v_in, lo_in = r_ref[...], k_ref[...], v_ref[...], lo_ref[...]
    r = token_shift(r_in, 0, mr_ref[...])
    k = token_shift(k_in, 1, mk_ref[...])
    v = token_shift(v_in, 2, mv_ref[...])
    lo = token_shift(lo_in, 3, mlo_ref[...])
    prev_ref[0:1, :] = r_in[ch - 1:ch, :]
    prev_ref[1:2, :] = k_in[ch - 1:ch, :]
    prev_ref[2:3, :] = v_in[ch - 1:ch, :]
    prev_ref[3:4, 0:LANES] = lo_in[ch - 1:ch, :]

    z_w = w0_ref[...] + _dot(jnp.tanh(lo), wup_ref[...])
    logw = -DECAY_SCALE * jax.nn.sigmoid(z_w)
    a = jax.nn.sigmoid(a0_ref[...] + _dot(lo, aup_ref[...]))

    seg2 = seg_ref[...]

    def head_sum(x):
        tiles = []
        for t in range(npair):
            hi, lo_ = _split(x[:, t * LANES:(t + 1) * LANES])
            tiles.append(_dg(jnp.concatenate([hi, lo_], axis=1), seg2, NN))
        return jnp.concatenate(tiles, axis=1)

    kk = k * kk_ref[...]
    kk = kk * lax.rsqrt(jnp.maximum(head_sum(kk * kk), 1e-24))
    k2 = k * (1.0 + (a - 1.0) * ka_ref[...])
    ab = kk * a

    l_hi = logw.astype(BF16)
    rem = logw - l_hi.astype(F32)
    l_mid = rem.astype(BF16)
    l_lo = (rem - l_mid.astype(F32)).astype(BF16)
    cum = _dg(tri_ref[...], jnp.concatenate([l_hi, l_mid, l_lo], axis=0), NN)
    cum_last = cum[ch - 1:ch, :]
    e_cum = jnp.exp(cum)
    e_ncum = jnp.exp(-cum)
    e_tail = jnp.exp(cum_last - cum)
    rt = r * e_cum
    kt = k2 * e_ncum
    bt = ab * e_ncum
    at = -kk * jnp.exp(cum - logw)
    khat = k2 * e_tail
    bhat = ab * e_tail
    p_last = e_cum[ch - 1:ch, :]

    lane = lax.broadcasted_iota(jnp.int32, (ch, LANES), 1)
    head0 = lane < HEAD_DIM
    strict = strict_ref[...] > 0.5
    incl = incl_ref[...] > 0.5
    eye = eye_ref[...]

    def stack(x):
        return jnp.concatenate([jnp.where(head0, x, 0.0), jnp.where(head0, 0.0, x)], axis=0)

    def mm(x, w, dn=NN):
        return _dg(x.astype(BF16), w.astype(BF16), dn)

    pairs = range(npair)
    rows = 2 * ch
    sl = [slice(p * LANES, (p + 1) * LANES) for p in pairs]
    at2, rt2, bt2, kt2, v2, khat2, bhat2 = ([stack(t[:, sl[p]]) for p in pairs]
                                            for t in (at, rt, bt, kt, v, khat, bhat))
    gram = []
    for p in pairs:
        lh, ll = _split(jnp.concatenate([at2[p], rt2[p]], axis=0))
        rh, rl = _split(jnp.concatenate([bt2[p], kt2[p]], axis=0))
        gram.append(_dg(jnp.concatenate([lh, lh, ll], axis=1), jnp.concatenate([rh, rl, rh], axis=1), NT))
    a_ab = [jnp.where(strict, g[:rows, :rows], 0.0) for g in gram]
    a_ak = [jnp.where(strict, g[:rows, rows:], 0.0) for g in gram]
    a_rb = [jnp.where(incl, g[rows:, :rows], 0.0) for g in gram]
    a_rk = [jnp.where(incl, g[rows:, rows:], 0.0) for g in gram]
    akv = [mm(a_ak[p], v2[p]) for p in pairs]

    tinv = [eye + a for a in a_ab]
    pw = a_ab
    for _ in range(int(math.log2(ch)) - 1):
        pw = [mm(x, x) for x in pw]
        tinv = [tinv[p] + mm(tinv[p], pw[p]) for p in pairs]

    s = [s_ref[p] for p in pairs]
    tw = [mm(tinv[p], jnp.concatenate([at2[p], akv[p]], axis=1)) for p in pairs]
    ws = [mm(jnp.concatenate([tw[p][:, :LANES], rt2[p]], axis=0), s[p], NT) for p in pairs]
    uv = [jnp.concatenate([ws[p][:rows] + tw[p][:, LANES:], v2[p]], axis=0) for p in pairs]
    y2 = [ws[p][rows:] + mm(jnp.concatenate([a_rb[p], a_rk[p]], axis=1), uv[p]) for p in pairs]
    for p in pairs:
        s_ref[p] = s[p] * p_last[:, sl[p]] + mm(uv[p], jnp.concatenate([bhat2[p], khat2[p]], axis=0), TN)
    y = jnp.concatenate([t[:ch, :] + t[ch:, :] for t in y2], axis=1)

    inv_n = 1.0 / HEAD_DIM
    mu = head_sum(y) * inv_n
    d = y - mu
    var = head_sum(d * d) * inv_n
    yn = d * lax.rsqrt(var + GN_EPS) * gnw_ref[...] + gnb_ref[...]
    y_ref[...] = yn + head_sum(r * k2 * rk_ref[...]) * v


def rwkv7(p, batch, seqlen, shift_mix, w_up, w0, a_up, a0, k_k, k_a, r_k, gn_w, gn_b):
    width = w0.shape[0]
    npair = width // LANES
    ch = RWKV_CHUNK
    nchunk = seqlen // ch
    assert seqlen % ch == 0 and 2 * LORA_RANK == LANES and 2 * HEAD_DIM == LANES
    zeros = jnp.zeros((LORA_RANK, width), F32)
    wup_pad = jnp.concatenate([w_up, zeros], axis=0)
    aup_pad = jnp.concatenate([zeros, a_up], axis=0)
    hb = RWKV_PAIRS
    assert npair % hb == 0
    idx = jnp.arange(LANES)
    seg = (idx[:, None] // HEAD_DIM == idx[None, :] // HEAD_DIM).astype(BF16)
    seg = jnp.concatenate([seg, seg], axis=0)
    t = jnp.arange(ch)
    tri = (t[None, :] <= t[:, None]).astype(BF16)
    tri = jnp.concatenate([tri, tri, tri], axis=1)
    i2 = jnp.arange(2 * ch)
    same = (i2[:, None] // ch) == (i2[None, :] // ch)
    strict = (same & ((i2[None, :] % ch) < (i2[:, None] % ch))).astype(F32)
    incl = (same & ((i2[None, :] % ch) <= (i2[:, None] % ch))).astype(F32)
    eye = jnp.eye(2 * ch, dtype=F32)
    row2 = lambda x: x.reshape(1, -1)

    wd = hb * LANES
    ngrp = npair // hb

    def tok(off):
        return pl.BlockSpec((ch, wd), lambda b, h, c: (b * nchunk + c, off + h))

    def par(off):
        return pl.BlockSpec((1, wd), lambda b, h, c: (0, off + h))

    def const(shape):
        return pl.BlockSpec(shape, lambda b, h, c: (0, 0))

    up = pl.BlockSpec((LANES, wd), lambda b, h, c: (0, h))
    lora_tok = pl.BlockSpec((ch, LANES), lambda b, h, c: (b * nchunk + c, 3 * npair))
    lora_par = pl.BlockSpec((1, LANES), lambda b, h, c: (0, 3 * npair))
    return pl.pallas_call(
        _rwkv_kernel,
        grid=(batch, ngrp, nchunk),
        in_specs=[tok(0), tok(ngrp), tok(2 * ngrp), lora_tok,
                  par(0), par(ngrp), par(2 * ngrp), lora_par, up, up,
                  par(0), par(0), par(0), par(0), par(0), par(0), par(0),
                  const((2 * LANES, LANES)), const((ch, 3 * ch)), const((2 * ch, 2 * ch)),
                  const((2 * ch, 2 * ch)), const((2 * ch, 2 * ch))],
        out_specs=pl.BlockSpec((ch, wd), lambda b, h, c: (b * nchunk + c, h)),
        out_shape=jax.ShapeDtypeStruct((batch * seqlen, width), F32),
        scratch_shapes=[pltpu.VMEM((hb, LANES, LANES), F32), pltpu.VMEM((SUBLANES, wd), F32)],
        compiler_params=_cparams(("parallel", "parallel", "arbitrary")),
        name="rwkv7",
    )(p, p, p, p, row2(shift_mix), row2(shift_mix), row2(shift_mix), row2(shift_mix), wup_pad, aup_pad,
      row2(w0), row2(a0), row2(k_k), row2(k_a), row2(r_k), row2(gn_w), row2(gn_b),
      seg, tri, strict, incl, eye)


def _sb_kernel(q_ref, k_ref, v_ref, uo_ref, o_ref, k2_ref, v2_ref, acc_ref, carry_ref, lb_ref, sums_ref):
    qi = pl.program_id(2)
    kb, qb = SB_BLOCK, SB_QBLOCK
    nsub = qb // kb

    @pl.when(qi == 0)
    def _():
        lane = lax.broadcasted_iota(jnp.int32, (kb, LANES), 1)
        head0 = lane < HEAD_DIM

        def fill(i, c):
            start = pl.multiple_of(i * kb, kb)
            for src, dst in ((k_ref, k2_ref), (v_ref, v2_ref)):
                t = src[pl.ds(start, kb), :].astype(F32)
                dst[i, 0:kb, :] = jnp.where(head0, t, 0.0).astype(BF16)
                dst[i, kb:2 * kb, :] = jnp.where(head0, 0.0, t).astype(BF16)
            return c

        lax.fori_loop(0, k_ref.shape[0] // kb, fill, 0)

    q = q_ref[...]
    uo = uo_ref[...]
    acc_ref[...] = jnp.zeros_like(acc_ref)
    carry_ref[...] = jnp.zeros_like(carry_ref)
    tpos = lax.broadcasted_iota(jnp.int32, (qb, kb), 0)
    spos = lax.broadcasted_iota(jnp.int32, (qb, kb), 1)

    def scores(sb, slot, diagonal):
        base = pl.multiple_of(sb * nsub, nsub)
        keys = k2_ref[pl.ds(base, nsub)].reshape(nsub * 2 * kb, LANES)
        z_all = _dg(q, keys, NT)
        for j in range(nsub):
            for h in range(2):
                z = z_all[:, (2 * j + h) * kb:(2 * j + h + 1) * kb]
                log_keep = -(jnp.maximum(z, 0.0) + jnp.log(1.0 + jnp.exp(-jnp.abs(z))))
                log_beta = log_keep + z
                if diagonal:
                    causal = (spos + j * kb) < tpos
                    log_keep = jnp.where(causal, log_keep, 0.0)
                    log_beta = jnp.where(causal, log_beta, -jnp.inf)
                hi = log_keep.astype(BF16)
                lo = (log_keep - hi.astype(F32)).astype(BF16)
                lb_ref[slot, 2 * j + h] = log_beta
                sums_ref[slot, 2 * j + h] = _dg(jnp.concatenate([hi, lo], axis=1), uo, NN)

    def accumulate(sb, slot):
        base = pl.multiple_of(sb * nsub, nsub)
        attn = {}
        for h in range(2):
            carry = carry_ref[h]
            for j in reversed(range(nsub)):
                s = sums_ref[slot, 2 * j + h]
                attn[j, h] = jnp.exp(lb_ref[slot, 2 * j + h] + carry + s[:, :kb]).astype(BF16)
                carry = carry + s[:, kb:]
            carry_ref[h] = carry
        weights = jnp.concatenate([attn[j, h] for j in range(nsub) for h in range(2)], axis=1)
        values = v2_ref[pl.ds(base, nsub)].reshape(nsub * 2 * kb, LANES)
        acc_ref[...] += _dg(weights, values, NN)

    scores(qi, 0, True)
    pairs = qi // 2

    def body(i, c):
        sb = qi - 1 - 2 * i
        scores(sb, 1, False)
        accumulate(sb + 1, 0)
        scores(sb - 1, 0, False)
        accumulate(sb, 1)
        return c

    lax.fori_loop(0, pairs, body, 0)
    odd = qi - 2 * pairs == 1

    @pl.when(odd)
    def _():
        scores(0, 1, False)
        accumulate(1, 0)
        accumulate(0, 1)

    @pl.when(jnp.logical_not(odd))
    def _():
        accumulate(0, 0)

    o_ref[...] = acc_ref[...]


def stick_breaking(p, batch, seqlen, width):
    npair = width // LANES
    kb, qb = SB_BLOCK, SB_QBLOCK
    nq = seqlen // qb
    j = jnp.arange(kb)
    later = (j[:, None] > j[None, :]).astype(BF16)
    uo = jnp.concatenate([later, jnp.ones((kb, kb), BF16)], axis=1)
    uo = jnp.concatenate([uo, uo], axis=0)
    return pl.pallas_call(
        _sb_kernel,
        grid=(batch, npair, nq),
        in_specs=[pl.BlockSpec((qb, LANES), lambda b, h, i: (b * nq + i, h)),
                  pl.BlockSpec((seqlen, LANES), lambda b, h, i: (b, npair + h)),
                  pl.BlockSpec((seqlen, LANES), lambda b, h, i: (b, 2 * npair + h)),
                  pl.BlockSpec((2 * kb, 2 * kb), lambda b, h, i: (0, 0))],
        out_specs=pl.BlockSpec((qb, LANES), lambda b, h, i: (b * nq + i, h)),
        out_shape=jax.ShapeDtypeStruct((batch * seqlen, width), F32),
        scratch_shapes=[pltpu.VMEM((seqlen // kb, 2 * kb, LANES), BF16),
                        pltpu.VMEM((seqlen // kb, 2 * kb, LANES), BF16),
                        pltpu.VMEM((qb, LANES), F32), pltpu.VMEM((2, qb, kb), F32),
                        pltpu.VMEM((2, 2 * qb // kb, qb, kb), F32), pltpu.VMEM((2, 2 * qb // kb, qb, 2 * kb), F32)],
        compiler_params=_cparams(("parallel", "parallel", "arbitrary")),
        name="stick_breaking",
    )(p, p, p, uo)


def _gate_out_kernel(ya_ref, yb_ref, g_ref, x_ref, w_ref, ng_ref, h_ref, hn_ref):
    gate = g_ref[...].astype(F32)
    y = jnp.concatenate([ya_ref[...], yb_ref[...]], axis=-1) * (gate * jax.nn.sigmoid(gate))
    h = x_ref[...] + jnp.dot(y.astype(BF16), w_ref[...], preferred_element_type=F32)
    h_ref[...] = h
    ms = jnp.mean(h * h, axis=-1, keepdims=True)
    hn_ref[...] = ((h * lax.rsqrt(ms + RMS_EPS)) * ng_ref[...]).astype(hn_ref.dtype)


def gate_out(ya, yb, gate, x, w, next_g, tm=256):
    m, d = x.shape
    half = ya.shape[1]
    tm = min(tm, m)
    row = lambda n: pl.BlockSpec((tm, n), lambda i: (i, 0))
    return pl.pallas_call(
        _gate_out_kernel,
        grid=(m // tm,),
        in_specs=[row(half), row(half), row(d), row(d),
                  pl.BlockSpec((d, d), lambda i: (0, 0)), pl.BlockSpec((1, d), lambda i: (0, 0))],
        out_specs=[row(d), row(d)],
        out_shape=[jax.ShapeDtypeStruct((m, d), F32), jax.ShapeDtypeStruct((m, d), BF16)],
        compiler_params=_cparams(("parallel",)),
        name="gate_out",
    )(ya, yb, gate, x, w, next_g.reshape(1, d))


def _s5_prep_kernel(lre_ref, lim_ref, ldt_ref, cre_ref, cim_ref, btre_ref, btim_ref,
                    kt_ref, wre_ref, wim_ref, vre_ref, vim_ref, qre_ref, qim_ref, ckre_ref, ckim_ref):
    blk = S5_BLOCK
    h = S5_GROUP
    lre, lim = lre_ref[...], lim_ref[...]
    dt = jnp.exp(ldt_ref[...])
    mag = jnp.exp(lre * dt)
    bre = mag * jnp.cos(lim * dt)
    bim = mag * jnp.sin(lim * dt)
    den = lre * lre + lim * lim
    nre, nim = bre - 1.0, bim
    fre = (nre * lre + nim * lim) / den
    fim = (nim * lre - nre * lim) / den
    cre, cim = cre_ref[...], cim_ref[...]
    ckr, cki = cre * fre - cim * fim, cre * fim + cim * fre
    btre, btim = btre_ref[...], btim_ref[...]
    pr, pi = jnp.ones_like(bre), jnp.zeros_like(bre)
    powers = []
    for tau in range(blk + 1):
        ckre_ref[:, tau * h:(tau + 1) * h, :] = ckr
        ckim_ref[:, tau * h:(tau + 1) * h, :] = cki
        powers.append((pr, pi))
        ckr, cki = ckr * bre - cki * bim, ckr * bim + cki * bre
        pr, pi = pr * bre - pi * bim, pr * bim + pi * bre
    for i in range(blk):
        pr, pi = powers[blk - 1 - i]
        wre_ref[:, i * h:(i + 1) * h, :] = pr * btre - pi * btim
        wim_ref[:, i * h:(i + 1) * h, :] = pr * btim + pi * btre
    vre_ref[...] = ckre_ref[:, h:(blk + 1) * h, :]
    vim_ref[...] = -ckim_ref[:, h:(blk + 1) * h, :]
    qr, qi = powers[blk]
    pr, pi = qr, qi
    for r in range(SUBLANES):
        qre_ref[:, r:r + 1, :] = pr
        qim_ref[:, r:r + 1, :] = pi
        pr, pi = pr * qr - pi * qi, pr * qi + pi * qr
    for g in range(kt_ref.shape[0]):
        kt_ref[g] = (_dot(btre[g], ckre_ref[g, 0:blk * h, :], NT) - _dot(btim[g], ckim_ref[g, 0:blk * h, :], NT))


def _s5_kernel(nbatch, p_ref, kt_ref, wre_ref, wim_ref, vt_ref, cre_ref, cim_ref, d_ref, o_ref,
               u_ref, gre_ref, gim_ref, xre_ref, xim_ref, yp_ref, stage_ref):
    npair = u_ref.shape[0] // 2
    rows = u_ref.shape[1]
    wlane = u_ref.shape[2]
    blk = S5_BLOCK
    ngrp = 2 * npair
    rc = S5_RELAYOUT_ROWS
    lane_chunk = lax.broadcasted_iota(jnp.int32, (rc, LANES), 1) // S5_GROUP

    def chunk_transpose(arrs):
        arrs = list(arrs)
        s = ngrp // 2
        while s:
            upper = (lane_chunk & s) != 0
            for x in range(ngrp):
                if x & s == 0:
                    ax, ay = arrs[x], arrs[x + s]
                    arrs[x] = jnp.where(upper, pltpu.roll(ay, s * S5_GROUP, 1), ax)
                    arrs[x + s] = jnp.where(upper, ay, pltpu.roll(ax, LANES - s * S5_GROUP, 1))
            s //= 2
        return arrs

    def relayout_in(c, carry):
        t0 = pl.multiple_of(c * (rc * blk), rc * blk)
        r0 = pl.multiple_of(c * rc, rc)
        stage_ref[...] = p_ref[pl.ds(t0, rc * blk), :].astype(F32)
        for half in range(blk // ngrp):
            z = chunk_transpose(stage_ref[pl.ds(ngrp * half + i, rc, stride=blk), :] for i in range(ngrp))
            for g in range(ngrp):
                u_ref[g, pl.ds(r0, rc), half * LANES:(half + 1) * LANES] = z[g].astype(BF16)
        return carry

    lax.fori_loop(0, rows // rc, relayout_in, 0)
    lane = lax.broadcasted_iota(jnp.int32, (S5_GROUP, wlane), 1)

    for g in range(2 * npair):
        kt = kt_ref[g]
        blocks = [kt] + [jnp.where(lane >= i * S5_GROUP, pltpu.roll(kt, i * S5_GROUP, 1), 0.0)
                         for i in range(1, S5_BLOCK)]
        kmat = jnp.concatenate(blocks, axis=0).astype(BF16)
        yp_ref[g] = _dg(u_ref[g], kmat, NN)
    for p in range(npair):
        ucat = jnp.concatenate([u_ref[2 * p], u_ref[2 * p + 1]], axis=1)
        gre_ref[p] = _dg(ucat, wre_ref[p], NN)
        gim_ref[p] = _dg(ucat, wim_ref[p], NN)

    def cma(xr, xi, ar, ai, sr, si):
        return xr + (ar * sr - ai * si), xi + (ar * si + ai * sr)

    row = lax.broadcasted_iota(jnp.int32, (SUBLANES, LANES), 0)
    per_batch = rows // nbatch

    def tile(t, carry, base):
        r0 = pl.multiple_of(base + t * SUBLANES, SUBLANES)
        new_carry = []
        for p in range(npair):
            xr = gre_ref[p, pl.ds(r0, SUBLANES), :]
            xi = gim_ref[p, pl.ds(r0, SUBLANES), :]
            for lvl, sh in enumerate((1, 2, 4)):
                ar = cre_ref[p, lvl * SUBLANES:(lvl + 1) * SUBLANES, :]
                ai = cim_ref[p, lvl * SUBLANES:(lvl + 1) * SUBLANES, :]
                xr, xi = cma(xr, xi, ar, ai, pltpu.roll(xr, sh, 0), pltpu.roll(xi, sh, 0))
            ar = cre_ref[p, 3 * SUBLANES:4 * SUBLANES, :]
            ai = cim_ref[p, 3 * SUBLANES:4 * SUBLANES, :]
            cr, ci = carry[2 * p], carry[2 * p + 1]
            xr, xi = cma(xr, xi, ar, ai, cr, ci)
            xre_ref[p, pl.ds(r0, SUBLANES), :] = jnp.where(row == 0, cr, pltpu.roll(xr, 1, 0))
            xim_ref[p, pl.ds(r0, SUBLANES), :] = jnp.where(row == 0, ci, pltpu.roll(xi, 1, 0))
            new_carry.append(jnp.broadcast_to(xr[SUBLANES - 1:SUBLANES, :], (SUBLANES, LANES)))
            new_carry.append(jnp.broadcast_to(xi[SUBLANES - 1:SUBLANES, :], (SUBLANES, LANES)))
        return tuple(new_carry)

    zero = jnp.zeros((SUBLANES, LANES), F32)
    for b in range(nbatch):
        lax.fori_loop(0, per_batch // SUBLANES, functools.partial(tile, base=b * per_batch), (zero,) * (2 * npair))

    for p in range(npair):
        xs = jnp.concatenate([xre_ref[p], xim_ref[p]], axis=1).astype(BF16)
        corr = _dg(xs, vt_ref[p], NT)
        for k in range(2):
            g = 2 * p + k
            y = yp_ref[g] + corr[:, k * wlane:(k + 1) * wlane] + d_ref[g] * u_ref[g].astype(F32)
            yp_ref[g] = jax.nn.gelu(y)

    def relayout_out(c, carry):
        t0 = pl.multiple_of(c * (rc * blk), rc * blk)
        r0 = pl.multiple_of(c * rc, rc)
        for half in range(blk // ngrp):
            yt = chunk_transpose(yp_ref[g, pl.ds(r0, rc), half * LANES:(half + 1) * LANES] for g in range(ngrp))
            for j in range(ngrp):
                stage_ref[pl.ds(ngrp * half + j, rc, stride=blk), :] = yt[j]
        o_ref[pl.ds(t0, rc * blk), :] = stage_ref[...].astype(o_ref.dtype)
        return carry

    lax.fori_loop(0, rows // rc, relayout_out, 0)


def s5_ssm(p, batch, seqlen, lam_re, lam_im, log_dt, b_re, b_im, c_re, c_im, d_skip):
    ngroup, nstate = lam_re.shape
    h, blk = S5_GROUP, S5_BLOCK
    width = ngroup * h
    wl = blk * h
    nb = batch * seqlen // blk
    gp = S5_PREP_GROUPS
    assert nstate == S5_STATE and 2 * nstate == LANES
    assert seqlen % (blk * SUBLANES) == 0 and ngroup % gp == 0 and ngroup % (2 * S5_PAIRS) == 0

    g3 = lambda x: x.reshape(ngroup, 1, nstate)
    bt = lambda x: jnp.swapaxes(x, 1, 2)
    spec1 = pl.BlockSpec((gp, 1, nstate), lambda i: (i, 0, 0))
    spec_c = pl.BlockSpec((gp, h, nstate), lambda i: (i, 0, 0))
    spec_w = pl.BlockSpec((gp, wl, nstate), lambda i: (i, 0, 0))
    spec_q = pl.BlockSpec((gp, SUBLANES, nstate), lambda i: (i, 0, 0))
    spec_k = pl.BlockSpec((gp, h, wl), lambda i: (i, 0, 0))
    f = lambda *s: jax.ShapeDtypeStruct(s, F32)
    kt, w_re, w_im, v_re, v_im, q_re, q_im = pl.pallas_call(
        _s5_prep_kernel,
        grid=(ngroup // gp,),
        in_specs=[spec1, spec1, spec1, spec_c, spec_c, spec_c, spec_c],
        out_specs=[spec_k, spec_w, spec_w, spec_w, spec_w, spec_q, spec_q],
        out_shape=[f(ngroup, h, wl), f(ngroup, wl, nstate), f(ngroup, wl, nstate), f(ngroup, wl, nstate),
                   f(ngroup, wl, nstate), f(ngroup, SUBLANES, nstate), f(ngroup, SUBLANES, nstate)],
        scratch_shapes=[pltpu.VMEM((gp, (blk + 1) * h, nstate), F32), pltpu.VMEM((gp, (blk + 1) * h, nstate), F32)],
        compiler_params=_cparams(("parallel",)),
        name="s5_prep",
    )(g3(lam_re), g3(lam_im), jnp.broadcast_to(log_dt[:, None, None], (ngroup, 1, nstate)), c_re, c_im,
      bt(b_re), bt(b_im))

    npairs = ngroup // 2

    def pair_rows(x):
        x = x.reshape(npairs, 2, x.shape[1], nstate)
        z = jnp.zeros_like(x[:, 0])
        return jnp.concatenate([jnp.concatenate([x[:, 0], z], axis=2), jnp.concatenate([z, x[:, 1]], axis=2)], axis=1)

    w_re2 = pair_rows(w_re).astype(BF16)
    w_im2 = pair_rows(w_im).astype(BF16)
    vt2 = jnp.concatenate([pair_rows(v_re), pair_rows(v_im)], axis=2).astype(BF16)

    def pair_lanes(x):
        x = x.reshape(npairs, 2, SUBLANES, nstate)
        return jnp.concatenate([x[:, 0], x[:, 1]], axis=2)

    rows = jnp.arange(SUBLANES)[None, :, None]

    def scan_consts(q):
        q = pair_lanes(q)
        levels = [jnp.where(rows >= sh, q[:, sh - 1:sh, :], 0.0) for sh in (1, 2, 4)]
        return jnp.concatenate(levels + [q], axis=1)

    c_re2, c_im2 = scan_consts(q_re), scan_consts(q_im)
    d_row = jnp.tile(d_skip.reshape(ngroup, 1, h), (1, 1, blk))

    np_ = S5_PAIRS
    gs = 2 * np_
    assert gs * h == LANES and blk % gs == 0 and nb % S5_RELAYOUT_ROWS == 0
    tok = pl.BlockSpec((nb * blk, LANES), lambda i: (0, i))
    grp = lambda r, c: pl.BlockSpec((gs, r, c), lambda i: (i, 0, 0))
    par = lambda r, c: pl.BlockSpec((np_, r, c), lambda i: (i, 0, 0))
    return pl.pallas_call(
        functools.partial(_s5_kernel, batch),
        grid=(ngroup // gs,),
        in_specs=[tok, grp(h, wl), par(2 * wl, LANES), par(2 * wl, LANES), par(2 * wl, 2 * LANES),
                  par(4 * SUBLANES, LANES), par(4 * SUBLANES, LANES), grp(1, wl)],
        out_specs=tok,
        out_shape=jax.ShapeDtypeStruct((batch * seqlen, width), BF16),
        scratch_shapes=[pltpu.VMEM((gs, nb, wl), BF16)] + [pltpu.VMEM((np_, nb, LANES), F32)] * 4
        + [pltpu.VMEM((gs, nb, wl), F32), pltpu.VMEM((S5_RELAYOUT_ROWS * blk, LANES), F32)],
        compiler_params=_cparams(("parallel",)),
        name="s5_ssm",
    )(p, kt, w_re2, w_im2, vt2, c_re2, c_im2, d_row)


def _glu_kernel(y_ref, yj_ref, gj_ref, w_ref, b_ref, o_ref):
    z = jnp.dot(y_ref[...], w_ref[...], preferred_element_type=F32) + b_ref[...]
    gate = gj_ref[...].astype(F32)
    yj = yj_ref[...].astype(F32)
    o_ref[...] = (yj * jax.nn.sigmoid(z) * (gate * jax.nn.sigmoid(gate))).astype(o_ref.dtype)


def glu_gate(y, p, w, b, tm=1024, tn=1024):
    m, d = y.shape
    tm = min(tm, m)
    goff = d // tn
    return pl.pallas_call(
        _glu_kernel,
        grid=(m // tm, d // tn),
        in_specs=[pl.BlockSpec((tm, d), lambda i, j: (i, 0)),
                  pl.BlockSpec((tm, tn), lambda i, j: (i, j)),
                  pl.BlockSpec((tm, tn), lambda i, j: (i, goff + j)),
                  pl.BlockSpec((d, tn), lambda i, j: (0, j)),
                  pl.BlockSpec((1, tn), lambda i, j: (0, j))],
        out_specs=pl.BlockSpec((tm, tn), lambda i, j: (i, j)),
        out_shape=jax.ShapeDtypeStruct((m, d), BF16),
        compiler_params=_cparams(("parallel", "arbitrary")),
        name="glu_gate",
    )(y, y, p, w, b.reshape(1, d))


def _final_kernel(a_ref, h_ref, w_ref, g_ref, o_ref):
    h = h_ref[...] + jnp.dot(a_ref[...], w_ref[...], preferred_element_type=F32)
    ms = jnp.mean(h * h, axis=-1, keepdims=True)
    o_ref[...] = (h * lax.rsqrt(ms + RMS_EPS)) * g_ref[...]


def final_out(a, h, w, g, tm=256):
    m, d = h.shape
    tm = min(tm, m)
    row = pl.BlockSpec((tm, d), lambda i: (i, 0))
    return pl.pallas_call(
        _final_kernel,
        grid=(m // tm,),
        in_specs=[row, row, pl.BlockSpec((d, d), lambda i: (0, 0)), pl.BlockSpec((1, d), lambda i: (0, 0))],
        out_specs=row,
        out_shape=jax.ShapeDtypeStruct((m, d), F32),
        compiler_params=_cparams(("parallel",)),
        name="final_out",
    )(a, h, w, g.reshape(1, d))


def kernel(x, norm_g, final_g, ab_w_in, rwkv_shift_mix, rwkv_w_up, rwkv_w0, rwkv_a_up, rwkv_a0, rwkv_k_k, rwkv_k_a, rwkv_r_k, rwkv_gn_w, rwkv_gn_b, ab_w_out, s5_w_in, s5_lam_re, s5_lam_im, s5_log_dt, s5_b_re, s5_b_im, s5_c_re, s5_c_im, s5_d, s5_w_glu, s5_b_glu, s5_w_out):
    batch, seqlen, d = x.shape
    m = batch * seqlen
    rwkv_w = rwkv_w0.shape[1]
    rwkv_proj = 3 * rwkv_w + 2 * LORA_RANK
    sb_w = (ab_w_in.shape[2] - rwkv_proj - d) // 3
    x2 = x.reshape(m, d)

    w_in = ab_w_in[0]
    scale = HEAD_DIM ** -0.5
    col_scale = jnp.concatenate([jnp.full((sb_w,), scale, F32), jnp.ones((2 * sb_w,), F32)])
    w_rwkv = w_in[:, :rwkv_proj].astype(BF16)
    w_sb = (w_in[:, rwkv_proj:rwkv_proj + 3 * sb_w] * col_scale).astype(BF16)
    w_gate = w_in[:, rwkv_proj + 3 * sb_w:].astype(BF16)
    xn = rmsnorm(x2, norm_g[0], BF16)
    p_rwkv = matmul(xn, w_rwkv, F32, tn=640, name="proj_rwkv")
    p_sb = matmul(xn, w_sb, BF16, name="proj_sb")
    gate0 = matmul(xn, w_gate, BF16, name="proj_gate")
    y_a = rwkv7(p_rwkv, batch, seqlen, rwkv_shift_mix[0], rwkv_w_up[0], rwkv_w0[0], rwkv_a_up[0], rwkv_a0[0],
                rwkv_k_k[0], rwkv_k_a[0], rwkv_r_k[0], rwkv_gn_w[0], rwkv_gn_b[0])
    y_b = stick_breaking(p_sb, batch, seqlen, sb_w)
    h1, hn1 = gate_out(y_a, y_b, gate0, x2, ab_w_out[0].astype(BF16), norm_g[1])

    p1 = matmul(hn1, s5_w_in[0].astype(BF16), BF16, name="proj_s5")
    y_s5 = s5_ssm(p1, batch, seqlen, s5_lam_re[0], s5_lam_im[0], s5_log_dt[0], s5_b_re[0], s5_b_im[0],
                  s5_c_re[0], s5_c_im[0], s5_d[0])
    act = glu_gate(y_s5, p1, s5_w_glu[0].astype(BF16), s5_b_glu[0])
    out = final_out(act, h1, s5_w_out[0].astype(BF16), final_g)
    return out.reshape(batch, seqlen, d)
```

```python
import functools
import math

import jax
import jax.numpy as jnp
from jax import lax
from jax.experimental import pallas as pl
from jax.experimental.pallas import tpu as pltpu

F32 = jnp.float32
BF16 = jnp.bfloat16

HEAD_DIM = 64
LANES = 128
SUBLANES = 8
LORA_RANK = 64
S5_GROUP = 16
S5_STATE = 64
RMS_EPS = 1e-6
GN_EPS = 64e-5
DECAY_SCALE = math.exp(-0.5)

RWKV_CHUNK = 64
RWKV_PAIRS = 8
SB_BLOCK = 128
SB_QBLOCK = 256
SB_PAIRS = 4
S5_BLOCK = 16
S5_PAIRS = 4
S5_PREP_GROUPS = 16
S5_RELAYOUT_ROWS = 64
VMEM_LIMIT = 56 * 1024 * 1024

NN = (((1,), (0,)), ((), ()))
NT = (((1,), (1,)), ((), ()))
TN = (((0,), (0,)), ((), ()))


def _cparams(sem):
    return pltpu.CompilerParams(dimension_semantics=sem, vmem_limit_bytes=VMEM_LIMIT)


def _split(x):
    hi = x.astype(BF16)
    lo = (x - hi.astype(F32)).astype(BF16)
    return hi, lo


def _dg(a, b, dn):
    return lax.dot_general(a, b, dn, preferred_element_type=F32)


def _dot(a, b, dn=NN, passes=3):
    if passes == 1:
        return _dg(a.astype(BF16), b.astype(BF16), dn)
    ah, al = _split(a)
    bh, bl = _split(b)
    return _dg(ah, bh, dn) + (_dg(ah, bl, dn) + _dg(al, bh, dn))


def _dot_rhs_exact(a, b01, passes=3):
    ah = a.astype(BF16)
    out = _dg(ah, b01, NN)
    rem = a - ah.astype(F32)
    for _ in range(passes - 1):
        rh = rem.astype(BF16)
        out = out + _dg(rh, b01, NN)
        rem = rem - rh.astype(F32)
    return out


def _dot_lhs_exact(a01, b, passes=3):
    bh = b.astype(BF16)
    out = _dg(a01, bh, NN)
    rem = b - bh.astype(F32)
    for _ in range(passes - 1):
        rh = rem.astype(BF16)
        out = out + _dg(a01, rh, NN)
        rem = rem - rh.astype(F32)
    return out


def _rmsnorm_kernel(x_ref, g_ref, o_ref):
    x = x_ref[...]
    ms = jnp.mean(x * x, axis=-1, keepdims=True)
    o_ref[...] = ((x * lax.rsqrt(ms + RMS_EPS)) * g_ref[...]).astype(o_ref.dtype)


def rmsnorm(x, g, out_dtype, tm=512):
    m, d = x.shape
    tm = min(tm, m)
    return pl.pallas_call(
        _rmsnorm_kernel,
        grid=(m // tm,),
        in_specs=[pl.BlockSpec((tm, d), lambda i: (i, 0)), pl.BlockSpec((1, d), lambda i: (0, 0))],
        out_specs=pl.BlockSpec((tm, d), lambda i: (i, 0)),
        out_shape=jax.ShapeDtypeStruct((m, d), out_dtype),
        compiler_params=_cparams(("parallel",)),
        name="rmsnorm",
    )(x, g.reshape(1, d))


def _mm_kernel(a_ref, w_ref, o_ref):
    o_ref[...] = jnp.dot(a_ref[...], w_ref[...], preferred_element_type=F32).astype(o_ref.dtype)


def matmul(a, w, out_dtype, tm=1024, tn=1024, name="matmul"):
    m, k = a.shape
    n = w.shape[1]
    tm = min(tm, m)
    assert m % tm == 0 and n % tn == 0
    return pl.pallas_call(
        _mm_kernel,
        grid=(m // tm, n // tn),
        in_specs=[pl.BlockSpec((tm, k), lambda i, j: (i, 0)), pl.BlockSpec((k, tn), lambda i, j: (0, j))],
        out_specs=pl.BlockSpec((tm, tn), lambda i, j: (i, j)),
        out_shape=jax.ShapeDtypeStruct((m, n), out_dtype),
        compiler_params=_cparams(("parallel", "arbitrary")),
        name=name,
    )(a, w)


def _rwkv_kernel(r_ref, k_ref, v_ref, lo_ref, mr_ref, mk_ref, mv_ref, mlo_ref, wup_ref, aup_ref,
                 w0_ref, a0_ref, kk_ref, ka_ref, rk_ref, gnw_ref, gnb_ref,
                 seg_ref, tri_ref, strict_ref, incl_ref, eye_ref,
                 y_ref, s_ref, prev_ref):
    c = pl.program_id(2)
    ch = RWKV_CHUNK
    npair = r_ref.shape[1] // LANES

    @pl.when(c == 0)
    def _():
        s_ref[...] = jnp.zeros_like(s_ref)
        prev_ref[...] = jnp.zeros_like(prev_ref)

    def token_shift(x, idx, mix):
        row = lax.broadcasted_iota(jnp.int32, x.shape, 0)
        prev = prev_ref[idx:idx + 1, 0:x.shape[1]]
        shifted = jnp.where(row == 0, prev, pltpu.roll(x, 1, 0))
        return x + (shifted - x) * mix

    r_in, k_in, v_in, lo_in = r_ref[...], k_ref[...], v_ref[...], lo_ref[...]
    r = token_shift(r_in, 0, mr_ref[...])
    k = token_shift(k_in, 1, mk_ref[...])
    v = token_shift(v_in, 2, mv_ref[...])
    lo = token_shift(lo_in, 3, mlo_ref[...])
    prev_ref[0:1, :] = r_in[ch - 1:ch, :]
    prev_ref[1:2, :] = k_in[ch - 1:ch, :]
    prev_ref[2:3, :] = v_in[ch - 1:ch, :]
    prev_ref[3:4, 0:LANES] = lo_in[ch - 1:ch, :]

    z_w = w0_ref[...] + _dot(jnp.tanh(lo), wup_ref[...])
    logw = -DECAY_SCALE * jax.nn.sigmoid(z_w)
    a = jax.nn.sigmoid(a0_ref[...] + _dot(lo, aup_ref[...]))

    seg2 = seg_ref[...]

    def head_sum(x):
        tiles = []
        for t in range(npair):
            hi, lo_ = _split(x[:, t * LANES:(t + 1) * LANES])
            tiles.append(_dg(jnp.concatenate([hi, lo_], axis=1), seg2, NN))
        return jnp.concatenate(tiles, axis=1)

    kk = k * kk_ref[...]
    kk = kk * lax.rsqrt(jnp.maximum(head_sum(kk * kk), 1e-24))
    k2 = k * (1.0 + (a - 1.0) * ka_ref[...])
    ab = kk * a

    l_hi = logw.astype(BF16)
    rem = logw - l_hi.astype(F32)
    l_mid = rem.astype(BF16)
    l_lo = (rem - l_mid.astype(F32)).astype(BF16)
    cum = _dg(tri_ref[...], jnp.concatenate([l_hi, l_mid, l_lo], axis=0), NN)
    cum_last = cum[ch - 1:ch, :]
    e_cum = jnp.exp(cum)
    e_ncum = jnp.exp(-cum)
    e_tail = jnp.exp(cum_last - cum)
    rt = r * e_cum
    kt = k2 * e_ncum
    bt = ab * e_ncum
    at = -kk * jnp.exp(cum - logw)
    khat = k2 * e_tail
    bhat = ab * e_tail
    p_last = e_cum[ch - 1:ch, :]

    lane = lax.broadcasted_iota(jnp.int32, (ch, LANES), 1)
    head0 = lane < HEAD_DIM
    strict = strict_ref[...] > 0.5
    incl = incl_ref[...] > 0.5
    eye = eye_ref[...]

    def stack(x):
        return jnp.concatenate([jnp.where(head0, x, 0.0), jnp.where(head0, 0.0, x)], axis=0)

    def mm(x, w, dn=NN):
        return _dg(x.astype(BF16), w.astype(BF16), dn)

    pairs = range(npair)
    rows = 2 * ch
    sl = [slice(p * LANES, (p + 1) * LANES) for p in pairs]
    at2, rt2, bt2, kt2, v2, khat2, bhat2 = ([stack(t[:, sl[p]]) for p in pairs]
                                            for t in (at, rt, bt, kt, v, khat, bhat))
    gram = []
    for p in pairs:
        lh, ll = _split(jnp.concatenate([at2[p], rt2[p]], axis=0))
        rh, rl = _split(jnp.concatenate([bt2[p], kt2[p]], axis=0))
        gram.append(_dg(jnp.concatenate([lh, lh, ll], axis=1), jnp.concatenate([rh, rl, rh], axis=1), NT))
    a_ab = [jnp.where(strict, g[:rows, :rows], 0.0) for g in gram]
    a_ak = [jnp.where(strict, g[:rows, rows:], 0.0) for g in gram]
    a_rb = [jnp.where(incl, g[rows:, :rows], 0.0) for g in gram]
    a_rk = [jnp.where(incl, g[rows:, rows:], 0.0) for g in gram]
    akv = [mm(a_ak[p], v2[p]) for p in pairs]

    tinv = [eye + a for a in a_ab]
    pw = a_ab
    for _ in range(int(math.log2(ch)) - 1):
        pw = [mm(x, x) for x in pw]
        tinv = [tinv[p] + mm(tinv[p], pw[p]) for p in pairs]

    s = [s_ref[p] for p in pairs]
    tw = [mm(tinv[p], jnp.concatenate([at2[p], akv[p]], axis=1)) for p in pairs]
    ws = [mm(jnp.concatenate([tw[p][:, :LANES], rt2[p]], axis=0), s[p], NT) for p in pairs]
    uv = [jnp.concatenate([ws[p][:rows] + tw[p][:, LANES:], v2[p]], axis=0) for p in pairs]
    y2 = [ws[p][rows:] + mm(jnp.concatenate([a_rb[p], a_rk[p]], axis=1), uv[p]) for p in pairs]
    for p in pairs:
        s_ref[p] = s[p] * p_last[:, sl[p]] + mm(uv[p], jnp.concatenate([bhat2[p], khat2[p]], axis=0), TN)
    y = jnp.concatenate([t[:ch, :] + t[ch:, :] for t in y2], axis=1)

    inv_n = 1.0 / HEAD_DIM
    mu = head_sum(y) * inv_n
    d = y - mu
    var = head_sum(d * d) * inv_n
    yn = d * lax.rsqrt(var + GN_EPS) * gnw_ref[...] + gnb_ref[...]
    y_ref[...] = (yn + head_sum(r * k2 * rk_ref[...]) * v).astype(y_ref.dtype)


def rwkv7(p, batch, seqlen, shift_mix, w_up, w0, a_up, a0, k_k, k_a, r_k, gn_w, gn_b):
    width = w0.shape[0]
    npair = width // LANES
    ch = RWKV_CHUNK
    nchunk = seqlen // ch
    assert seqlen % ch == 0 and 2 * LORA_RANK == LANES and 2 * HEAD_DIM == LANES
    zeros = jnp.zeros((LORA_RANK, width), F32)
    wup_pad = jnp.concatenate([w_up, zeros], axis=0)
    aup_pad = jnp.concatenate([zeros, a_up], axis=0)
    hb = RWKV_PAIRS
    assert npair % hb == 0
    idx = jnp.arange(LANES)
    seg = (idx[:, None] // HEAD_DIM == idx[None, :] // HEAD_DIM).astype(BF16)
    seg = jnp.concatenate([seg, seg], axis=0)
    t = jnp.arange(ch)
    tri = (t[None, :] <= t[:, None]).astype(BF16)
    tri = jnp.concatenate([tri, tri, tri], axis=1)
    i2 = jnp.arange(2 * ch)
    same = (i2[:, None] // ch) == (i2[None, :] // ch)
    strict = (same & ((i2[None, :] % ch) < (i2[:, None] % ch))).astype(F32)
    incl = (same & ((i2[None, :] % ch) <= (i2[:, None] % ch))).astype(F32)
    eye = jnp.eye(2 * ch, dtype=F32)
    row2 = lambda x: x.reshape(1, -1)

    wd = hb * LANES
    ngrp = npair // hb

    def tok(off):
        return pl.BlockSpec((ch, wd), lambda b, h, c: (b * nchunk + c, off + h))

    def par(off):
        return pl.BlockSpec((1, wd), lambda b, h, c: (0, off + h))

    def const(shape):
        return pl.BlockSpec(shape, lambda b, h, c: (0, 0))

    up = pl.BlockSpec((LANES, wd), lambda b, h, c: (0, h))
    lora_tok = pl.BlockSpec((ch, LANES), lambda b, h, c: (b * nchunk + c, 3 * npair))
    lora_par = pl.BlockSpec((1, LANES), lambda b, h, c: (0, 3 * npair))
    return pl.pallas_call(
        _rwkv_kernel,
        grid=(batch, ngrp, nchunk),
        in_specs=[tok(0), tok(ngrp), tok(2 * ngrp), lora_tok,
                  par(0), par(ngrp), par(2 * ngrp), lora_par, up, up,
                  par(0), par(0), par(0), par(0), par(0), par(0), par(0),
                  const((2 * LANES, LANES)), const((ch, 3 * ch)), const((2 * ch, 2 * ch)),
                  const((2 * ch, 2 * ch)), const((2 * ch, 2 * ch))],
        out_specs=pl.BlockSpec((ch, wd), lambda b, h, c: (b * nchunk + c, h)),
        out_shape=jax.ShapeDtypeStruct((batch * seqlen, width), BF16),
        scratch_shapes=[pltpu.VMEM((hb, LANES, LANES), F32), pltpu.VMEM((SUBLANES, wd), F32)],
        compiler_params=_cparams(("parallel", "parallel", "arbitrary")),
        name="rwkv7",
    )(p, p, p, p, row2(shift_mix), row2(shift_mix), row2(shift_mix), row2(shift_mix), wup_pad, aup_pad,
      row2(w0), row2(a0), row2(k_k), row2(k_a), row2(r_k), row2(gn_w), row2(gn_b),
      seg, tri, strict, incl, eye)


def _sb_kernel(q_ref, k_ref, v_ref, uo_ref, o_ref, k2_ref, v2_ref, acc_ref, carry_ref, lb_ref, sums_ref):
    qi = pl.program_id(2)
    kb, qb = SB_BLOCK, SB_QBLOCK
    nsub = qb // kb
    npair = q_ref.shape[1] // LANES
    pairs_ = range(npair)

    @pl.when(qi == 0)
    def _():
        lane = lax.broadcasted_iota(jnp.int32, (kb, LANES), 1)
        head0 = lane < HEAD_DIM

        def fill(i, c):
            start = pl.multiple_of(i * kb, kb)
            for p in pairs_:
                for src, dst in ((k_ref, k2_ref), (v_ref, v2_ref)):
                    t = src[pl.ds(start, kb), p * LANES:(p + 1) * LANES].astype(F32)
                    dst[p, i, 0:kb, :] = jnp.where(head0, t, 0.0).astype(BF16)
                    dst[p, i, kb:2 * kb, :] = jnp.where(head0, 0.0, t).astype(BF16)
            return c

        lax.fori_loop(0, k_ref.shape[0] // kb, fill, 0)

    q = [q_ref[:, p * LANES:(p + 1) * LANES] for p in pairs_]
    uo = uo_ref[...]
    acc_ref[...] = jnp.zeros_like(acc_ref)
    carry_ref[...] = jnp.zeros_like(carry_ref)
    tpos = lax.broadcasted_iota(jnp.int32, (qb, kb), 0)
    spos = lax.broadcasted_iota(jnp.int32, (qb, kb), 1)

    def logits(p, sb):
        base = pl.multiple_of(sb * nsub, nsub)
        keys = k2_ref[p, pl.ds(base, nsub)].reshape(nsub * 2 * kb, LANES)
        return _dg(q[p], keys, NT)

    def scores(p, z_all, slot, diagonal):
        for j in range(nsub):
            for h in range(2):
                z = z_all[:, (2 * j + h) * kb:(2 * j + h + 1) * kb]
                log_keep = -(jnp.maximum(z, 0.0) + jnp.log(1.0 + jnp.exp(-jnp.abs(z))))
                log_beta = log_keep + z
                if diagonal:
                    causal = (spos + j * kb) < tpos
                    log_keep = jnp.where(causal, log_keep, 0.0)
                    log_beta = jnp.where(causal, log_beta, -jnp.inf)
                hi = log_keep.astype(BF16)
                lo = (log_keep - hi.astype(F32)).astype(BF16)
                lb_ref[slot, p, 2 * j + h] = log_beta
                sums_ref[slot, p, 2 * j + h] = _dg(jnp.concatenate([hi, lo], axis=1), uo, NN)

    def accumulate(p, sb, slot):
        base = pl.multiple_of(sb * nsub, nsub)
        attn = {}
        for h in range(2):
            carry = carry_ref[p, h]
            for j in reversed(range(nsub)):
                s = sums_ref[slot, p, 2 * j + h]
                attn[j, h] = jnp.exp(lb_ref[slot, p, 2 * j + h] + carry + s[:, :kb]).astype(BF16)
                carry = carry + s[:, kb:]
            carry_ref[p, h] = carry
        weights = jnp.concatenate([attn[j, h] for j in range(nsub) for h in range(2)], axis=1)
        values = v2_ref[p, pl.ds(base, nsub)].reshape(nsub * 2 * kb, LANES)
        acc_ref[p] += _dg(weights, values, NN)

    zd = [logits(p, qi) for p in pairs_]
    for p in pairs_:
        scores(p, zd[p], 0, True)
    trips = qi // 2

    def body(i, c):
        sb = qi - 1 - 2 * i
        za = [logits(p, sb) for p in pairs_]
        zb = [logits(p, sb - 1) for p in pairs_]
        for p in pairs_:
            accumulate(p, sb + 1, 0)
        for p in pairs_:
            scores(p, za[p], 1, False)
        for p in pairs_:
            scores(p, zb[p], 0, False)
        for p in pairs_:
            accumulate(p, sb, 1)
        return c

    lax.fori_loop(0, trips, body, 0)
    odd = qi - 2 * trips == 1

    @pl.when(odd)
    def _():
        z = [logits(p, 0) for p in pairs_]
        for p in pairs_:
            accumulate(p, 1, 0)
        for p in pairs_:
            scores(p, z[p], 1, False)
        for p in pairs_:
            accumulate(p, 0, 1)

    @pl.when(jnp.logical_not(odd))
    def _():
        for p in pairs_:
            accumulate(p, 0, 0)

    for p in pairs_:
        o_ref[:, p * LANES:(p + 1) * LANES] = acc_ref[p].astype(o_ref.dtype)


def stick_breaking(p, batch, seqlen, width):
    npair = width // LANES
    kb, qb = SB_BLOCK, SB_QBLOCK
    nq = seqlen // qb
    hp = SB_PAIRS
    ngrp = npair // hp
    wd = hp * LANES
    nsl = 2 * qb // kb
    j = jnp.arange(kb)
    later = (j[:, None] > j[None, :]).astype(BF16)
    uo = jnp.concatenate([later, jnp.ones((kb, kb), BF16)], axis=1)
    uo = jnp.concatenate([uo, uo], axis=0)
    return pl.pallas_call(
        _sb_kernel,
        grid=(batch, ngrp, nq),
        in_specs=[pl.BlockSpec((qb, wd), lambda b, h, i: (b * nq + i, h)),
                  pl.BlockSpec((seqlen, wd), lambda b, h, i: (b, ngrp + h)),
                  pl.BlockSpec((seqlen, wd), lambda b, h, i: (b, 2 * ngrp + h)),
                  pl.BlockSpec((2 * kb, 2 * kb), lambda b, h, i: (0, 0))],
        out_specs=pl.BlockSpec((qb, wd), lambda b, h, i: (b * nq + i, h)),
        out_shape=jax.ShapeDtypeStruct((batch * seqlen, width), BF16),
        scratch_shapes=[pltpu.VMEM((hp, seqlen // kb, 2 * kb, LANES), BF16),
                        pltpu.VMEM((hp, seqlen // kb, 2 * kb, LANES), BF16),
                        pltpu.VMEM((hp, qb, LANES), F32), pltpu.VMEM((hp, 2, qb, kb), F32),
                        pltpu.VMEM((2, hp, nsl, qb, kb), F32), pltpu.VMEM((2, hp, nsl, qb, 2 * kb), F32)],
        compiler_params=_cparams(("parallel", "parallel", "arbitrary")),
        name="stick_breaking",
    )(p, p, p, uo)


def _gate_out_kernel(ya_ref, yb_ref, g_ref, x_ref, w_ref, ng_ref, h_ref, hn_ref):
    gate = g_ref[...].astype(F32)
    y = jnp.concatenate([ya_ref[...], yb_ref[...]], axis=-1).astype(F32) * (gate * jax.nn.sigmoid(gate))
    h = x_ref[...] + jnp.dot(y.astype(BF16), w_ref[...], preferred_element_type=F32)
    h_ref[...] = h
    ms = jnp.mean(h * h, axis=-1, keepdims=True)
    hn_ref[...] = ((h * lax.rsqrt(ms + RMS_EPS)) * ng_ref[...]).astype(hn_ref.dtype)


def gate_out(ya, yb, gate, x, w, next_g, tm=512):
    m, d = x.shape
    half = ya.shape[1]
    tm = min(tm, m)
    row = lambda n: pl.BlockSpec((tm, n), lambda i: (i, 0))
    return pl.pallas_call(
        _gate_out_kernel,
        grid=(m // tm,),
        in_specs=[row(half), row(half), row(d), row(d),
                  pl.BlockSpec((d, d), lambda i: (0, 0), pipeline_mode=pl.Buffered(1)),
                  pl.BlockSpec((1, d), lambda i: (0, 0))],
        out_specs=[row(d), row(d)],
        out_shape=[jax.ShapeDtypeStruct((m, d), F32), jax.ShapeDtypeStruct((m, d), BF16)],
        compiler_params=_cparams(("parallel",)),
        name="gate_out",
    )(ya, yb, gate, x, w, next_g.reshape(1, d))


def _s5_prep_kernel(lre_ref, lim_ref, ldt_ref, cre_ref, cim_ref, btre_ref, btim_ref,
                    kt_ref, wre_ref, wim_ref, vre_ref, vim_ref, qre_ref, qim_ref, ckre_ref, ckim_ref):
    blk = S5_BLOCK
    h = S5_GROUP
    lre, lim = lre_ref[...], lim_ref[...]
    dt = jnp.exp(ldt_ref[...])
    mag = jnp.exp(lre * dt)
    bre = mag * jnp.cos(lim * dt)
    bim = mag * jnp.sin(lim * dt)
    den = lre * lre + lim * lim
    nre, nim = bre - 1.0, bim
    fre = (nre * lre + nim * lim) / den
    fim = (nim * lre - nre * lim) / den
    cre, cim = cre_ref[...], cim_ref[...]
    ckr, cki = cre * fre - cim * fim, cre * fim + cim * fre
    btre, btim = btre_ref[...], btim_ref[...]
    pr, pi = jnp.ones_like(bre), jnp.zeros_like(bre)
    powers = []
    for tau in range(blk + 1):
        ckre_ref[:, tau * h:(tau + 1) * h, :] = ckr
        ckim_ref[:, tau * h:(tau + 1) * h, :] = cki
        powers.append((pr, pi))
        ckr, cki = ckr * bre - cki * bim, ckr * bim + cki * bre
        pr, pi = pr * bre - pi * bim, pr * bim + pi * bre
    for i in range(blk):
        pr, pi = powers[blk - 1 - i]
        wre_ref[:, i * h:(i + 1) * h, :] = pr * btre - pi * btim
        wim_ref[:, i * h:(i + 1) * h, :] = pr * btim + pi * btre
    vre_ref[...] = ckre_ref[:, h:(blk + 1) * h, :]
    vim_ref[...] = -ckim_ref[:, h:(blk + 1) * h, :]
    qr, qi = powers[blk]
    pr, pi = qr, qi
    for r in range(SUBLANES):
        qre_ref[:, r:r + 1, :] = pr
        qim_ref[:, r:r + 1, :] = pi
        pr, pi = pr * qr - pi * qi, pr * qi + pi * qr
    for g in range(kt_ref.shape[0]):
        kt_ref[g] = (_dot(btre[g], ckre_ref[g, 0:blk * h, :], NT) - _dot(btim[g], ckim_ref[g, 0:blk * h, :], NT))


def _s5_kernel(nbatch, p_ref, kt_ref, wre_ref, wim_ref, vt_ref, cre_ref, cim_ref, d_ref, o_ref,
               u_ref, gre_ref, gim_ref, xre_ref, xim_ref, yp_ref, stage_ref):
    npair = u_ref.shape[0] // 2
    rows = u_ref.shape[1]
    wlane = u_ref.shape[2]
    blk = S5_BLOCK
    ngrp = 2 * npair
    rc = S5_RELAYOUT_ROWS
    lane_chunk = lax.broadcasted_iota(jnp.int32, (rc, LANES), 1) // S5_GROUP

    def chunk_transpose(arrs):
        arrs = list(arrs)
        s = ngrp // 2
        while s:
            upper = (lane_chunk & s) != 0
            for x in range(ngrp):
                if x & s == 0:
                    ax, ay = arrs[x], arrs[x + s]
                    arrs[x] = jnp.where(upper, pltpu.roll(ay, s * S5_GROUP, 1), ax)
                    arrs[x + s] = jnp.where(upper, ay, pltpu.roll(ax, LANES - s * S5_GROUP, 1))
            s //= 2
        return arrs

    def relayout_in(c, carry):
        t0 = pl.multiple_of(c * (rc * blk), rc * blk)
        r0 = pl.multiple_of(c * rc, rc)
        stage_ref[...] = p_ref[pl.ds(t0, rc * blk), :].astype(F32)
        for half in range(blk // ngrp):
            z = chunk_transpose(stage_ref[pl.ds(ngrp * half + i, rc, stride=blk), :] for i in range(ngrp))
            for g in range(ngrp):
                u_ref[g, pl.ds(r0, rc), half * LANES:(half + 1) * LANES] = z[g].astype(BF16)
        return carry

    lax.fori_loop(0, rows // rc, relayout_in, 0)
    lane = lax.broadcasted_iota(jnp.int32, (S5_GROUP, wlane), 1)

    for g in range(2 * npair):
        kt = kt_ref[g]
        blocks = [kt] + [jnp.where(lane >= i * S5_GROUP, pltpu.roll(kt, i * S5_GROUP, 1), 0.0)
                         for i in range(1, S5_BLOCK)]
        kmat = jnp.concatenate(blocks, axis=0).astype(BF16)
        yp_ref[g] = _dg(u_ref[g], kmat, NN)
    for p in range(npair):
        ucat = jnp.concatenate([u_ref[2 * p], u_ref[2 * p + 1]], axis=1)
        gre_ref[p] = _dg(ucat, wre_ref[p], NN)
        gim_ref[p] = _dg(ucat, wim_ref[p], NN)

    def cma(xr, xi, ar, ai, sr, si):
        return xr + (ar * sr - ai * si), xi + (ar * si + ai * sr)

    row = lax.broadcasted_iota(jnp.int32, (SUBLANES, LANES), 0)
    per_batch = rows // nbatch

    def tile(t, carry, base):
        r0 = pl.multiple_of(base + t * SUBLANES, SUBLANES)
        new_carry = []
        for p in range(npair):
            xr = gre_ref[p, pl.ds(r0, SUBLANES), :]
            xi = gim_ref[p, pl.ds(r0, SUBLANES), :]
            for lvl, sh in enumerate((1, 2, 4)):
                ar = cre_ref[p, lvl * SUBLANES:(lvl + 1) * SUBLANES, :]
                ai = cim_ref[p, lvl * SUBLANES:(lvl + 1) * SUBLANES, :]
                xr, xi = cma(xr, xi, ar, ai, pltpu.roll(xr, sh, 0), pltpu.roll(xi, sh, 0))
            ar = cre_ref[p, 3 * SUBLANES:4 * SUBLANES, :]
            ai = cim_ref[p, 3 * SUBLANES:4 * SUBLANES, :]
            cr, ci = carry[2 * p], carry[2 * p + 1]
            xr, xi = cma(xr, xi, ar, ai, cr, ci)
            xre_ref[p, pl.ds(r0, SUBLANES), :] = jnp.where(row == 0, cr, pltpu.roll(xr, 1, 0))
            xim_ref[p, pl.ds(r0, SUBLANES), :] = jnp.where(row == 0, ci, pltpu.roll(xi, 1, 0))
            new_carry.append(jnp.broadcast_to(xr[SUBLANES - 1:SUBLANES, :], (SUBLANES, LANES)))
            new_carry.append(jnp.broadcast_to(xi[SUBLANES - 1:SUBLANES, :], (SUBLANES, LANES)))
        return tuple(new_carry)

    zero = jnp.zeros((SUBLANES, LANES), F32)
    for b in range(nbatch):
        lax.fori_loop(0, per_batch // SUBLANES, functools.partial(tile, base=b * per_batch), (zero,) * (2 * npair))

    for p in range(npair):
        xs = jnp.concatenate([xre_ref[p], xim_ref[p]], axis=1).astype(BF16)
        corr = _dg(xs, vt_ref[p], NT)
        for k in range(2):
            g = 2 * p + k
            y = yp_ref[g] + corr[:, k * wlane:(k + 1) * wlane] + d_ref[g] * u_ref[g].astype(F32)
            yp_ref[g] = jax.nn.gelu(y)

    def relayout_out(c, carry):
        t0 = pl.multiple_of(c * (rc * blk), rc * blk)
        r0 = pl.multiple_of(c * rc, rc)
        for half in range(blk // ngrp):
            yt = chunk_transpose(yp_ref[g, pl.ds(r0, rc), half * LANES:(half + 1) * LANES] for g in range(ngrp))
            for j in range(ngrp):
                stage_ref[pl.ds(ngrp * half + j, rc, stride=blk), :] = yt[j]
        o_ref[pl.ds(t0, rc * blk), :] = stage_ref[...].astype(o_ref.dtype)
        return carry

    lax.fori_loop(0, rows // rc, relayout_out, 0)


def s5_ssm(p, batch, seqlen, lam_re, lam_im, log_dt, b_re, b_im, c_re, c_im, d_skip):
    ngroup, nstate = lam_re.shape
    h, blk = S5_GROUP, S5_BLOCK
    width = ngroup * h
    wl = blk * h
    nb = batch * seqlen // blk
    gp = S5_PREP_GROUPS
    assert nstate == S5_STATE and 2 * nstate == LANES
    assert seqlen % (blk * SUBLANES) == 0 and ngroup % gp == 0 and ngroup % (2 * S5_PAIRS) == 0

    g3 = lambda x: x.reshape(ngroup, 1, nstate)
    bt = lambda x: jnp.swapaxes(x, 1, 2)
    spec1 = pl.BlockSpec((gp, 1, nstate), lambda i: (i, 0, 0))
    spec_c = pl.BlockSpec((gp, h, nstate), lambda i: (i, 0, 0))
    spec_w = pl.BlockSpec((gp, wl, nstate), lambda i: (i, 0, 0))
    spec_q = pl.BlockSpec((gp, SUBLANES, nstate), lambda i: (i, 0, 0))
    spec_k = pl.BlockSpec((gp, h, wl), lambda i: (i, 0, 0))
    f = lambda *s: jax.ShapeDtypeStruct(s, F32)
    kt, w_re, w_im, v_re, v_im, q_re, q_im = pl.pallas_call(
        _s5_prep_kernel,
        grid=(ngroup // gp,),
        in_specs=[spec1, spec1, spec1, spec_c, spec_c, spec_c, spec_c],
        out_specs=[spec_k, spec_w, spec_w, spec_w, spec_w, spec_q, spec_q],
        out_shape=[f(ngroup, h, wl), f(ngroup, wl, nstate), f(ngroup, wl, nstate), f(ngroup, wl, nstate),
                   f(ngroup, wl, nstate), f(ngroup, SUBLANES, nstate), f(ngroup, SUBLANES, nstate)],
        scratch_shapes=[pltpu.VMEM((gp, (blk + 1) * h, nstate), F32), pltpu.VMEM((gp, (blk + 1) * h, nstate), F32)],
        compiler_params=_cparams(("parallel",)),
        name="s5_prep",
    )(g3(lam_re), g3(lam_im), jnp.broadcast_to(log_dt[:, None, None], (ngroup, 1, nstate)), c_re, c_im,
      bt(b_re), bt(b_im))

    npairs = ngroup // 2

    def pair_rows(x):
        x = x.reshape(npairs, 2, x.shape[1], nstate)
        z = jnp.zeros_like(x[:, 0])
        return jnp.concatenate([jnp.concatenate([x[:, 0], z], axis=2), jnp.concatenate([z, x[:, 1]], axis=2)], axis=1)

    w_re2 = pair_rows(w_re).astype(BF16)
    w_im2 = pair_rows(w_im).astype(BF16)
    vt2 = jnp.concatenate([pair_rows(v_re), pair_rows(v_im)], axis=2).astype(BF16)

    def pair_lanes(x):
        x = x.reshape(npairs, 2, SUBLANES, nstate)
        return jnp.concatenate([x[:, 0], x[:, 1]], axis=2)

    rows = jnp.arange(SUBLANES)[None, :, None]

    def scan_consts(q):
        q = pair_lanes(q)
        levels = [jnp.where(rows >= sh, q[:, sh - 1:sh, :], 0.0) for sh in (1, 2, 4)]
        return jnp.concatenate(levels + [q], axis=1)

    c_re2, c_im2 = scan_consts(q_re), scan_consts(q_im)
    d_row = jnp.tile(d_skip.reshape(ngroup, 1, h), (1, 1, blk))

    np_ = S5_PAIRS
    gs = 2 * np_
    assert gs * h == LANES and blk % gs == 0 and nb % S5_RELAYOUT_ROWS == 0
    tok = pl.BlockSpec((nb * blk, LANES), lambda i: (0, i))
    grp = lambda r, c: pl.BlockSpec((gs, r, c), lambda i: (i, 0, 0))
    par = lambda r, c: pl.BlockSpec((np_, r, c), lambda i: (i, 0, 0))
    return pl.pallas_call(
        functools.partial(_s5_kernel, batch),
        grid=(ngroup // gs,),
        in_specs=[tok, grp(h, wl), par(2 * wl, LANES), par(2 * wl, LANES), par(2 * wl, 2 * LANES),
                  par(4 * SUBLANES, LANES), par(4 * SUBLANES, LANES), grp(1, wl)],
        out_specs=tok,
        out_shape=jax.ShapeDtypeStruct((batch * seqlen, width), BF16),
        scratch_shapes=[pltpu.VMEM((gs, nb, wl), BF16)] + [pltpu.VMEM((np_, nb, LANES), F32)] * 4
        + [pltpu.VMEM((gs, nb, wl), F32), pltpu.VMEM((S5_RELAYOUT_ROWS * blk, LANES), F32)],
        compiler_params=_cparams(("parallel",)),
        name="s5_ssm",
    )(p, kt, w_re2, w_im2, vt2, c_re2, c_im2, d_row)


def _glu_kernel(y_ref, yj_ref, gj_ref, w_ref, b_ref, o_ref):
    z = jnp.dot(y_ref[...], w_ref[...], preferred_element_type=F32) + b_ref[...]
    gate = gj_ref[...].astype(F32)
    yj = yj_ref[...].astype(F32)
    o_ref[...] = (yj * jax.nn.sigmoid(z) * (gate * jax.nn.sigmoid(gate))).astype(o_ref.dtype)


def glu_gate(y, p, w, b, tm=1024, tn=1024):
    m, d = y.shape
    tm = min(tm, m)
    goff = d // tn
    return pl.pallas_call(
        _glu_kernel,
        grid=(m // tm, d // tn),
        in_specs=[pl.BlockSpec((tm, d), lambda i, j: (i, 0)),
                  pl.BlockSpec((tm, tn), lambda i, j: (i, j)),
                  pl.BlockSpec((tm, tn), lambda i, j: (i, goff + j)),
                  pl.BlockSpec((d, tn), lambda i, j: (0, j)),
                  pl.BlockSpec((1, tn), lambda i, j: (0, j))],
        out_specs=pl.BlockSpec((tm, tn), lambda i, j: (i, j)),
        out_shape=jax.ShapeDtypeStruct((m, d), BF16),
        compiler_params=_cparams(("parallel", "arbitrary")),
        name="glu_gate",
    )(y, y, p, w, b.reshape(1, d))


def _final_kernel(a_ref, h_ref, w_ref, g_ref, o_ref):
    h = h_ref[...] + jnp.dot(a_ref[...], w_ref[...], preferred_element_type=F32)
    ms = jnp.mean(h * h, axis=-1, keepdims=True)
    o_ref[...] = (h * lax.rsqrt(ms + RMS_EPS)) * g_ref[...]


def final_out(a, h, w, g, tm=512):
    m, d = h.shape
    tm = min(tm, m)
    row = pl.BlockSpec((tm, d), lambda i: (i, 0))
    return pl.pallas_call(
        _final_kernel,
        grid=(m // tm,),
        in_specs=[row, row, pl.BlockSpec((d, d), lambda i: (0, 0), pipeline_mode=pl.Buffered(1)),
                  pl.BlockSpec((1, d), lambda i: (0, 0))],
        out_specs=row,
        out_shape=jax.ShapeDtypeStruct((m, d), F32),
        compiler_params=_cparams(("parallel",)),
        name="final_out",
    )(a, h, w, g.reshape(1, d))


def kernel(x, norm_g, final_g, ab_w_in, rwkv_shift_mix, rwkv_w_up, rwkv_w0, rwkv_a_up, rwkv_a0, rwkv_k_k, rwkv_k_a, rwkv_r_k, rwkv_gn_w, rwkv_gn_b, ab_w_out, s5_w_in, s5_lam_re, s5_lam_im, s5_log_dt, s5_b_re, s5_b_im, s5_c_re, s5_c_im, s5_d, s5_w_glu, s5_b_glu, s5_w_out):
    batch, seqlen, d = x.shape
    m = batch * seqlen
    rwkv_w = rwkv_w0.shape[1]
    rwkv_proj = 3 * rwkv_w + 2 * LORA_RANK
    sb_w = (ab_w_in.shape[2] - rwkv_proj - d) // 3
    x2 = x.reshape(m, d)

    w_in = ab_w_in[0]
    scale = HEAD_DIM ** -0.5
    col_scale = jnp.concatenate([jnp.full((sb_w,), scale, F32), jnp.ones((2 * sb_w,), F32)])
    w_rwkv = w_in[:, :rwkv_proj].astype(BF16)
    w_sb = (w_in[:, rwkv_proj:rwkv_proj + 3 * sb_w] * col_scale).astype(BF16)
    w_gate = w_in[:, rwkv_proj + 3 * sb_w:].astype(BF16)
    xn = rmsnorm(x2, norm_g[0], BF16)
    p_rwkv = matmul(xn, w_rwkv, F32, tn=640, name="proj_rwkv")
    p_sb = matmul(xn, w_sb, BF16, name="proj_sb")
    gate0 = matmul(xn, w_gate, BF16, name="proj_gate")
    y_a = rwkv7(p_rwkv, batch, seqlen, rwkv_shift_mix[0], rwkv_w_up[0], rwkv_w0[0], rwkv_a_up[0], rwkv_a0[0],
                rwkv_k_k[0], rwkv_k_a[0], rwkv_r_k[0], rwkv_gn_w[0], rwkv_gn_b[0])
    y_b = stick_breaking(p_sb, batch, seqlen, sb_w)
    h1, hn1 = gate_out(y_a, y_b, gate0, x2, ab_w_out[0].astype(BF16), norm_g[1])

    p1 = matmul(hn1, s5_w_in[0].astype(BF16), BF16, name="proj_s5")
    y_s5 = s5_ssm(p1, batch, seqlen, s5_lam_re[0], s5_lam_im[0], s5_log_dt[0], s5_b_re[0], s5_b_im[0],
                  s5_c_re[0], s5_c_im[0], s5_d[0])
    act = glu_gate(y_s5, p1, s5_w_glu[0].astype(BF16), s5_b_glu[0])
    out = final_out(act, h1, s5_w_out[0].astype(BF16), final_g)
    return out.reshape(batch, seqlen, d)
```

```python
import functools
import math

import jax
import jax.numpy as jnp
from jax import lax
from jax.experimental import pallas as pl
from jax.experimental.pallas import tpu as pltpu

F32 = jnp.float32
BF16 = jnp.bfloat16

HEAD_DIM = 64
LANES = 128
SUBLANES = 8
LORA_RANK = 64
S5_GROUP = 16
S5_STATE = 64
RMS_EPS = 1e-6
GN_EPS = 64e-5
DECAY_SCALE = math.exp(-0.5)

RWKV_CHUNK = 64
RWKV_PAIRS = 8
SB_BLOCK = 128
SB_QBLOCK = 256
SB_PAIRS = 4
S5_BLOCK = 16
S5_PAIRS = 4
S5_PREP_GROUPS = 16
S5_RELAYOUT_ROWS = 64
VMEM_LIMIT = 56 * 1024 * 1024

NN = (((1,), (0,)), ((), ()))
NT = (((1,), (1,)), ((), ()))
TN = (((0,), (0,)), ((), ()))


def _cparams(sem):
    return pltpu.CompilerParams(dimension_semantics=sem, vmem_limit_bytes=VMEM_LIMIT)


def _split(x):
    hi = x.astype(BF16)
    lo = (x - hi.astype(F32)).astype(BF16)
    return hi, lo


def _dg(a, b, dn):
    return lax.dot_general(a, b, dn, preferred_element_type=F32)


def _dot(a, b, dn=NN, passes=3):
    if passes == 1:
        return _dg(a.astype(BF16), b.astype(BF16), dn)
    ah, al = _split(a)
    bh, bl = _split(b)
    return _dg(ah, bh, dn) + (_dg(ah, bl, dn) + _dg(al, bh, dn))


def _dot_rhs_exact(a, b01, passes=3):
    ah = a.astype(BF16)
    out = _dg(ah, b01, NN)
    rem = a - ah.astype(F32)
    for _ in range(passes - 1):
        rh = rem.astype(BF16)
        out = out + _dg(rh, b01, NN)
        rem = rem - rh.astype(F32)
    return out


def _dot_lhs_exact(a01, b, passes=3):
    bh = b.astype(BF16)
    out = _dg(a01, bh, NN)
    rem = b - bh.astype(F32)
    for _ in range(passes - 1):
        rh = rem.astype(BF16)
        out = out + _dg(a01, rh, NN)
        rem = rem - rh.astype(F32)
    return out


def _rmsnorm_kernel(x_ref, g_ref, o_ref):
    x = x_ref[...]
    ms = jnp.mean(x * x, axis=-1, keepdims=True)
    o_ref[...] = ((x * lax.rsqrt(ms + RMS_EPS)) * g_ref[...]).astype(o_ref.dtype)


def rmsnorm(x, g, out_dtype, tm=512):
    m, d = x.shape
    tm = min(tm, m)
    return pl.pallas_call(
        _rmsnorm_kernel,
        grid=(m // tm,),
        in_specs=[pl.BlockSpec((tm, d), lambda i: (i, 0)), pl.BlockSpec((1, d), lambda i: (0, 0))],
        out_specs=pl.BlockSpec((tm, d), lambda i: (i, 0)),
        out_shape=jax.ShapeDtypeStruct((m, d), out_dtype),
        compiler_params=_cparams(("parallel",)),
        name="rmsnorm",
    )(x, g.reshape(1, d))


def _mm_kernel(a_ref, w_ref, o_ref):
    o_ref[...] = jnp.dot(a_ref[...], w_ref[...].astype(BF16), preferred_element_type=F32).astype(o_ref.dtype)


def _mm_scaled_kernel(a_ref, w_ref, s_ref, o_ref):
    acc = jnp.dot(a_ref[...], w_ref[...].astype(BF16), preferred_element_type=F32)
    o_ref[...] = (acc * s_ref[...]).astype(o_ref.dtype)


def matmul(a, w, out_dtype, col0, n, col_scale=None, tm=1024, tn=1024, name="matmul"):
    m, k = a.shape
    tm = min(tm, m)
    assert m % tm == 0 and n % tn == 0 and col0 % LANES == 0
    in_specs = [pl.BlockSpec((tm, k), lambda i, j: (i, 0)),
                pl.BlockSpec((pl.Element(k), pl.Element(tn)), lambda i, j: (0, pl.multiple_of(col0 + j * tn, LANES)))]
    args = [a, w]
    if col_scale is not None:
        in_specs.append(pl.BlockSpec((1, tn), lambda i, j: (0, j)))
        args.append(col_scale.reshape(1, n))
    return pl.pallas_call(
        _mm_kernel if col_scale is None else _mm_scaled_kernel,
        grid=(m // tm, n // tn),
        in_specs=in_specs,
        out_specs=pl.BlockSpec((tm, tn), lambda i, j: (i, j)),
        out_shape=jax.ShapeDtypeStruct((m, n), out_dtype),
        compiler_params=_cparams(("parallel", "arbitrary")),
        name=name,
    )(*args)


def _rwkv_kernel(r_ref, k_ref, v_ref, lo_ref, mr_ref, mk_ref, mv_ref, mlo_ref, wup_ref, aup_ref,
                 w0_ref, a0_ref, kk_ref, ka_ref, rk_ref, gnw_ref, gnb_ref,
                 seg_ref, tri_ref, strict_ref, incl_ref, eye_ref,
                 y_ref, s_ref, prev_ref):
    c = pl.program_id(2)
    ch = RWKV_CHUNK
    npair = r_ref.shape[1] // LANES

    @pl.when(c == 0)
    def _():
        s_ref[...] = jnp.zeros_like(s_ref)
        prev_ref[...] = jnp.zeros_like(prev_ref)

    def token_shift(x, idx, mix):
        row = lax.broadcasted_iota(jnp.int32, x.shape, 0)
        prev = prev_ref[idx:idx + 1, 0:x.shape[1]]
        shifted = jnp.where(row == 0, prev, pltpu.roll(x, 1, 0))
        return x + (shifted - x) * mix

    r_in, k_in, v_in, lo_in = r_ref[...], k_ref[...], v_ref[...], lo_ref[...]
    r = token_shift(r_in, 0, mr_ref[...])
    k = token_shift(k_in, 1, mk_ref[...])
    v = token_shift(v_in, 2, mv_ref[...])
    lo = token_shift(lo_in, 3, mlo_ref[...])
    prev_ref[0:1, :] = r_in[ch - 1:ch, :]
    prev_ref[1:2, :] = k_in[ch - 1:ch, :]
    prev_ref[2:3, :] = v_in[ch - 1:ch, :]
    prev_ref[3:4, 0:LANES] = lo_in[ch - 1:ch, :]

    z_w = w0_ref[...] + _dot(jnp.tanh(lo), wup_ref[...])
    logw = -DECAY_SCALE * jax.nn.sigmoid(z_w)
    a = jax.nn.sigmoid(a0_ref[...] + _dot(lo, aup_ref[...]))

    seg2 = seg_ref[...]

    def head_sum(x):
        tiles = []
        for t in range(npair):
            hi, lo_ = _split(x[:, t * LANES:(t + 1) * LANES])
            tiles.append(_dg(jnp.concatenate([hi, lo_], axis=1), seg2, NN))
        return jnp.concatenate(tiles, axis=1)

    kk = k * kk_ref[...]
    kk = kk * lax.rsqrt(jnp.maximum(head_sum(kk * kk), 1e-24))
    k2 = k * (1.0 + (a - 1.0) * ka_ref[...])
    ab = kk * a

    l_hi = logw.astype(BF16)
    rem = logw - l_hi.astype(F32)
    l_mid = rem.astype(BF16)
    l_lo = (rem - l_mid.astype(F32)).astype(BF16)
    cum = _dg(tri_ref[...], jnp.concatenate([l_hi, l_mid, l_lo], axis=0), NN)
    cum_last = cum[ch - 1:ch, :]
    e_cum = jnp.exp(cum)
    e_ncum = jnp.exp(-cum)
    e_tail = jnp.exp(cum_last - cum)
    rt = r * e_cum
    kt = k2 * e_ncum
    bt = ab * e_ncum
    at = -kk * jnp.exp(cum - logw)
    khat = k2 * e_tail
    bhat = ab * e_tail
    p_last = e_cum[ch - 1:ch, :]

    lane = lax.broadcasted_iota(jnp.int32, (ch, LANES), 1)
    head0 = lane < HEAD_DIM
    strict = strict_ref[...] > 0.5
    incl = incl_ref[...] > 0.5
    eye = eye_ref[...]

    def stack(x):
        return jnp.concatenate([jnp.where(head0, x, 0.0), jnp.where(head0, 0.0, x)], axis=0)

    def mm(x, w, dn=NN):
        return _dg(x.astype(BF16), w.astype(BF16), dn)

    pairs = range(npair)
    rows = 2 * ch
    sl = [slice(p * LANES, (p + 1) * LANES) for p in pairs]
    at2, rt2, bt2, kt2, v2, khat2, bhat2 = ([stack(t[:, sl[p]]) for p in pairs]
                                            for t in (at, rt, bt, kt, v, khat, bhat))
    gram = []
    for p in pairs:
        lh, ll = _split(jnp.concatenate([at2[p], rt2[p]], axis=0))
        rh, rl = _split(jnp.concatenate([bt2[p], kt2[p]], axis=0))
        gram.append(_dg(jnp.concatenate([lh, lh, ll], axis=1), jnp.concatenate([rh, rl, rh], axis=1), NT))
    a_ab = [jnp.where(strict, g[:rows, :rows], 0.0) for g in gram]
    a_ak = [jnp.where(strict, g[:rows, rows:], 0.0) for g in gram]
    a_rb = [jnp.where(incl, g[rows:, :rows], 0.0) for g in gram]
    a_rk = [jnp.where(incl, g[rows:, rows:], 0.0) for g in gram]
    akv = [mm(a_ak[p], v2[p]) for p in pairs]

    tinv = [eye + a for a in a_ab]
    pw = [mm(a, a) for a in a_ab]
    for _ in range(int(math.log2(ch)) - 2):
        both = [mm(pw[p], jnp.concatenate([pw[p], tinv[p]], axis=1)) for p in pairs]
        pw = [b[:, :rows] for b in both]
        tinv = [tinv[p] + both[p][:, rows:] for p in pairs]
    tinv = [tinv[p] + mm(pw[p], tinv[p]) for p in pairs]

    s = [s_ref[p] for p in pairs]
    tw = [mm(tinv[p], jnp.concatenate([at2[p], akv[p]], axis=1)) for p in pairs]
    ws = [mm(jnp.concatenate([tw[p][:, :LANES], rt2[p]], axis=0), s[p], NT) for p in pairs]
    uv = [jnp.concatenate([ws[p][:rows] + tw[p][:, LANES:], v2[p]], axis=0) for p in pairs]
    y2 = [ws[p][rows:] + mm(jnp.concatenate([a_rb[p], a_rk[p]], axis=1), uv[p]) for p in pairs]
    for p in pairs:
        s_ref[p] = s[p] * p_last[:, sl[p]] + mm(uv[p], jnp.concatenate([bhat2[p], khat2[p]], axis=0), TN)
    y = jnp.concatenate([t[:ch, :] + t[ch:, :] for t in y2], axis=1)

    inv_n = 1.0 / HEAD_DIM
    mu = head_sum(y) * inv_n
    d = y - mu
    var = head_sum(d * d) * inv_n
    yn = d * lax.rsqrt(var + GN_EPS) * gnw_ref[...] + gnb_ref[...]
    y_ref[...] = (yn + head_sum(r * k2 * rk_ref[...]) * v).astype(y_ref.dtype)


def rwkv7(p, p_lora, batch, seqlen, shift_mix, w_up, w0, a_up, a0, k_k, k_a, r_k, gn_w, gn_b):
    width = w0.shape[0]
    npair = width // LANES
    ch = RWKV_CHUNK
    nchunk = seqlen // ch
    assert seqlen % ch == 0 and 2 * LORA_RANK == LANES and 2 * HEAD_DIM == LANES
    zeros = jnp.zeros((LORA_RANK, width), F32)
    wup_pad = jnp.concatenate([w_up, zeros], axis=0)
    aup_pad = jnp.concatenate([zeros, a_up], axis=0)
    hb = RWKV_PAIRS
    assert npair % hb == 0
    idx = jnp.arange(LANES)
    seg = (idx[:, None] // HEAD_DIM == idx[None, :] // HEAD_DIM).astype(BF16)
    seg = jnp.concatenate([seg, seg], axis=0)
    t = jnp.arange(ch)
    tri = (t[None, :] <= t[:, None]).astype(BF16)
    tri = jnp.concatenate([tri, tri, tri], axis=1)
    i2 = jnp.arange(2 * ch)
    same = (i2[:, None] // ch) == (i2[None, :] // ch)
    strict = (same & ((i2[None, :] % ch) < (i2[:, None] % ch))).astype(F32)
    incl = (same & ((i2[None, :] % ch) <= (i2[:, None] % ch))).astype(F32)
    eye = jnp.eye(2 * ch, dtype=F32)
    row2 = lambda x: x.reshape(1, -1)

    wd = hb * LANES
    ngrp = npair // hb

    def tok(off):
        return pl.BlockSpec((ch, wd), lambda b, h, c: (b * nchunk + c, off + h))

    def par(off):
        return pl.BlockSpec((1, wd), lambda b, h, c: (0, off + h))

    def const(shape):
        return pl.BlockSpec(shape, lambda b, h, c: (0, 0))

    up = pl.BlockSpec((LANES, wd), lambda b, h, c: (0, h))
    lora_tok = pl.BlockSpec((ch, LANES), lambda b, h, c: (b * nchunk + c, 0))
    lora_par = pl.BlockSpec((1, LANES), lambda b, h, c: (0, 3 * npair))
    return pl.pallas_call(
        _rwkv_kernel,
        grid=(batch, ngrp, nchunk),
        in_specs=[tok(0), tok(ngrp), tok(2 * ngrp), lora_tok,
                  par(0), par(ngrp), par(2 * ngrp), lora_par, up, up,
                  par(0), par(0), par(0), par(0), par(0), par(0), par(0),
                  const((2 * LANES, LANES)), const((ch, 3 * ch)), const((2 * ch, 2 * ch)),
                  const((2 * ch, 2 * ch)), const((2 * ch, 2 * ch))],
        out_specs=pl.BlockSpec((ch, wd), lambda b, h, c: (b * nchunk + c, h)),
        out_shape=jax.ShapeDtypeStruct((batch * seqlen, width), BF16),
        scratch_shapes=[pltpu.VMEM((hb, LANES, LANES), F32), pltpu.VMEM((SUBLANES, wd), F32)],
        compiler_params=_cparams(("parallel", "parallel", "arbitrary")),
        name="rwkv7",
    )(p, p, p, p_lora, row2(shift_mix), row2(shift_mix), row2(shift_mix), row2(shift_mix), wup_pad, aup_pad,
      row2(w0), row2(a0), row2(k_k), row2(k_a), row2(r_k), row2(gn_w), row2(gn_b),
      seg, tri, strict, incl, eye)


def _sb_kernel(q_ref, k_ref, v_ref, uo_ref, o_ref, k2_ref, v2_ref, acc_ref, carry_ref, lb_ref, sums_ref):
    qi = pl.program_id(2)
    kb, qb = SB_BLOCK, SB_QBLOCK
    nsub = qb // kb
    npair = q_ref.shape[1] // LANES
    pairs_ = range(npair)

    @pl.when(qi == 0)
    def _():
        lane = lax.broadcasted_iota(jnp.int32, (kb, LANES), 1)
        head0 = lane < HEAD_DIM

        def fill(i, c):
            start = pl.multiple_of(i * kb, kb)
            for p in pairs_:
                for src, dst in ((k_ref, k2_ref), (v_ref, v2_ref)):
                    t = src[pl.ds(start, kb), p * LANES:(p + 1) * LANES].astype(F32)
                    dst[p, i, 0:kb, :] = jnp.where(head0, t, 0.0).astype(BF16)
                    dst[p, i, kb:2 * kb, :] = jnp.where(head0, 0.0, t).astype(BF16)
            return c

        lax.fori_loop(0, k_ref.shape[0] // kb, fill, 0)

    q = [q_ref[:, p * LANES:(p + 1) * LANES] for p in pairs_]
    uo = uo_ref[...]
    acc_ref[...] = jnp.zeros_like(acc_ref)
    carry_ref[...] = jnp.zeros_like(carry_ref)
    tpos = lax.broadcasted_iota(jnp.int32, (qb, kb), 0)
    spos = lax.broadcasted_iota(jnp.int32, (qb, kb), 1)

    def logits(p, sb):
        base = pl.multiple_of(sb * nsub, nsub)
        keys = k2_ref[p, pl.ds(base, nsub)].reshape(nsub * 2 * kb, LANES)
        return _dg(q[p], keys, NT)

    def scores(p, z_all, slot, diagonal):
        for j in range(nsub):
            for h in range(2):
                z = z_all[:, (2 * j + h) * kb:(2 * j + h + 1) * kb]
                nz = -z
                log_keep = jnp.minimum(nz, 0.0) - jnp.log(1.0 + jnp.exp(jnp.minimum(z, nz)))
                log_beta = log_keep + z
                if diagonal:
                    causal = (spos + j * kb) < tpos
                    log_keep = jnp.where(causal, log_keep, 0.0)
                    log_beta = jnp.where(causal, log_beta, -jnp.inf)
                hi = log_keep.astype(BF16)
                lo = (log_keep - hi.astype(F32)).astype(BF16)
                lb_ref[slot, p, 2 * j + h] = log_beta
                sums_ref[slot, p, 2 * j + h] = _dg(jnp.concatenate([hi, lo], axis=1), uo, NN)

    def accumulate(p, sb, slot):
        base = pl.multiple_of(sb * nsub, nsub)
        attn = {}
        for h in range(2):
            carry = carry_ref[p, h]
            for j in reversed(range(nsub)):
                s = sums_ref[slot, p, 2 * j + h]
                attn[j, h] = jnp.exp(lb_ref[slot, p, 2 * j + h] + carry + s[:, :kb]).astype(BF16)
                carry = carry + s[:, kb:]
            carry_ref[p, h] = carry
        weights = jnp.concatenate([attn[j, h] for j in range(nsub) for h in range(2)], axis=1)
        values = v2_ref[p, pl.ds(base, nsub)].reshape(nsub * 2 * kb, LANES)
        acc_ref[p] += _dg(weights, values, NN)

    zd = [logits(p, qi) for p in pairs_]
    for p in pairs_:
        scores(p, zd[p], 0, True)
    trips = qi // 2

    def body(i, c):
        sb = qi - 1 - 2 * i
        za = [logits(p, sb) for p in pairs_]
        zb = [logits(p, sb - 1) for p in pairs_]
        for p in pairs_:
            accumulate(p, sb + 1, 0)
        for p in pairs_:
            scores(p, za[p], 1, False)
        for p in pairs_:
            scores(p, zb[p], 0, False)
        for p in pairs_:
            accumulate(p, sb, 1)
        return c

    lax.fori_loop(0, trips, body, 0)
    odd = qi - 2 * trips == 1

    @pl.when(odd)
    def _():
        z = [logits(p, 0) for p in pairs_]
        for p in pairs_:
            accumulate(p, 1, 0)
        for p in pairs_:
            scores(p, z[p], 1, False)
        for p in pairs_:
            accumulate(p, 0, 1)

    @pl.when(jnp.logical_not(odd))
    def _():
        for p in pairs_:
            accumulate(p, 0, 0)

    for p in pairs_:
        o_ref[:, p * LANES:(p + 1) * LANES] = acc_ref[p].astype(o_ref.dtype)


def stick_breaking(p, batch, seqlen, width):
    npair = width // LANES
    kb, qb = SB_BLOCK, SB_QBLOCK
    nq = seqlen // qb
    hp = SB_PAIRS
    ngrp = npair // hp
    wd = hp * LANES
    nsl = 2 * qb // kb
    j = jnp.arange(kb)
    later = (j[:, None] > j[None, :]).astype(BF16)
    uo = jnp.concatenate([later, jnp.ones((kb, kb), BF16)], axis=1)
    uo = jnp.concatenate([uo, uo], axis=0)
    return pl.pallas_call(
        _sb_kernel,
        grid=(batch, ngrp, nq),
        in_specs=[pl.BlockSpec((qb, wd), lambda b, h, i: (b * nq + i, h)),
                  pl.BlockSpec((seqlen, wd), lambda b, h, i: (b, ngrp + h)),
                  pl.BlockSpec((seqlen, wd), lambda b, h, i: (b, 2 * ngrp + h)),
                  pl.BlockSpec((2 * kb, 2 * kb), lambda b, h, i: (0, 0))],
        out_specs=pl.BlockSpec((qb, wd), lambda b, h, i: (b * nq + i, h)),
        out_shape=jax.ShapeDtypeStruct((batch * seqlen, width), BF16),
        scratch_shapes=[pltpu.VMEM((hp, seqlen // kb, 2 * kb, LANES), BF16),
                        pltpu.VMEM((hp, seqlen // kb, 2 * kb, LANES), BF16),
                        pltpu.VMEM((hp, qb, LANES), F32), pltpu.VMEM((hp, 2, qb, kb), F32),
                        pltpu.VMEM((2, hp, nsl, qb, kb), F32), pltpu.VMEM((2, hp, nsl, qb, 2 * kb), F32)],
        compiler_params=_cparams(("parallel", "parallel", "arbitrary")),
        name="stick_breaking",
    )(p, p, p, uo)


def _gate_out_kernel(ya_ref, yb_ref, g0_ref, g1_ref, x_ref, w_ref, ng_ref, h_ref, hn_ref):
    gate = jnp.concatenate([g0_ref[...], g1_ref[...]], axis=-1).astype(F32)
    y = jnp.concatenate([ya_ref[...], yb_ref[...]], axis=-1).astype(F32) * (gate * jax.nn.sigmoid(gate))
    h = x_ref[...] + jnp.dot(y.astype(BF16), w_ref[...], preferred_element_type=F32)
    h_ref[...] = h
    ms = jnp.mean(h * h, axis=-1, keepdims=True)
    hn_ref[...] = ((h * lax.rsqrt(ms + RMS_EPS)) * ng_ref[...]).astype(hn_ref.dtype)


def gate_out(ya, yb, p, gate_col, x, w, next_g, tm=512):
    m, d = x.shape
    half = ya.shape[1]
    tm = min(tm, m)
    assert gate_col % half == 0 and d == 2 * half
    g0 = gate_col // half
    row = lambda n: pl.BlockSpec((tm, n), lambda i: (i, 0))
    return pl.pallas_call(
        _gate_out_kernel,
        grid=(m // tm,),
        in_specs=[row(half), row(half),
                  pl.BlockSpec((tm, half), lambda i: (i, g0)), pl.BlockSpec((tm, half), lambda i: (i, g0 + 1)), row(d),
                  pl.BlockSpec((d, d), lambda i: (0, 0), pipeline_mode=pl.Buffered(1)),
                  pl.BlockSpec((1, d), lambda i: (0, 0))],
        out_specs=[row(d), row(d)],
        out_shape=[jax.ShapeDtypeStruct((m, d), F32), jax.ShapeDtypeStruct((m, d), BF16)],
        compiler_params=_cparams(("parallel",)),
        name="gate_out",
    )(ya, yb, p, p, x, w, next_g.reshape(1, d))


def _s5_prep_kernel(lre_ref, lim_ref, ldt_ref, cre_ref, cim_ref, btre_ref, btim_ref,
                    kt_ref, wre_ref, wim_ref, vre_ref, vim_ref, qre_ref, qim_ref, ckre_ref, ckim_ref):
    blk = S5_BLOCK
    h = S5_GROUP
    lre, lim = lre_ref[...], lim_ref[...]
    dt = jnp.exp(ldt_ref[...])
    mag = jnp.exp(lre * dt)
    bre = mag * jnp.cos(lim * dt)
    bim = mag * jnp.sin(lim * dt)
    den = lre * lre + lim * lim
    nre, nim = bre - 1.0, bim
    fre = (nre * lre + nim * lim) / den
    fim = (nim * lre - nre * lim) / den
    cre, cim = cre_ref[...], cim_ref[...]
    ckr, cki = cre * fre - cim * fim, cre * fim + cim * fre
    btre, btim = btre_ref[...], btim_ref[...]
    pr, pi = jnp.ones_like(bre), jnp.zeros_like(bre)
    powers = []
    for tau in range(blk + 1):
        ckre_ref[:, tau * h:(tau + 1) * h, :] = ckr
        ckim_ref[:, tau * h:(tau + 1) * h, :] = cki
        powers.append((pr, pi))
        ckr, cki = ckr * bre - cki * bim, ckr * bim + cki * bre
        pr, pi = pr * bre - pi * bim, pr * bim + pi * bre
    for i in range(blk):
        pr, pi = powers[blk - 1 - i]
        wre_ref[:, i * h:(i + 1) * h, :] = pr * btre - pi * btim
        wim_ref[:, i * h:(i + 1) * h, :] = pr * btim + pi * btre
    vre_ref[...] = ckre_ref[:, h:(blk + 1) * h, :]
    vim_ref[...] = -ckim_ref[:, h:(blk + 1) * h, :]
    qr, qi = powers[blk]
    pr, pi = qr, qi
    for r in range(SUBLANES):
        qre_ref[:, r:r + 1, :] = pr
        qim_ref[:, r:r + 1, :] = pi
        pr, pi = pr * qr - pi * qi, pr * qi + pi * qr
    for g in range(kt_ref.shape[0]):
        kt_ref[g] = (_dot(btre[g], ckre_ref[g, 0:blk * h, :], NT) - _dot(btim[g], ckim_ref[g, 0:blk * h, :], NT))


def _s5_kernel(nbatch, p_ref, kt_ref, wre_ref, wim_ref, vt_ref, cre_ref, cim_ref, d_ref, o_ref,
               u_ref, gre_ref, gim_ref, xre_ref, xim_ref, yp_ref, stage_ref):
    npair = u_ref.shape[0] // 2
    rows = u_ref.shape[1]
    wlane = u_ref.shape[2]
    blk = S5_BLOCK
    ngrp = 2 * npair
    rc = S5_RELAYOUT_ROWS
    lane_chunk = lax.broadcasted_iota(jnp.int32, (rc, LANES), 1) // S5_GROUP

    def chunk_transpose(arrs):
        arrs = list(arrs)
        s = ngrp // 2
        while s:
            upper = (lane_chunk & s) != 0
            for x in range(ngrp):
                if x & s == 0:
                    ax, ay = arrs[x], arrs[x + s]
                    arrs[x] = jnp.where(upper, pltpu.roll(ay, s * S5_GROUP, 1), ax)
                    arrs[x + s] = jnp.where(upper, ay, pltpu.roll(ax, LANES - s * S5_GROUP, 1))
            s //= 2
        return arrs

    def relayout_in(c, carry):
        t0 = pl.multiple_of(c * (rc * blk), rc * blk)
        r0 = pl.multiple_of(c * rc, rc)
        stage_ref[...] = p_ref[pl.ds(t0, rc * blk), :].astype(F32)
        for half in range(blk // ngrp):
            z = chunk_transpose(stage_ref[pl.ds(ngrp * half + i, rc, stride=blk), :] for i in range(ngrp))
            for g in range(ngrp):
                u_ref[g, pl.ds(r0, rc), half * LANES:(half + 1) * LANES] = z[g].astype(BF16)
        return carry

    lax.fori_loop(0, rows // rc, relayout_in, 0)
    lane = lax.broadcasted_iota(jnp.int32, (S5_GROUP, wlane), 1)

    for g in range(2 * npair):
        kt = kt_ref[g]
        blocks = [kt] + [jnp.where(lane >= i * S5_GROUP, pltpu.roll(kt, i * S5_GROUP, 1), 0.0)
                         for i in range(1, S5_BLOCK)]
        kmat = jnp.concatenate(blocks, axis=0).astype(BF16)
        yp_ref[g] = _dg(u_ref[g], kmat, NN)
    for p in range(npair):
        ucat = jnp.concatenate([u_ref[2 * p], u_ref[2 * p + 1]], axis=1)
        gre_ref[p] = _dg(ucat, wre_ref[p], NN)
        gim_ref[p] = _dg(ucat, wim_ref[p], NN)

    def cma(xr, xi, ar, ai, sr, si):
        return xr + (ar * sr - ai * si), xi + (ar * si + ai * sr)

    row = lax.broadcasted_iota(jnp.int32, (SUBLANES, LANES), 0)
    per_batch = rows // nbatch

    def tile(t, carry, base):
        r0 = pl.multiple_of(base + t * SUBLANES, SUBLANES)
        new_carry = []
        for p in range(npair):
            xr = gre_ref[p, pl.ds(r0, SUBLANES), :]
            xi = gim_ref[p, pl.ds(r0, SUBLANES), :]
            for lvl, sh in enumerate((1, 2, 4)):
                ar = cre_ref[p, lvl * SUBLANES:(lvl + 1) * SUBLANES, :]
                ai = cim_ref[p, lvl * SUBLANES:(lvl + 1) * SUBLANES, :]
                xr, xi = cma(xr, xi, ar, ai, pltpu.roll(xr, sh, 0), pltpu.roll(xi, sh, 0))
            ar = cre_ref[p, 3 * SUBLANES:4 * SUBLANES, :]
            ai = cim_ref[p, 3 * SUBLANES:4 * SUBLANES, :]
            cr, ci = carry[2 * p], carry[2 * p + 1]
            xr, xi = cma(xr, xi, ar, ai, cr, ci)
            xre_ref[p, pl.ds(r0, SUBLANES), :] = jnp.where(row == 0, cr, pltpu.roll(xr, 1, 0))
            xim_ref[p, pl.ds(r0, SUBLANES), :] = jnp.where(row == 0, ci, pltpu.roll(xi, 1, 0))
            new_carry.append(jnp.broadcast_to(xr[SUBLANES - 1:SUBLANES, :], (SUBLANES, LANES)))
            new_carry.append(jnp.broadcast_to(xi[SUBLANES - 1:SUBLANES, :], (SUBLANES, LANES)))
        return tuple(new_carry)

    zero = jnp.zeros((SUBLANES, LANES), F32)
    for b in range(nbatch):
        lax.fori_loop(0, per_batch // SUBLANES, functools.partial(tile, base=b * per_batch), (zero,) * (2 * npair))

    for p in range(npair):
        xs = jnp.concatenate([xre_ref[p], xim_ref[p]], axis=1).astype(BF16)
        corr = _dg(xs, vt_ref[p], NT)
        for k in range(2):
            g = 2 * p + k
            y = yp_ref[g] + corr[:, k * wlane:(k + 1) * wlane] + d_ref[g] * u_ref[g].astype(F32)
            yp_ref[g] = jax.nn.gelu(y)

    def relayout_out(c, carry):
        t0 = pl.multiple_of(c * (rc * blk), rc * blk)
        r0 = pl.multiple_of(c * rc, rc)
        for half in range(blk // ngrp):
            yt = chunk_transpose(yp_ref[g, pl.ds(r0, rc), half * LANES:(half + 1) * LANES] for g in range(ngrp))
            for j in range(ngrp):
                stage_ref[pl.ds(ngrp * half + j, rc, stride=blk), :] = yt[j]
        o_ref[pl.ds(t0, rc * blk), :] = stage_ref[...].astype(o_ref.dtype)
        return carry

    lax.fori_loop(0, rows // rc, relayout_out, 0)


def s5_ssm(p, batch, seqlen, lam_re, lam_im, log_dt, b_re, b_im, c_re, c_im, d_skip):
    ngroup, nstate = lam_re.shape
    h, blk = S5_GROUP, S5_BLOCK
    width = ngroup * h
    wl = blk * h
    nb = batch * seqlen // blk
    gp = S5_PREP_GROUPS
    assert nstate == S5_STATE and 2 * nstate == LANES
    assert seqlen % (blk * SUBLANES) == 0 and ngroup % gp == 0 and ngroup % (2 * S5_PAIRS) == 0

    g3 = lambda x: x.reshape(ngroup, 1, nstate)
    bt = lambda x: jnp.swapaxes(x, 1, 2)
    spec1 = pl.BlockSpec((gp, 1, nstate), lambda i: (i, 0, 0))
    spec_c = pl.BlockSpec((gp, h, nstate), lambda i: (i, 0, 0))
    spec_w = pl.BlockSpec((gp, wl, nstate), lambda i: (i, 0, 0))
    spec_q = pl.BlockSpec((gp, SUBLANES, nstate), lambda i: (i, 0, 0))
    spec_k = pl.BlockSpec((gp, h, wl), lambda i: (i, 0, 0))
    f = lambda *s: jax.ShapeDtypeStruct(s, F32)
    kt, w_re, w_im, v_re, v_im, q_re, q_im = pl.pallas_call(
        _s5_prep_kernel,
        grid=(ngroup // gp,),
        in_specs=[spec1, spec1, spec1, spec_c, spec_c, spec_c, spec_c],
        out_specs=[spec_k, spec_w, spec_w, spec_w, spec_w, spec_q, spec_q],
        out_shape=[f(ngroup, h, wl), f(ngroup, wl, nstate), f(ngroup, wl, nstate), f(ngroup, wl, nstate),
                   f(ngroup, wl, nstate), f(ngroup, SUBLANES, nstate), f(ngroup, SUBLANES, nstate)],
        scratch_shapes=[pltpu.VMEM((gp, (blk + 1) * h, nstate), F32), pltpu.VMEM((gp, (blk + 1) * h, nstate), F32)],
        compiler_params=_cparams(("parallel",)),
        name="s5_prep",
    )(g3(lam_re), g3(lam_im), jnp.broadcast_to(log_dt[:, None, None], (ngroup, 1, nstate)), c_re, c_im,
      bt(b_re), bt(b_im))

    npairs = ngroup // 2

    def pair_rows(x):
        x = x.reshape(npairs, 2, x.shape[1], nstate)
        z = jnp.zeros_like(x[:, 0])
        return jnp.concatenate([jnp.concatenate([x[:, 0], z], axis=2), jnp.concatenate([z, x[:, 1]], axis=2)], axis=1)

    w_re2 = pair_rows(w_re).astype(BF16)
    w_im2 = pair_rows(w_im).astype(BF16)
    vt2 = jnp.concatenate([pair_rows(v_re), pair_rows(v_im)], axis=2).astype(BF16)

    def pair_lanes(x):
        x = x.reshape(npairs, 2, SUBLANES, nstate)
        return jnp.concatenate([x[:, 0], x[:, 1]], axis=2)

    rows = jnp.arange(SUBLANES)[None, :, None]

    def scan_consts(q):
        q = pair_lanes(q)
        levels = [jnp.where(rows >= sh, q[:, sh - 1:sh, :], 0.0) for sh in (1, 2, 4)]
        return jnp.concatenate(levels + [q], axis=1)

    c_re2, c_im2 = scan_consts(q_re), scan_consts(q_im)
    d_row = jnp.tile(d_skip.reshape(ngroup, 1, h), (1, 1, blk))

    np_ = S5_PAIRS
    gs = 2 * np_
    assert gs * h == LANES and blk % gs == 0 and nb % S5_RELAYOUT_ROWS == 0
    tok = pl.BlockSpec((nb * blk, LANES), lambda i: (0, i))
    grp = lambda r, c: pl.BlockSpec((gs, r, c), lambda i: (i, 0, 0))
    par = lambda r, c: pl.BlockSpec((np_, r, c), lambda i: (i, 0, 0))
    return pl.pallas_call(
        functools.partial(_s5_kernel, batch),
        grid=(ngroup // gs,),
        in_specs=[tok, grp(h, wl), par(2 * wl, LANES), par(2 * wl, LANES), par(2 * wl, 2 * LANES),
                  par(4 * SUBLANES, LANES), par(4 * SUBLANES, LANES), grp(1, wl)],
        out_specs=tok,
        out_shape=jax.ShapeDtypeStruct((batch * seqlen, width), BF16),
        scratch_shapes=[pltpu.VMEM((gs, nb, wl), BF16)] + [pltpu.VMEM((np_, nb, LANES), F32)] * 4
        + [pltpu.VMEM((gs, nb, wl), F32), pltpu.VMEM((S5_RELAYOUT_ROWS * blk, LANES), F32)],
        compiler_params=_cparams(("parallel",)),
        name="s5_ssm",
    )(p, kt, w_re2, w_im2, vt2, c_re2, c_im2, d_row)


def _glu_kernel(y_ref, yj_ref, gj_ref, w_ref, b_ref, o_ref):
    z = jnp.dot(y_ref[...], w_ref[...].astype(BF16), preferred_element_type=F32) + b_ref[...]
    gate = gj_ref[...].astype(F32)
    yj = yj_ref[...].astype(F32)
    o_ref[...] = (yj * jax.nn.sigmoid(z) * (gate * jax.nn.sigmoid(gate))).astype(o_ref.dtype)


def glu_gate(y, p, w, b, tm=1024, tn=1024):
    m, d = y.shape
    tm = min(tm, m)
    goff = d // tn
    return pl.pallas_call(
        _glu_kernel,
        grid=(m // tm, d // tn),
        in_specs=[pl.BlockSpec((tm, d), lambda i, j: (i, 0)),
                  pl.BlockSpec((tm, tn), lambda i, j: (i, j)),
                  pl.BlockSpec((tm, tn), lambda i, j: (i, goff + j)),
                  pl.BlockSpec((d, tn), lambda i, j: (0, j)),
                  pl.BlockSpec((1, tn), lambda i, j: (0, j))],
        out_specs=pl.BlockSpec((tm, tn), lambda i, j: (i, j)),
        out_shape=jax.ShapeDtypeStruct((m, d), BF16),
        compiler_params=_cparams(("parallel", "arbitrary")),
        name="glu_gate",
    )(y, y, p, w, b.reshape(1, d))


def _final_kernel(a_ref, h_ref, w_ref, g_ref, o_ref):
    h = h_ref[...] + jnp.dot(a_ref[...], w_ref[...], preferred_element_type=F32)
    ms = jnp.mean(h * h, axis=-1, keepdims=True)
    o_ref[...] = (h * lax.rsqrt(ms + RMS_EPS)) * g_ref[...]


def final_out(a, h, w, g, tm=512):
    m, d = h.shape
    tm = min(tm, m)
    row = pl.BlockSpec((tm, d), lambda i: (i, 0))
    return pl.pallas_call(
        _final_kernel,
        grid=(m // tm,),
        in_specs=[row, row, pl.BlockSpec((d, d), lambda i: (0, 0), pipeline_mode=pl.Buffered(1)),
                  pl.BlockSpec((1, d), lambda i: (0, 0))],
        out_specs=row,
        out_shape=jax.ShapeDtypeStruct((m, d), F32),
        compiler_params=_cparams(("parallel",)),
        name="final_out",
    )(a, h, w, g.reshape(1, d))


def kernel(x, norm_g, final_g, ab_w_in, rwkv_shift_mix, rwkv_w_up, rwkv_w0, rwkv_a_up, rwkv_a0, rwkv_k_k, rwkv_k_a, rwkv_r_k, rwkv_gn_w, rwkv_gn_b, ab_w_out, s5_w_in, s5_lam_re, s5_lam_im, s5_log_dt, s5_b_re, s5_b_im, s5_c_re, s5_c_im, s5_d, s5_w_glu, s5_b_glu, s5_w_out):
    batch, seqlen, d = x.shape
    m = batch * seqlen
    rwkv_w = rwkv_w0.shape[1]
    rwkv_proj = 3 * rwkv_w + 2 * LORA_RANK
    sb_w = (ab_w_in.shape[2] - rwkv_proj - d) // 3
    x2 = x.reshape(m, d)

    w_in = ab_w_in[0]
    scale = HEAD_DIM ** -0.5
    col_scale = jnp.concatenate([jnp.full((sb_w,), scale, F32), jnp.ones((2 * sb_w + d,), F32)])
    xn = rmsnorm(x2, norm_g[0], BF16)
    p_rkv = matmul(xn, w_in, F32, 0, 3 * rwkv_w, name="proj_rkv")
    p_lora = matmul(xn, w_in, F32, 3 * rwkv_w, 2 * LORA_RANK, tn=2 * LORA_RANK, name="proj_lora")
    p_sb = matmul(xn, w_in, BF16, rwkv_proj, 3 * sb_w + d, col_scale, name="proj_sb_gate")
    y_a = rwkv7(p_rkv, p_lora, batch, seqlen, rwkv_shift_mix[0], rwkv_w_up[0], rwkv_w0[0], rwkv_a_up[0], rwkv_a0[0],
                rwkv_k_k[0], rwkv_k_a[0], rwkv_r_k[0], rwkv_gn_w[0], rwkv_gn_b[0])
    y_b = stick_breaking(p_sb, batch, seqlen, sb_w)
    h1, hn1 = gate_out(y_a, y_b, p_sb, 3 * sb_w, x2, ab_w_out[0].astype(BF16), norm_g[1])

    p1 = matmul(hn1, s5_w_in[0], BF16, 0, 2 * d, name="proj_s5")
    y_s5 = s5_ssm(p1, batch, seqlen, s5_lam_re[0], s5_lam_im[0], s5_log_dt[0], s5_b_re[0], s5_b_im[0],
                  s5_c_re[0], s5_c_im[0], s5_d[0])
    act = glu_gate(y_s5, p1, s5_w_glu[0], s5_b_glu[0])
    out = final_out(act, h1, s5_w_out[0].astype(BF16), final_g)
    return out.reshape(batch, seqlen, d)
```

```python
import functools
import math

import jax
import jax.numpy as jnp
from jax import lax
from jax.experimental import pallas as pl
from jax.experimental.pallas import tpu as pltpu

F32 = jnp.float32
BF16 = jnp.bfloat16

HEAD_DIM = 64
LANES = 128
SUBLANES = 8
LORA_RANK = 64
S5_GROUP = 16
S5_STATE = 64
RMS_EPS = 1e-6
GN_EPS = 64e-5
DECAY_SCALE = math.exp(-0.5)

RWKV_CHUNK = 64
RWKV_PAIRS = 8
SB_BLOCK = 128
SB_QBLOCK = 256
SB_PAIRS = 4
S5_BLOCK = 16
S5_PAIRS = 4
S5_PREP_GROUPS = 16
S5_RELAYOUT_ROWS = 64
VMEM_LIMIT = 56 * 1024 * 1024

NN = (((1,), (0,)), ((), ()))
NT = (((1,), (1,)), ((), ()))
TN = (((0,), (0,)), ((), ()))


def _cparams(sem):
    return pltpu.CompilerParams(dimension_semantics=sem, vmem_limit_bytes=VMEM_LIMIT)


def _split(x):
    hi = x.astype(BF16)
    lo = (x - hi.astype(F32)).astype(BF16)
    return hi, lo


def _dg(a, b, dn):
    return lax.dot_general(a, b, dn, preferred_element_type=F32)


def _dot(a, b, dn=NN, passes=3):
    if passes == 1:
        return _dg(a.astype(BF16), b.astype(BF16), dn)
    ah, al = _split(a)
    bh, bl = _split(b)
    return _dg(ah, bh, dn) + (_dg(ah, bl, dn) + _dg(al, bh, dn))


def _dot_rhs_exact(a, b01, passes=3):
    ah = a.astype(BF16)
    out = _dg(ah, b01, NN)
    rem = a - ah.astype(F32)
    for _ in range(passes - 1):
        rh = rem.astype(BF16)
        out = out + _dg(rh, b01, NN)
        rem = rem - rh.astype(F32)
    return out


def _dot_lhs_exact(a01, b, passes=3):
    bh = b.astype(BF16)
    out = _dg(a01, bh, NN)
    rem = b - bh.astype(F32)
    for _ in range(passes - 1):
        rh = rem.astype(BF16)
        out = out + _dg(a01, rh, NN)
        rem = rem - rh.astype(F32)
    return out


def _rmsnorm_kernel(x_ref, g_ref, o_ref):
    x = x_ref[...]
    ms = jnp.mean(x * x, axis=-1, keepdims=True)
    o_ref[...] = ((x * lax.rsqrt(ms + RMS_EPS)) * g_ref[...]).astype(o_ref.dtype)


def rmsnorm(x, g, out_dtype, tm=512):
    m, d = x.shape
    tm = min(tm, m)
    return pl.pallas_call(
        _rmsnorm_kernel,
        grid=(m // tm,),
        in_specs=[pl.BlockSpec((tm, d), lambda i: (i, 0)), pl.BlockSpec((1, d), lambda i: (0, 0))],
        out_specs=pl.BlockSpec((tm, d), lambda i: (i, 0)),
        out_shape=jax.ShapeDtypeStruct((m, d), out_dtype),
        compiler_params=_cparams(("parallel",)),
        name="rmsnorm",
    )(x, g.reshape(1, d))


def _mm_kernel(a_ref, w_ref, o_ref):
    o_ref[...] = jnp.dot(a_ref[...], w_ref[...].astype(BF16), preferred_element_type=F32).astype(o_ref.dtype)


def _mm_scaled_kernel(a_ref, w_ref, s_ref, o_ref):
    acc = jnp.dot(a_ref[...], w_ref[...].astype(BF16), preferred_element_type=F32)
    o_ref[...] = (acc * s_ref[...]).astype(o_ref.dtype)


def matmul(a, w, out_dtype, col0, n, col_scale=None, tm=1024, tn=1024, name="matmul"):
    m, k = a.shape
    tm = min(tm, m)
    assert m % tm == 0 and n % tn == 0 and col0 % LANES == 0
    in_specs = [pl.BlockSpec((tm, k), lambda i, j: (i, 0)),
                pl.BlockSpec((pl.Element(k), pl.Element(tn)), lambda i, j: (0, pl.multiple_of(col0 + j * tn, LANES)))]
    args = [a, w]
    if col_scale is not None:
        in_specs.append(pl.BlockSpec((1, tn), lambda i, j: (0, j)))
        args.append(col_scale.reshape(1, n))
    return pl.pallas_call(
        _mm_kernel if col_scale is None else _mm_scaled_kernel,
        grid=(m // tm, n // tn),
        in_specs=in_specs,
        out_specs=pl.BlockSpec((tm, tn), lambda i, j: (i, j)),
        out_shape=jax.ShapeDtypeStruct((m, n), out_dtype),
        compiler_params=_cparams(("parallel", "arbitrary")),
        name=name,
    )(*args)


def _rwkv_kernel(r_ref, k_ref, v_ref, lo_ref, mr_ref, mk_ref, mv_ref, mlo_ref, wup_ref, aup_ref,
                 w0_ref, a0_ref, kk_ref, ka_ref, rk_ref, gnw_ref, gnb_ref,
                 seg_ref, tri_ref, strict_ref, incl_ref, eye_ref,
                 y_ref, s_ref, prev_ref):
    c = pl.program_id(2)
    ch = RWKV_CHUNK
    npair = r_ref.shape[1] // LANES

    @pl.when(c == 0)
    def _():
        s_ref[...] = jnp.zeros_like(s_ref)
        prev_ref[...] = jnp.zeros_like(prev_ref)

    def token_shift(x, idx, mix):
        row = lax.broadcasted_iota(jnp.int32, x.shape, 0)
        prev = prev_ref[idx:idx + 1, 0:x.shape[1]]
        shifted = jnp.where(row == 0, prev, pltpu.roll(x, 1, 0))
        return x + (shifted - x) * mix

    r_in, k_in, v_in, lo_in = r_ref[...], k_ref[...], v_ref[...], lo_ref[...]
    r = token_shift(r_in, 0, mr_ref[...])
    k = token_shift(k_in, 1, mk_ref[...])
    v = token_shift(v_in, 2, mv_ref[...])
    lo = token_shift(lo_in, 3, mlo_ref[...])
    prev_ref[0:1, :] = r_in[ch - 1:ch, :]
    prev_ref[1:2, :] = k_in[ch - 1:ch, :]
    prev_ref[2:3, :] = v_in[ch - 1:ch, :]
    prev_ref[3:4, 0:LANES] = lo_in[ch - 1:ch, :]

    z_w = w0_ref[...] + _dot(jnp.tanh(lo), wup_ref[...], passes=1)
    logw = -DECAY_SCALE * jax.nn.sigmoid(z_w)
    a = jax.nn.sigmoid(a0_ref[...] + _dot(lo, aup_ref[...], passes=1))

    seg2 = seg_ref[...]

    def head_sum(x):
        tiles = []
        for t in range(npair):
            hi, lo_ = _split(x[:, t * LANES:(t + 1) * LANES])
            tiles.append(_dg(jnp.concatenate([hi, lo_], axis=1), seg2, NN))
        return jnp.concatenate(tiles, axis=1)

    kk = k * kk_ref[...]
    kk = kk * lax.rsqrt(jnp.maximum(head_sum(kk * kk), 1e-24))
    k2 = k * (1.0 + (a - 1.0) * ka_ref[...])
    ab = kk * a

    l_hi = logw.astype(BF16)
    rem = logw - l_hi.astype(F32)
    l_mid = rem.astype(BF16)
    l_lo = (rem - l_mid.astype(F32)).astype(BF16)
    cum = _dg(tri_ref[...], jnp.concatenate([l_hi, l_mid, l_lo], axis=0), NN)
    cum_last = cum[ch - 1:ch, :]
    e_cum = jnp.exp(cum)
    e_ncum = jnp.exp(-cum)
    e_tail = jnp.exp(cum_last - cum)
    rt = r * e_cum
    kt = k2 * e_ncum
    bt = ab * e_ncum
    at = -kk * jnp.exp(cum - logw)
    khat = k2 * e_tail
    bhat = ab * e_tail
    p_last = e_cum[ch - 1:ch, :]

    lane = lax.broadcasted_iota(jnp.int32, (ch, LANES), 1)
    head0 = lane < HEAD_DIM
    strict = strict_ref[...] > 0.5
    incl = incl_ref[...] > 0.5
    eye = eye_ref[...]

    def stack(x):
        return jnp.concatenate([jnp.where(head0, x, 0.0), jnp.where(head0, 0.0, x)], axis=0)

    def mm(x, w, dn=NN):
        return _dg(x.astype(BF16), w.astype(BF16), dn)

    pairs = range(npair)
    rows = 2 * ch
    sl = [slice(p * LANES, (p + 1) * LANES) for p in pairs]
    at2, rt2, bt2, kt2, v2, khat2, bhat2 = ([stack(t[:, sl[p]]) for p in pairs]
                                            for t in (at, rt, bt, kt, v, khat, bhat))
    gram = []
    for p in pairs:
        lh, ll = _split(jnp.concatenate([at2[p], rt2[p]], axis=0))
        rh, rl = _split(jnp.concatenate([bt2[p], kt2[p]], axis=0))
        gram.append(_dg(jnp.concatenate([lh, ll], axis=1), jnp.concatenate([rh, rh], axis=1), NT))
    a_ab = [jnp.where(strict, g[:rows, :rows], 0.0) for g in gram]
    a_ak = [jnp.where(strict, g[:rows, rows:], 0.0) for g in gram]
    a_rb = [jnp.where(incl, g[rows:, :rows], 0.0) for g in gram]
    a_rk = [jnp.where(incl, g[rows:, rows:], 0.0) for g in gram]
    akv = [mm(a_ak[p], v2[p]) for p in pairs]

    tinv = [eye + a for a in a_ab]
    pw = [mm(a, a) for a in a_ab]
    for _ in range(int(math.log2(ch)) - 2):
        both = [mm(pw[p], jnp.concatenate([pw[p], tinv[p]], axis=1)) for p in pairs]
        pw = [b[:, :rows] for b in both]
        tinv = [tinv[p] + both[p][:, rows:] for p in pairs]
    tinv = [tinv[p] + mm(pw[p], tinv[p]) for p in pairs]

    s = [s_ref[p] for p in pairs]
    tw = [mm(tinv[p], jnp.concatenate([at2[p], akv[p]], axis=1)) for p in pairs]
    ws = [mm(jnp.concatenate([tw[p][:, :LANES], rt2[p]], axis=0), s[p], NT) for p in pairs]
    uv = [jnp.concatenate([ws[p][:rows] + tw[p][:, LANES:], v2[p]], axis=0) for p in pairs]
    y2 = [ws[p][rows:] + mm(jnp.concatenate([a_rb[p], a_rk[p]], axis=1), uv[p]) for p in pairs]
    for p in pairs:
        s_ref[p] = s[p] * p_last[:, sl[p]] + mm(uv[p], jnp.concatenate([bhat2[p], khat2[p]], axis=0), TN)
    y = jnp.concatenate([t[:ch, :] + t[ch:, :] for t in y2], axis=1)

    inv_n = 1.0 / HEAD_DIM
    mu = head_sum(y) * inv_n
    d = y - mu
    var = head_sum(d * d) * inv_n
    yn = d * lax.rsqrt(var + GN_EPS) * gnw_ref[...] + gnb_ref[...]
    y_ref[...] = (yn + head_sum(r * k2 * rk_ref[...]) * v).astype(y_ref.dtype)


def rwkv7(p, p_lora, batch, seqlen, shift_mix, w_up, w0, a_up, a0, k_k, k_a, r_k, gn_w, gn_b):
    width = w0.shape[0]
    npair = width // LANES
    ch = RWKV_CHUNK
    nchunk = seqlen // ch
    assert seqlen % ch == 0 and 2 * LORA_RANK == LANES and 2 * HEAD_DIM == LANES
    zeros = jnp.zeros((LORA_RANK, width), F32)
    wup_pad = jnp.concatenate([w_up, zeros], axis=0)
    aup_pad = jnp.concatenate([zeros, a_up], axis=0)
    hb = RWKV_PAIRS
    assert npair % hb == 0
    idx = jnp.arange(LANES)
    seg = (idx[:, None] // HEAD_DIM == idx[None, :] // HEAD_DIM).astype(BF16)
    seg = jnp.concatenate([seg, seg], axis=0)
    t = jnp.arange(ch)
    tri = (t[None, :] <= t[:, None]).astype(BF16)
    tri = jnp.concatenate([tri, tri, tri], axis=1)
    i2 = jnp.arange(2 * ch)
    same = (i2[:, None] // ch) == (i2[None, :] // ch)
    strict = (same & ((i2[None, :] % ch) < (i2[:, None] % ch))).astype(F32)
    incl = (same & ((i2[None, :] % ch) <= (i2[:, None] % ch))).astype(F32)
    eye = jnp.eye(2 * ch, dtype=F32)
    row2 = lambda x: x.reshape(1, -1)

    wd = hb * LANES
    ngrp = npair // hb

    def tok(off):
        return pl.BlockSpec((ch, wd), lambda b, h, c: (b * nchunk + c, off + h))

    def par(off):
        return pl.BlockSpec((1, wd), lambda b, h, c: (0, off + h))

    def const(shape):
        return pl.BlockSpec(shape, lambda b, h, c: (0, 0))

    up = pl.BlockSpec((LANES, wd), lambda b, h, c: (0, h))
    lora_tok = pl.BlockSpec((ch, LANES), lambda b, h, c: (b * nchunk + c, 0))
    lora_par = pl.BlockSpec((1, LANES), lambda b, h, c: (0, 3 * npair))
    return pl.pallas_call(
        _rwkv_kernel,
        grid=(batch, ngrp, nchunk),
        in_specs=[tok(0), tok(ngrp), tok(2 * ngrp), lora_tok,
                  par(0), par(ngrp), par(2 * ngrp), lora_par, up, up,
                  par(0), par(0), par(0), par(0), par(0), par(0), par(0),
                  const((2 * LANES, LANES)), const((ch, 3 * ch)), const((2 * ch, 2 * ch)),
                  const((2 * ch, 2 * ch)), const((2 * ch, 2 * ch))],
        out_specs=pl.BlockSpec((ch, wd), lambda b, h, c: (b * nchunk + c, h)),
        out_shape=jax.ShapeDtypeStruct((batch * seqlen, width), BF16),
        scratch_shapes=[pltpu.VMEM((hb, LANES, LANES), F32), pltpu.VMEM((SUBLANES, wd), F32)],
        compiler_params=_cparams(("parallel", "parallel", "arbitrary")),
        name="rwkv7",
    )(p, p, p, p_lora, row2(shift_mix), row2(shift_mix), row2(shift_mix), row2(shift_mix), wup_pad, aup_pad,
      row2(w0), row2(a0), row2(k_k), row2(k_a), row2(r_k), row2(gn_w), row2(gn_b),
      seg, tri, strict, incl, eye)


def _sb_kernel(q_ref, k_ref, v_ref, uo_ref, o_ref, k2_ref, v2_ref, acc_ref, carry_ref, lb_ref, sums_ref):
    qi = pl.program_id(2)
    kb, qb = SB_BLOCK, SB_QBLOCK
    nsub = qb // kb
    npair = q_ref.shape[1] // LANES
    pairs_ = range(npair)

    @pl.when(qi == 0)
    def _():
        lane = lax.broadcasted_iota(jnp.int32, (kb, LANES), 1)
        head0 = lane < HEAD_DIM

        def fill(i, c):
            start = pl.multiple_of(i * kb, kb)
            for p in pairs_:
                for src, dst in ((k_ref, k2_ref), (v_ref, v2_ref)):
                    t = src[pl.ds(start, kb), p * LANES:(p + 1) * LANES].astype(F32)
                    dst[p, i, 0:kb, :] = jnp.where(head0, t, 0.0).astype(BF16)
                    dst[p, i, kb:2 * kb, :] = jnp.where(head0, 0.0, t).astype(BF16)
            return c

        lax.fori_loop(0, k_ref.shape[0] // kb, fill, 0)

    q = [q_ref[:, p * LANES:(p + 1) * LANES] for p in pairs_]
    uo = uo_ref[...]
    acc_ref[...] = jnp.zeros_like(acc_ref)
    carry_ref[...] = jnp.zeros_like(carry_ref)
    tpos = lax.broadcasted_iota(jnp.int32, (qb, kb), 0)
    spos = lax.broadcasted_iota(jnp.int32, (qb, kb), 1)

    def logits(p, sb):
        base = pl.multiple_of(sb * nsub, nsub)
        keys = k2_ref[p, pl.ds(base, nsub)].reshape(nsub * 2 * kb, LANES)
        return _dg(q[p], keys, NT)

    def scores(p, z_all, slot, diagonal):
        for j in range(nsub):
            for h in range(2):
                z = z_all[:, (2 * j + h) * kb:(2 * j + h + 1) * kb]
                nz = -z
                log_keep = jnp.minimum(nz, 0.0) - jnp.log(1.0 + jnp.exp(jnp.minimum(z, nz)))
                if diagonal:
                    causal = (spos + j * kb) < tpos
                    log_keep = jnp.where(causal, log_keep, 0.0)
                    z = jnp.where(causal, z, -jnp.inf)
                hi = log_keep.astype(BF16)
                lo = (log_keep - hi.astype(F32)).astype(BF16)
                lb_ref[slot, p, 2 * j + h] = z
                sums_ref[slot, p, 2 * j + h] = _dg(jnp.concatenate([hi, lo], axis=1), uo, NN)

    def accumulate(p, sb, slot):
        base = pl.multiple_of(sb * nsub, nsub)
        attn = {}
        for h in range(2):
            carry = carry_ref[p, h]
            for j in reversed(range(nsub)):
                s = sums_ref[slot, p, 2 * j + h]
                attn[j, h] = jnp.exp(lb_ref[slot, p, 2 * j + h] + carry + s[:, :kb]).astype(BF16)
                carry = carry + s[:, kb:]
            carry_ref[p, h] = carry
        weights = jnp.concatenate([attn[j, h] for j in range(nsub) for h in range(2)], axis=1)
        values = v2_ref[p, pl.ds(base, nsub)].reshape(nsub * 2 * kb, LANES)
        acc_ref[p] += _dg(weights, values, NN)

    zd = [logits(p, qi) for p in pairs_]
    for p in pairs_:
        scores(p, zd[p], 0, True)
    trips = qi // 2

    def body(i, c):
        sb = qi - 1 - 2 * i
        za = [logits(p, sb) for p in pairs_]
        zb = [logits(p, sb - 1) for p in pairs_]
        for p in pairs_:
            accumulate(p, sb + 1, 0)
        for p in pairs_:
            scores(p, za[p], 1, False)
        for p in pairs_:
            scores(p, zb[p], 0, False)
        for p in pairs_:
            accumulate(p, sb, 1)
        return c

    lax.fori_loop(0, trips, body, 0)
    odd = qi - 2 * trips == 1

    @pl.when(odd)
    def _():
        z = [logits(p, 0) for p in pairs_]
        for p in pairs_:
            accumulate(p, 1, 0)
        for p in pairs_:
            scores(p, z[p], 1, False)
        for p in pairs_:
            accumulate(p, 0, 1)

    @pl.when(jnp.logical_not(odd))
    def _():
        for p in pairs_:
            accumulate(p, 0, 0)

    for p in pairs_:
        o_ref[:, p * LANES:(p + 1) * LANES] = acc_ref[p].astype(o_ref.dtype)


def stick_breaking(p, batch, seqlen, width):
    npair = width // LANES
    kb, qb = SB_BLOCK, SB_QBLOCK
    nq = seqlen // qb
    hp = SB_PAIRS
    ngrp = npair // hp
    wd = hp * LANES
    nsl = 2 * qb // kb
    j = jnp.arange(kb)
    later = (j[:, None] >= j[None, :]).astype(BF16)
    uo = jnp.concatenate([later, jnp.ones((kb, kb), BF16)], axis=1)
    uo = jnp.concatenate([uo, uo], axis=0)
    return pl.pallas_call(
        _sb_kernel,
        grid=(batch, ngrp, nq),
        in_specs=[pl.BlockSpec((qb, wd), lambda b, h, i: (b * nq + i, h)),
                  pl.BlockSpec((seqlen, wd), lambda b, h, i: (b, ngrp + h), pipeline_mode=pl.Buffered(1)),
                  pl.BlockSpec((seqlen, wd), lambda b, h, i: (b, 2 * ngrp + h), pipeline_mode=pl.Buffered(1)),
                  pl.BlockSpec((2 * kb, 2 * kb), lambda b, h, i: (0, 0))],
        out_specs=pl.BlockSpec((qb, wd), lambda b, h, i: (b * nq + i, h)),
        out_shape=jax.ShapeDtypeStruct((batch * seqlen, width), BF16),
        scratch_shapes=[pltpu.VMEM((hp, seqlen // kb, 2 * kb, LANES), BF16),
                        pltpu.VMEM((hp, seqlen // kb, 2 * kb, LANES), BF16),
                        pltpu.VMEM((hp, qb, LANES), F32), pltpu.VMEM((hp, 2, qb, kb), F32),
                        pltpu.VMEM((2, hp, nsl, qb, kb), F32), pltpu.VMEM((2, hp, nsl, qb, 2 * kb), F32)],
        compiler_params=_cparams(("parallel", "parallel", "arbitrary")),
        name="stick_breaking",
    )(p, p, p, uo)


def _gate_out_kernel(ya_ref, yb_ref, g0_ref, g1_ref, x_ref, w_ref, ng_ref, h_ref, hn_ref):
    gate = jnp.concatenate([g0_ref[...], g1_ref[...]], axis=-1).astype(F32)
    y = jnp.concatenate([ya_ref[...], yb_ref[...]], axis=-1).astype(F32) * (gate * jax.nn.sigmoid(gate))
    h = x_ref[...] + jnp.dot(y.astype(BF16), w_ref[...], preferred_element_type=F32)
    h_ref[...] = h
    ms = jnp.mean(h * h, axis=-1, keepdims=True)
    hn_ref[...] = ((h * lax.rsqrt(ms + RMS_EPS)) * ng_ref[...]).astype(hn_ref.dtype)


def gate_out(ya, yb, p, gate_col, x, w, next_g, tm=512):
    m, d = x.shape
    half = ya.shape[1]
    tm = min(tm, m)
    assert gate_col % half == 0 and d == 2 * half
    g0 = gate_col // half
    row = lambda n: pl.BlockSpec((tm, n), lambda i: (i, 0))
    return pl.pallas_call(
        _gate_out_kernel,
        grid=(m // tm,),
        in_specs=[row(half), row(half),
                  pl.BlockSpec((tm, half), lambda i: (i, g0)), pl.BlockSpec((tm, half), lambda i: (i, g0 + 1)), row(d),
                  pl.BlockSpec((d, d), lambda i: (0, 0), pipeline_mode=pl.Buffered(1)),
                  pl.BlockSpec((1, d), lambda i: (0, 0))],
        out_specs=[row(d), row(d)],
        out_shape=[jax.ShapeDtypeStruct((m, d), F32), jax.ShapeDtypeStruct((m, d), BF16)],
        compiler_params=_cparams(("parallel",)),
        name="gate_out",
    )(ya, yb, p, p, x, w, next_g.reshape(1, d))


def _s5_prep_kernel(lre_ref, lim_ref, ldt_ref, cre_ref, cim_ref, btre_ref, btim_ref,
                    kt_ref, wre_ref, wim_ref, vre_ref, vim_ref, qre_ref, qim_ref, ckre_ref, ckim_ref):
    blk = S5_BLOCK
    h = S5_GROUP
    lre, lim = lre_ref[...], lim_ref[...]
    dt = jnp.exp(ldt_ref[...])
    mag = jnp.exp(lre * dt)
    bre = mag * jnp.cos(lim * dt)
    bim = mag * jnp.sin(lim * dt)
    den = lre * lre + lim * lim
    nre, nim = bre - 1.0, bim
    fre = (nre * lre + nim * lim) / den
    fim = (nim * lre - nre * lim) / den
    cre, cim = cre_ref[...], cim_ref[...]
    ckr, cki = cre * fre - cim * fim, cre * fim + cim * fre
    btre, btim = btre_ref[...], btim_ref[...]
    pr, pi = jnp.ones_like(bre), jnp.zeros_like(bre)
    powers = []
    for tau in range(blk + 1):
        ckre_ref[:, tau * h:(tau + 1) * h, :] = ckr
        ckim_ref[:, tau * h:(tau + 1) * h, :] = cki
        powers.append((pr, pi))
        ckr, cki = ckr * bre - cki * bim, ckr * bim + cki * bre
        pr, pi = pr * bre - pi * bim, pr * bim + pi * bre
    for i in range(blk):
        pr, pi = powers[blk - 1 - i]
        wre_ref[:, i * h:(i + 1) * h, :] = pr * btre - pi * btim
        wim_ref[:, i * h:(i + 1) * h, :] = pr * btim + pi * btre
    vre_ref[...] = ckre_ref[:, h:(blk + 1) * h, :]
    vim_ref[...] = -ckim_ref[:, h:(blk + 1) * h, :]
    qr, qi = powers[blk]
    pr, pi = qr, qi
    for r in range(SUBLANES):
        qre_ref[:, r:r + 1, :] = pr
        qim_ref[:, r:r + 1, :] = pi
        pr, pi = pr * qr - pi * qi, pr * qi + pi * qr
    for g in range(kt_ref.shape[0]):
        kt_ref[g] = (_dot(btre[g], ckre_ref[g, 0:blk * h, :], NT) - _dot(btim[g], ckim_ref[g, 0:blk * h, :], NT))


def _s5_kernel(nbatch, p_ref, kt_ref, wre_ref, wim_ref, vt_ref, cre_ref, cim_ref, d_ref, o_ref,
               u_ref, gre_ref, gim_ref, xre_ref, xim_ref, yp_ref, stage_ref):
    npair = u_ref.shape[0] // 2
    rows = u_ref.shape[1]
    wlane = u_ref.shape[2]
    blk = S5_BLOCK
    ngrp = 2 * npair
    rc = S5_RELAYOUT_ROWS
    lane_chunk = lax.broadcasted_iota(jnp.int32, (rc, LANES), 1) // S5_GROUP

    def chunk_transpose(arrs):
        arrs = list(arrs)
        s = ngrp // 2
        while s:
            upper = (lane_chunk & s) != 0
            for x in range(ngrp):
                if x & s == 0:
                    ax, ay = arrs[x], arrs[x + s]
                    arrs[x] = jnp.where(upper, pltpu.roll(ay, s * S5_GROUP, 1), ax)
                    arrs[x + s] = jnp.where(upper, ay, pltpu.roll(ax, LANES - s * S5_GROUP, 1))
            s //= 2
        return arrs

    def relayout_in(c, carry):
        t0 = pl.multiple_of(c * (rc * blk), rc * blk)
        r0 = pl.multiple_of(c * rc, rc)
        stage_ref[...] = p_ref[pl.ds(t0, rc * blk), :].astype(F32)
        for half in range(blk // ngrp):
            z = chunk_transpose(stage_ref[pl.ds(ngrp * half + i, rc, stride=blk), :] for i in range(ngrp))
            for g in range(ngrp):
                u_ref[g, pl.ds(r0, rc), half * LANES:(half + 1) * LANES] = z[g].astype(BF16)
        return carry

    lax.fori_loop(0, rows // rc, relayout_in, 0)
    lane = lax.broadcasted_iota(jnp.int32, (S5_GROUP, wlane), 1)

    for g in range(2 * npair):
        kt = kt_ref[g]
        blocks = [kt] + [jnp.where(lane >= i * S5_GROUP, pltpu.roll(kt, i * S5_GROUP, 1), 0.0)
                         for i in range(1, S5_BLOCK)]
        kmat = jnp.concatenate(blocks, axis=0).astype(BF16)
        yp_ref[g] = _dg(u_ref[g], kmat, NN)
    for p in range(npair):
        ucat = jnp.concatenate([u_ref[2 * p], u_ref[2 * p + 1]], axis=1)
        gre_ref[p] = _dg(ucat, wre_ref[p], NN)
        gim_ref[p] = _dg(ucat, wim_ref[p], NN)

    def cma(xr, xi, ar, ai, sr, si):
        return xr + (ar * sr - ai * si), xi + (ar * si + ai * sr)

    row = lax.broadcasted_iota(jnp.int32, (SUBLANES, LANES), 0)
    per_batch = rows // nbatch

    def tile(t, carry, base):
        r0 = pl.multiple_of(base + t * SUBLANES, SUBLANES)
        new_carry = []
        for p in range(npair):
            xr = gre_ref[p, pl.ds(r0, SUBLANES), :]
            xi = gim_ref[p, pl.ds(r0, SUBLANES), :]
            for lvl, sh in enumerate((1, 2, 4)):
                ar = cre_ref[p, lvl * SUBLANES:(lvl + 1) * SUBLANES, :]
                ai = cim_ref[p, lvl * SUBLANES:(lvl + 1) * SUBLANES, :]
                xr, xi = cma(xr, xi, ar, ai, pltpu.roll(xr, sh, 0), pltpu.roll(xi, sh, 0))
            ar = cre_ref[p, 3 * SUBLANES:4 * SUBLANES, :]
            ai = cim_ref[p, 3 * SUBLANES:4 * SUBLANES, :]
            cr, ci = carry[2 * p], carry[2 * p + 1]
            xr, xi = cma(xr, xi, ar, ai, cr, ci)
            xre_ref[p, pl.ds(r0, SUBLANES), :] = jnp.where(row == 0, cr, pltpu.roll(xr, 1, 0))
            xim_ref[p, pl.ds(r0, SUBLANES), :] = jnp.where(row == 0, ci, pltpu.roll(xi, 1, 0))
            new_carry.append(jnp.broadcast_to(xr[SUBLANES - 1:SUBLANES, :], (SUBLANES, LANES)))
            new_carry.append(jnp.broadcast_to(xi[SUBLANES - 1:SUBLANES, :], (SUBLANES, LANES)))
        return tuple(new_carry)

    zero = jnp.zeros((SUBLANES, LANES), F32)
    for b in range(nbatch):
        lax.fori_loop(0, per_batch // SUBLANES, functools.partial(tile, base=b * per_batch), (zero,) * (2 * npair))

    for p in range(npair):
        xs = jnp.concatenate([xre_ref[p], xim_ref[p]], axis=1).astype(BF16)
        corr = _dg(xs, vt_ref[p], NT)
        for k in range(2):
            g = 2 * p + k
            y = yp_ref[g] + corr[:, k * wlane:(k + 1) * wlane] + d_ref[g] * u_ref[g].astype(F32)
            yp_ref[g] = jax.nn.gelu(y)

    def relayout_out(c, carry):
        t0 = pl.multiple_of(c * (rc * blk), rc * blk)
        r0 = pl.multiple_of(c * rc, rc)
        for half in range(blk // ngrp):
            yt = chunk_transpose(yp_ref[g, pl.ds(r0, rc), half * LANES:(half + 1) * LANES] for g in range(ngrp))
            for j in range(ngrp):
                stage_ref[pl.ds(ngrp * half + j, rc, stride=blk), :] = yt[j]
        o_ref[pl.ds(t0, rc * blk), :] = stage_ref[...].astype(o_ref.dtype)
        return carry

    lax.fori_loop(0, rows // rc, relayout_out, 0)


def s5_ssm(p, batch, seqlen, lam_re, lam_im, log_dt, b_re, b_im, c_re, c_im, d_skip):
    ngroup, nstate = lam_re.shape
    h, blk = S5_GROUP, S5_BLOCK
    width = ngroup * h
    wl = blk * h
    nb = batch * seqlen // blk
    gp = S5_PREP_GROUPS
    assert nstate == S5_STATE and 2 * nstate == LANES
    assert seqlen % (blk * SUBLANES) == 0 and ngroup % gp == 0 and ngroup % (2 * S5_PAIRS) == 0

    g3 = lambda x: x.reshape(ngroup, 1, nstate)
    bt = lambda x: jnp.swapaxes(x, 1, 2)
    spec1 = pl.BlockSpec((gp, 1, nstate), lambda i: (i, 0, 0))
    spec_c = pl.BlockSpec((gp, h, nstate), lambda i: (i, 0, 0))
    spec_w = pl.BlockSpec((gp, wl, nstate), lambda i: (i, 0, 0))
    spec_q = pl.BlockSpec((gp, SUBLANES, nstate), lambda i: (i, 0, 0))
    spec_k = pl.BlockSpec((gp, h, wl), lambda i: (i, 0, 0))
    f = lambda *s: jax.ShapeDtypeStruct(s, F32)
    kt, w_re, w_im, v_re, v_im, q_re, q_im = pl.pallas_call(
        _s5_prep_kernel,
        grid=(ngroup // gp,),
        in_specs=[spec1, spec1, spec1, spec_c, spec_c, spec_c, spec_c],
        out_specs=[spec_k, spec_w, spec_w, spec_w, spec_w, spec_q, spec_q],
        out_shape=[f(ngroup, h, wl), f(ngroup, wl, nstate), f(ngroup, wl, nstate), f(ngroup, wl, nstate),
                   f(ngroup, wl, nstate), f(ngroup, SUBLANES, nstate), f(ngroup, SUBLANES, nstate)],
        scratch_shapes=[pltpu.VMEM((gp, (blk + 1) * h, nstate), F32), pltpu.VMEM((gp, (blk + 1) * h, nstate), F32)],
        compiler_params=_cparams(("parallel",)),
        name="s5_prep",
    )(g3(lam_re), g3(lam_im), jnp.broadcast_to(log_dt[:, None, None], (ngroup, 1, nstate)), c_re, c_im,
      bt(b_re), bt(b_im))

    npairs = ngroup // 2

    def pair_rows(x):
        x = x.reshape(npairs, 2, x.shape[1], nstate)
        z = jnp.zeros_like(x[:, 0])
        return jnp.concatenate([jnp.concatenate([x[:, 0], z], axis=2), jnp.concatenate([z, x[:, 1]], axis=2)], axis=1)

    w_re2 = pair_rows(w_re).astype(BF16)
    w_im2 = pair_rows(w_im).astype(BF16)
    vt2 = jnp.concatenate([pair_rows(v_re), pair_rows(v_im)], axis=2).astype(BF16)

    def pair_lanes(x):
        x = x.reshape(npairs, 2, SUBLANES, nstate)
        return jnp.concatenate([x[:, 0], x[:, 1]], axis=2)

    rows = jnp.arange(SUBLANES)[None, :, None]

    def scan_consts(q):
        q = pair_lanes(q)
        levels = [jnp.where(rows >= sh, q[:, sh - 1:sh, :], 0.0) for sh in (1, 2, 4)]
        return jnp.concatenate(levels + [q], axis=1)

    c_re2, c_im2 = scan_consts(q_re), scan_consts(q_im)
    d_row = jnp.tile(d_skip.reshape(ngroup, 1, h), (1, 1, blk))

    np_ = S5_PAIRS
    gs = 2 * np_
    assert gs * h == LANES and blk % gs == 0 and nb % S5_RELAYOUT_ROWS == 0
    tok = pl.BlockSpec((nb * blk, LANES), lambda i: (0, i))
    grp = lambda r, c: pl.BlockSpec((gs, r, c), lambda i: (i, 0, 0))
    par = lambda r, c: pl.BlockSpec((np_, r, c), lambda i: (i, 0, 0))
    return pl.pallas_call(
        functools.partial(_s5_kernel, batch),
        grid=(ngroup // gs,),
        in_specs=[tok, grp(h, wl), par(2 * wl, LANES), par(2 * wl, LANES), par(2 * wl, 2 * LANES),
                  par(4 * SUBLANES, LANES), par(4 * SUBLANES, LANES), grp(1, wl)],
        out_specs=tok,
        out_shape=jax.ShapeDtypeStruct((batch * seqlen, width), BF16),
        scratch_shapes=[pltpu.VMEM((gs, nb, wl), BF16)] + [pltpu.VMEM((np_, nb, LANES), F32)] * 4
        + [pltpu.VMEM((gs, nb, wl), F32), pltpu.VMEM((S5_RELAYOUT_ROWS * blk, LANES), F32)],
        compiler_params=_cparams(("parallel",)),
        name="s5_ssm",
    )(p, kt, w_re2, w_im2, vt2, c_re2, c_im2, d_row)


def _glu_kernel(y_ref, yj_ref, gj_ref, w_ref, b_ref, o_ref):
    z = jnp.dot(y_ref[...], w_ref[...].astype(BF16), preferred_element_type=F32) + b_ref[...]
    gate = gj_ref[...].astype(F32)
    yj = yj_ref[...].astype(F32)
    o_ref[...] = (yj * jax.nn.sigmoid(z) * (gate * jax.nn.sigmoid(gate))).astype(o_ref.dtype)


def glu_gate(y, p, w, b, tm=1024, tn=1024):
    m, d = y.shape
    tm = min(tm, m)
    goff = d // tn
    return pl.pallas_call(
        _glu_kernel,
        grid=(m // tm, d // tn),
        in_specs=[pl.BlockSpec((tm, d), lambda i, j: (i, 0)),
                  pl.BlockSpec((tm, tn), lambda i, j: (i, j)),
                  pl.BlockSpec((tm, tn), lambda i, j: (i, goff + j)),
                  pl.BlockSpec((d, tn), lambda i, j: (0, j)),
                  pl.BlockSpec((1, tn), lambda i, j: (0, j))],
        out_specs=pl.BlockSpec((tm, tn), lambda i, j: (i, j)),
        out_shape=jax.ShapeDtypeStruct((m, d), BF16),
        compiler_params=_cparams(("parallel", "arbitrary")),
        name="glu_gate",
    )(y, y, p, w, b.reshape(1, d))


def _final_kernel(a_ref, h_ref, w_ref, g_ref, o_ref):
    h = h_ref[...] + jnp.dot(a_ref[...], w_ref[...], preferred_element_type=F32)
    ms = jnp.mean(h * h, axis=-1, keepdims=True)
    o_ref[...] = (h * lax.rsqrt(ms + RMS_EPS)) * g_ref[...]


def final_out(a, h, w, g, tm=512):
    m, d = h.shape
    tm = min(tm, m)
    row = pl.BlockSpec((tm, d), lambda i: (i, 0))
    return pl.pallas_call(
        _final_kernel,
        grid=(m // tm,),
        in_specs=[row, row, pl.BlockSpec((d, d), lambda i: (0, 0), pipeline_mode=pl.Buffered(1)),
                  pl.BlockSpec((1, d), lambda i: (0, 0))],
        out_specs=row,
        out_shape=jax.ShapeDtypeStruct((m, d), F32),
        compiler_params=_cparams(("parallel",)),
        name="final_out",
    )(a, h, w, g.reshape(1, d))


def kernel(x, norm_g, final_g, ab_w_in, rwkv_shift_mix, rwkv_w_up, rwkv_w0, rwkv_a_up, rwkv_a0, rwkv_k_k, rwkv_k_a, rwkv_r_k, rwkv_gn_w, rwkv_gn_b, ab_w_out, s5_w_in, s5_lam_re, s5_lam_im, s5_log_dt, s5_b_re, s5_b_im, s5_c_re, s5_c_im, s5_d, s5_w_glu, s5_b_glu, s5_w_out):
    batch, seqlen, d = x.shape
    m = batch * seqlen
    rwkv_w = rwkv_w0.shape[1]
    rwkv_proj = 3 * rwkv_w + 2 * LORA_RANK
    sb_w = (ab_w_in.shape[2] - rwkv_proj - d) // 3
    x2 = x.reshape(m, d)

    w_in = ab_w_in[0]
    scale = HEAD_DIM ** -0.5
    col_scale = jnp.concatenate([jnp.full((sb_w,), scale, F32), jnp.ones((2 * sb_w + d,), F32)])
    xn = rmsnorm(x2, norm_g[0], BF16)
    p_rkv = matmul(xn, w_in, F32, 0, 3 * rwkv_w, name="proj_rkv")
    p_lora = matmul(xn, w_in, F32, 3 * rwkv_w, 2 * LORA_RANK, tn=2 * LORA_RANK, name="proj_lora")
    p_sb = matmul(xn, w_in, BF16, rwkv_proj, 3 * sb_w + d, col_scale, tm=2048, name="proj_sb_gate")
    y_a = rwkv7(p_rkv, p_lora, batch, seqlen, rwkv_shift_mix[0], rwkv_w_up[0], rwkv_w0[0], rwkv_a_up[0], rwkv_a0[0],
                rwkv_k_k[0], rwkv_k_a[0], rwkv_r_k[0], rwkv_gn_w[0], rwkv_gn_b[0])
    y_b = stick_breaking(p_sb, batch, seqlen, sb_w)
    h1, hn1 = gate_out(y_a, y_b, p_sb, 3 * sb_w, x2, ab_w_out[0].astype(BF16), norm_g[1])

    p1 = matmul(hn1, s5_w_in[0], BF16, 0, 2 * d, tm=2048, name="proj_s5")
    y_s5 = s5_ssm(p1, batch, seqlen, s5_lam_re[0], s5_lam_im[0], s5_log_dt[0], s5_b_re[0], s5_b_im[0],
                  s5_c_re[0], s5_c_im[0], s5_d[0])
    act = glu_gate(y_s5, p1, s5_w_glu[0], s5_b_glu[0])
    out = final_out(act, h1, s5_w_out[0].astype(BF16), final_g)
    return out.reshape(batch, seqlen, d)
```

```python
import functools
import math

import jax
import jax.numpy as jnp
from jax import lax
from jax.experimental import pallas as pl
from jax.experimental.pallas import tpu as pltpu

F32 = jnp.float32
BF16 = jnp.bfloat16

HEAD_DIM = 64
LANES = 128
SUBLANES = 8
LORA_RANK = 64
S5_GROUP = 16
S5_STATE = 64
RMS_EPS = 1e-6
GN_EPS = 64e-5
DECAY_SCALE = math.exp(-0.5)

RWKV_CHUNK = 64
RWKV_PAIRS = 8
SB_BLOCK = 128
SB_QBLOCK = 256
SB_PAIRS = 4
S5_BLOCK = 16
S5_PAIRS = 4
S5_PREP_GROUPS = 16
S5_RELAYOUT_ROWS = 64
VMEM_LIMIT = 56 * 1024 * 1024

NN = (((1,), (0,)), ((), ()))
NT = (((1,), (1,)), ((), ()))
TN = (((0,), (0,)), ((), ()))


def _cparams(sem):
    return pltpu.CompilerParams(dimension_semantics=sem, vmem_limit_bytes=VMEM_LIMIT)


def _split(x):
    hi = x.astype(BF16)
    lo = (x - hi.astype(F32)).astype(BF16)
    return hi, lo


def _dg(a, b, dn):
    return lax.dot_general(a, b, dn, preferred_element_type=F32)


def _dot(a, b, dn=NN, passes=3):
    if passes == 1:
        return _dg(a.astype(BF16), b.astype(BF16), dn)
    ah, al = _split(a)
    bh, bl = _split(b)
    return _dg(ah, bh, dn) + (_dg(ah, bl, dn) + _dg(al, bh, dn))


def _dot_rhs_exact(a, b01, passes=3):
    ah = a.astype(BF16)
    out = _dg(ah, b01, NN)
    rem = a - ah.astype(F32)
    for _ in range(passes - 1):
        rh = rem.astype(BF16)
        out = out + _dg(rh, b01, NN)
        rem = rem - rh.astype(F32)
    return out


def _dot_lhs_exact(a01, b, passes=3):
    bh = b.astype(BF16)
    out = _dg(a01, bh, NN)
    rem = b - bh.astype(F32)
    for _ in range(passes - 1):
        rh = rem.astype(BF16)
        out = out + _dg(a01, rh, NN)
        rem = rem - rh.astype(F32)
    return out


def _rmsnorm_kernel(x_ref, g_ref, o_ref):
    x = x_ref[...]
    ms = jnp.mean(x * x, axis=-1, keepdims=True)
    o_ref[...] = ((x * lax.rsqrt(ms + RMS_EPS)) * g_ref[...]).astype(o_ref.dtype)


def rmsnorm(x, g, out_dtype, tm=512):
    m, d = x.shape
    tm = min(tm, m)
    return pl.pallas_call(
        _rmsnorm_kernel,
        grid=(m // tm,),
        in_specs=[pl.BlockSpec((tm, d), lambda i: (i, 0)), pl.BlockSpec((1, d), lambda i: (0, 0))],
        out_specs=pl.BlockSpec((tm, d), lambda i: (i, 0)),
        out_shape=jax.ShapeDtypeStruct((m, d), out_dtype),
        compiler_params=_cparams(("parallel",)),
        name="rmsnorm",
    )(x, g.reshape(1, d))


def _mm_kernel(a_ref, w_ref, o_ref):
    o_ref[...] = jnp.dot(a_ref[...], w_ref[...].astype(BF16), preferred_element_type=F32).astype(o_ref.dtype)


def _mm_scaled_kernel(a_ref, w_ref, s_ref, o_ref):
    acc = jnp.dot(a_ref[...], w_ref[...].astype(BF16), preferred_element_type=F32)
    o_ref[...] = (acc * s_ref[...]).astype(o_ref.dtype)


def matmul(a, w, out_dtype, col0, n, col_scale=None, tm=1024, tn=1024, name="matmul"):
    m, k = a.shape
    tm = min(tm, m)
    assert m % tm == 0 and n % tn == 0 and col0 % LANES == 0
    in_specs = [pl.BlockSpec((tm, k), lambda i, j: (i, 0)),
                pl.BlockSpec((pl.Element(k), pl.Element(tn)), lambda i, j: (0, pl.multiple_of(col0 + j * tn, LANES)))]
    args = [a, w]
    if col_scale is not None:
        in_specs.append(pl.BlockSpec((1, tn), lambda i, j: (0, j)))
        args.append(col_scale.reshape(1, n))
    return pl.pallas_call(
        _mm_kernel if col_scale is None else _mm_scaled_kernel,
        grid=(m // tm, n // tn),
        in_specs=in_specs,
        out_specs=pl.BlockSpec((tm, tn), lambda i, j: (i, j)),
        out_shape=jax.ShapeDtypeStruct((m, n), out_dtype),
        compiler_params=_cparams(("parallel", "arbitrary")),
        name=name,
    )(*args)


def _rwkv_kernel(r_ref, k_ref, v_ref, lo_ref, mr_ref, mk_ref, mv_ref, mlo_ref, wup_ref, aup_ref,
                 w0_ref, a0_ref, kk_ref, ka_ref, rk_ref, gnw_ref, gnb_ref,
                 seg_ref, tri_ref, strict_ref, incl_ref, eye_ref,
                 y_ref, s_ref, prev_ref):
    c = pl.program_id(2)
    ch = RWKV_CHUNK
    npair = r_ref.shape[1] // LANES

    @pl.when(c == 0)
    def _():
        s_ref[...] = jnp.zeros_like(s_ref)
        prev_ref[...] = jnp.zeros_like(prev_ref)

    def token_shift(x, idx, mix):
        row = lax.broadcasted_iota(jnp.int32, x.shape, 0)
        prev = prev_ref[idx:idx + 1, 0:x.shape[1]]
        shifted = jnp.where(row == 0, prev, pltpu.roll(x, 1, 0))
        return x + (shifted - x) * mix

    r_in, k_in, v_in, lo_in = r_ref[...], k_ref[...], v_ref[...], lo_ref[...]
    r = token_shift(r_in, 0, mr_ref[...])
    k = token_shift(k_in, 1, mk_ref[...])
    v = token_shift(v_in, 2, mv_ref[...])
    lo = token_shift(lo_in, 3, mlo_ref[...])
    prev_ref[0:1, :] = r_in[ch - 1:ch, :]
    prev_ref[1:2, :] = k_in[ch - 1:ch, :]
    prev_ref[2:3, :] = v_in[ch - 1:ch, :]
    prev_ref[3:4, 0:LANES] = lo_in[ch - 1:ch, :]

    z_w = w0_ref[...] + _dot(jnp.tanh(lo), wup_ref[...], passes=1)
    logw = -DECAY_SCALE * jax.nn.sigmoid(z_w)
    a = jax.nn.sigmoid(a0_ref[...] + _dot(lo, aup_ref[...], passes=1))

    seg2 = seg_ref[...]

    def head_sum(x):
        tiles = []
        for t in range(npair):
            hi, lo_ = _split(x[:, t * LANES:(t + 1) * LANES])
            tiles.append(_dg(jnp.concatenate([hi, lo_], axis=1), seg2, NN))
        return jnp.concatenate(tiles, axis=1)

    kk = k * kk_ref[...]
    kk = kk * lax.rsqrt(jnp.maximum(head_sum(kk * kk), 1e-24))
    k2 = k * (1.0 + (a - 1.0) * ka_ref[...])
    ab = kk * a

    l_hi = logw.astype(BF16)
    rem = logw - l_hi.astype(F32)
    l_mid = rem.astype(BF16)
    l_lo = (rem - l_mid.astype(F32)).astype(BF16)
    cum = _dg(tri_ref[...], jnp.concatenate([l_hi, l_mid, l_lo], axis=0), NN)
    cum_last = cum[ch - 1:ch, :]
    e_cum = jnp.exp(cum)
    e_ncum = jnp.exp(-cum)
    e_tail = jnp.exp(cum_last - cum)
    rt = r * e_cum
    kt = k2 * e_ncum
    bt = ab * e_ncum
    at = -kk * jnp.exp(cum - logw)
    khat = k2 * e_tail
    bhat = ab * e_tail
    p_last = e_cum[ch - 1:ch, :]

    lane = lax.broadcasted_iota(jnp.int32, (ch, LANES), 1)
    head0 = lane < HEAD_DIM
    strict = strict_ref[...] > 0.5
    incl = incl_ref[...] > 0.5
    eye = eye_ref[...]

    def stack(x):
        return jnp.concatenate([jnp.where(head0, x, 0.0), jnp.where(head0, 0.0, x)], axis=0)

    def mm(x, w, dn=NN):
        return _dg(x.astype(BF16), w.astype(BF16), dn)

    pairs = range(npair)
    rows = 2 * ch
    sl = [slice(p * LANES, (p + 1) * LANES) for p in pairs]
    at2, rt2, bt2, kt2, v2, khat2, bhat2 = ([stack(t[:, sl[p]]) for p in pairs]
                                            for t in (at, rt, bt, kt, v, khat, bhat))
    gram = []
    for p in pairs:
        lh, ll = _split(jnp.concatenate([at2[p], rt2[p]], axis=0))
        rh, rl = _split(jnp.concatenate([bt2[p], kt2[p]], axis=0))
        gram.append(_dg(jnp.concatenate([lh, ll], axis=1), jnp.concatenate([rh, rh], axis=1), NT))
    a_ab = [jnp.where(strict, g[:rows, :rows], 0.0) for g in gram]
    a_ak = [jnp.where(strict, g[:rows, rows:], 0.0) for g in gram]
    a_rb = [jnp.where(incl, g[rows:, :rows], 0.0) for g in gram]
    a_rk = [jnp.where(incl, g[rows:, rows:], 0.0) for g in gram]
    akv = [mm(a_ak[p], v2[p]) for p in pairs]

    tinv = [eye + a for a in a_ab]
    pw = [mm(a, a) for a in a_ab]
    for _ in range(int(math.log2(ch)) - 2):
        both = [mm(pw[p], jnp.concatenate([pw[p], tinv[p]], axis=1)) for p in pairs]
        pw = [b[:, :rows] for b in both]
        tinv = [tinv[p] + both[p][:, rows:] for p in pairs]
    tinv = [tinv[p] + mm(pw[p], tinv[p]) for p in pairs]

    s = [s_ref[p] for p in pairs]
    tw = [mm(tinv[p], jnp.concatenate([at2[p], akv[p]], axis=1)) for p in pairs]
    ws = [mm(jnp.concatenate([tw[p][:, :LANES], rt2[p]], axis=0), s[p], NT) for p in pairs]
    uv = [jnp.concatenate([ws[p][:rows] + tw[p][:, LANES:], v2[p]], axis=0) for p in pairs]
    y2 = [ws[p][rows:] + mm(jnp.concatenate([a_rb[p], a_rk[p]], axis=1), uv[p]) for p in pairs]
    for p in pairs:
        s_ref[p] = s[p] * p_last[:, sl[p]] + mm(uv[p], jnp.concatenate([bhat2[p], khat2[p]], axis=0), TN)
    y = jnp.concatenate([t[:ch, :] + t[ch:, :] for t in y2], axis=1)

    inv_n = 1.0 / HEAD_DIM
    mu = head_sum(y) * inv_n
    d = y - mu
    var = head_sum(d * d) * inv_n
    yn = d * lax.rsqrt(var + GN_EPS) * gnw_ref[...] + gnb_ref[...]
    y_ref[...] = (yn + head_sum(r * k2 * rk_ref[...]) * v).astype(y_ref.dtype)


def rwkv7(p, p_lora, batch, seqlen, shift_mix, w_up, w0, a_up, a0, k_k, k_a, r_k, gn_w, gn_b):
    width = w0.shape[0]
    npair = width // LANES
    ch = RWKV_CHUNK
    nchunk = seqlen // ch
    assert seqlen % ch == 0 and 2 * LORA_RANK == LANES and 2 * HEAD_DIM == LANES
    zeros = jnp.zeros((LORA_RANK, width), F32)
    wup_pad = jnp.concatenate([w_up, zeros], axis=0)
    aup_pad = jnp.concatenate([zeros, a_up], axis=0)
    hb = RWKV_PAIRS
    assert npair % hb == 0
    idx = jnp.arange(LANES)
    seg = (idx[:, None] // HEAD_DIM == idx[None, :] // HEAD_DIM).astype(BF16)
    seg = jnp.concatenate([seg, seg], axis=0)
    t = jnp.arange(ch)
    tri = (t[None, :] <= t[:, None]).astype(BF16)
    tri = jnp.concatenate([tri, tri, tri], axis=1)
    i2 = jnp.arange(2 * ch)
    same = (i2[:, None] // ch) == (i2[None, :] // ch)
    strict = (same & ((i2[None, :] % ch) < (i2[:, None] % ch))).astype(F32)
    incl = (same & ((i2[None, :] % ch) <= (i2[:, None] % ch))).astype(F32)
    eye = jnp.eye(2 * ch, dtype=F32)
    row2 = lambda x: x.reshape(1, -1)

    wd = hb * LANES
    ngrp = npair // hb

    def tok(off):
        return pl.BlockSpec((ch, wd), lambda b, h, c: (b * nchunk + c, off + h))

    def par(off):
        return pl.BlockSpec((1, wd), lambda b, h, c: (0, off + h))

    def const(shape):
        return pl.BlockSpec(shape, lambda b, h, c: (0, 0))

    up = pl.BlockSpec((LANES, wd), lambda b, h, c: (0, h))
    lora_tok = pl.BlockSpec((ch, LANES), lambda b, h, c: (b * nchunk + c, 0))
    lora_par = pl.BlockSpec((1, LANES), lambda b, h, c: (0, 3 * npair))
    return pl.pallas_call(
        _rwkv_kernel,
        grid=(batch, ngrp, nchunk),
        in_specs=[tok(0), tok(ngrp), tok(2 * ngrp), lora_tok,
                  par(0), par(ngrp), par(2 * ngrp), lora_par, up, up,
                  par(0), par(0), par(0), par(0), par(0), par(0), par(0),
                  const((2 * LANES, LANES)), const((ch, 3 * ch)), const((2 * ch, 2 * ch)),
                  const((2 * ch, 2 * ch)), const((2 * ch, 2 * ch))],
        out_specs=pl.BlockSpec((ch, wd), lambda b, h, c: (b * nchunk + c, h)),
        out_shape=jax.ShapeDtypeStruct((batch * seqlen, width), BF16),
        scratch_shapes=[pltpu.VMEM((hb, LANES, LANES), F32), pltpu.VMEM((SUBLANES, wd), F32)],
        compiler_params=_cparams(("parallel", "parallel", "arbitrary")),
        name="rwkv7",
    )(p, p, p, p_lora, row2(shift_mix), row2(shift_mix), row2(shift_mix), row2(shift_mix), wup_pad, aup_pad,
      row2(w0), row2(a0), row2(k_k), row2(k_a), row2(r_k), row2(gn_w), row2(gn_b),
      seg, tri, strict, incl, eye)


def _sb_kernel(q_ref, k_ref, v_ref, uo_ref, o_ref, k2_ref, v2_ref, acc_ref, carry_ref, lb_ref, sums_ref, z_ref):
    qi = pl.program_id(2)
    kb, qb = SB_BLOCK, SB_QBLOCK
    nsub = qb // kb
    npair = q_ref.shape[1] // LANES
    pairs_ = range(npair)

    @pl.when(qi == 0)
    def _():
        lane = lax.broadcasted_iota(jnp.int32, (kb, LANES), 1)
        head0 = lane < HEAD_DIM

        def fill(i, c):
            start = pl.multiple_of(i * kb, kb)
            for p in pairs_:
                for src, dst in ((k_ref, k2_ref), (v_ref, v2_ref)):
                    t = src[pl.ds(start, kb), p * LANES:(p + 1) * LANES].astype(F32)
                    dst[p, i, 0:kb, :] = jnp.where(head0, t, 0.0).astype(BF16)
                    dst[p, i, kb:2 * kb, :] = jnp.where(head0, 0.0, t).astype(BF16)
            return c

        lax.fori_loop(0, k_ref.shape[0] // kb, fill, 0)

    q = [q_ref[:, p * LANES:(p + 1) * LANES] for p in pairs_]
    uo = uo_ref[...]
    acc_ref[...] = jnp.zeros_like(acc_ref)
    carry_ref[...] = jnp.zeros_like(carry_ref)
    tpos = lax.broadcasted_iota(jnp.int32, (qb, kb), 0)
    spos = lax.broadcasted_iota(jnp.int32, (qb, kb), 1)

    def logits(p, sb, zslot):
        base = pl.multiple_of(sb * nsub, nsub)
        keys = k2_ref[p, pl.ds(base, nsub)].reshape(nsub * 2 * kb, LANES)
        z_ref[zslot, p] = _dg(q[p], keys, NT)

    def scores(p, zslot, slot, diagonal):
        for j in range(nsub):
            for h in range(2):
                c = 2 * j + h
                z = z_ref[zslot, p, :, c * kb:(c + 1) * kb]
                nz = -z
                log_keep = jnp.minimum(nz, 0.0) - jnp.log(1.0 + jnp.exp(jnp.minimum(z, nz)))
                if diagonal:
                    causal = (spos + j * kb) < tpos
                    log_keep = jnp.where(causal, log_keep, 0.0)
                    z = jnp.where(causal, z, -jnp.inf)
                hi = log_keep.astype(BF16)
                lo = (log_keep - hi.astype(F32)).astype(BF16)
                lb_ref[slot, p, c] = z
                sums_ref[slot, p, c] = _dg(jnp.concatenate([hi, lo], axis=1), uo, NN)

    def accumulate(p, sb, slot):
        base = pl.multiple_of(sb * nsub, nsub)
        attn = {}
        for h in range(2):
            carry = carry_ref[p, h]
            for j in reversed(range(nsub)):
                s = sums_ref[slot, p, 2 * j + h]
                attn[j, h] = jnp.exp(lb_ref[slot, p, 2 * j + h] + carry + s[:, :kb]).astype(BF16)
                carry = carry + s[:, kb:]
            carry_ref[p, h] = carry
        weights = jnp.concatenate([attn[j, h] for j in range(nsub) for h in range(2)], axis=1)
        values = v2_ref[p, pl.ds(base, nsub)].reshape(nsub * 2 * kb, LANES)
        acc_ref[p] += _dg(weights, values, NN)

    for p in pairs_:
        logits(p, qi, 0)
    for p in pairs_:
        scores(p, 0, 0, True)
    trips = qi // 2

    def body(i, c):
        sb = qi - 1 - 2 * i
        for p in pairs_:
            logits(p, sb, 0)
        for p in pairs_:
            logits(p, sb - 1, 1)
        for p in pairs_:
            accumulate(p, sb + 1, 0)
        for p in pairs_:
            scores(p, 0, 1, False)
        for p in pairs_:
            scores(p, 1, 0, False)
        for p in pairs_:
            accumulate(p, sb, 1)
        return c

    lax.fori_loop(0, trips, body, 0)
    odd = qi - 2 * trips == 1

    @pl.when(odd)
    def _():
        for p in pairs_:
            logits(p, 0, 0)
        for p in pairs_:
            accumulate(p, 1, 0)
        for p in pairs_:
            scores(p, 0, 1, False)
        for p in pairs_:
            accumulate(p, 0, 1)

    @pl.when(jnp.logical_not(odd))
    def _():
        for p in pairs_:
            accumulate(p, 0, 0)

    for p in pairs_:
        o_ref[:, p * LANES:(p + 1) * LANES] = acc_ref[p].astype(o_ref.dtype)


def stick_breaking(p, batch, seqlen, width):
    npair = width // LANES
    kb, qb = SB_BLOCK, SB_QBLOCK
    nq = seqlen // qb
    hp = SB_PAIRS
    ngrp = npair // hp
    wd = hp * LANES
    nsl = 2 * qb // kb
    j = jnp.arange(kb)
    later = (j[:, None] >= j[None, :]).astype(BF16)
    uo = jnp.concatenate([later, jnp.ones((kb, kb), BF16)], axis=1)
    uo = jnp.concatenate([uo, uo], axis=0)
    return pl.pallas_call(
        _sb_kernel,
        grid=(batch, ngrp, nq),
        in_specs=[pl.BlockSpec((qb, wd), lambda b, h, i: (b * nq + i, h)),
                  pl.BlockSpec((seqlen, wd), lambda b, h, i: (b, ngrp + h), pipeline_mode=pl.Buffered(1)),
                  pl.BlockSpec((seqlen, wd), lambda b, h, i: (b, 2 * ngrp + h), pipeline_mode=pl.Buffered(1)),
                  pl.BlockSpec((2 * kb, 2 * kb), lambda b, h, i: (0, 0))],
        out_specs=pl.BlockSpec((qb, wd), lambda b, h, i: (b * nq + i, h)),
        out_shape=jax.ShapeDtypeStruct((batch * seqlen, width), BF16),
        scratch_shapes=[pltpu.VMEM((hp, seqlen // kb, 2 * kb, LANES), BF16),
                        pltpu.VMEM((hp, seqlen // kb, 2 * kb, LANES), BF16),
                        pltpu.VMEM((hp, qb, LANES), F32), pltpu.VMEM((hp, 2, qb, kb), F32),
                        pltpu.VMEM((2, hp, nsl, qb, kb), F32), pltpu.VMEM((2, hp, nsl, qb, 2 * kb), F32),
                        pltpu.VMEM((2, hp, qb, nsl * kb), F32)],
        compiler_params=_cparams(("parallel", "parallel", "arbitrary")),
        name="stick_breaking",
    )(p, p, p, uo)


def _gate_out_kernel(ya_ref, yb_ref, g0_ref, g1_ref, x_ref, w_ref, ng_ref, h_ref, hn_ref):
    gate = jnp.concatenate([g0_ref[...], g1_ref[...]], axis=-1).astype(F32)
    y = jnp.concatenate([ya_ref[...], yb_ref[...]], axis=-1).astype(F32) * (gate * jax.nn.sigmoid(gate))
    h = x_ref[...] + jnp.dot(y.astype(BF16), w_ref[...], preferred_element_type=F32)
    h_ref[...] = h
    ms = jnp.mean(h * h, axis=-1, keepdims=True)
    hn_ref[...] = ((h * lax.rsqrt(ms + RMS_EPS)) * ng_ref[...]).astype(hn_ref.dtype)


def gate_out(ya, yb, p, gate_col, x, w, next_g, tm=512):
    m, d = x.shape
    half = ya.shape[1]
    tm = min(tm, m)
    assert gate_col % half == 0 and d == 2 * half
    g0 = gate_col // half
    row = lambda n: pl.BlockSpec((tm, n), lambda i: (i, 0))
    return pl.pallas_call(
        _gate_out_kernel,
        grid=(m // tm,),
        in_specs=[row(half), row(half),
                  pl.BlockSpec((tm, half), lambda i: (i, g0)), pl.BlockSpec((tm, half), lambda i: (i, g0 + 1)), row(d),
                  pl.BlockSpec((d, d), lambda i: (0, 0), pipeline_mode=pl.Buffered(1)),
                  pl.BlockSpec((1, d), lambda i: (0, 0))],
        out_specs=[row(d), row(d)],
        out_shape=[jax.ShapeDtypeStruct((m, d), F32), jax.ShapeDtypeStruct((m, d), BF16)],
        compiler_params=_cparams(("parallel",)),
        name="gate_out",
    )(ya, yb, p, p, x, w, next_g.reshape(1, d))


def _s5_prep_kernel(lre_ref, lim_ref, ldt_ref, cre_ref, cim_ref, btre_ref, btim_ref,
                    kt_ref, wre_ref, wim_ref, vre_ref, vim_ref, qre_ref, qim_ref, ckre_ref, ckim_ref):
    blk = S5_BLOCK
    h = S5_GROUP
    lre, lim = lre_ref[...], lim_ref[...]
    dt = jnp.exp(ldt_ref[...])
    mag = jnp.exp(lre * dt)
    bre = mag * jnp.cos(lim * dt)
    bim = mag * jnp.sin(lim * dt)
    den = lre * lre + lim * lim
    nre, nim = bre - 1.0, bim
    fre = (nre * lre + nim * lim) / den
    fim = (nim * lre - nre * lim) / den
    cre, cim = cre_ref[...], cim_ref[...]
    ckr, cki = cre * fre - cim * fim, cre * fim + cim * fre
    btre, btim = btre_ref[...], btim_ref[...]
    pr, pi = jnp.ones_like(bre), jnp.zeros_like(bre)
    powers = []
    for tau in range(blk + 1):
        ckre_ref[:, tau * h:(tau + 1) * h, :] = ckr
        ckim_ref[:, tau * h:(tau + 1) * h, :] = cki
        powers.append((pr, pi))
        ckr, cki = ckr * bre - cki * bim, ckr * bim + cki * bre
        pr, pi = pr * bre - pi * bim, pr * bim + pi * bre
    for i in range(blk):
        pr, pi = powers[blk - 1 - i]
        wre_ref[:, i * h:(i + 1) * h, :] = pr * btre - pi * btim
        wim_ref[:, i * h:(i + 1) * h, :] = pr * btim + pi * btre
    vre_ref[...] = ckre_ref[:, h:(blk + 1) * h, :]
    vim_ref[...] = -ckim_ref[:, h:(blk + 1) * h, :]
    qr, qi = powers[blk]
    pr, pi = qr, qi
    for r in range(SUBLANES):
        qre_ref[:, r:r + 1, :] = pr
        qim_ref[:, r:r + 1, :] = pi
        pr, pi = pr * qr - pi * qi, pr * qi + pi * qr
    for g in range(kt_ref.shape[0]):
        kt_ref[g] = (_dot(btre[g], ckre_ref[g, 0:blk * h, :], NT) - _dot(btim[g], ckim_ref[g, 0:blk * h, :], NT))


def _s5_kernel(nbatch, p_ref, kt_ref, wre_ref, wim_ref, vt_ref, cre_ref, cim_ref, d_ref, o_ref,
               u_ref, gre_ref, gim_ref, xre_ref, xim_ref, yp_ref, stage_ref):
    npair = u_ref.shape[0] // 2
    rows = u_ref.shape[1]
    wlane = u_ref.shape[2]
    blk = S5_BLOCK
    ngrp = 2 * npair
    rc = S5_RELAYOUT_ROWS
    lane_chunk = lax.broadcasted_iota(jnp.int32, (rc, LANES), 1) // S5_GROUP

    def chunk_transpose(arrs):
        arrs = list(arrs)
        s = ngrp // 2
        while s:
            upper = (lane_chunk & s) != 0
            for x in range(ngrp):
                if x & s == 0:
                    ax, ay = arrs[x], arrs[x + s]
                    arrs[x] = jnp.where(upper, pltpu.roll(ay, s * S5_GROUP, 1), ax)
                    arrs[x + s] = jnp.where(upper, ay, pltpu.roll(ax, LANES - s * S5_GROUP, 1))
            s //= 2
        return arrs

    def relayout_in(c, carry):
        t0 = pl.multiple_of(c * (rc * blk), rc * blk)
        r0 = pl.multiple_of(c * rc, rc)
        stage_ref[...] = p_ref[pl.ds(t0, rc * blk), :].astype(F32)
        for half in range(blk // ngrp):
            z = chunk_transpose(stage_ref[pl.ds(ngrp * half + i, rc, stride=blk), :] for i in range(ngrp))
            for g in range(ngrp):
                u_ref[g, pl.ds(r0, rc), half * LANES:(half + 1) * LANES] = z[g].astype(BF16)
        return carry

    lax.fori_loop(0, rows // rc, relayout_in, 0)
    lane = lax.broadcasted_iota(jnp.int32, (S5_GROUP, wlane), 1)

    for g in range(2 * npair):
        kt = kt_ref[g]
        blocks = [kt] + [jnp.where(lane >= i * S5_GROUP, pltpu.roll(kt, i * S5_GROUP, 1), 0.0)
                         for i in range(1, S5_BLOCK)]
        kmat = jnp.concatenate(blocks, axis=0).astype(BF16)
        yp_ref[g] = _dg(u_ref[g], kmat, NN)
    for p in range(npair):
        ucat = jnp.concatenate([u_ref[2 * p], u_ref[2 * p + 1]], axis=1)
        gre_ref[p] = _dg(ucat, wre_ref[p], NN)
        gim_ref[p] = _dg(ucat, wim_ref[p], NN)

    def cma(xr, xi, ar, ai, sr, si):
        return xr + (ar * sr - ai * si), xi + (ar * si + ai * sr)

    row = lax.broadcasted_iota(jnp.int32, (SUBLANES, LANES), 0)
    per_batch = rows // nbatch

    def tile(t, carry, base):
        r0 = pl.multiple_of(base + t * SUBLANES, SUBLANES)
        new_carry = []
        for p in range(npair):
            xr = gre_ref[p, pl.ds(r0, SUBLANES), :]
            xi = gim_ref[p, pl.ds(r0, SUBLANES), :]
            for lvl, sh in enumerate((1, 2, 4)):
                ar = cre_ref[p, lvl * SUBLANES:(lvl + 1) * SUBLANES, :]
                ai = cim_ref[p, lvl * SUBLANES:(lvl + 1) * SUBLANES, :]
                xr, xi = cma(xr, xi, ar, ai, pltpu.roll(xr, sh, 0), pltpu.roll(xi, sh, 0))
            ar = cre_ref[p, 3 * SUBLANES:4 * SUBLANES, :]
            ai = cim_ref[p, 3 * SUBLANES:4 * SUBLANES, :]
            cr, ci = carry[2 * p], carry[2 * p + 1]
            xr, xi = cma(xr, xi, ar, ai, cr, ci)
            xre_ref[p, pl.ds(r0, SUBLANES), :] = jnp.where(row == 0, cr, pltpu.roll(xr, 1, 0))
            xim_ref[p, pl.ds(r0, SUBLANES), :] = jnp.where(row == 0, ci, pltpu.roll(xi, 1, 0))
            new_carry.append(jnp.broadcast_to(xr[SUBLANES - 1:SUBLANES, :], (SUBLANES, LANES)))
            new_carry.append(jnp.broadcast_to(xi[SUBLANES - 1:SUBLANES, :], (SUBLANES, LANES)))
        return tuple(new_carry)

    zero = jnp.zeros((SUBLANES, LANES), F32)
    for b in range(nbatch):
        lax.fori_loop(0, per_batch // SUBLANES, functools.partial(tile, base=b * per_batch), (zero,) * (2 * npair))

    for p in range(npair):
        xs = jnp.concatenate([xre_ref[p], xim_ref[p]], axis=1).astype(BF16)
        corr = _dg(xs, vt_ref[p], NT)
        for k in range(2):
            g = 2 * p + k
            y = yp_ref[g] + corr[:, k * wlane:(k + 1) * wlane] + d_ref[g] * u_ref[g].astype(F32)
            yp_ref[g] = jax.nn.gelu(y)

    def relayout_out(c, carry):
        t0 = pl.multiple_of(c * (rc * blk), rc * blk)
        r0 = pl.multiple_of(c * rc, rc)
        for half in range(blk // ngrp):
            yt = chunk_transpose(yp_ref[g, pl.ds(r0, rc), half * LANES:(half + 1) * LANES] for g in range(ngrp))
            for j in range(ngrp):
                stage_ref[pl.ds(ngrp * half + j, rc, stride=blk), :] = yt[j]
        o_ref[pl.ds(t0, rc * blk), :] = stage_ref[...].astype(o_ref.dtype)
        return carry

    lax.fori_loop(0, rows // rc, relayout_out, 0)


def s5_ssm(p, batch, seqlen, lam_re, lam_im, log_dt, b_re, b_im, c_re, c_im, d_skip):
    ngroup, nstate = lam_re.shape
    h, blk = S5_GROUP, S5_BLOCK
    width = ngroup * h
    wl = blk * h
    nb = batch * seqlen // blk
    gp = S5_PREP_GROUPS
    assert nstate == S5_STATE and 2 * nstate == LANES
    assert seqlen % (blk * SUBLANES) == 0 and ngroup % gp == 0 and ngroup % (2 * S5_PAIRS) == 0

    g3 = lambda x: x.reshape(ngroup, 1, nstate)
    bt = lambda x: jnp.swapaxes(x, 1, 2)
    spec1 = pl.BlockSpec((gp, 1, nstate), lambda i: (i, 0, 0))
    spec_c = pl.BlockSpec((gp, h, nstate), lambda i: (i, 0, 0))
    spec_w = pl.BlockSpec((gp, wl, nstate), lambda i: (i, 0, 0))
    spec_q = pl.BlockSpec((gp, SUBLANES, nstate), lambda i: (i, 0, 0))
    spec_k = pl.BlockSpec((gp, h, wl), lambda i: (i, 0, 0))
    f = lambda *s: jax.ShapeDtypeStruct(s, F32)
    kt, w_re, w_im, v_re, v_im, q_re, q_im = pl.pallas_call(
        _s5_prep_kernel,
        grid=(ngroup // gp,),
        in_specs=[spec1, spec1, spec1, spec_c, spec_c, spec_c, spec_c],
        out_specs=[spec_k, spec_w, spec_w, spec_w, spec_w, spec_q, spec_q],
        out_shape=[f(ngroup, h, wl), f(ngroup, wl, nstate), f(ngroup, wl, nstate), f(ngroup, wl, nstate),
                   f(ngroup, wl, nstate), f(ngroup, SUBLANES, nstate), f(ngroup, SUBLANES, nstate)],
        scratch_shapes=[pltpu.VMEM((gp, (blk + 1) * h, nstate), F32), pltpu.VMEM((gp, (blk + 1) * h, nstate), F32)],
        compiler_params=_cparams(("parallel",)),
        name="s5_prep",
    )(g3(lam_re), g3(lam_im), jnp.broadcast_to(log_dt[:, None, None], (ngroup, 1, nstate)), c_re, c_im,
      bt(b_re), bt(b_im))

    npairs = ngroup // 2

    def pair_rows(x):
        x = x.reshape(npairs, 2, x.shape[1], nstate)
        z = jnp.zeros_like(x[:, 0])
        return jnp.concatenate([jnp.concatenate([x[:, 0], z], axis=2), jnp.concatenate([z, x[:, 1]], axis=2)], axis=1)

    w_re2 = pair_rows(w_re).astype(BF16)
    w_im2 = pair_rows(w_im).astype(BF16)
    vt2 = jnp.concatenate([pair_rows(v_re), pair_rows(v_im)], axis=2).astype(BF16)

    def pair_lanes(x):
        x = x.reshape(npairs, 2, SUBLANES, nstate)
        return jnp.concatenate([x[:, 0], x[:, 1]], axis=2)

    rows = jnp.arange(SUBLANES)[None, :, None]

    def scan_consts(q):
        q = pair_lanes(q)
        levels = [jnp.where(rows >= sh, q[:, sh - 1:sh, :], 0.0) for sh in (1, 2, 4)]
        return jnp.concatenate(levels + [q], axis=1)

    c_re2, c_im2 = scan_consts(q_re), scan_consts(q_im)
    d_row = jnp.tile(d_skip.reshape(ngroup, 1, h), (1, 1, blk))

    np_ = S5_PAIRS
    gs = 2 * np_
    assert gs * h == LANES and blk % gs == 0 and nb % S5_RELAYOUT_ROWS == 0
    tok = pl.BlockSpec((nb * blk, LANES), lambda i: (0, i))
    grp = lambda r, c: pl.BlockSpec((gs, r, c), lambda i: (i, 0, 0))
    par = lambda r, c: pl.BlockSpec((np_, r, c), lambda i: (i, 0, 0))
    return pl.pallas_call(
        functools.partial(_s5_kernel, batch),
        grid=(ngroup // gs,),
        in_specs=[tok, grp(h, wl), par(2 * wl, LANES), par(2 * wl, LANES), par(2 * wl, 2 * LANES),
                  par(4 * SUBLANES, LANES), par(4 * SUBLANES, LANES), grp(1, wl)],
        out_specs=tok,
        out_shape=jax.ShapeDtypeStruct((batch * seqlen, width), BF16),
        scratch_shapes=[pltpu.VMEM((gs, nb, wl), BF16)] + [pltpu.VMEM((np_, nb, LANES), F32)] * 4
        + [pltpu.VMEM((gs, nb, wl), F32), pltpu.VMEM((S5_RELAYOUT_ROWS * blk, LANES), F32)],
        compiler_params=_cparams(("parallel",)),
        name="s5_ssm",
    )(p, kt, w_re2, w_im2, vt2, c_re2, c_im2, d_row)


def _glu_kernel(y_ref, yj_ref, gj_ref, w_ref, b_ref, o_ref):
    z = jnp.dot(y_ref[...], w_ref[...].astype(BF16), preferred_element_type=F32) + b_ref[...]
    gate = gj_ref[...].astype(F32)
    yj = yj_ref[...].astype(F32)
    o_ref[...] = (yj * jax.nn.sigmoid(z) * (gate * jax.nn.sigmoid(gate))).astype(o_ref.dtype)


def glu_gate(y, p, w, b, tm=1024, tn=1024):
    m, d = y.shape
    tm = min(tm, m)
    goff = d // tn
    return pl.pallas_call(
        _glu_kernel,
        grid=(m // tm, d // tn),
        in_specs=[pl.BlockSpec((tm, d), lambda i, j: (i, 0)),
                  pl.BlockSpec((tm, tn), lambda i, j: (i, j)),
                  pl.BlockSpec((tm, tn), lambda i, j: (i, goff + j)),
                  pl.BlockSpec((d, tn), lambda i, j: (0, j)),
                  pl.BlockSpec((1, tn), lambda i, j: (0, j))],
        out_specs=pl.BlockSpec((tm, tn), lambda i, j: (i, j)),
        out_shape=jax.ShapeDtypeStruct((m, d), BF16),
        compiler_params=_cparams(("parallel", "arbitrary")),
        name="glu_gate",
    )(y, y, p, w, b.reshape(1, d))


def _final_kernel(a_ref, h_ref, w_ref, g_ref, o_ref):
    h = h_ref[...] + jnp.dot(a_ref[...], w_ref[...], preferred_element_type=F32)
    ms = jnp.mean(h * h, axis=-1, keepdims=True)
    o_ref[...] = (h * lax.rsqrt(ms + RMS_EPS)) * g_ref[...]


def final_out(a, h, w, g, tm=512):
    m, d = h.shape
    tm = min(tm, m)
    row = pl.BlockSpec((tm, d), lambda i: (i, 0))
    return pl.pallas_call(
        _final_kernel,
        grid=(m // tm,),
        in_specs=[row, row, pl.BlockSpec((d, d), lambda i: (0, 0), pipeline_mode=pl.Buffered(1)),
                  pl.BlockSpec((1, d), lambda i: (0, 0))],
        out_specs=row,
        out_shape=jax.ShapeDtypeStruct((m, d), F32),
        compiler_params=_cparams(("parallel",)),
        name="final_out",
    )(a, h, w, g.reshape(1, d))


def kernel(x, norm_g, final_g, ab_w_in, rwkv_shift_mix, rwkv_w_up, rwkv_w0, rwkv_a_up, rwkv_a0, rwkv_k_k, rwkv_k_a, rwkv_r_k, rwkv_gn_w, rwkv_gn_b, ab_w_out, s5_w_in, s5_lam_re, s5_lam_im, s5_log_dt, s5_b_re, s5_b_im, s5_c_re, s5_c_im, s5_d, s5_w_glu, s5_b_glu, s5_w_out):
    batch, seqlen, d = x.shape
    m = batch * seqlen
    rwkv_w = rwkv_w0.shape[1]
    rwkv_proj = 3 * rwkv_w + 2 * LORA_RANK
    sb_w = (ab_w_in.shape[2] - rwkv_proj - d) // 3
    x2 = x.reshape(m, d)

    w_in = ab_w_in[0]
    scale = HEAD_DIM ** -0.5
    col_scale = jnp.concatenate([jnp.full((sb_w,), scale, F32), jnp.ones((2 * sb_w + d,), F32)])
    xn = rmsnorm(x2, norm_g[0], BF16)
    p_rkv = matmul(xn, w_in, F32, 0, 3 * rwkv_w, name="proj_rkv")
    p_lora = matmul(xn, w_in, F32, 3 * rwkv_w, 2 * LORA_RANK, tn=2 * LORA_RANK, name="proj_lora")
    p_sb = matmul(xn, w_in, BF16, rwkv_proj, 3 * sb_w + d, col_scale, tm=2048, name="proj_sb_gate")
    y_a = rwkv7(p_rkv, p_lora, batch, seqlen, rwkv_shift_mix[0], rwkv_w_up[0], rwkv_w0[0], rwkv_a_up[0], rwkv_a0[0],
                rwkv_k_k[0], rwkv_k_a[0], rwkv_r_k[0], rwkv_gn_w[0], rwkv_gn_b[0])
    y_b = stick_breaking(p_sb, batch, seqlen, sb_w)
    h1, hn1 = gate_out(y_a, y_b, p_sb, 3 * sb_w, x2, ab_w_out[0].astype(BF16), norm_g[1])

    p1 = matmul(hn1, s5_w_in[0], BF16, 0, 2 * d, tm=2048, name="proj_s5")
    y_s5 = s5_ssm(p1, batch, seqlen, s5_lam_re[0], s5_lam_im[0], s5_log_dt[0], s5_b_re[0], s5_b_im[0],
                  s5_c_re[0], s5_c_im[0], s5_d[0])
    act = glu_gate(y_s5, p1, s5_w_glu[0], s5_b_glu[0])
    out = final_out(act, h1, s5_w_out[0].astype(BF16), final_g)
    return out.reshape(batch, seqlen, d)
```

```python
import functools
import math

import jax
import jax.numpy as jnp
from jax import lax
from jax.experimental import pallas as pl
from jax.experimental.pallas import tpu as pltpu

F32 = jnp.float32
BF16 = jnp.bfloat16

HEAD_DIM = 64
LANES = 128
SUBLANES = 8
LORA_RANK = 64
S5_GROUP = 16
S5_STATE = 64
RMS_EPS = 1e-6
GN_EPS = 64e-5
DECAY_SCALE = math.exp(-0.5)

RWKV_CHUNK = 64
RWKV_PAIRS = 8
SB_BLOCK = 128
SB_QBLOCK = 256
SB_PAIRS = 4
S5_BLOCK = 16
S5_PAIRS = 4
S5_PREP_GROUPS = 16
S5_RELAYOUT_ROWS = 64
VMEM_LIMIT = 56 * 1024 * 1024

NN = (((1,), (0,)), ((), ()))
NT = (((1,), (1,)), ((), ()))
TN = (((0,), (0,)), ((), ()))


def _cparams(sem):
    return pltpu.CompilerParams(dimension_semantics=sem, vmem_limit_bytes=VMEM_LIMIT)


def _split(x):
    hi = x.astype(BF16)
    lo = (x - hi.astype(F32)).astype(BF16)
    return hi, lo


def _dg(a, b, dn):
    return lax.dot_general(a, b, dn, preferred_element_type=F32)


def _dot(a, b, dn=NN, passes=3):
    if passes == 1:
        return _dg(a.astype(BF16), b.astype(BF16), dn)
    ah, al = _split(a)
    bh, bl = _split(b)
    return _dg(ah, bh, dn) + (_dg(ah, bl, dn) + _dg(al, bh, dn))


def _norm_proj_kernel(x_ref, g_ref, w_ref, xn_ref, o_ref):
    x = x_ref[...]
    ms = jnp.mean(x * x, axis=-1, keepdims=True)
    xn = ((x * lax.rsqrt(ms + RMS_EPS)) * g_ref[...]).astype(xn_ref.dtype)
    xn_ref[...] = xn
    o_ref[...] = jnp.dot(xn, w_ref[...].astype(BF16), preferred_element_type=F32)


def rmsnorm_proj(x, g, w, col0, n, tm=1024):
    m, d = x.shape
    tm = min(tm, m)
    assert col0 % LANES == 0 and n % LANES == 0
    return pl.pallas_call(
        _norm_proj_kernel,
        grid=(m // tm,),
        in_specs=[pl.BlockSpec((tm, d), lambda i: (i, 0)), pl.BlockSpec((1, d), lambda i: (0, 0)),
                  pl.BlockSpec((pl.Element(d), pl.Element(n)), lambda i: (0, col0))],
        out_specs=[pl.BlockSpec((tm, d), lambda i: (i, 0)), pl.BlockSpec((tm, n), lambda i: (i, 0))],
        out_shape=[jax.ShapeDtypeStruct((m, d), BF16), jax.ShapeDtypeStruct((m, n), F32)],
        compiler_params=_cparams(("parallel",)),
        name="rmsnorm_proj_lora",
    )(x, g.reshape(1, d), w)


def _mm_kernel(a_ref, w_ref, o_ref):
    o_ref[...] = jnp.dot(a_ref[...], w_ref[...].astype(BF16), preferred_element_type=F32).astype(o_ref.dtype)


def _mm_scaled_kernel(a_ref, w_ref, s_ref, o_ref):
    acc = jnp.dot(a_ref[...], w_ref[...].astype(BF16), preferred_element_type=F32)
    o_ref[...] = (acc * s_ref[...]).astype(o_ref.dtype)


def matmul(a, w, out_dtype, col0, n, col_scale=None, tm=1024, tn=1024, name="matmul"):
    m, k = a.shape
    tm = min(tm, m)
    assert m % tm == 0 and n % tn == 0 and col0 % LANES == 0
    in_specs = [pl.BlockSpec((tm, k), lambda i, j: (i, 0)),
                pl.BlockSpec((pl.Element(k), pl.Element(tn)), lambda i, j: (0, pl.multiple_of(col0 + j * tn, LANES)))]
    args = [a, w]
    if col_scale is not None:
        in_specs.append(pl.BlockSpec((1, tn), lambda i, j: (0, j)))
        args.append(col_scale.reshape(1, n))
    return pl.pallas_call(
        _mm_kernel if col_scale is None else _mm_scaled_kernel,
        grid=(m // tm, n // tn),
        in_specs=in_specs,
        out_specs=pl.BlockSpec((tm, tn), lambda i, j: (i, j)),
        out_shape=jax.ShapeDtypeStruct((m, n), out_dtype),
        compiler_params=_cparams(("parallel", "arbitrary")),
        name=name,
    )(*args)


def _rwkv_kernel(r_ref, k_ref, v_ref, lo_ref, mr_ref, mk_ref, mv_ref, mlo_ref, wup_ref, aup_ref,
                 w0_ref, a0_ref, kk_ref, ka_ref, rk_ref, gnw_ref, gnb_ref,
                 seg_ref, tri_ref, strict_ref, incl_ref, eye_ref,
                 y_ref, s_ref, prev_ref):
    c = pl.program_id(2)
    ch = RWKV_CHUNK
    npair = r_ref.shape[1] // LANES

    @pl.when(c == 0)
    def _():
        s_ref[...] = jnp.zeros_like(s_ref)
        prev_ref[...] = jnp.zeros_like(prev_ref)

    def token_shift(x, idx, mix):
        row = lax.broadcasted_iota(jnp.int32, x.shape, 0)
        prev = prev_ref[idx:idx + 1, 0:x.shape[1]]
        shifted = jnp.where(row == 0, prev, pltpu.roll(x, 1, 0))
        return x + (shifted - x) * mix

    r_in, k_in, v_in, lo_in = r_ref[...], k_ref[...], v_ref[...], lo_ref[...]
    r = token_shift(r_in, 0, mr_ref[...])
    k = token_shift(k_in, 1, mk_ref[...])
    v = token_shift(v_in, 2, mv_ref[...])
    lo = token_shift(lo_in, 3, mlo_ref[...])
    prev_ref[0:1, :] = r_in[ch - 1:ch, :]
    prev_ref[1:2, :] = k_in[ch - 1:ch, :]
    prev_ref[2:3, :] = v_in[ch - 1:ch, :]
    prev_ref[3:4, 0:LANES] = lo_in[ch - 1:ch, :]

    z_w = w0_ref[...] + _dot(jnp.tanh(lo), wup_ref[...], passes=1)
    logw = -DECAY_SCALE * jax.nn.sigmoid(z_w)
    a = jax.nn.sigmoid(a0_ref[...] + _dot(lo, aup_ref[...], passes=1))

    seg2 = seg_ref[...]

    def head_sum(x):
        tiles = []
        for t in range(npair):
            hi, lo_ = _split(x[:, t * LANES:(t + 1) * LANES])
            tiles.append(_dg(jnp.concatenate([hi, lo_], axis=1), seg2, NN))
        return jnp.concatenate(tiles, axis=1)

    kk = k * kk_ref[...]
    kk = kk * lax.rsqrt(jnp.maximum(head_sum(kk * kk), 1e-24))
    k2 = k * (1.0 + (a - 1.0) * ka_ref[...])
    ab = kk * a

    l_hi = logw.astype(BF16)
    rem = logw - l_hi.astype(F32)
    l_mid = rem.astype(BF16)
    l_lo = (rem - l_mid.astype(F32)).astype(BF16)
    cum = _dg(tri_ref[...], jnp.concatenate([l_hi, l_mid, l_lo], axis=0), NN)
    cum_last = cum[ch - 1:ch, :]
    e_cum = jnp.exp(cum)
    e_ncum = jnp.exp(-cum)
    e_tail = jnp.exp(cum_last - cum)
    rt = r * e_cum
    kt = k2 * e_ncum
    bt = ab * e_ncum
    at = -kk * jnp.exp(cum - logw)
    khat = k2 * e_tail
    bhat = ab * e_tail
    p_last = e_cum[ch - 1:ch, :]

    lane = lax.broadcasted_iota(jnp.int32, (ch, LANES), 1)
    head0 = lane < HEAD_DIM
    strict = strict_ref[...] > 0.5
    incl = incl_ref[...] > 0.5
    eye = eye_ref[...]

    def stack(x):
        return jnp.concatenate([jnp.where(head0, x, 0.0), jnp.where(head0, 0.0, x)], axis=0)

    def mm(x, w, dn=NN):
        return _dg(x.astype(BF16), w.astype(BF16), dn)

    pairs = range(npair)
    rows = 2 * ch
    sl = [slice(p * LANES, (p + 1) * LANES) for p in pairs]
    at2, rt2, bt2, kt2, v2, khat2, bhat2 = ([stack(t[:, sl[p]]) for p in pairs]
                                            for t in (at, rt, bt, kt, v, khat, bhat))
    gram = []
    for p in pairs:
        lh, ll = _split(jnp.concatenate([at2[p], rt2[p]], axis=0))
        rh, rl = _split(jnp.concatenate([bt2[p], kt2[p]], axis=0))
        gram.append(_dg(jnp.concatenate([lh, ll], axis=1), jnp.concatenate([rh, rh], axis=1), NT))
    a_ab = [jnp.where(strict, g[:rows, :rows], 0.0) for g in gram]
    a_ak = [jnp.where(strict, g[:rows, rows:], 0.0) for g in gram]
    a_rb = [jnp.where(incl, g[rows:, :rows], 0.0) for g in gram]
    a_rk = [jnp.where(incl, g[rows:, rows:], 0.0) for g in gram]
    akv = [mm(a_ak[p], v2[p]) for p in pairs]

    tinv = [eye + a for a in a_ab]
    pw = [mm(a, a) for a in a_ab]
    for _ in range(int(math.log2(ch)) - 2):
        both = [mm(pw[p], jnp.concatenate([pw[p], tinv[p]], axis=1)) for p in pairs]
        pw = [b[:, :rows] for b in both]
        tinv = [tinv[p] + both[p][:, rows:] for p in pairs]
    tinv = [tinv[p] + mm(pw[p], tinv[p]) for p in pairs]

    s = [s_ref[p] for p in pairs]
    tw = [mm(tinv[p], jnp.concatenate([at2[p], akv[p]], axis=1)) for p in pairs]
    ws = [mm(jnp.concatenate([tw[p][:, :LANES], rt2[p]], axis=0), s[p], NT) for p in pairs]
    uv = [jnp.concatenate([ws[p][:rows] + tw[p][:, LANES:], v2[p]], axis=0) for p in pairs]
    y2 = [ws[p][rows:] + mm(jnp.concatenate([a_rb[p], a_rk[p]], axis=1), uv[p]) for p in pairs]
    for p in pairs:
        s_ref[p] = s[p] * p_last[:, sl[p]] + mm(uv[p], jnp.concatenate([bhat2[p], khat2[p]], axis=0), TN)
    y = jnp.concatenate([t[:ch, :] + t[ch:, :] for t in y2], axis=1)

    inv_n = 1.0 / HEAD_DIM
    mu = head_sum(y) * inv_n
    d = y - mu
    var = head_sum(d * d) * inv_n
    yn = d * lax.rsqrt(var + GN_EPS) * gnw_ref[...] + gnb_ref[...]
    y_ref[...] = (yn + head_sum(r * k2 * rk_ref[...]) * v).astype(y_ref.dtype)


def rwkv7(p, p_lora, batch, seqlen, shift_mix, w_up, w0, a_up, a0, k_k, k_a, r_k, gn_w, gn_b):
    width = w0.shape[0]
    npair = width // LANES
    ch = RWKV_CHUNK
    nchunk = seqlen // ch
    assert seqlen % ch == 0 and 2 * LORA_RANK == LANES and 2 * HEAD_DIM == LANES
    zeros = jnp.zeros((LORA_RANK, width), F32)
    wup_pad = jnp.concatenate([w_up, zeros], axis=0)
    aup_pad = jnp.concatenate([zeros, a_up], axis=0)
    hb = RWKV_PAIRS
    assert npair % hb == 0
    idx = jnp.arange(LANES)
    seg = (idx[:, None] // HEAD_DIM == idx[None, :] // HEAD_DIM).astype(BF16)
    seg = jnp.concatenate([seg, seg], axis=0)
    t = jnp.arange(ch)
    tri = (t[None, :] <= t[:, None]).astype(BF16)
    tri = jnp.concatenate([tri, tri, tri], axis=1)
    i2 = jnp.arange(2 * ch)
    same = (i2[:, None] // ch) == (i2[None, :] // ch)
    strict = (same & ((i2[None, :] % ch) < (i2[:, None] % ch))).astype(F32)
    incl = (same & ((i2[None, :] % ch) <= (i2[:, None] % ch))).astype(F32)
    eye = jnp.eye(2 * ch, dtype=F32)
    row2 = lambda x: x.reshape(1, -1)

    wd = hb * LANES
    ngrp = npair // hb

    def tok(off):
        return pl.BlockSpec((ch, wd), lambda b, h, c: (b * nchunk + c, off + h))

    def par(off):
        return pl.BlockSpec((1, wd), lambda b, h, c: (0, off + h))

    def const(shape):
        return pl.BlockSpec(shape, lambda b, h, c: (0, 0))

    up = pl.BlockSpec((LANES, wd), lambda b, h, c: (0, h))
    lora_tok = pl.BlockSpec((ch, LANES), lambda b, h, c: (b * nchunk + c, 0))
    lora_par = pl.BlockSpec((1, LANES), lambda b, h, c: (0, 3 * npair))
    return pl.pallas_call(
        _rwkv_kernel,
        grid=(batch, ngrp, nchunk),
        in_specs=[tok(0), tok(ngrp), tok(2 * ngrp), lora_tok,
                  par(0), par(ngrp), par(2 * ngrp), lora_par, up, up,
                  par(0), par(0), par(0), par(0), par(0), par(0), par(0),
                  const((2 * LANES, LANES)), const((ch, 3 * ch)), const((2 * ch, 2 * ch)),
                  const((2 * ch, 2 * ch)), const((2 * ch, 2 * ch))],
        out_specs=pl.BlockSpec((ch, wd), lambda b, h, c: (b * nchunk + c, h)),
        out_shape=jax.ShapeDtypeStruct((batch * seqlen, width), BF16),
        scratch_shapes=[pltpu.VMEM((hb, LANES, LANES), F32), pltpu.VMEM((SUBLANES, wd), F32)],
        compiler_params=_cparams(("parallel", "parallel", "arbitrary")),
        name="rwkv7",
    )(p, p, p, p_lora, row2(shift_mix), row2(shift_mix), row2(shift_mix), row2(shift_mix), wup_pad, aup_pad,
      row2(w0), row2(a0), row2(k_k), row2(k_a), row2(r_k), row2(gn_w), row2(gn_b),
      seg, tri, strict, incl, eye)


def _sb_kernel(q_ref, k_ref, v_ref, uo_ref, o_ref, k2_ref, v2_ref, acc_ref, carry_ref, lb_ref, sums_ref, z_ref):
    qi = pl.program_id(2)
    kb, qb = SB_BLOCK, SB_QBLOCK
    nsub = qb // kb
    npair = q_ref.shape[1] // LANES
    pairs_ = range(npair)

    @pl.when(qi == 0)
    def _():
        lane = lax.broadcasted_iota(jnp.int32, (kb, LANES), 1)
        head0 = lane < HEAD_DIM

        def fill(i, c):
            start = pl.multiple_of(i * kb, kb)
            for p in pairs_:
                for src, dst in ((k_ref, k2_ref), (v_ref, v2_ref)):
                    t = src[pl.ds(start, kb), p * LANES:(p + 1) * LANES].astype(F32)
                    dst[p, i, 0:kb, :] = jnp.where(head0, t, 0.0).astype(BF16)
                    dst[p, i, kb:2 * kb, :] = jnp.where(head0, 0.0, t).astype(BF16)
            return c

        lax.fori_loop(0, k_ref.shape[0] // kb, fill, 0)

    q = [q_ref[:, p * LANES:(p + 1) * LANES] for p in pairs_]
    uo = uo_ref[...]
    acc_ref[...] = jnp.zeros_like(acc_ref)
    carry_ref[...] = jnp.zeros_like(carry_ref)
    tpos = lax.broadcasted_iota(jnp.int32, (qb, kb), 0)
    spos = lax.broadcasted_iota(jnp.int32, (qb, kb), 1)

    def logits(p, sb, zslot):
        base = pl.multiple_of(sb * nsub, nsub)
        keys = k2_ref[p, pl.ds(base, nsub)].reshape(nsub * 2 * kb, LANES)
        z_ref[zslot, p] = _dg(q[p], keys, NT)

    def scores(p, zslot, slot, diagonal):
        for j in range(nsub):
            for h in range(2):
                c = 2 * j + h
                z = z_ref[zslot, p, :, c * kb:(c + 1) * kb]
                nz = -z
                log_keep = jnp.minimum(nz, 0.0) - jnp.log(1.0 + jnp.exp(jnp.minimum(z, nz)))
                if diagonal:
                    causal = (spos + j * kb) < tpos
                    log_keep = jnp.where(causal, log_keep, 0.0)
                    z = jnp.where(causal, z, -jnp.inf)
                hi = log_keep.astype(BF16)
                lo = (log_keep - hi.astype(F32)).astype(BF16)
                lb_ref[slot, p, c] = z
                sums_ref[slot, p, c] = _dg(jnp.concatenate([hi, lo], axis=1), uo, NN)

    def accumulate(p, sb, slot):
        base = pl.multiple_of(sb * nsub, nsub)
        attn = {}
        for h in range(2):
            carry = carry_ref[p, h]
            for j in reversed(range(nsub)):
                s = sums_ref[slot, p, 2 * j + h]
                attn[j, h] = jnp.exp(lb_ref[slot, p, 2 * j + h] + carry + s[:, :kb]).astype(BF16)
                carry = carry + s[:, kb:]
            carry_ref[p, h] = carry
        weights = jnp.concatenate([attn[j, h] for j in range(nsub) for h in range(2)], axis=1)
        values = v2_ref[p, pl.ds(base, nsub)].reshape(nsub * 2 * kb, LANES)
        acc_ref[p] += _dg(weights, values, NN)

    for p in pairs_:
        logits(p, qi, 0)
    for p in pairs_:
        scores(p, 0, 0, True)
    trips = qi // 2

    def body(i, c):
        sb = qi - 1 - 2 * i
        for p in pairs_:
            logits(p, sb, 0)
        for p in pairs_:
            logits(p, sb - 1, 1)
        for p in pairs_:
            accumulate(p, sb + 1, 0)
        for p in pairs_:
            scores(p, 0, 1, False)
        for p in pairs_:
            scores(p, 1, 0, False)
        for p in pairs_:
            accumulate(p, sb, 1)
        return c

    lax.fori_loop(0, trips, body, 0)
    odd = qi - 2 * trips == 1

    @pl.when(odd)
    def _():
        for p in pairs_:
            logits(p, 0, 0)
        for p in pairs_:
            accumulate(p, 1, 0)
        for p in pairs_:
            scores(p, 0, 1, False)
        for p in pairs_:
            accumulate(p, 0, 1)

    @pl.when(jnp.logical_not(odd))
    def _():
        for p in pairs_:
            accumulate(p, 0, 0)

    for p in pairs_:
        o_ref[:, p * LANES:(p + 1) * LANES] = acc_ref[p].astype(o_ref.dtype)


def stick_breaking(p, batch, seqlen, width):
    npair = width // LANES
    kb, qb = SB_BLOCK, SB_QBLOCK
    nq = seqlen // qb
    hp = SB_PAIRS
    ngrp = npair // hp
    wd = hp * LANES
    nsl = 2 * qb // kb
    j = jnp.arange(kb)
    later = (j[:, None] >= j[None, :]).astype(BF16)
    uo = jnp.concatenate([later, jnp.ones((kb, kb), BF16)], axis=1)
    uo = jnp.concatenate([uo, uo], axis=0)
    return pl.pallas_call(
        _sb_kernel,
        grid=(batch, ngrp, nq),
        in_specs=[pl.BlockSpec((qb, wd), lambda b, h, i: (b * nq + i, h)),
                  pl.BlockSpec((seqlen, wd), lambda b, h, i: (b, ngrp + h), pipeline_mode=pl.Buffered(1)),
                  pl.BlockSpec((seqlen, wd), lambda b, h, i: (b, 2 * ngrp + h), pipeline_mode=pl.Buffered(1)),
                  pl.BlockSpec((2 * kb, 2 * kb), lambda b, h, i: (0, 0))],
        out_specs=pl.BlockSpec((qb, wd), lambda b, h, i: (b * nq + i, h)),
        out_shape=jax.ShapeDtypeStruct((batch * seqlen, width), BF16),
        scratch_shapes=[pltpu.VMEM((hp, seqlen // kb, 2 * kb, LANES), BF16),
                        pltpu.VMEM((hp, seqlen // kb, 2 * kb, LANES), BF16),
                        pltpu.VMEM((hp, qb, LANES), F32), pltpu.VMEM((hp, 2, qb, kb), F32),
                        pltpu.VMEM((2, hp, nsl, qb, kb), F32), pltpu.VMEM((2, hp, nsl, qb, 2 * kb), F32),
                        pltpu.VMEM((2, hp, qb, nsl * kb), F32)],
        compiler_params=_cparams(("parallel", "parallel", "arbitrary")),
        name="stick_breaking",
    )(p, p, p, uo)


def _gate_out_kernel(ya_ref, yb_ref, g0_ref, g1_ref, x_ref, w_ref, ng_ref, h_ref, hn_ref):
    gate = jnp.concatenate([g0_ref[...], g1_ref[...]], axis=-1).astype(F32)
    y = jnp.concatenate([ya_ref[...], yb_ref[...]], axis=-1).astype(F32) * (gate * jax.nn.sigmoid(gate))
    h = x_ref[...] + jnp.dot(y.astype(BF16), w_ref[...], preferred_element_type=F32)
    h_ref[...] = h
    ms = jnp.mean(h * h, axis=-1, keepdims=True)
    hn_ref[...] = ((h * lax.rsqrt(ms + RMS_EPS)) * ng_ref[...]).astype(hn_ref.dtype)


def gate_out(ya, yb, p, gate_col, x, w, next_g, tm=512):
    m, d = x.shape
    half = ya.shape[1]
    tm = min(tm, m)
    assert gate_col % half == 0 and d == 2 * half
    g0 = gate_col // half
    row = lambda n: pl.BlockSpec((tm, n), lambda i: (i, 0))
    return pl.pallas_call(
        _gate_out_kernel,
        grid=(m // tm,),
        in_specs=[row(half), row(half),
                  pl.BlockSpec((tm, half), lambda i: (i, g0)), pl.BlockSpec((tm, half), lambda i: (i, g0 + 1)), row(d),
                  pl.BlockSpec((d, d), lambda i: (0, 0), pipeline_mode=pl.Buffered(1)),
                  pl.BlockSpec((1, d), lambda i: (0, 0))],
        out_specs=[row(d), row(d)],
        out_shape=[jax.ShapeDtypeStruct((m, d), F32), jax.ShapeDtypeStruct((m, d), BF16)],
        compiler_params=_cparams(("parallel",)),
        name="gate_out",
    )(ya, yb, p, p, x, w, next_g.reshape(1, d))


def _s5_prep_kernel(lre_ref, lim_ref, ldt_ref, cre_ref, cim_ref, btre_ref, btim_ref,
                    kt_ref, wre_ref, wim_ref, vre_ref, vim_ref, qre_ref, qim_ref, ckre_ref, ckim_ref):
    blk = S5_BLOCK
    h = S5_GROUP
    lre, lim = lre_ref[...], lim_ref[...]
    dt = jnp.exp(ldt_ref[...])
    mag = jnp.exp(lre * dt)
    bre = mag * jnp.cos(lim * dt)
    bim = mag * jnp.sin(lim * dt)
    den = lre * lre + lim * lim
    nre, nim = bre - 1.0, bim
    fre = (nre * lre + nim * lim) / den
    fim = (nim * lre - nre * lim) / den
    cre, cim = cre_ref[...], cim_ref[...]
    ckr, cki = cre * fre - cim * fim, cre * fim + cim * fre
    btre, btim = btre_ref[...], btim_ref[...]
    pr, pi = jnp.ones_like(bre), jnp.zeros_like(bre)
    powers = []
    for tau in range(blk + 1):
        ckre_ref[:, tau * h:(tau + 1) * h, :] = ckr
        ckim_ref[:, tau * h:(tau + 1) * h, :] = cki
        powers.append((pr, pi))
        ckr, cki = ckr * bre - cki * bim, ckr * bim + cki * bre
        pr, pi = pr * bre - pi * bim, pr * bim + pi * bre
    for i in range(blk):
        pr, pi = powers[blk - 1 - i]
        wre_ref[:, i * h:(i + 1) * h, :] = pr * btre - pi * btim
        wim_ref[:, i * h:(i + 1) * h, :] = pr * btim + pi * btre
    vre_ref[...] = ckre_ref[:, h:(blk + 1) * h, :]
    vim_ref[...] = -ckim_ref[:, h:(blk + 1) * h, :]
    qr, qi = powers[blk]
    pr, pi = qr, qi
    for r in range(SUBLANES):
        qre_ref[:, r:r + 1, :] = pr
        qim_ref[:, r:r + 1, :] = pi
        pr, pi = pr * qr - pi * qi, pr * qi + pi * qr
    for g in range(kt_ref.shape[0]):
        kt_ref[g] = (_dot(btre[g], ckre_ref[g, 0:blk * h, :], NT) - _dot(btim[g], ckim_ref[g, 0:blk * h, :], NT))


def _s5_kernel(nbatch, p_ref, kt_ref, wre_ref, wim_ref, vt_ref, cre_ref, cim_ref, d_ref, o_ref,
               u_ref, gre_ref, gim_ref, xre_ref, xim_ref, yp_ref, stage_ref):
    npair = u_ref.shape[0] // 2
    rows = u_ref.shape[1]
    wlane = u_ref.shape[2]
    blk = S5_BLOCK
    ngrp = 2 * npair
    rc = S5_RELAYOUT_ROWS
    lane_chunk = lax.broadcasted_iota(jnp.int32, (rc, LANES), 1) // S5_GROUP

    def chunk_transpose(arrs):
        arrs = list(arrs)
        s = ngrp // 2
        while s:
            upper = (lane_chunk & s) != 0
            for x in range(ngrp):
                if x & s == 0:
                    ax, ay = arrs[x], arrs[x + s]
                    arrs[x] = jnp.where(upper, pltpu.roll(ay, s * S5_GROUP, 1), ax)
                    arrs[x + s] = jnp.where(upper, ay, pltpu.roll(ax, LANES - s * S5_GROUP, 1))
            s //= 2
        return arrs

    def relayout_in(c, carry):
        t0 = pl.multiple_of(c * (rc * blk), rc * blk)
        r0 = pl.multiple_of(c * rc, rc)
        stage_ref[...] = p_ref[pl.ds(t0, rc * blk), :].astype(F32)
        for half in range(blk // ngrp):
            z = chunk_transpose(stage_ref[pl.ds(ngrp * half + i, rc, stride=blk), :] for i in range(ngrp))
            for g in range(ngrp):
                u_ref[g, pl.ds(r0, rc), half * LANES:(half + 1) * LANES] = z[g].astype(BF16)
        return carry

    lax.fori_loop(0, rows // rc, relayout_in, 0)
    lane = lax.broadcasted_iota(jnp.int32, (S5_GROUP, wlane), 1)

    for g in range(2 * npair):
        kt = kt_ref[g]
        blocks = [kt] + [jnp.where(lane >= i * S5_GROUP, pltpu.roll(kt, i * S5_GROUP, 1), 0.0)
                         for i in range(1, S5_BLOCK)]
        kmat = jnp.concatenate(blocks, axis=0).astype(BF16)
        yp_ref[g] = _dg(u_ref[g], kmat, NN)
    for p in range(npair):
        ucat = jnp.concatenate([u_ref[2 * p], u_ref[2 * p + 1]], axis=1)
        gre_ref[p] = _dg(ucat, wre_ref[p], NN)
        gim_ref[p] = _dg(ucat, wim_ref[p], NN)

    def cma(xr, xi, ar, ai, sr, si):
        return xr + (ar * sr - ai * si), xi + (ar * si + ai * sr)

    row = lax.broadcasted_iota(jnp.int32, (SUBLANES, LANES), 0)
    per_batch = rows // nbatch

    def tile(t, carry, base):
        r0 = pl.multiple_of(base + t * SUBLANES, SUBLANES)
        new_carry = []
        for p in range(npair):
            xr = gre_ref[p, pl.ds(r0, SUBLANES), :]
            xi = gim_ref[p, pl.ds(r0, SUBLANES), :]
            for lvl, sh in enumerate((1, 2, 4)):
                ar = cre_ref[p, lvl * SUBLANES:(lvl + 1) * SUBLANES, :]
                ai = cim_ref[p, lvl * SUBLANES:(lvl + 1) * SUBLANES, :]
                xr, xi = cma(xr, xi, ar, ai, pltpu.roll(xr, sh, 0), pltpu.roll(xi, sh, 0))
            ar = cre_ref[p, 3 * SUBLANES:4 * SUBLANES, :]
            ai = cim_ref[p, 3 * SUBLANES:4 * SUBLANES, :]
            cr, ci = carry[2 * p], carry[2 * p + 1]
            xr, xi = cma(xr, xi, ar, ai, cr, ci)
            xre_ref[p, pl.ds(r0, SUBLANES), :] = jnp.where(row == 0, cr, pltpu.roll(xr, 1, 0))
            xim_ref[p, pl.ds(r0, SUBLANES), :] = jnp.where(row == 0, ci, pltpu.roll(xi, 1, 0))
            new_carry.append(jnp.broadcast_to(xr[SUBLANES - 1:SUBLANES, :], (SUBLANES, LANES)))
            new_carry.append(jnp.broadcast_to(xi[SUBLANES - 1:SUBLANES, :], (SUBLANES, LANES)))
        return tuple(new_carry)

    zero = jnp.zeros((SUBLANES, LANES), F32)
    for b in range(nbatch):
        lax.fori_loop(0, per_batch // SUBLANES, functools.partial(tile, base=b * per_batch), (zero,) * (2 * npair))

    for p in range(npair):
        xs = jnp.concatenate([xre_ref[p], xim_ref[p]], axis=1).astype(BF16)
        corr = _dg(xs, vt_ref[p], NT)
        for k in range(2):
            g = 2 * p + k
            y = yp_ref[g] + corr[:, k * wlane:(k + 1) * wlane] + d_ref[g] * u_ref[g].astype(F32)
            yp_ref[g] = jax.nn.gelu(y)

    def relayout_out(c, carry):
        t0 = pl.multiple_of(c * (rc * blk), rc * blk)
        r0 = pl.multiple_of(c * rc, rc)
        for half in range(blk // ngrp):
            yt = chunk_transpose(yp_ref[g, pl.ds(r0, rc), half * LANES:(half + 1) * LANES] for g in range(ngrp))
            for j in range(ngrp):
                stage_ref[pl.ds(ngrp * half + j, rc, stride=blk), :] = yt[j]
        o_ref[pl.ds(t0, rc * blk), :] = stage_ref[...].astype(o_ref.dtype)
        return carry

    lax.fori_loop(0, rows // rc, relayout_out, 0)


def s5_ssm(p, batch, seqlen, lam_re, lam_im, log_dt, b_re, b_im, c_re, c_im, d_skip):
    ngroup, nstate = lam_re.shape
    h, blk = S5_GROUP, S5_BLOCK
    width = ngroup * h
    wl = blk * h
    nb = batch * seqlen // blk
    gp = S5_PREP_GROUPS
    assert nstate == S5_STATE and 2 * nstate == LANES
    assert seqlen % (blk * SUBLANES) == 0 and ngroup % gp == 0 and ngroup % (2 * S5_PAIRS) == 0

    g3 = lambda x: x.reshape(ngroup, 1, nstate)
    bt = lambda x: jnp.swapaxes(x, 1, 2)
    spec1 = pl.BlockSpec((gp, 1, nstate), lambda i: (i, 0, 0))
    spec_c = pl.BlockSpec((gp, h, nstate), lambda i: (i, 0, 0))
    spec_w = pl.BlockSpec((gp, wl, nstate), lambda i: (i, 0, 0))
    spec_q = pl.BlockSpec((gp, SUBLANES, nstate), lambda i: (i, 0, 0))
    spec_k = pl.BlockSpec((gp, h, wl), lambda i: (i, 0, 0))
    f = lambda *s: jax.ShapeDtypeStruct(s, F32)
    kt, w_re, w_im, v_re, v_im, q_re, q_im = pl.pallas_call(
        _s5_prep_kernel,
        grid=(ngroup // gp,),
        in_specs=[spec1, spec1, spec1, spec_c, spec_c, spec_c, spec_c],
        out_specs=[spec_k, spec_w, spec_w, spec_w, spec_w, spec_q, spec_q],
        out_shape=[f(ngroup, h, wl), f(ngroup, wl, nstate), f(ngroup, wl, nstate), f(ngroup, wl, nstate),
                   f(ngroup, wl, nstate), f(ngroup, SUBLANES, nstate), f(ngroup, SUBLANES, nstate)],
        scratch_shapes=[pltpu.VMEM((gp, (blk + 1) * h, nstate), F32), pltpu.VMEM((gp, (blk + 1) * h, nstate), F32)],
        compiler_params=_cparams(("parallel",)),
        name="s5_prep",
    )(g3(lam_re), g3(lam_im), jnp.broadcast_to(log_dt[:, None, None], (ngroup, 1, nstate)), c_re, c_im,
      bt(b_re), bt(b_im))

    npairs = ngroup // 2

    def pair_rows(x):
        x = x.reshape(npairs, 2, x.shape[1], nstate)
        z = jnp.zeros_like(x[:, 0])
        return jnp.concatenate([jnp.concatenate([x[:, 0], z], axis=2), jnp.concatenate([z, x[:, 1]], axis=2)], axis=1)

    w_re2 = pair_rows(w_re).astype(BF16)
    w_im2 = pair_rows(w_im).astype(BF16)
    vt2 = jnp.concatenate([pair_rows(v_re), pair_rows(v_im)], axis=2).astype(BF16)

    def pair_lanes(x):
        x = x.reshape(npairs, 2, SUBLANES, nstate)
        return jnp.concatenate([x[:, 0], x[:, 1]], axis=2)

    rows = jnp.arange(SUBLANES)[None, :, None]

    def scan_consts(q):
        q = pair_lanes(q)
        levels = [jnp.where(rows >= sh, q[:, sh - 1:sh, :], 0.0) for sh in (1, 2, 4)]
        return jnp.concatenate(levels + [q], axis=1)

    c_re2, c_im2 = scan_consts(q_re), scan_consts(q_im)
    d_row = jnp.tile(d_skip.reshape(ngroup, 1, h), (1, 1, blk))

    np_ = S5_PAIRS
    gs = 2 * np_
    assert gs * h == LANES and blk % gs == 0 and nb % S5_RELAYOUT_ROWS == 0
    tok = pl.BlockSpec((nb * blk, LANES), lambda i: (0, i))
    grp = lambda r, c: pl.BlockSpec((gs, r, c), lambda i: (i, 0, 0))
    par = lambda r, c: pl.BlockSpec((np_, r, c), lambda i: (i, 0, 0))
    return pl.pallas_call(
        functools.partial(_s5_kernel, batch),
        grid=(ngroup // gs,),
        in_specs=[tok, grp(h, wl), par(2 * wl, LANES), par(2 * wl, LANES), par(2 * wl, 2 * LANES),
                  par(4 * SUBLANES, LANES), par(4 * SUBLANES, LANES), grp(1, wl)],
        out_specs=tok,
        out_shape=jax.ShapeDtypeStruct((batch * seqlen, width), BF16),
        scratch_shapes=[pltpu.VMEM((gs, nb, wl), BF16)] + [pltpu.VMEM((np_, nb, LANES), F32)] * 4
        + [pltpu.VMEM((gs, nb, wl), F32), pltpu.VMEM((S5_RELAYOUT_ROWS * blk, LANES), F32)],
        compiler_params=_cparams(("parallel",)),
        name="s5_ssm",
    )(p, kt, w_re2, w_im2, vt2, c_re2, c_im2, d_row)


def _glu_kernel(y_ref, yj_ref, gj_ref, w_ref, b_ref, o_ref):
    z = jnp.dot(y_ref[...], w_ref[...].astype(BF16), preferred_element_type=F32) + b_ref[...]
    gate = gj_ref[...].astype(F32)
    yj = yj_ref[...].astype(F32)
    o_ref[...] = (yj * jax.nn.sigmoid(z) * (gate * jax.nn.sigmoid(gate))).astype(o_ref.dtype)


def glu_gate(y, p, w, b, tm=1024, tn=1024):
    m, d = y.shape
    tm = min(tm, m)
    goff = d // tn
    return pl.pallas_call(
        _glu_kernel,
        grid=(m // tm, d // tn),
        in_specs=[pl.BlockSpec((tm, d), lambda i, j: (i, 0)),
                  pl.BlockSpec((tm, tn), lambda i, j: (i, j)),
                  pl.BlockSpec((tm, tn), lambda i, j: (i, goff + j)),
                  pl.BlockSpec((d, tn), lambda i, j: (0, j)),
                  pl.BlockSpec((1, tn), lambda i, j: (0, j))],
        out_specs=pl.BlockSpec((tm, tn), lambda i, j: (i, j)),
        out_shape=jax.ShapeDtypeStruct((m, d), BF16),
        compiler_params=_cparams(("parallel", "arbitrary")),
        name="glu_gate",
    )(y, y, p, w, b.reshape(1, d))


def _final_kernel(a_ref, h_ref, w_ref, g_ref, o_ref):
    h = h_ref[...] + jnp.dot(a_ref[...], w_ref[...], preferred_element_type=F32)
    ms = jnp.mean(h * h, axis=-1, keepdims=True)
    o_ref[...] = (h * lax.rsqrt(ms + RMS_EPS)) * g_ref[...]


def final_out(a, h, w, g, tm=512):
    m, d = h.shape
    tm = min(tm, m)
    row = pl.BlockSpec((tm, d), lambda i: (i, 0))
    return pl.pallas_call(
        _final_kernel,
        grid=(m // tm,),
        in_specs=[row, row, pl.BlockSpec((d, d), lambda i: (0, 0), pipeline_mode=pl.Buffered(1)),
                  pl.BlockSpec((1, d), lambda i: (0, 0))],
        out_specs=row,
        out_shape=jax.ShapeDtypeStruct((m, d), F32),
        compiler_params=_cparams(("parallel",)),
        name="final_out",
    )(a, h, w, g.reshape(1, d))


def kernel(x, norm_g, final_g, ab_w_in, rwkv_shift_mix, rwkv_w_up, rwkv_w0, rwkv_a_up, rwkv_a0, rwkv_k_k, rwkv_k_a, rwkv_r_k, rwkv_gn_w, rwkv_gn_b, ab_w_out, s5_w_in, s5_lam_re, s5_lam_im, s5_log_dt, s5_b_re, s5_b_im, s5_c_re, s5_c_im, s5_d, s5_w_glu, s5_b_glu, s5_w_out):
    batch, seqlen, d = x.shape
    m = batch * seqlen
    rwkv_w = rwkv_w0.shape[1]
    rwkv_proj = 3 * rwkv_w + 2 * LORA_RANK
    sb_w = (ab_w_in.shape[2] - rwkv_proj - d) // 3
    x2 = x.reshape(m, d)

    w_in = ab_w_in[0]
    scale = HEAD_DIM ** -0.5
    col_scale = jnp.concatenate([jnp.full((sb_w,), scale, F32), jnp.ones((2 * sb_w + d,), F32)])
    xn, p_lora = rmsnorm_proj(x2, norm_g[0], w_in, 3 * rwkv_w, 2 * LORA_RANK)
    p_rkv = matmul(xn, w_in, F32, 0, 3 * rwkv_w, name="proj_rkv")
    p_sb = matmul(xn, w_in, BF16, rwkv_proj, 3 * sb_w + d, col_scale, tm=2048, name="proj_sb_gate")
    y_a = rwkv7(p_rkv, p_lora, batch, seqlen, rwkv_shift_mix[0], rwkv_w_up[0], rwkv_w0[0], rwkv_a_up[0], rwkv_a0[0],
                rwkv_k_k[0], rwkv_k_a[0], rwkv_r_k[0], rwkv_gn_w[0], rwkv_gn_b[0])
    y_b = stick_breaking(p_sb, batch, seqlen, sb_w)
    h1, hn1 = gate_out(y_a, y_b, p_sb, 3 * sb_w, x2, ab_w_out[0].astype(BF16), norm_g[1])

    p1 = matmul(hn1, s5_w_in[0], BF16, 0, 2 * d, tm=2048, name="proj_s5")
    y_s5 = s5_ssm(p1, batch, seqlen, s5_lam_re[0], s5_lam_im[0], s5_log_dt[0], s5_b_re[0], s5_b_im[0],
                  s5_c_re[0], s5_c_im[0], s5_d[0])
    act = glu_gate(y_s5, p1, s5_w_glu[0], s5_b_glu[0])
    out = final_out(act, h1, s5_w_out[0].astype(BF16), final_g)
    return out.reshape(batch, seqlen, d)
```

```python
import functools
import math

import jax
import jax.numpy as jnp
from jax import lax
from jax.experimental import pallas as pl
from jax.experimental.pallas import tpu as pltpu

F32 = jnp.float32
BF16 = jnp.bfloat16

HEAD_DIM = 64
LANES = 128
SUBLANES = 8
LORA_RANK = 64
S5_GROUP = 16
S5_STATE = 64
RMS_EPS = 1e-6
GN_EPS = 64e-5
DECAY_SCALE = math.exp(-0.5)

RWKV_CHUNK = 64
RWKV_PAIRS = 8
SB_BLOCK = 128
SB_QBLOCK = 256
SB_PAIRS = 4
S5_BLOCK = 16
S5_PAIRS = 4
S5_PREP_GROUPS = 16
S5_RELAYOUT_ROWS = 64
VMEM_LIMIT = 56 * 1024 * 1024

NN = (((1,), (0,)), ((), ()))
NT = (((1,), (1,)), ((), ()))
TN = (((0,), (0,)), ((), ()))


def _cparams(sem):
    return pltpu.CompilerParams(dimension_semantics=sem, vmem_limit_bytes=VMEM_LIMIT)


def _split(x):
    hi = x.astype(BF16)
    lo = (x - hi.astype(F32)).astype(BF16)
    return hi, lo


def _dg(a, b, dn):
    return lax.dot_general(a, b, dn, preferred_element_type=F32)


def _dot(a, b, dn=NN, passes=3):
    if passes == 1:
        return _dg(a.astype(BF16), b.astype(BF16), dn)
    ah, al = _split(a)
    bh, bl = _split(b)
    return _dg(ah, bh, dn) + (_dg(ah, bl, dn) + _dg(al, bh, dn))


def _norm_proj_kernel(x_ref, g_ref, w_ref, xn_ref, o_ref):
    x = x_ref[...]
    ms = jnp.mean(x * x, axis=-1, keepdims=True)
    xn = ((x * lax.rsqrt(ms + RMS_EPS)) * g_ref[...]).astype(xn_ref.dtype)
    xn_ref[...] = xn
    o_ref[...] = jnp.dot(xn, w_ref[...].astype(BF16), preferred_element_type=F32)


def rmsnorm_proj(x, g, w, col0, n, tm=1024):
    m, d = x.shape
    tm = min(tm, m)
    assert col0 % LANES == 0 and n % LANES == 0
    return pl.pallas_call(
        _norm_proj_kernel,
        grid=(m // tm,),
        in_specs=[pl.BlockSpec((tm, d), lambda i: (i, 0)), pl.BlockSpec((1, d), lambda i: (0, 0)),
                  pl.BlockSpec((pl.Element(d), pl.Element(n)), lambda i: (0, col0))],
        out_specs=[pl.BlockSpec((tm, d), lambda i: (i, 0)), pl.BlockSpec((tm, n), lambda i: (i, 0))],
        out_shape=[jax.ShapeDtypeStruct((m, d), BF16), jax.ShapeDtypeStruct((m, n), F32)],
        compiler_params=_cparams(("parallel",)),
        name="rmsnorm_proj_lora",
    )(x, g.reshape(1, d), w)


def _mm_kernel(a_ref, w_ref, o_ref):
    o_ref[...] = jnp.dot(a_ref[...], w_ref[...].astype(BF16), preferred_element_type=F32).astype(o_ref.dtype)


def _mm_scaled_kernel(a_ref, w_ref, s_ref, o_ref):
    acc = jnp.dot(a_ref[...], w_ref[...].astype(BF16), preferred_element_type=F32)
    o_ref[...] = (acc * s_ref[...]).astype(o_ref.dtype)


def matmul(a, w, out_dtype, col0, n, col_scale=None, tm=1024, tn=1024, name="matmul"):
    m, k = a.shape
    tm = min(tm, m)
    assert m % tm == 0 and n % tn == 0 and col0 % LANES == 0
    in_specs = [pl.BlockSpec((tm, k), lambda i, j: (i, 0)),
                pl.BlockSpec((pl.Element(k), pl.Element(tn)), lambda i, j: (0, pl.multiple_of(col0 + j * tn, LANES)))]
    args = [a, w]
    if col_scale is not None:
        in_specs.append(pl.BlockSpec((1, tn), lambda i, j: (0, j)))
        args.append(col_scale.reshape(1, n))
    return pl.pallas_call(
        _mm_kernel if col_scale is None else _mm_scaled_kernel,
        grid=(m // tm, n // tn),
        in_specs=in_specs,
        out_specs=pl.BlockSpec((tm, tn), lambda i, j: (i, j)),
        out_shape=jax.ShapeDtypeStruct((m, n), out_dtype),
        compiler_params=_cparams(("parallel", "arbitrary")),
        name=name,
    )(*args)


def _rwkv_kernel(r_ref, k_ref, v_ref, lo_ref, mr_ref, mk_ref, mv_ref, mlo_ref, wup_ref, aup_ref,
                 w0_ref, a0_ref, kk_ref, ka_ref, rk_ref, gnw_ref, gnb_ref,
                 seg_ref, tri_ref, strict_ref, incl_ref, eye_ref,
                 y_ref, s_ref, prev_ref):
    c = pl.program_id(2)
    ch = RWKV_CHUNK
    npair = r_ref.shape[1] // LANES

    @pl.when(c == 0)
    def _():
        s_ref[...] = jnp.zeros_like(s_ref)
        prev_ref[...] = jnp.zeros_like(prev_ref)

    def token_shift(x, idx, mix):
        row = lax.broadcasted_iota(jnp.int32, x.shape, 0)
        prev = prev_ref[idx:idx + 1, 0:x.shape[1]]
        shifted = jnp.where(row == 0, prev, pltpu.roll(x, 1, 0))
        return x + (shifted - x) * mix

    r_in, k_in, v_in, lo_in = r_ref[...], k_ref[...], v_ref[...], lo_ref[...]
    r = token_shift(r_in, 0, mr_ref[...])
    k = token_shift(k_in, 1, mk_ref[...])
    v = token_shift(v_in, 2, mv_ref[...])
    lo = token_shift(lo_in, 3, mlo_ref[...])
    prev_ref[0:1, :] = r_in[ch - 1:ch, :]
    prev_ref[1:2, :] = k_in[ch - 1:ch, :]
    prev_ref[2:3, :] = v_in[ch - 1:ch, :]
    prev_ref[3:4, 0:LANES] = lo_in[ch - 1:ch, :]

    z_w = w0_ref[...] + _dot(jnp.tanh(lo), wup_ref[...], passes=1)
    logw = -DECAY_SCALE * jax.nn.sigmoid(z_w)
    a = jax.nn.sigmoid(a0_ref[...] + _dot(lo, aup_ref[...], passes=1))

    seg2 = seg_ref[...]

    def head_sum(x):
        tiles = []
        for t in range(npair):
            hi, lo_ = _split(x[:, t * LANES:(t + 1) * LANES])
            tiles.append(_dg(jnp.concatenate([hi, lo_], axis=1), seg2, NN))
        return jnp.concatenate(tiles, axis=1)

    kk = k * kk_ref[...]
    kk = kk * lax.rsqrt(jnp.maximum(head_sum(kk * kk), 1e-24))
    k2 = k * (1.0 + (a - 1.0) * ka_ref[...])
    ab = kk * a

    l_hi = logw.astype(BF16)
    rem = logw - l_hi.astype(F32)
    l_mid = rem.astype(BF16)
    l_lo = (rem - l_mid.astype(F32)).astype(BF16)
    cum = _dg(tri_ref[...], jnp.concatenate([l_hi, l_mid, l_lo], axis=0), NN)
    cum_last = cum[ch - 1:ch, :]
    e_cum = jnp.exp(cum)
    e_ncum = jnp.exp(-cum)
    e_tail = jnp.exp(cum_last - cum)
    rt = r * e_cum
    kt = k2 * e_ncum
    bt = ab * e_ncum
    at = -kk * jnp.exp(cum - logw)
    khat = k2 * e_tail
    bhat = ab * e_tail
    p_last = e_cum[ch - 1:ch, :]

    lane = lax.broadcasted_iota(jnp.int32, (ch, LANES), 1)
    head0 = lane < HEAD_DIM
    strict = strict_ref[...] > 0.5
    incl = incl_ref[...] > 0.5
    eye = eye_ref[...]

    def stack(x):
        return jnp.concatenate([jnp.where(head0, x, 0.0), jnp.where(head0, 0.0, x)], axis=0)

    def mm(x, w, dn=NN):
        return _dg(x.astype(BF16), w.astype(BF16), dn)

    pairs = range(npair)
    rows = 2 * ch
    sl = [slice(p * LANES, (p + 1) * LANES) for p in pairs]
    at2, rt2, bt2, kt2, v2, khat2, bhat2 = ([stack(t[:, sl[p]]) for p in pairs]
                                            for t in (at, rt, bt, kt, v, khat, bhat))
    gram = []
    for p in pairs:
        lh, ll = _split(jnp.concatenate([at2[p], rt2[p]], axis=0))
        rh, rl = _split(jnp.concatenate([bt2[p], kt2[p]], axis=0))
        gram.append(_dg(jnp.concatenate([lh, ll], axis=1), jnp.concatenate([rh, rh], axis=1), NT))
    a_ab = [jnp.where(strict, g[:rows, :rows], 0.0) for g in gram]
    a_ak = [jnp.where(strict, g[:rows, rows:], 0.0) for g in gram]
    a_rb = [jnp.where(incl, g[rows:, :rows], 0.0) for g in gram]
    a_rk = [jnp.where(incl, g[rows:, rows:], 0.0) for g in gram]
    akv = [mm(a_ak[p], v2[p]) for p in pairs]

    tinv = [eye + a for a in a_ab]
    pw = [mm(a, a) for a in a_ab]
    for _ in range(int(math.log2(ch)) - 2):
        both = [mm(pw[p], jnp.concatenate([pw[p], tinv[p]], axis=1)) for p in pairs]
        pw = [b[:, :rows] for b in both]
        tinv = [tinv[p] + both[p][:, rows:] for p in pairs]
    tinv = [tinv[p] + mm(pw[p], tinv[p]) for p in pairs]

    s = [s_ref[p] for p in pairs]
    tw = [mm(tinv[p], jnp.concatenate([at2[p], akv[p]], axis=1)) for p in pairs]
    ws = [mm(jnp.concatenate([tw[p][:, :LANES], rt2[p]], axis=0), s[p], NT) for p in pairs]
    uv = [jnp.concatenate([ws[p][:rows] + tw[p][:, LANES:], v2[p]], axis=0) for p in pairs]
    y2 = [ws[p][rows:] + mm(jnp.concatenate([a_rb[p], a_rk[p]], axis=1), uv[p]) for p in pairs]
    for p in pairs:
        s_ref[p] = s[p] * p_last[:, sl[p]] + mm(uv[p], jnp.concatenate([bhat2[p], khat2[p]], axis=0), TN)
    y = jnp.concatenate([t[:ch, :] + t[ch:, :] for t in y2], axis=1)

    inv_n = 1.0 / HEAD_DIM
    mu = head_sum(y) * inv_n
    d = y - mu
    var = head_sum(d * d) * inv_n
    yn = d * lax.rsqrt(var + GN_EPS) * gnw_ref[...] + gnb_ref[...]
    y_ref[...] = (yn + head_sum(r * k2 * rk_ref[...]) * v).astype(y_ref.dtype)


def rwkv7(p, p_lora, batch, seqlen, shift_mix, w_up, w0, a_up, a0, k_k, k_a, r_k, gn_w, gn_b):
    width = w0.shape[0]
    npair = width // LANES
    ch = RWKV_CHUNK
    nchunk = seqlen // ch
    assert seqlen % ch == 0 and 2 * LORA_RANK == LANES and 2 * HEAD_DIM == LANES
    zeros = jnp.zeros((LORA_RANK, width), F32)
    wup_pad = jnp.concatenate([w_up, zeros], axis=0)
    aup_pad = jnp.concatenate([zeros, a_up], axis=0)
    hb = RWKV_PAIRS
    assert npair % hb == 0
    idx = jnp.arange(LANES)
    seg = (idx[:, None] // HEAD_DIM == idx[None, :] // HEAD_DIM).astype(BF16)
    seg = jnp.concatenate([seg, seg], axis=0)
    t = jnp.arange(ch)
    tri = (t[None, :] <= t[:, None]).astype(BF16)
    tri = jnp.concatenate([tri, tri, tri], axis=1)
    i2 = jnp.arange(2 * ch)
    same = (i2[:, None] // ch) == (i2[None, :] // ch)
    strict = (same & ((i2[None, :] % ch) < (i2[:, None] % ch))).astype(F32)
    incl = (same & ((i2[None, :] % ch) <= (i2[:, None] % ch))).astype(F32)
    eye = jnp.eye(2 * ch, dtype=F32)
    row2 = lambda x: x.reshape(1, -1)

    wd = hb * LANES
    ngrp = npair // hb

    def tok(off):
        return pl.BlockSpec((ch, wd), lambda b, h, c: (b * nchunk + c, off + h))

    def par(off):
        return pl.BlockSpec((1, wd), lambda b, h, c: (0, off + h))

    def const(shape):
        return pl.BlockSpec(shape, lambda b, h, c: (0, 0))

    up = pl.BlockSpec((LANES, wd), lambda b, h, c: (0, h))
    lora_tok = pl.BlockSpec((ch, LANES), lambda b, h, c: (b * nchunk + c, 0))
    lora_par = pl.BlockSpec((1, LANES), lambda b, h, c: (0, 3 * npair))
    return pl.pallas_call(
        _rwkv_kernel,
        grid=(batch, ngrp, nchunk),
        in_specs=[tok(0), tok(ngrp), tok(2 * ngrp), lora_tok,
                  par(0), par(ngrp), par(2 * ngrp), lora_par, up, up,
                  par(0), par(0), par(0), par(0), par(0), par(0), par(0),
                  const((2 * LANES, LANES)), const((ch, 3 * ch)), const((2 * ch, 2 * ch)),
                  const((2 * ch, 2 * ch)), const((2 * ch, 2 * ch))],
        out_specs=pl.BlockSpec((ch, wd), lambda b, h, c: (b * nchunk + c, h)),
        out_shape=jax.ShapeDtypeStruct((batch * seqlen, width), BF16),
        scratch_shapes=[pltpu.VMEM((hb, LANES, LANES), F32), pltpu.VMEM((SUBLANES, wd), F32)],
        compiler_params=_cparams(("parallel", "parallel", "arbitrary")),
        name="rwkv7",
    )(p, p, p, p_lora, row2(shift_mix), row2(shift_mix), row2(shift_mix), row2(shift_mix), wup_pad, aup_pad,
      row2(w0), row2(a0), row2(k_k), row2(k_a), row2(r_k), row2(gn_w), row2(gn_b),
      seg, tri, strict, incl, eye)


def _sb_kernel(q_ref, k_ref, v_ref, uo_ref, o_ref, k2_ref, v2_ref, acc_ref, carry_ref, lb_ref, sums_ref, z_ref):
    qi = pl.program_id(2)
    kb, qb = SB_BLOCK, SB_QBLOCK
    nsub = qb // kb
    npair = q_ref.shape[1] // LANES
    pairs_ = range(npair)

    @pl.when(qi == 0)
    def _():
        lane = lax.broadcasted_iota(jnp.int32, (kb, LANES), 1)
        head0 = lane < HEAD_DIM

        def fill(i, c):
            start = pl.multiple_of(i * kb, kb)
            for p in pairs_:
                for src, dst in ((k_ref, k2_ref), (v_ref, v2_ref)):
                    t = src[pl.ds(start, kb), p * LANES:(p + 1) * LANES].astype(F32)
                    dst[p, i, 0:kb, :] = jnp.where(head0, t, 0.0).astype(BF16)
                    dst[p, i, kb:2 * kb, :] = jnp.where(head0, 0.0, t).astype(BF16)
            return c

        lax.fori_loop(0, k_ref.shape[0] // kb, fill, 0)

    q = [q_ref[:, p * LANES:(p + 1) * LANES] for p in pairs_]
    uo = uo_ref[...]
    acc_ref[...] = jnp.zeros_like(acc_ref)
    carry_ref[...] = jnp.zeros_like(carry_ref)
    tpos = lax.broadcasted_iota(jnp.int32, (qb, kb), 0)
    spos = lax.broadcasted_iota(jnp.int32, (qb, kb), 1)

    def logits(p, sb, zslot):
        base = pl.multiple_of(sb * nsub, nsub)
        keys = k2_ref[p, pl.ds(base, nsub)].reshape(nsub * 2 * kb, LANES)
        z_ref[zslot, p] = _dg(q[p], keys, NT)

    def scores(p, zslot, slot, diagonal):
        for j in range(nsub):
            for h in range(2):
                c = 2 * j + h
                z = z_ref[zslot, p, :, c * kb:(c + 1) * kb]
                nz = -z
                log_keep = jnp.minimum(nz, 0.0) - jnp.log(1.0 + jnp.exp(jnp.minimum(z, nz)))
                if diagonal:
                    causal = (spos + j * kb) < tpos
                    log_keep = jnp.where(causal, log_keep, 0.0)
                    z = jnp.where(causal, z, -jnp.inf)
                hi = log_keep.astype(BF16)
                lo = (log_keep - hi.astype(F32)).astype(BF16)
                lb_ref[slot, p, c] = z
                sums_ref[slot, p, c] = _dg(jnp.concatenate([hi, lo], axis=1), uo, NN)

    def accumulate(p, sb, slot):
        base = pl.multiple_of(sb * nsub, nsub)
        attn = {}
        for h in range(2):
            carry = carry_ref[p, h]
            for j in reversed(range(nsub)):
                s = sums_ref[slot, p, 2 * j + h]
                attn[j, h] = jnp.exp(lb_ref[slot, p, 2 * j + h] + carry + s[:, :kb]).astype(BF16)
                carry = carry + s[:, kb:]
            carry_ref[p, h] = carry
        weights = jnp.concatenate([attn[j, h] for j in range(nsub) for h in range(2)], axis=1)
        values = v2_ref[p, pl.ds(base, nsub)].reshape(nsub * 2 * kb, LANES)
        acc_ref[p] += _dg(weights, values, NN)

    for p in pairs_:
        logits(p, qi, 0)
    for p in pairs_:
        scores(p, 0, 0, True)
    trips = qi // 2

    def body(i, c):
        sb = qi - 1 - 2 * i
        for p in pairs_:
            logits(p, sb, 0)
        for p in pairs_:
            logits(p, sb - 1, 1)
        for p in pairs_:
            accumulate(p, sb + 1, 0)
        for p in pairs_:
            scores(p, 0, 1, False)
        for p in pairs_:
            scores(p, 1, 0, False)
        for p in pairs_:
            accumulate(p, sb, 1)
        return c

    lax.fori_loop(0, trips, body, 0)
    odd = qi - 2 * trips == 1

    @pl.when(odd)
    def _():
        for p in pairs_:
            logits(p, 0, 0)
        for p in pairs_:
            accumulate(p, 1, 0)
        for p in pairs_:
            scores(p, 0, 1, False)
        for p in pairs_:
            accumulate(p, 0, 1)

    @pl.when(jnp.logical_not(odd))
    def _():
        for p in pairs_:
            accumulate(p, 0, 0)

    for p in pairs_:
        o_ref[:, p * LANES:(p + 1) * LANES] = acc_ref[p].astype(o_ref.dtype)


def stick_breaking(p, batch, seqlen, width):
    npair = width // LANES
    kb, qb = SB_BLOCK, SB_QBLOCK
    nq = seqlen // qb
    hp = SB_PAIRS
    ngrp = npair // hp
    wd = hp * LANES
    nsl = 2 * qb // kb
    j = jnp.arange(kb)
    later = (j[:, None] >= j[None, :]).astype(BF16)
    uo = jnp.concatenate([later, jnp.ones((kb, kb), BF16)], axis=1)
    uo = jnp.concatenate([uo, uo], axis=0)
    return pl.pallas_call(
        _sb_kernel,
        grid=(batch, ngrp, nq),
        in_specs=[pl.BlockSpec((qb, wd), lambda b, h, i: (b * nq + i, h)),
                  pl.BlockSpec((seqlen, wd), lambda b, h, i: (b, ngrp + h), pipeline_mode=pl.Buffered(1)),
                  pl.BlockSpec((seqlen, wd), lambda b, h, i: (b, 2 * ngrp + h), pipeline_mode=pl.Buffered(1)),
                  pl.BlockSpec((2 * kb, 2 * kb), lambda b, h, i: (0, 0))],
        out_specs=pl.BlockSpec((qb, wd), lambda b, h, i: (b * nq + i, h)),
        out_shape=jax.ShapeDtypeStruct((batch * seqlen, width), BF16),
        scratch_shapes=[pltpu.VMEM((hp, seqlen // kb, 2 * kb, LANES), BF16),
                        pltpu.VMEM((hp, seqlen // kb, 2 * kb, LANES), BF16),
                        pltpu.VMEM((hp, qb, LANES), F32), pltpu.VMEM((hp, 2, qb, kb), F32),
                        pltpu.VMEM((2, hp, nsl, qb, kb), F32), pltpu.VMEM((2, hp, nsl, qb, 2 * kb), F32),
                        pltpu.VMEM((2, hp, qb, nsl * kb), F32)],
        compiler_params=_cparams(("parallel", "parallel", "arbitrary")),
        name="stick_breaking",
    )(p, p, p, uo)


def _gate_out_kernel(ya_ref, yb_ref, g0_ref, g1_ref, x_ref, w_ref, ng_ref, h_ref, hn_ref):
    gate = jnp.concatenate([g0_ref[...], g1_ref[...]], axis=-1).astype(F32)
    y = jnp.concatenate([ya_ref[...], yb_ref[...]], axis=-1).astype(F32) * (gate * jax.nn.sigmoid(gate))
    h = x_ref[...] + jnp.dot(y.astype(BF16), w_ref[...], preferred_element_type=F32)
    h_ref[...] = h
    ms = jnp.mean(h * h, axis=-1, keepdims=True)
    hn_ref[...] = ((h * lax.rsqrt(ms + RMS_EPS)) * ng_ref[...]).astype(hn_ref.dtype)


def gate_out(ya, yb, p, gate_col, x, w, next_g, tm=512):
    m, d = x.shape
    half = ya.shape[1]
    tm = min(tm, m)
    assert gate_col % half == 0 and d == 2 * half
    g0 = gate_col // half
    row = lambda n: pl.BlockSpec((tm, n), lambda i: (i, 0))
    return pl.pallas_call(
        _gate_out_kernel,
        grid=(m // tm,),
        in_specs=[row(half), row(half),
                  pl.BlockSpec((tm, half), lambda i: (i, g0)), pl.BlockSpec((tm, half), lambda i: (i, g0 + 1)), row(d),
                  pl.BlockSpec((d, d), lambda i: (0, 0), pipeline_mode=pl.Buffered(1)),
                  pl.BlockSpec((1, d), lambda i: (0, 0))],
        out_specs=[row(d), row(d)],
        out_shape=[jax.ShapeDtypeStruct((m, d), F32), jax.ShapeDtypeStruct((m, d), BF16)],
        compiler_params=_cparams(("parallel",)),
        name="gate_out",
    )(ya, yb, p, p, x, w, next_g.reshape(1, d))


def _s5_prep_kernel(lre_ref, lim_ref, ldt_ref, cre_ref, cim_ref, btre_ref, btim_ref,
                    kt_ref, w2re_ref, w2im_ref, vt2_ref, c2re_ref, c2im_ref, ckre_ref, ckim_ref):
    blk = S5_BLOCK
    h = S5_GROUP
    ns = S5_STATE
    gp = kt_ref.shape[0]
    wl = blk * h
    lre, lim = lre_ref[...], lim_ref[...]
    dt = jnp.exp(ldt_ref[...])
    mag = jnp.exp(lre * dt)
    bre = mag * jnp.cos(lim * dt)
    bim = mag * jnp.sin(lim * dt)
    den = lre * lre + lim * lim
    nre, nim = bre - 1.0, bim
    fre = (nre * lre + nim * lim) / den
    fim = (nim * lre - nre * lim) / den
    cre, cim = cre_ref[...], cim_ref[...]
    ckr, cki = cre * fre - cim * fim, cre * fim + cim * fre
    btre, btim = btre_ref[...], btim_ref[...]
    pr, pi = jnp.ones_like(bre), jnp.zeros_like(bre)
    powers = []
    for tau in range(blk + 1):
        ckre_ref[:, tau * h:(tau + 1) * h, :] = ckr
        ckim_ref[:, tau * h:(tau + 1) * h, :] = cki
        powers.append((pr, pi))
        ckr, cki = ckr * bre - cki * bim, ckr * bim + cki * bre
        pr, pi = pr * bre - pi * bim, pr * bim + pi * bre

    def put(ref, rows, lane0, x):
        r0, r1 = rows
        x = x.reshape(gp // 2, 2, r1 - r0, ns).astype(ref.dtype)
        ref[:, r0:r1, lane0:lane0 + ns] = x[:, 0]
        ref[:, ref.shape[1] // 2 + r0:ref.shape[1] // 2 + r1, lane0 + ns:lane0 + 2 * ns] = x[:, 1]

    w2re_ref[...] = jnp.zeros_like(w2re_ref)
    w2im_ref[...] = jnp.zeros_like(w2im_ref)
    vt2_ref[...] = jnp.zeros_like(vt2_ref)
    for i in range(blk):
        pr, pi = powers[blk - 1 - i]
        put(w2re_ref, (i * h, (i + 1) * h), 0, pr * btre - pi * btim)
        put(w2im_ref, (i * h, (i + 1) * h), 0, pr * btim + pi * btre)
    put(vt2_ref, (0, wl), 0, ckre_ref[:, h:(blk + 1) * h, :])
    put(vt2_ref, (0, wl), 2 * ns, -ckim_ref[:, h:(blk + 1) * h, :])

    def put_lanes(ref, r0, x):
        x = x.reshape(gp // 2, 2, x.shape[1], ns)
        ref[:, r0:r0 + x.shape[2], 0:ns] = x[:, 0]
        ref[:, r0:r0 + x.shape[2], ns:2 * ns] = x[:, 1]

    row = lax.broadcasted_iota(jnp.int32, (gp, SUBLANES, ns), 1)
    qr, qi = powers[blk]
    pr, pi = qr, qi
    for r in range(SUBLANES):
        if r + 1 in (1, 2, 4):
            lvl = (1, 2, 4).index(r + 1)
            put_lanes(c2re_ref, lvl * SUBLANES, jnp.where(row >= r + 1, pr, 0.0))
            put_lanes(c2im_ref, lvl * SUBLANES, jnp.where(row >= r + 1, pi, 0.0))
        put_lanes(c2re_ref, 3 * SUBLANES + r, pr)
        put_lanes(c2im_ref, 3 * SUBLANES + r, pi)
        pr, pi = pr * qr - pi * qi, pr * qi + pi * qr
    for g in range(gp):
        kt_ref[g] = (_dot(btre[g], ckre_ref[g, 0:blk * h, :], NT) - _dot(btim[g], ckim_ref[g, 0:blk * h, :], NT))


def _s5_kernel(nbatch, p_ref, kt_ref, wre_ref, wim_ref, vt_ref, cre_ref, cim_ref, d_ref, o_ref,
               u_ref, gre_ref, gim_ref, xre_ref, xim_ref, yp_ref, stage_ref):
    npair = u_ref.shape[0] // 2
    rows = u_ref.shape[1]
    wlane = u_ref.shape[2]
    blk = S5_BLOCK
    ngrp = 2 * npair
    rc = S5_RELAYOUT_ROWS
    lane_chunk = lax.broadcasted_iota(jnp.int32, (rc, LANES), 1) // S5_GROUP

    def chunk_transpose(arrs):
        arrs = list(arrs)
        s = ngrp // 2
        while s:
            upper = (lane_chunk & s) != 0
            for x in range(ngrp):
                if x & s == 0:
                    ax, ay = arrs[x], arrs[x + s]
                    arrs[x] = jnp.where(upper, pltpu.roll(ay, s * S5_GROUP, 1), ax)
                    arrs[x + s] = jnp.where(upper, ay, pltpu.roll(ax, LANES - s * S5_GROUP, 1))
            s //= 2
        return arrs

    def relayout_in(c, carry):
        t0 = pl.multiple_of(c * (rc * blk), rc * blk)
        r0 = pl.multiple_of(c * rc, rc)
        stage_ref[...] = p_ref[pl.ds(t0, rc * blk), :].astype(F32)
        for half in range(blk // ngrp):
            z = chunk_transpose(stage_ref[pl.ds(ngrp * half + i, rc, stride=blk), :] for i in range(ngrp))
            for g in range(ngrp):
                u_ref[g, pl.ds(r0, rc), half * LANES:(half + 1) * LANES] = z[g].astype(BF16)
        return carry

    lax.fori_loop(0, rows // rc, relayout_in, 0)
    lane = lax.broadcasted_iota(jnp.int32, (S5_GROUP, wlane), 1)

    for g in range(2 * npair):
        kt = kt_ref[g]
        blocks = [kt] + [jnp.where(lane >= i * S5_GROUP, pltpu.roll(kt, i * S5_GROUP, 1), 0.0)
                         for i in range(1, S5_BLOCK)]
        kmat = jnp.concatenate(blocks, axis=0).astype(BF16)
        yp_ref[g] = _dg(u_ref[g], kmat, NN)
    for p in range(npair):
        ucat = jnp.concatenate([u_ref[2 * p], u_ref[2 * p + 1]], axis=1)
        gre_ref[p] = _dg(ucat, wre_ref[p], NN)
        gim_ref[p] = _dg(ucat, wim_ref[p], NN)

    def cma(xr, xi, ar, ai, sr, si):
        return xr + (ar * sr - ai * si), xi + (ar * si + ai * sr)

    row = lax.broadcasted_iota(jnp.int32, (SUBLANES, LANES), 0)
    per_batch = rows // nbatch

    def tile(t, carry, base):
        r0 = pl.multiple_of(base + t * SUBLANES, SUBLANES)
        new_carry = []
        for p in range(npair):
            xr = gre_ref[p, pl.ds(r0, SUBLANES), :]
            xi = gim_ref[p, pl.ds(r0, SUBLANES), :]
            for lvl, sh in enumerate((1, 2, 4)):
                ar = cre_ref[p, lvl * SUBLANES:(lvl + 1) * SUBLANES, :]
                ai = cim_ref[p, lvl * SUBLANES:(lvl + 1) * SUBLANES, :]
                xr, xi = cma(xr, xi, ar, ai, pltpu.roll(xr, sh, 0), pltpu.roll(xi, sh, 0))
            ar = cre_ref[p, 3 * SUBLANES:4 * SUBLANES, :]
            ai = cim_ref[p, 3 * SUBLANES:4 * SUBLANES, :]
            cr, ci = carry[2 * p], carry[2 * p + 1]
            xr, xi = cma(xr, xi, ar, ai, cr, ci)
            xre_ref[p, pl.ds(r0, SUBLANES), :] = jnp.where(row == 0, cr, pltpu.roll(xr, 1, 0))
            xim_ref[p, pl.ds(r0, SUBLANES), :] = jnp.where(row == 0, ci, pltpu.roll(xi, 1, 0))
            new_carry.append(jnp.broadcast_to(xr[SUBLANES - 1:SUBLANES, :], (SUBLANES, LANES)))
            new_carry.append(jnp.broadcast_to(xi[SUBLANES - 1:SUBLANES, :], (SUBLANES, LANES)))
        return tuple(new_carry)

    zero = jnp.zeros((SUBLANES, LANES), F32)
    for b in range(nbatch):
        lax.fori_loop(0, per_batch // SUBLANES, functools.partial(tile, base=b * per_batch), (zero,) * (2 * npair))

    for p in range(npair):
        xs = jnp.concatenate([xre_ref[p], xim_ref[p]], axis=1).astype(BF16)
        corr = _dg(xs, vt_ref[p], NT)
        for k in range(2):
            g = 2 * p + k
            y = yp_ref[g] + corr[:, k * wlane:(k + 1) * wlane] + d_ref[g] * u_ref[g].astype(F32)
            yp_ref[g] = jax.nn.gelu(y)

    def relayout_out(c, carry):
        t0 = pl.multiple_of(c * (rc * blk), rc * blk)
        r0 = pl.multiple_of(c * rc, rc)
        for half in range(blk // ngrp):
            yt = chunk_transpose(yp_ref[g, pl.ds(r0, rc), half * LANES:(half + 1) * LANES] for g in range(ngrp))
            for j in range(ngrp):
                stage_ref[pl.ds(ngrp * half + j, rc, stride=blk), :] = yt[j]
        o_ref[pl.ds(t0, rc * blk), :] = stage_ref[...].astype(o_ref.dtype)
        return carry

    lax.fori_loop(0, rows // rc, relayout_out, 0)


def s5_ssm(p, batch, seqlen, lam_re, lam_im, log_dt, b_re, b_im, c_re, c_im, d_skip):
    ngroup, nstate = lam_re.shape
    h, blk = S5_GROUP, S5_BLOCK
    width = ngroup * h
    wl = blk * h
    nb = batch * seqlen // blk
    gp = S5_PREP_GROUPS
    assert nstate == S5_STATE and 2 * nstate == LANES
    assert seqlen % (blk * SUBLANES) == 0 and ngroup % gp == 0 and ngroup % (2 * S5_PAIRS) == 0

    g3 = lambda x: x.reshape(ngroup, 1, nstate)
    bt = lambda x: jnp.swapaxes(x, 1, 2)
    spec1 = pl.BlockSpec((gp, 1, nstate), lambda i: (i, 0, 0))
    spec_c = pl.BlockSpec((gp, h, nstate), lambda i: (i, 0, 0))
    spec_k = pl.BlockSpec((gp, h, wl), lambda i: (i, 0, 0))
    pair = lambda r, c: pl.BlockSpec((gp // 2, r, c), lambda i: (i, 0, 0))
    npairs = ngroup // 2
    kt, w_re2, w_im2, vt2, c_re2, c_im2 = pl.pallas_call(
        _s5_prep_kernel,
        grid=(ngroup // gp,),
        in_specs=[spec1, spec1, spec1, spec_c, spec_c, spec_c, spec_c],
        out_specs=[spec_k, pair(2 * wl, LANES), pair(2 * wl, LANES), pair(2 * wl, 2 * LANES),
                   pair(4 * SUBLANES, LANES), pair(4 * SUBLANES, LANES)],
        out_shape=[jax.ShapeDtypeStruct((ngroup, h, wl), F32),
                   jax.ShapeDtypeStruct((npairs, 2 * wl, LANES), BF16), jax.ShapeDtypeStruct((npairs, 2 * wl, LANES), BF16),
                   jax.ShapeDtypeStruct((npairs, 2 * wl, 2 * LANES), BF16),
                   jax.ShapeDtypeStruct((npairs, 4 * SUBLANES, LANES), F32),
                   jax.ShapeDtypeStruct((npairs, 4 * SUBLANES, LANES), F32)],
        scratch_shapes=[pltpu.VMEM((gp, (blk + 1) * h, nstate), F32), pltpu.VMEM((gp, (blk + 1) * h, nstate), F32)],
        compiler_params=_cparams(("parallel",)),
        name="s5_prep",
    )(g3(lam_re), g3(lam_im), jnp.broadcast_to(log_dt[:, None, None], (ngroup, 1, nstate)), c_re, c_im,
      bt(b_re), bt(b_im))

    d_row = jnp.tile(d_skip.reshape(ngroup, 1, h), (1, 1, blk))

    np_ = S5_PAIRS
    gs = 2 * np_
    assert gs * h == LANES and blk % gs == 0 and nb % S5_RELAYOUT_ROWS == 0
    tok = pl.BlockSpec((nb * blk, LANES), lambda i: (0, i))
    grp = lambda r, c: pl.BlockSpec((gs, r, c), lambda i: (i, 0, 0))
    par = lambda r, c: pl.BlockSpec((np_, r, c), lambda i: (i, 0, 0))
    return pl.pallas_call(
        functools.partial(_s5_kernel, batch),
        grid=(ngroup // gs,),
        in_specs=[tok, grp(h, wl), par(2 * wl, LANES), par(2 * wl, LANES), par(2 * wl, 2 * LANES),
                  par(4 * SUBLANES, LANES), par(4 * SUBLANES, LANES), grp(1, wl)],
        out_specs=tok,
        out_shape=jax.ShapeDtypeStruct((batch * seqlen, width), BF16),
        scratch_shapes=[pltpu.VMEM((gs, nb, wl), BF16)] + [pltpu.VMEM((np_, nb, LANES), F32)] * 4
        + [pltpu.VMEM((gs, nb, wl), F32), pltpu.VMEM((S5_RELAYOUT_ROWS * blk, LANES), F32)],
        compiler_params=_cparams(("parallel",)),
        name="s5_ssm",
    )(p, kt, w_re2, w_im2, vt2, c_re2, c_im2, d_row)


def _glu_kernel(y_ref, yj_ref, gj_ref, w_ref, b_ref, o_ref):
    z = jnp.dot(y_ref[...], w_ref[...].astype(BF16), preferred_element_type=F32) + b_ref[...]
    gate = gj_ref[...].astype(F32)
    yj = yj_ref[...].astype(F32)
    o_ref[...] = (yj * jax.nn.sigmoid(z) * (gate * jax.nn.sigmoid(gate))).astype(o_ref.dtype)


def glu_gate(y, p, w, b, tm=1024, tn=1024):
    m, d = y.shape
    tm = min(tm, m)
    goff = d // tn
    return pl.pallas_call(
        _glu_kernel,
        grid=(m // tm, d // tn),
        in_specs=[pl.BlockSpec((tm, d), lambda i, j: (i, 0)),
                  pl.BlockSpec((tm, tn), lambda i, j: (i, j)),
                  pl.BlockSpec((tm, tn), lambda i, j: (i, goff + j)),
                  pl.BlockSpec((d, tn), lambda i, j: (0, j)),
                  pl.BlockSpec((1, tn), lambda i, j: (0, j))],
        out_specs=pl.BlockSpec((tm, tn), lambda i, j: (i, j)),
        out_shape=jax.ShapeDtypeStruct((m, d), BF16),
        compiler_params=_cparams(("parallel", "arbitrary")),
        name="glu_gate",
    )(y, y, p, w, b.reshape(1, d))


def _final_kernel(a_ref, h_ref, w_ref, g_ref, o_ref, wb_ref):
    @pl.when(pl.program_id(0) == 0)
    def _():
        wb_ref[...] = w_ref[...].astype(BF16)

    h = h_ref[...] + jnp.dot(a_ref[...], wb_ref[...], preferred_element_type=F32)
    ms = jnp.mean(h * h, axis=-1, keepdims=True)
    o_ref[...] = (h * lax.rsqrt(ms + RMS_EPS)) * g_ref[...]


def final_out(a, h, w, g, tm=512):
    m, d = h.shape
    tm = min(tm, m)
    row = pl.BlockSpec((tm, d), lambda i: (i, 0))
    return pl.pallas_call(
        _final_kernel,
        grid=(m // tm,),
        in_specs=[row, row, pl.BlockSpec((d, d), lambda i: (0, 0), pipeline_mode=pl.Buffered(1)),
                  pl.BlockSpec((1, d), lambda i: (0, 0))],
        out_specs=row,
        out_shape=jax.ShapeDtypeStruct((m, d), F32),
        scratch_shapes=[pltpu.VMEM((d, d), BF16)],
        compiler_params=_cparams(("arbitrary",)),
        name="final_out",
    )(a, h, w, g.reshape(1, d))


def kernel(x, norm_g, final_g, ab_w_in, rwkv_shift_mix, rwkv_w_up, rwkv_w0, rwkv_a_up, rwkv_a0, rwkv_k_k, rwkv_k_a, rwkv_r_k, rwkv_gn_w, rwkv_gn_b, ab_w_out, s5_w_in, s5_lam_re, s5_lam_im, s5_log_dt, s5_b_re, s5_b_im, s5_c_re, s5_c_im, s5_d, s5_w_glu, s5_b_glu, s5_w_out):
    batch, seqlen, d = x.shape
    m = batch * seqlen
    rwkv_w = rwkv_w0.shape[1]
    rwkv_proj = 3 * rwkv_w + 2 * LORA_RANK
    sb_w = (ab_w_in.shape[2] - rwkv_proj - d) // 3
    x2 = x.reshape(m, d)

    w_in = ab_w_in[0]
    scale = HEAD_DIM ** -0.5
    col_scale = jnp.concatenate([jnp.full((sb_w,), scale, F32), jnp.ones((2 * sb_w + d,), F32)])
    xn, p_lora = rmsnorm_proj(x2, norm_g[0], w_in, 3 * rwkv_w, 2 * LORA_RANK)
    p_rkv = matmul(xn, w_in, F32, 0, 3 * rwkv_w, name="proj_rkv")
    p_sb = matmul(xn, w_in, BF16, rwkv_proj, 3 * sb_w + d, col_scale, tm=2048, name="proj_sb_gate")
    y_a = rwkv7(p_rkv, p_lora, batch, seqlen, rwkv_shift_mix[0], rwkv_w_up[0], rwkv_w0[0], rwkv_a_up[0], rwkv_a0[0],
                rwkv_k_k[0], rwkv_k_a[0], rwkv_r_k[0], rwkv_gn_w[0], rwkv_gn_b[0])
    y_b = stick_breaking(p_sb, batch, seqlen, sb_w)
    h1, hn1 = gate_out(y_a, y_b, p_sb, 3 * sb_w, x2, ab_w_out[0].astype(BF16), norm_g[1])

    p1 = matmul(hn1, s5_w_in[0], BF16, 0, 2 * d, tm=2048, name="proj_s5")
    y_s5 = s5_ssm(p1, batch, seqlen, s5_lam_re[0], s5_lam_im[0], s5_log_dt[0], s5_b_re[0], s5_b_im[0],
                  s5_c_re[0], s5_c_im[0], s5_d[0])
    act = glu_gate(y_s5, p1, s5_w_glu[0], s5_b_glu[0])
    out = final_out(act, h1, s5_w_out[0], final_g)
    return out.reshape(batch, seqlen, d)
```

```python
import functools
import math

import jax
import jax.numpy as jnp
from jax import lax
from jax.experimental import pallas as pl
from jax.experimental.pallas import tpu as pltpu

F32 = jnp.float32
BF16 = jnp.bfloat16

HEAD_DIM = 64
LANES = 128
SUBLANES = 8
LORA_RANK = 64
S5_GROUP = 16
S5_STATE = 64
RMS_EPS = 1e-6
GN_EPS = 64e-5
DECAY_SCALE = math.exp(-0.5)

RWKV_CHUNK = 64
RWKV_PAIRS = 8
SB_BLOCK = 128
SB_QBLOCK = 256
SB_PAIRS = 4
S5_BLOCK = 16
S5_PAIRS = 4
S5_PREP_GROUPS = 16
S5_RELAYOUT_ROWS = 64
GLU_SUBTILE = 256
VMEM_LIMIT = 56 * 1024 * 1024

NN = (((1,), (0,)), ((), ()))
NT = (((1,), (1,)), ((), ()))
TN = (((0,), (0,)), ((), ()))


def _cparams(sem):
    return pltpu.CompilerParams(dimension_semantics=sem, vmem_limit_bytes=VMEM_LIMIT)


def _split(x):
    hi = x.astype(BF16)
    lo = (x - hi.astype(F32)).astype(BF16)
    return hi, lo


def _dg(a, b, dn):
    return lax.dot_general(a, b, dn, preferred_element_type=F32)


def _dot(a, b, dn=NN, passes=3):
    if passes == 1:
        return _dg(a.astype(BF16), b.astype(BF16), dn)
    ah, al = _split(a)
    bh, bl = _split(b)
    return _dg(ah, bh, dn) + (_dg(ah, bl, dn) + _dg(al, bh, dn))


def _norm_proj_kernel(x_ref, g_ref, w_ref, xn_ref, o_ref):
    x = x_ref[...]
    ms = jnp.mean(x * x, axis=-1, keepdims=True)
    xn = ((x * lax.rsqrt(ms + RMS_EPS)) * g_ref[...]).astype(xn_ref.dtype)
    xn_ref[...] = xn
    o_ref[...] = jnp.dot(xn, w_ref[...].astype(BF16), preferred_element_type=F32)


def rmsnorm_proj(x, g, w, col0, n, tm=1024):
    m, d = x.shape
    tm = min(tm, m)
    assert col0 % LANES == 0 and n % LANES == 0
    return pl.pallas_call(
        _norm_proj_kernel,
        grid=(m // tm,),
        in_specs=[pl.BlockSpec((tm, d), lambda i: (i, 0)), pl.BlockSpec((1, d), lambda i: (0, 0)),
                  pl.BlockSpec((pl.Element(d), pl.Element(n)), lambda i: (0, col0))],
        out_specs=[pl.BlockSpec((tm, d), lambda i: (i, 0)), pl.BlockSpec((tm, n), lambda i: (i, 0))],
        out_shape=[jax.ShapeDtypeStruct((m, d), BF16), jax.ShapeDtypeStruct((m, n), F32)],
        compiler_params=_cparams(("parallel",)),
        name="rmsnorm_proj_lora",
    )(x, g.reshape(1, d), w)


def _mm_kernel(a_ref, w_ref, o_ref):
    o_ref[...] = jnp.dot(a_ref[...], w_ref[...].astype(BF16), preferred_element_type=F32).astype(o_ref.dtype)


def _mm_scaled_kernel(a_ref, w_ref, s_ref, o_ref):
    acc = jnp.dot(a_ref[...], w_ref[...].astype(BF16), preferred_element_type=F32)
    o_ref[...] = (acc * s_ref[...]).astype(o_ref.dtype)


def matmul(a, w, out_dtype, col0, n, col_scale=None, tm=1024, tn=1024, name="matmul"):
    m, k = a.shape
    tm = min(tm, m)
    assert m % tm == 0 and n % tn == 0 and col0 % LANES == 0
    in_specs = [pl.BlockSpec((tm, k), lambda i, j: (i, 0)),
                pl.BlockSpec((pl.Element(k), pl.Element(tn)), lambda i, j: (0, pl.multiple_of(col0 + j * tn, LANES)))]
    args = [a, w]
    if col_scale is not None:
        in_specs.append(pl.BlockSpec((1, tn), lambda i, j: (0, j)))
        args.append(col_scale.reshape(1, n))
    return pl.pallas_call(
        _mm_kernel if col_scale is None else _mm_scaled_kernel,
        grid=(m // tm, n // tn),
        in_specs=in_specs,
        out_specs=pl.BlockSpec((tm, tn), lambda i, j: (i, j)),
        out_shape=jax.ShapeDtypeStruct((m, n), out_dtype),
        compiler_params=_cparams(("parallel", "arbitrary")),
        name=name,
    )(*args)


def _rwkv_kernel(r_ref, k_ref, v_ref, lo_ref, mr_ref, mk_ref, mv_ref, mlo_ref, wup_ref, aup_ref,
                 w0_ref, a0_ref, kk_ref, ka_ref, rk_ref, gnw_ref, gnb_ref,
                 seg_ref, tri_ref, strict_ref, incl_ref, eye_ref,
                 y_ref, s_ref, prev_ref):
    c = pl.program_id(1)
    ch = RWKV_CHUNK
    nbatch = r_ref.shape[0]
    npair = nbatch * r_ref.shape[2] // LANES

    def lanes(ref):
        return jnp.concatenate([ref[b] for b in range(nbatch)], axis=1)

    def tiled(ref):
        return jnp.concatenate([ref[...]] * nbatch, axis=1)

    @pl.when(c == 0)
    def _():
        s_ref[...] = jnp.zeros_like(s_ref)
        prev_ref[...] = jnp.zeros_like(prev_ref)

    def token_shift(x, idx, mix):
        row = lax.broadcasted_iota(jnp.int32, x.shape, 0)
        prev = prev_ref[idx:idx + 1, 0:x.shape[1]]
        shifted = jnp.where(row == 0, prev, pltpu.roll(x, 1, 0))
        return x + (shifted - x) * mix

    r_in, k_in, v_in, lo_in = lanes(r_ref), lanes(k_ref), lanes(v_ref), lanes(lo_ref)
    r = token_shift(r_in, 0, tiled(mr_ref))
    k = token_shift(k_in, 1, tiled(mk_ref))
    v = token_shift(v_in, 2, tiled(mv_ref))
    lo = token_shift(lo_in, 3, tiled(mlo_ref))
    prev_ref[0:1, :] = r_in[ch - 1:ch, :]
    prev_ref[1:2, :] = k_in[ch - 1:ch, :]
    prev_ref[2:3, :] = v_in[ch - 1:ch, :]
    prev_ref[3:4, 0:nbatch * LANES] = lo_in[ch - 1:ch, :]

    lo_b = [lo[:, b * LANES:(b + 1) * LANES] for b in range(nbatch)]
    z_w = tiled(w0_ref) + jnp.concatenate([_dot(jnp.tanh(x), wup_ref[...], passes=1) for x in lo_b], axis=1)
    logw = -DECAY_SCALE * jax.nn.sigmoid(z_w)
    a = jax.nn.sigmoid(tiled(a0_ref) + jnp.concatenate([_dot(x, aup_ref[...], passes=1) for x in lo_b], axis=1))

    seg2 = seg_ref[...]

    def head_sum(x):
        tiles = []
        for t in range(npair):
            hi, lo_ = _split(x[:, t * LANES:(t + 1) * LANES])
            tiles.append(_dg(jnp.concatenate([hi, lo_], axis=1), seg2, NN))
        return jnp.concatenate(tiles, axis=1)

    kk = k * tiled(kk_ref)
    kk = kk * lax.rsqrt(jnp.maximum(head_sum(kk * kk), 1e-24))
    k2 = k * (1.0 + (a - 1.0) * tiled(ka_ref))
    ab = kk * a

    l_hi = logw.astype(BF16)
    rem = logw - l_hi.astype(F32)
    l_mid = rem.astype(BF16)
    l_lo = (rem - l_mid.astype(F32)).astype(BF16)
    cum = _dg(tri_ref[...], jnp.concatenate([l_hi, l_mid, l_lo], axis=0), NN)
    cum_last = cum[ch - 1:ch, :]
    e_cum = jnp.exp(cum)
    e_ncum = jnp.exp(-cum)
    e_tail = jnp.exp(cum_last - cum)
    rt = r * e_cum
    kt = k2 * e_ncum
    bt = ab * e_ncum
    at = -kk * jnp.exp(cum - logw)
    khat = k2 * e_tail
    bhat = ab * e_tail
    p_last = e_cum[ch - 1:ch, :]

    lane = lax.broadcasted_iota(jnp.int32, (ch, LANES), 1)
    head0 = lane < HEAD_DIM
    strict = strict_ref[...] > 0.5
    incl = incl_ref[...] > 0.5
    eye = eye_ref[...]

    def stack(x):
        return jnp.concatenate([jnp.where(head0, x, 0.0), jnp.where(head0, 0.0, x)], axis=0)

    def mm(x, w, dn=NN):
        return _dg(x.astype(BF16), w.astype(BF16), dn)

    pairs = range(npair)
    rows = 2 * ch
    sl = [slice(p * LANES, (p + 1) * LANES) for p in pairs]
    at2, rt2, bt2, kt2, v2, khat2, bhat2 = ([stack(t[:, sl[p]]) for p in pairs]
                                            for t in (at, rt, bt, kt, v, khat, bhat))
    gram = []
    for p in pairs:
        lh, ll = _split(jnp.concatenate([at2[p], rt2[p]], axis=0))
        rh, rl = _split(jnp.concatenate([bt2[p], kt2[p]], axis=0))
        gram.append(_dg(jnp.concatenate([lh, ll], axis=1), jnp.concatenate([rh, rh], axis=1), NT))
    a_ab = [jnp.where(strict, g[:rows, :rows], 0.0) for g in gram]
    a_ak = [jnp.where(strict, g[:rows, rows:], 0.0) for g in gram]
    a_rb = [jnp.where(incl, g[rows:, :rows], 0.0) for g in gram]
    a_rk = [jnp.where(incl, g[rows:, rows:], 0.0) for g in gram]
    akv = [mm(a_ak[p], v2[p]) for p in pairs]

    tinv = [eye + a for a in a_ab]
    pw = [mm(a, a) for a in a_ab]
    for _ in range(int(math.log2(ch)) - 2):
        both = [mm(pw[p], jnp.concatenate([pw[p], tinv[p]], axis=1)) for p in pairs]
        pw = [b[:, :rows] for b in both]
        tinv = [tinv[p] + both[p][:, rows:] for p in pairs]
    tinv = [tinv[p] + mm(pw[p], tinv[p]) for p in pairs]

    s = [s_ref[p] for p in pairs]
    tw = [mm(tinv[p], jnp.concatenate([at2[p], akv[p]], axis=1)) for p in pairs]
    ws = [mm(jnp.concatenate([tw[p][:, :LANES], rt2[p]], axis=0), s[p], NT) for p in pairs]
    uv = [jnp.concatenate([ws[p][:rows] + tw[p][:, LANES:], v2[p]], axis=0) for p in pairs]
    y2 = [ws[p][rows:] + mm(jnp.concatenate([a_rb[p], a_rk[p]], axis=1), uv[p]) for p in pairs]
    for p in pairs:
        s_ref[p] = s[p] * p_last[:, sl[p]] + mm(uv[p], jnp.concatenate([bhat2[p], khat2[p]], axis=0), TN)
    y = jnp.concatenate([t[:ch, :] + t[ch:, :] for t in y2], axis=1)

    inv_n = 1.0 / HEAD_DIM
    mu = head_sum(y) * inv_n
    d = y - mu
    var = head_sum(d * d) * inv_n
    yn = d * lax.rsqrt(var + GN_EPS) * tiled(gnw_ref) + tiled(gnb_ref)
    y = (yn + head_sum(r * k2 * tiled(rk_ref)) * v).astype(y_ref.dtype)
    wd = y_ref.shape[2]
    for b in range(nbatch):
        y_ref[b] = y[:, b * wd:(b + 1) * wd]


def rwkv7(p, p_lora, batch, seqlen, shift_mix, w_up, w0, a_up, a0, k_k, k_a, r_k, gn_w, gn_b):
    width = w0.shape[0]
    npair = width // LANES
    ch = RWKV_CHUNK
    nchunk = seqlen // ch
    assert seqlen % ch == 0 and 2 * LORA_RANK == LANES and 2 * HEAD_DIM == LANES
    zeros = jnp.zeros((LORA_RANK, width), F32)
    wup_pad = jnp.concatenate([w_up, zeros], axis=0)
    aup_pad = jnp.concatenate([zeros, a_up], axis=0)
    hb = RWKV_PAIRS
    assert npair % hb == 0
    idx = jnp.arange(LANES)
    seg = (idx[:, None] // HEAD_DIM == idx[None, :] // HEAD_DIM).astype(BF16)
    seg = jnp.concatenate([seg, seg], axis=0)
    t = jnp.arange(ch)
    tri = (t[None, :] <= t[:, None]).astype(BF16)
    tri = jnp.concatenate([tri, tri, tri], axis=1)
    i2 = jnp.arange(2 * ch)
    same = (i2[:, None] // ch) == (i2[None, :] // ch)
    strict = (same & ((i2[None, :] % ch) < (i2[:, None] % ch))).astype(F32)
    incl = (same & ((i2[None, :] % ch) <= (i2[:, None] % ch))).astype(F32)
    eye = jnp.eye(2 * ch, dtype=F32)
    row2 = lambda x: x.reshape(1, -1)

    wd = hb * LANES
    ngrp = npair // hb

    def tok(off):
        return pl.BlockSpec((batch, ch, wd), lambda h, c: (0, c, off + h))

    def par(off):
        return pl.BlockSpec((1, wd), lambda h, c: (0, off + h))

    def const(shape):
        return pl.BlockSpec(shape, lambda h, c: (0, 0))

    up = pl.BlockSpec((LANES, wd), lambda h, c: (0, h))
    lora_tok = pl.BlockSpec((batch, ch, LANES), lambda h, c: (0, c, 0))
    lora_par = pl.BlockSpec((1, LANES), lambda h, c: (0, 3 * npair))
    p3 = p.reshape(batch, seqlen, 3 * width)
    y = pl.pallas_call(
        _rwkv_kernel,
        grid=(ngrp, nchunk),
        in_specs=[tok(0), tok(ngrp), tok(2 * ngrp), lora_tok,
                  par(0), par(ngrp), par(2 * ngrp), lora_par, up, up,
                  par(0), par(0), par(0), par(0), par(0), par(0), par(0),
                  const((2 * LANES, LANES)), const((ch, 3 * ch)), const((2 * ch, 2 * ch)),
                  const((2 * ch, 2 * ch)), const((2 * ch, 2 * ch))],
        out_specs=pl.BlockSpec((batch, ch, wd), lambda h, c: (0, c, h)),
        out_shape=jax.ShapeDtypeStruct((batch, seqlen, width), BF16),
        scratch_shapes=[pltpu.VMEM((batch * hb, LANES, LANES), F32), pltpu.VMEM((SUBLANES, batch * wd), F32)],
        compiler_params=_cparams(("parallel", "arbitrary")),
        name="rwkv7",
    )(p3, p3, p3, p_lora.reshape(batch, seqlen, LANES), row2(shift_mix), row2(shift_mix), row2(shift_mix),
      row2(shift_mix), wup_pad, aup_pad,
      row2(w0), row2(a0), row2(k_k), row2(k_a), row2(r_k), row2(gn_w), row2(gn_b),
      seg, tri, strict, incl, eye)
    return y.reshape(batch * seqlen, width)


def _sb_kernel(q_ref, k_ref, v_ref, uo_ref, o_ref, k2_ref, v2_ref, acc_ref, carry_ref, lb_ref, sums_ref, z_ref):
    qi = pl.program_id(2)
    kb, qb = SB_BLOCK, SB_QBLOCK
    nsub = qb // kb
    npair = q_ref.shape[1] // LANES
    pairs_ = range(npair)

    @pl.when(qi == 0)
    def _():
        lane = lax.broadcasted_iota(jnp.int32, (kb, LANES), 1)
        head0 = lane < HEAD_DIM

        def fill(i, c):
            start = pl.multiple_of(i * kb, kb)
            for p in pairs_:
                for src, dst in ((k_ref, k2_ref), (v_ref, v2_ref)):
                    t = src[pl.ds(start, kb), p * LANES:(p + 1) * LANES].astype(F32)
                    dst[p, i, 0:kb, :] = jnp.where(head0, t, 0.0).astype(BF16)
                    dst[p, i, kb:2 * kb, :] = jnp.where(head0, 0.0, t).astype(BF16)
            return c

        lax.fori_loop(0, k_ref.shape[0] // kb, fill, 0)

    q = [q_ref[:, p * LANES:(p + 1) * LANES] for p in pairs_]
    uo = uo_ref[...]
    acc_ref[...] = jnp.zeros_like(acc_ref)
    carry_ref[...] = jnp.zeros_like(carry_ref)
    tpos = lax.broadcasted_iota(jnp.int32, (qb, kb), 0)
    spos = lax.broadcasted_iota(jnp.int32, (qb, kb), 1)

    def logits(p, sb, zslot):
        base = pl.multiple_of(sb * nsub, nsub)
        keys = k2_ref[p, pl.ds(base, nsub)].reshape(nsub * 2 * kb, LANES)
        z_ref[zslot, p] = _dg(q[p], keys, NT)

    def scores(p, zslot, slot, diagonal):
        for j in range(nsub):
            for h in range(2):
                c = 2 * j + h
                z = z_ref[zslot, p, :, c * kb:(c + 1) * kb]
                nz = -z
                log_keep = jnp.minimum(nz, 0.0) - jnp.log(1.0 + jnp.exp(jnp.minimum(z, nz)))
                if diagonal:
                    causal = (spos + j * kb) < tpos
                    log_keep = jnp.where(causal, log_keep, 0.0)
                    z = jnp.where(causal, z, -jnp.inf)
                hi = log_keep.astype(BF16)
                lo = (log_keep - hi.astype(F32)).astype(BF16)
                lb_ref[slot, p, c] = z
                sums_ref[slot, p, c] = _dg(jnp.concatenate([hi, lo], axis=1), uo, NN)

    def accumulate(p, sb, slot):
        base = pl.multiple_of(sb * nsub, nsub)
        attn = {}
        for h in range(2):
            carry = carry_ref[p, h]
            for j in reversed(range(nsub)):
                s = sums_ref[slot, p, 2 * j + h]
                attn[j, h] = jnp.exp(lb_ref[slot, p, 2 * j + h] + carry + s[:, :kb]).astype(BF16)
                carry = carry + s[:, kb:]
            carry_ref[p, h] = carry
        weights = jnp.concatenate([attn[j, h] for j in range(nsub) for h in range(2)], axis=1)
        values = v2_ref[p, pl.ds(base, nsub)].reshape(nsub * 2 * kb, LANES)
        acc_ref[p] += _dg(weights, values, NN)

    for p in pairs_:
        logits(p, qi, 0)
    for p in pairs_:
        scores(p, 0, 0, True)
    trips = qi // 2

    def body(i, c):
        sb = qi - 1 - 2 * i
        for p in pairs_:
            logits(p, sb, 0)
        for p in pairs_:
            logits(p, sb - 1, 1)
        for p in pairs_:
            accumulate(p, sb + 1, 0)
        for p in pairs_:
            scores(p, 0, 1, False)
        for p in pairs_:
            scores(p, 1, 0, False)
        for p in pairs_:
            accumulate(p, sb, 1)
        return c

    lax.fori_loop(0, trips, body, 0)
    odd = qi - 2 * trips == 1

    @pl.when(odd)
    def _():
        for p in pairs_:
            logits(p, 0, 0)
        for p in pairs_:
            accumulate(p, 1, 0)
        for p in pairs_:
            scores(p, 0, 1, False)
        for p in pairs_:
            accumulate(p, 0, 1)

    @pl.when(jnp.logical_not(odd))
    def _():
        for p in pairs_:
            accumulate(p, 0, 0)

    for p in pairs_:
        o_ref[:, p * LANES:(p + 1) * LANES] = acc_ref[p].astype(o_ref.dtype)


def stick_breaking(p, batch, seqlen, width):
    npair = width // LANES
    kb, qb = SB_BLOCK, SB_QBLOCK
    nq = seqlen // qb
    hp = SB_PAIRS
    ngrp = npair // hp
    wd = hp * LANES
    nsl = 2 * qb // kb
    j = jnp.arange(kb)
    later = (j[:, None] >= j[None, :]).astype(BF16)
    uo = jnp.concatenate([later, jnp.ones((kb, kb), BF16)], axis=1)
    uo = jnp.concatenate([uo, uo], axis=0)
    return pl.pallas_call(
        _sb_kernel,
        grid=(batch, ngrp, nq),
        in_specs=[pl.BlockSpec((qb, wd), lambda b, h, i: (b * nq + i, h)),
                  pl.BlockSpec((seqlen, wd), lambda b, h, i: (b, ngrp + h), pipeline_mode=pl.Buffered(1)),
                  pl.BlockSpec((seqlen, wd), lambda b, h, i: (b, 2 * ngrp + h), pipeline_mode=pl.Buffered(1)),
                  pl.BlockSpec((2 * kb, 2 * kb), lambda b, h, i: (0, 0))],
        out_specs=pl.BlockSpec((qb, wd), lambda b, h, i: (b * nq + i, h)),
        out_shape=jax.ShapeDtypeStruct((batch * seqlen, width), BF16),
        scratch_shapes=[pltpu.VMEM((hp, seqlen // kb, 2 * kb, LANES), BF16),
                        pltpu.VMEM((hp, seqlen // kb, 2 * kb, LANES), BF16),
                        pltpu.VMEM((hp, qb, LANES), F32), pltpu.VMEM((hp, 2, qb, kb), F32),
                        pltpu.VMEM((2, hp, nsl, qb, kb), F32), pltpu.VMEM((2, hp, nsl, qb, 2 * kb), F32),
                        pltpu.VMEM((2, hp, qb, nsl * kb), F32)],
        compiler_params=_cparams(("parallel", "parallel", "arbitrary")),
        name="stick_breaking",
    )(p, p, p, uo)


def _gate_out_kernel(ya_ref, yb_ref, g0_ref, g1_ref, x_ref, w_ref, ng_ref, h_ref, hn_ref):
    gate = jnp.concatenate([g0_ref[...], g1_ref[...]], axis=-1).astype(F32)
    y = jnp.concatenate([ya_ref[...], yb_ref[...]], axis=-1).astype(F32) * (gate * jax.nn.sigmoid(gate))
    h = x_ref[...] + jnp.dot(y.astype(BF16), w_ref[...], preferred_element_type=F32)
    h_ref[...] = h
    ms = jnp.mean(h * h, axis=-1, keepdims=True)
    hn_ref[...] = ((h * lax.rsqrt(ms + RMS_EPS)) * ng_ref[...]).astype(hn_ref.dtype)


def gate_out(ya, yb, p, gate_col, x, w, next_g, tm=512):
    m, d = x.shape
    half = ya.shape[1]
    tm = min(tm, m)
    assert gate_col % half == 0 and d == 2 * half
    g0 = gate_col // half
    row = lambda n: pl.BlockSpec((tm, n), lambda i: (i, 0))
    return pl.pallas_call(
        _gate_out_kernel,
        grid=(m // tm,),
        in_specs=[row(half), row(half),
                  pl.BlockSpec((tm, half), lambda i: (i, g0)), pl.BlockSpec((tm, half), lambda i: (i, g0 + 1)), row(d),
                  pl.BlockSpec((d, d), lambda i: (0, 0), pipeline_mode=pl.Buffered(1)),
                  pl.BlockSpec((1, d), lambda i: (0, 0))],
        out_specs=[row(d), row(d)],
        out_shape=[jax.ShapeDtypeStruct((m, d), F32), jax.ShapeDtypeStruct((m, d), BF16)],
        compiler_params=_cparams(("parallel",)),
        name="gate_out",
    )(ya, yb, p, p, x, w, next_g.reshape(1, d))


def _s5_prep_kernel(lre_ref, lim_ref, ldt_ref, cre_ref, cim_ref, btre_ref, btim_ref,
                    kt_ref, w2re_ref, w2im_ref, vt2_ref, c2re_ref, c2im_ref, ckre_ref, ckim_ref):
    blk = S5_BLOCK
    h = S5_GROUP
    ns = S5_STATE
    gp = kt_ref.shape[0]
    wl = blk * h
    lre, lim = lre_ref[...], lim_ref[...]
    dt = jnp.exp(ldt_ref[...])
    mag = jnp.exp(lre * dt)
    bre = mag * jnp.cos(lim * dt)
    bim = mag * jnp.sin(lim * dt)
    den = lre * lre + lim * lim
    nre, nim = bre - 1.0, bim
    fre = (nre * lre + nim * lim) / den
    fim = (nim * lre - nre * lim) / den
    cre, cim = cre_ref[...], cim_ref[...]
    ckr, cki = cre * fre - cim * fim, cre * fim + cim * fre
    btre, btim = btre_ref[...], btim_ref[...]
    pr, pi = jnp.ones_like(bre), jnp.zeros_like(bre)
    powers = []
    for tau in range(blk + 1):
        ckre_ref[:, tau * h:(tau + 1) * h, :] = ckr
        ckim_ref[:, tau * h:(tau + 1) * h, :] = cki
        powers.append((pr, pi))
        ckr, cki = ckr * bre - cki * bim, ckr * bim + cki * bre
        pr, pi = pr * bre - pi * bim, pr * bim + pi * bre

    def put(ref, rows, lane0, x):
        r0, r1 = rows
        x = x.reshape(gp // 2, 2, r1 - r0, ns).astype(ref.dtype)
        ref[:, r0:r1, lane0:lane0 + ns] = x[:, 0]
        ref[:, ref.shape[1] // 2 + r0:ref.shape[1] // 2 + r1, lane0 + ns:lane0 + 2 * ns] = x[:, 1]

    w2re_ref[...] = jnp.zeros_like(w2re_ref)
    w2im_ref[...] = jnp.zeros_like(w2im_ref)
    vt2_ref[...] = jnp.zeros_like(vt2_ref)
    for i in range(blk):
        pr, pi = powers[blk - 1 - i]
        put(w2re_ref, (i * h, (i + 1) * h), 0, pr * btre - pi * btim)
        put(w2im_ref, (i * h, (i + 1) * h), 0, pr * btim + pi * btre)
    put(vt2_ref, (0, wl), 0, ckre_ref[:, h:(blk + 1) * h, :])
    put(vt2_ref, (0, wl), 2 * ns, -ckim_ref[:, h:(blk + 1) * h, :])

    def put_lanes(ref, r0, x):
        x = x.reshape(gp // 2, 2, x.shape[1], ns)
        ref[:, r0:r0 + x.shape[2], 0:ns] = x[:, 0]
        ref[:, r0:r0 + x.shape[2], ns:2 * ns] = x[:, 1]

    row = lax.broadcasted_iota(jnp.int32, (gp, SUBLANES, ns), 1)
    qr, qi = powers[blk]
    pr, pi = qr, qi
    for r in range(SUBLANES):
        if r + 1 in (1, 2, 4):
            lvl = (1, 2, 4).index(r + 1)
            put_lanes(c2re_ref, lvl * SUBLANES, jnp.where(row >= r + 1, pr, 0.0))
            put_lanes(c2im_ref, lvl * SUBLANES, jnp.where(row >= r + 1, pi, 0.0))
        put_lanes(c2re_ref, 3 * SUBLANES + r, pr)
        put_lanes(c2im_ref, 3 * SUBLANES + r, pi)
        pr, pi = pr * qr - pi * qi, pr * qi + pi * qr
    for g in range(gp):
        kt_ref[g] = (_dot(btre[g], ckre_ref[g, 0:blk * h, :], NT) - _dot(btim[g], ckim_ref[g, 0:blk * h, :], NT))


def _s5_kernel(nbatch, p_ref, kt_ref, wre_ref, wim_ref, vt_ref, cre_ref, cim_ref, d_ref, o_ref,
               u_ref, gre_ref, gim_ref, xre_ref, xim_ref, yp_ref, stage_ref):
    npair = u_ref.shape[0] // 2
    rows = u_ref.shape[1]
    wlane = u_ref.shape[2]
    blk = S5_BLOCK
    ngrp = 2 * npair
    rc = S5_RELAYOUT_ROWS
    lane_chunk = lax.broadcasted_iota(jnp.int32, (rc, LANES), 1) // S5_GROUP

    def chunk_transpose(arrs):
        arrs = list(arrs)
        s = ngrp // 2
        while s:
            upper = (lane_chunk & s) != 0
            for x in range(ngrp):
                if x & s == 0:
                    ax, ay = arrs[x], arrs[x + s]
                    arrs[x] = jnp.where(upper, pltpu.roll(ay, s * S5_GROUP, 1), ax)
                    arrs[x + s] = jnp.where(upper, ay, pltpu.roll(ax, LANES - s * S5_GROUP, 1))
            s //= 2
        return arrs

    def relayout_in(c, carry):
        t0 = pl.multiple_of(c * (rc * blk), rc * blk)
        r0 = pl.multiple_of(c * rc, rc)
        stage_ref[...] = p_ref[pl.ds(t0, rc * blk), :].astype(F32)
        for half in range(blk // ngrp):
            z = chunk_transpose(stage_ref[pl.ds(ngrp * half + i, rc, stride=blk), :] for i in range(ngrp))
            for g in range(ngrp):
                u_ref[g, pl.ds(r0, rc), half * LANES:(half + 1) * LANES] = z[g].astype(BF16)
        return carry

    lax.fori_loop(0, rows // rc, relayout_in, 0)
    lane = lax.broadcasted_iota(jnp.int32, (S5_GROUP, wlane), 1)

    for g in range(2 * npair):
        kt = kt_ref[g]
        blocks = [kt] + [jnp.where(lane >= i * S5_GROUP, pltpu.roll(kt, i * S5_GROUP, 1), 0.0)
                         for i in range(1, S5_BLOCK)]
        kmat = jnp.concatenate(blocks, axis=0).astype(BF16)
        yp_ref[g] = _dg(u_ref[g], kmat, NN)
    for p in range(npair):
        ucat = jnp.concatenate([u_ref[2 * p], u_ref[2 * p + 1]], axis=1)
        gre_ref[p] = _dg(ucat, wre_ref[p], NN)
        gim_ref[p] = _dg(ucat, wim_ref[p], NN)

    def cma(xr, xi, ar, ai, sr, si):
        return xr + (ar * sr - ai * si), xi + (ar * si + ai * sr)

    row = lax.broadcasted_iota(jnp.int32, (SUBLANES, LANES), 0)
    per_batch = rows // nbatch

    def tile(t, carry, base):
        r0 = pl.multiple_of(base + t * SUBLANES, SUBLANES)
        new_carry = []
        for p in range(npair):
            xr = gre_ref[p, pl.ds(r0, SUBLANES), :]
            xi = gim_ref[p, pl.ds(r0, SUBLANES), :]
            for lvl, sh in enumerate((1, 2, 4)):
                ar = cre_ref[p, lvl * SUBLANES:(lvl + 1) * SUBLANES, :]
                ai = cim_ref[p, lvl * SUBLANES:(lvl + 1) * SUBLANES, :]
                xr, xi = cma(xr, xi, ar, ai, pltpu.roll(xr, sh, 0), pltpu.roll(xi, sh, 0))
            ar = cre_ref[p, 3 * SUBLANES:4 * SUBLANES, :]
            ai = cim_ref[p, 3 * SUBLANES:4 * SUBLANES, :]
            cr, ci = carry[2 * p], carry[2 * p + 1]
            xr, xi = cma(xr, xi, ar, ai, cr, ci)
            xre_ref[p, pl.ds(r0, SUBLANES), :] = jnp.where(row == 0, cr, pltpu.roll(xr, 1, 0))
            xim_ref[p, pl.ds(r0, SUBLANES), :] = jnp.where(row == 0, ci, pltpu.roll(xi, 1, 0))
            new_carry.append(jnp.broadcast_to(xr[SUBLANES - 1:SUBLANES, :], (SUBLANES, LANES)))
            new_carry.append(jnp.broadcast_to(xi[SUBLANES - 1:SUBLANES, :], (SUBLANES, LANES)))
        return tuple(new_carry)

    zero = jnp.zeros((SUBLANES, LANES), F32)
    for b in range(nbatch):
        lax.fori_loop(0, per_batch // SUBLANES, functools.partial(tile, base=b * per_batch), (zero,) * (2 * npair))

    for p in range(npair):
        xs = jnp.concatenate([xre_ref[p], xim_ref[p]], axis=1).astype(BF16)
        corr = _dg(xs, vt_ref[p], NT)
        for k in range(2):
            g = 2 * p + k
            y = yp_ref[g] + corr[:, k * wlane:(k + 1) * wlane] + d_ref[g] * u_ref[g].astype(F32)
            yp_ref[g] = jax.nn.gelu(y)

    def relayout_out(c, carry):
        t0 = pl.multiple_of(c * (rc * blk), rc * blk)
        r0 = pl.multiple_of(c * rc, rc)
        for half in range(blk // ngrp):
            yt = chunk_transpose(yp_ref[g, pl.ds(r0, rc), half * LANES:(half + 1) * LANES] for g in range(ngrp))
            for j in range(ngrp):
                stage_ref[pl.ds(ngrp * half + j, rc, stride=blk), :] = yt[j]
        o_ref[pl.ds(t0, rc * blk), :] = stage_ref[...].astype(o_ref.dtype)
        return carry

    lax.fori_loop(0, rows // rc, relayout_out, 0)


def s5_ssm(p, batch, seqlen, lam_re, lam_im, log_dt, b_re, b_im, c_re, c_im, d_skip):
    ngroup, nstate = lam_re.shape
    h, blk = S5_GROUP, S5_BLOCK
    width = ngroup * h
    wl = blk * h
    nb = batch * seqlen // blk
    gp = S5_PREP_GROUPS
    assert nstate == S5_STATE and 2 * nstate == LANES
    assert seqlen % (blk * SUBLANES) == 0 and ngroup % gp == 0 and ngroup % (2 * S5_PAIRS) == 0

    g3 = lambda x: x.reshape(ngroup, 1, nstate)
    bt = lambda x: jnp.swapaxes(x, 1, 2)
    spec1 = pl.BlockSpec((gp, 1, nstate), lambda i: (i, 0, 0))
    spec_c = pl.BlockSpec((gp, h, nstate), lambda i: (i, 0, 0))
    spec_k = pl.BlockSpec((gp, h, wl), lambda i: (i, 0, 0))
    pair = lambda r, c: pl.BlockSpec((gp // 2, r, c), lambda i: (i, 0, 0))
    npairs = ngroup // 2
    kt, w_re2, w_im2, vt2, c_re2, c_im2 = pl.pallas_call(
        _s5_prep_kernel,
        grid=(ngroup // gp,),
        in_specs=[spec1, spec1, spec1, spec_c, spec_c, spec_c, spec_c],
        out_specs=[spec_k, pair(2 * wl, LANES), pair(2 * wl, LANES), pair(2 * wl, 2 * LANES),
                   pair(4 * SUBLANES, LANES), pair(4 * SUBLANES, LANES)],
        out_shape=[jax.ShapeDtypeStruct((ngroup, h, wl), F32),
                   jax.ShapeDtypeStruct((npairs, 2 * wl, LANES), BF16), jax.ShapeDtypeStruct((npairs, 2 * wl, LANES), BF16),
                   jax.ShapeDtypeStruct((npairs, 2 * wl, 2 * LANES), BF16),
                   jax.ShapeDtypeStruct((npairs, 4 * SUBLANES, LANES), F32),
                   jax.ShapeDtypeStruct((npairs, 4 * SUBLANES, LANES), F32)],
        scratch_shapes=[pltpu.VMEM((gp, (blk + 1) * h, nstate), F32), pltpu.VMEM((gp, (blk + 1) * h, nstate), F32)],
        compiler_params=_cparams(("parallel",)),
        name="s5_prep",
    )(g3(lam_re), g3(lam_im), jnp.broadcast_to(log_dt[:, None, None], (ngroup, 1, nstate)), c_re, c_im,
      bt(b_re), bt(b_im))

    d_row = jnp.tile(d_skip.reshape(ngroup, 1, h), (1, 1, blk))

    np_ = S5_PAIRS
    gs = 2 * np_
    assert gs * h == LANES and blk % gs == 0 and nb % S5_RELAYOUT_ROWS == 0
    tok = pl.BlockSpec((nb * blk, LANES), lambda i: (0, i))
    grp = lambda r, c: pl.BlockSpec((gs, r, c), lambda i: (i, 0, 0))
    par = lambda r, c: pl.BlockSpec((np_, r, c), lambda i: (i, 0, 0))
    return pl.pallas_call(
        functools.partial(_s5_kernel, batch),
        grid=(ngroup // gs,),
        in_specs=[tok, grp(h, wl), par(2 * wl, LANES), par(2 * wl, LANES), par(2 * wl, 2 * LANES),
                  par(4 * SUBLANES, LANES), par(4 * SUBLANES, LANES), grp(1, wl)],
        out_specs=tok,
        out_shape=jax.ShapeDtypeStruct((batch * seqlen, width), BF16),
        scratch_shapes=[pltpu.VMEM((gs, nb, wl), BF16)] + [pltpu.VMEM((np_, nb, LANES), F32)] * 4
        + [pltpu.VMEM((gs, nb, wl), F32), pltpu.VMEM((S5_RELAYOUT_ROWS * blk, LANES), F32)],
        compiler_params=_cparams(("parallel",)),
        name="s5_ssm",
    )(p, kt, w_re2, w_im2, vt2, c_re2, c_im2, d_row)


def _glu_kernel(y_ref, yj_ref, gj_ref, w_ref, b_ref, o_ref):
    w = w_ref[...].astype(BF16)
    b = b_ref[...]
    for r in range(0, y_ref.shape[0], GLU_SUBTILE):
        rows = slice(r, r + GLU_SUBTILE)
        z = jnp.dot(y_ref[rows, :], w, preferred_element_type=F32) + b
        gate = gj_ref[rows, :].astype(F32)
        yj = yj_ref[rows, :].astype(F32)
        o_ref[rows, :] = (yj * jax.nn.sigmoid(z) * (gate * jax.nn.sigmoid(gate))).astype(o_ref.dtype)


def glu_gate(y, p, w, b, tm=1024, tn=1024):
    m, d = y.shape
    tm = min(tm, m)
    goff = d // tn
    return pl.pallas_call(
        _glu_kernel,
        grid=(m // tm, d // tn),
        in_specs=[pl.BlockSpec((tm, d), lambda i, j: (i, 0)),
                  pl.BlockSpec((tm, tn), lambda i, j: (i, j)),
                  pl.BlockSpec((tm, tn), lambda i, j: (i, goff + j)),
                  pl.BlockSpec((d, tn), lambda i, j: (0, j)),
                  pl.BlockSpec((1, tn), lambda i, j: (0, j))],
        out_specs=pl.BlockSpec((tm, tn), lambda i, j: (i, j)),
        out_shape=jax.ShapeDtypeStruct((m, d), BF16),
        compiler_params=_cparams(("parallel", "arbitrary")),
        name="glu_gate",
    )(y, y, p, w, b.reshape(1, d))


def _final_kernel(a_ref, h_ref, w_ref, g_ref, o_ref, wb_ref):
    @pl.when(pl.program_id(0) == 0)
    def _():
        wb_ref[...] = w_ref[...].astype(BF16)

    h = h_ref[...] + jnp.dot(a_ref[...], wb_ref[...], preferred_element_type=F32)
    ms = jnp.mean(h * h, axis=-1, keepdims=True)
    o_ref[...] = (h * lax.rsqrt(ms + RMS_EPS)) * g_ref[...]


def final_out(a, h, w, g, tm=512):
    m, d = h.shape
    tm = min(tm, m)
    row = pl.BlockSpec((tm, d), lambda i: (i, 0))
    return pl.pallas_call(
        _final_kernel,
        grid=(m // tm,),
        in_specs=[row, row, pl.BlockSpec((d, d), lambda i: (0, 0), pipeline_mode=pl.Buffered(1)),
                  pl.BlockSpec((1, d), lambda i: (0, 0))],
        out_specs=row,
        out_shape=jax.ShapeDtypeStruct((m, d), F32),
        scratch_shapes=[pltpu.VMEM((d, d), BF16)],
        compiler_params=_cparams(("arbitrary",)),
        name="final_out",
    )(a, h, w, g.reshape(1, d))


def kernel(x, norm_g, final_g, ab_w_in, rwkv_shift_mix, rwkv_w_up, rwkv_w0, rwkv_a_up, rwkv_a0, rwkv_k_k, rwkv_k_a, rwkv_r_k, rwkv_gn_w, rwkv_gn_b, ab_w_out, s5_w_in, s5_lam_re, s5_lam_im, s5_log_dt, s5_b_re, s5_b_im, s5_c_re, s5_c_im, s5_d, s5_w_glu, s5_b_glu, s5_w_out):
    batch, seqlen, d = x.shape
    m = batch * seqlen
    rwkv_w = rwkv_w0.shape[1]
    rwkv_proj = 3 * rwkv_w + 2 * LORA_RANK
    sb_w = (ab_w_in.shape[2] - rwkv_proj - d) // 3
    x2 = x.reshape(m, d)

    w_in = ab_w_in[0]
    scale = HEAD_DIM ** -0.5
    col_scale = jnp.concatenate([jnp.full((sb_w,), scale, F32), jnp.ones((2 * sb_w + d,), F32)])
    xn, p_lora = rmsnorm_proj(x2, norm_g[0], w_in, 3 * rwkv_w, 2 * LORA_RANK)
    p_rkv = matmul(xn, w_in, F32, 0, 3 * rwkv_w, name="proj_rkv")
    p_sb = matmul(xn, w_in, BF16, rwkv_proj, 3 * sb_w + d, col_scale, tm=2048, name="proj_sb_gate")
    y_a = rwkv7(p_rkv, p_lora, batch, seqlen, rwkv_shift_mix[0], rwkv_w_up[0], rwkv_w0[0], rwkv_a_up[0], rwkv_a0[0],
                rwkv_k_k[0], rwkv_k_a[0], rwkv_r_k[0], rwkv_gn_w[0], rwkv_gn_b[0])
    y_b = stick_breaking(p_sb, batch, seqlen, sb_w)
    h1, hn1 = gate_out(y_a, y_b, p_sb, 3 * sb_w, x2, ab_w_out[0].astype(BF16), norm_g[1])

    p1 = matmul(hn1, s5_w_in[0], BF16, 0, 2 * d, tm=2048, name="proj_s5")
    y_s5 = s5_ssm(p1, batch, seqlen, s5_lam_re[0], s5_lam_im[0], s5_log_dt[0], s5_b_re[0], s5_b_im[0],
                  s5_c_re[0], s5_c_im[0], s5_d[0])
    act = glu_gate(y_s5, p1, s5_w_glu[0], s5_b_glu[0])
    out = final_out(act, h1, s5_w_out[0], final_g)
    return out.reshape(batch, seqlen, d)
```

```python
import functools
import math

import jax
import jax.numpy as jnp
from jax import lax
from jax.experimental import pallas as pl
from jax.experimental.pallas import tpu as pltpu

F32 = jnp.float32
BF16 = jnp.bfloat16

HEAD_DIM = 64
LANES = 128
SUBLANES = 8
LORA_RANK = 64
S5_GROUP = 16
S5_STATE = 64
RMS_EPS = 1e-6
GN_EPS = 64e-5
DECAY_SCALE = math.exp(-0.5)

RWKV_CHUNK = 64
RWKV_PAIRS = 8
SB_BLOCK = 128
SB_QBLOCK = 256
SB_PAIRS = 4
S5_BLOCK = 16
S5_PAIRS = 4
S5_PREP_GROUPS = 16
S5_RELAYOUT_ROWS = 64
GLU_SUBTILE = 256
VMEM_LIMIT = 56 * 1024 * 1024

NN = (((1,), (0,)), ((), ()))
NT = (((1,), (1,)), ((), ()))
TN = (((0,), (0,)), ((), ()))


def _cparams(sem):
    return pltpu.CompilerParams(dimension_semantics=sem, vmem_limit_bytes=VMEM_LIMIT)


def _split(x):
    hi = x.astype(BF16)
    lo = (x - hi.astype(F32)).astype(BF16)
    return hi, lo


def _dg(a, b, dn):
    return lax.dot_general(a, b, dn, preferred_element_type=F32)


def _dot(a, b, dn=NN, passes=3):
    if passes == 1:
        return _dg(a.astype(BF16), b.astype(BF16), dn)
    ah, al = _split(a)
    bh, bl = _split(b)
    return _dg(ah, bh, dn) + (_dg(ah, bl, dn) + _dg(al, bh, dn))


def _norm_proj_kernel(x_ref, g_ref, w_ref, xn_ref, o_ref):
    x = x_ref[...]
    ms = jnp.mean(x * x, axis=-1, keepdims=True)
    xn = ((x * lax.rsqrt(ms + RMS_EPS)) * g_ref[...]).astype(xn_ref.dtype)
    xn_ref[...] = xn
    o_ref[...] = jnp.dot(xn, w_ref[...].astype(BF16), preferred_element_type=F32)


def rmsnorm_proj(x, g, w, col0, n, tm=1024):
    m, d = x.shape
    tm = min(tm, m)
    assert col0 % LANES == 0 and n % LANES == 0
    return pl.pallas_call(
        _norm_proj_kernel,
        grid=(m // tm,),
        in_specs=[pl.BlockSpec((tm, d), lambda i: (i, 0)), pl.BlockSpec((1, d), lambda i: (0, 0)),
                  pl.BlockSpec((pl.Element(d), pl.Element(n)), lambda i: (0, col0))],
        out_specs=[pl.BlockSpec((tm, d), lambda i: (i, 0)), pl.BlockSpec((tm, n), lambda i: (i, 0))],
        out_shape=[jax.ShapeDtypeStruct((m, d), BF16), jax.ShapeDtypeStruct((m, n), F32)],
        compiler_params=_cparams(("parallel",)),
        name="rmsnorm_proj_lora",
    )(x, g.reshape(1, d), w)


def _mm_kernel(a_ref, w_ref, o_ref):
    o_ref[...] = jnp.dot(a_ref[...], w_ref[...].astype(BF16), preferred_element_type=F32).astype(o_ref.dtype)


def _mm_scaled_kernel(a_ref, w_ref, s_ref, o_ref):
    acc = jnp.dot(a_ref[...], w_ref[...].astype(BF16), preferred_element_type=F32)
    o_ref[...] = (acc * s_ref[...]).astype(o_ref.dtype)


def matmul(a, w, out_dtype, col0, n, col_scale=None, tm=1024, tn=1024, name="matmul"):
    m, k = a.shape
    tm = min(tm, m)
    assert m % tm == 0 and n % tn == 0 and col0 % LANES == 0
    in_specs = [pl.BlockSpec((tm, k), lambda i, j: (i, 0)),
                pl.BlockSpec((pl.Element(k), pl.Element(tn)), lambda i, j: (0, pl.multiple_of(col0 + j * tn, LANES)))]
    args = [a, w]
    if col_scale is not None:
        in_specs.append(pl.BlockSpec((1, tn), lambda i, j: (0, j)))
        args.append(col_scale.reshape(1, n))
    return pl.pallas_call(
        _mm_kernel if col_scale is None else _mm_scaled_kernel,
        grid=(m // tm, n // tn),
        in_specs=in_specs,
        out_specs=pl.BlockSpec((tm, tn), lambda i, j: (i, j)),
        out_shape=jax.ShapeDtypeStruct((m, n), out_dtype),
        compiler_params=_cparams(("parallel", "arbitrary")),
        name=name,
    )(*args)


def _rwkv_kernel(r_ref, k_ref, v_ref, lo_ref, mr_ref, mk_ref, mv_ref, mlo_ref, wup_ref, aup_ref,
                 w0_ref, a0_ref, kk_ref, ka_ref, rk_ref, gnw_ref, gnb_ref,
                 seg_ref, tri_ref, strict_ref, incl_ref, eye_ref,
                 y_ref, s_ref, prev_ref):
    c = pl.program_id(1)
    ch = RWKV_CHUNK
    nbatch = r_ref.shape[0]
    npair = nbatch * r_ref.shape[2] // LANES

    def lanes(ref):
        return jnp.concatenate([ref[b] for b in range(nbatch)], axis=1)

    def tiled(ref):
        return jnp.concatenate([ref[...]] * nbatch, axis=1)

    @pl.when(c == 0)
    def _():
        s_ref[...] = jnp.zeros_like(s_ref)
        prev_ref[...] = jnp.zeros_like(prev_ref)

    def token_shift(x, idx, mix):
        row = lax.broadcasted_iota(jnp.int32, x.shape, 0)
        prev = prev_ref[idx:idx + 1, 0:x.shape[1]]
        shifted = jnp.where(row == 0, prev, pltpu.roll(x, 1, 0))
        return x + (shifted - x) * mix

    r_in, k_in, v_in, lo_in = lanes(r_ref), lanes(k_ref), lanes(v_ref), lanes(lo_ref)
    r = token_shift(r_in, 0, tiled(mr_ref))
    k = token_shift(k_in, 1, tiled(mk_ref))
    v = token_shift(v_in, 2, tiled(mv_ref))
    lo = token_shift(lo_in, 3, tiled(mlo_ref))
    prev_ref[0:1, :] = r_in[ch - 1:ch, :]
    prev_ref[1:2, :] = k_in[ch - 1:ch, :]
    prev_ref[2:3, :] = v_in[ch - 1:ch, :]
    prev_ref[3:4, 0:nbatch * LANES] = lo_in[ch - 1:ch, :]

    lo_b = [lo[:, b * LANES:(b + 1) * LANES] for b in range(nbatch)]
    z_w = tiled(w0_ref) + jnp.concatenate([_dot(jnp.tanh(x), wup_ref[...], passes=1) for x in lo_b], axis=1)
    logw = -DECAY_SCALE * jax.nn.sigmoid(z_w)
    a = jax.nn.sigmoid(tiled(a0_ref) + jnp.concatenate([_dot(x, aup_ref[...], passes=1) for x in lo_b], axis=1))

    seg2 = seg_ref[...]

    def head_sum(x):
        tiles = []
        for t in range(npair):
            hi, lo_ = _split(x[:, t * LANES:(t + 1) * LANES])
            tiles.append(_dg(jnp.concatenate([hi, lo_], axis=1), seg2, NN))
        return jnp.concatenate(tiles, axis=1)

    kk = k * tiled(kk_ref)
    kk = kk * lax.rsqrt(jnp.maximum(head_sum(kk * kk), 1e-24))
    k2 = k * (1.0 + (a - 1.0) * tiled(ka_ref))
    ab = kk * a

    l_hi = logw.astype(BF16)
    rem = logw - l_hi.astype(F32)
    l_mid = rem.astype(BF16)
    l_lo = (rem - l_mid.astype(F32)).astype(BF16)
    cum = _dg(tri_ref[...], jnp.concatenate([l_hi, l_mid, l_lo], axis=0), NN)
    cum_last = cum[ch - 1:ch, :]
    e_cum = jnp.exp(cum)
    e_ncum = jnp.exp(-cum)
    e_tail = jnp.exp(cum_last - cum)
    rt = r * e_cum
    kt = k2 * e_ncum
    bt = ab * e_ncum
    at = -kk * jnp.exp(cum - logw)
    khat = k2 * e_tail
    bhat = ab * e_tail
    p_last = e_cum[ch - 1:ch, :]

    lane = lax.broadcasted_iota(jnp.int32, (ch, LANES), 1)
    head0 = lane < HEAD_DIM
    strict = strict_ref[...] > 0.5
    incl = incl_ref[...] > 0.5
    eye = eye_ref[...]

    def stack(x):
        return jnp.concatenate([jnp.where(head0, x, 0.0), jnp.where(head0, 0.0, x)], axis=0)

    def mm(x, w, dn=NN):
        return _dg(x.astype(BF16), w.astype(BF16), dn)

    pairs = range(npair)
    rows = 2 * ch
    sl = [slice(p * LANES, (p + 1) * LANES) for p in pairs]
    at2, rt2, bt2, kt2, v2, khat2, bhat2 = ([stack(t[:, sl[p]]) for p in pairs]
                                            for t in (at, rt, bt, kt, v, khat, bhat))
    gram = []
    for p in pairs:
        lh, ll = _split(jnp.concatenate([at2[p], rt2[p]], axis=0))
        rh, rl = _split(jnp.concatenate([bt2[p], kt2[p]], axis=0))
        gram.append(_dg(jnp.concatenate([lh, ll], axis=1), jnp.concatenate([rh, rh], axis=1), NT))
    a_ab = [jnp.where(strict, g[:rows, :rows], 0.0) for g in gram]
    a_ak = [jnp.where(strict, g[:rows, rows:], 0.0) for g in gram]
    a_rb = [jnp.where(incl, g[rows:, :rows], 0.0) for g in gram]
    a_rk = [jnp.where(incl, g[rows:, rows:], 0.0) for g in gram]
    akv = [mm(a_ak[p], v2[p]) for p in pairs]

    tinv = [eye + a for a in a_ab]
    pw = [mm(a, a) for a in a_ab]
    for _ in range(int(math.log2(ch)) - 2):
        both = [mm(pw[p], jnp.concatenate([pw[p], tinv[p]], axis=1)) for p in pairs]
        pw = [b[:, :rows] for b in both]
        tinv = [tinv[p] + both[p][:, rows:] for p in pairs]
    tinv = [tinv[p] + mm(pw[p], tinv[p]) for p in pairs]

    s = [s_ref[p] for p in pairs]
    tw = [mm(tinv[p], jnp.concatenate([at2[p], akv[p]], axis=1)) for p in pairs]
    ws = [mm(jnp.concatenate([tw[p][:, :LANES], rt2[p]], axis=0), s[p], NT) for p in pairs]
    uv = [jnp.concatenate([ws[p][:rows] + tw[p][:, LANES:], v2[p]], axis=0) for p in pairs]
    y2 = [ws[p][rows:] + mm(jnp.concatenate([a_rb[p], a_rk[p]], axis=1), uv[p]) for p in pairs]
    for p in pairs:
        s_ref[p] = s[p] * p_last[:, sl[p]] + mm(uv[p], jnp.concatenate([bhat2[p], khat2[p]], axis=0), TN)
    y = jnp.concatenate([t[:ch, :] + t[ch:, :] for t in y2], axis=1)

    inv_n = 1.0 / HEAD_DIM
    mu = head_sum(y) * inv_n
    d = y - mu
    var = head_sum(d * d) * inv_n
    yn = d * lax.rsqrt(var + GN_EPS) * tiled(gnw_ref) + tiled(gnb_ref)
    y = (yn + head_sum(r * k2 * tiled(rk_ref)) * v).astype(y_ref.dtype)
    wd = y_ref.shape[2]
    for b in range(nbatch):
        y_ref[b] = y[:, b * wd:(b + 1) * wd]


def rwkv7(p, p_lora, batch, seqlen, shift_mix, w_up, w0, a_up, a0, k_k, k_a, r_k, gn_w, gn_b):
    width = w0.shape[0]
    npair = width // LANES
    ch = RWKV_CHUNK
    nchunk = seqlen // ch
    assert seqlen % ch == 0 and 2 * LORA_RANK == LANES and 2 * HEAD_DIM == LANES
    zeros = jnp.zeros((LORA_RANK, width), F32)
    wup_pad = jnp.concatenate([w_up, zeros], axis=0)
    aup_pad = jnp.concatenate([zeros, a_up], axis=0)
    hb = RWKV_PAIRS
    assert npair % hb == 0
    idx = jnp.arange(LANES)
    seg = (idx[:, None] // HEAD_DIM == idx[None, :] // HEAD_DIM).astype(BF16)
    seg = jnp.concatenate([seg, seg], axis=0)
    t = jnp.arange(ch)
    tri = (t[None, :] <= t[:, None]).astype(BF16)
    tri = jnp.concatenate([tri, tri, tri], axis=1)
    i2 = jnp.arange(2 * ch)
    same = (i2[:, None] // ch) == (i2[None, :] // ch)
    strict = (same & ((i2[None, :] % ch) < (i2[:, None] % ch))).astype(F32)
    incl = (same & ((i2[None, :] % ch) <= (i2[:, None] % ch))).astype(F32)
    eye = jnp.eye(2 * ch, dtype=F32)
    row2 = lambda x: x.reshape(1, -1)

    wd = hb * LANES
    ngrp = npair // hb

    def tok(off):
        return pl.BlockSpec((batch, ch, wd), lambda h, c: (0, c, off + h))

    def par(off):
        return pl.BlockSpec((1, wd), lambda h, c: (0, off + h))

    def const(shape):
        return pl.BlockSpec(shape, lambda h, c: (0, 0))

    up = pl.BlockSpec((LANES, wd), lambda h, c: (0, h))
    lora_tok = pl.BlockSpec((batch, ch, LANES), lambda h, c: (0, c, 0))
    lora_par = pl.BlockSpec((1, LANES), lambda h, c: (0, 3 * npair))
    p3 = p.reshape(batch, seqlen, 3 * width)
    y = pl.pallas_call(
        _rwkv_kernel,
        grid=(ngrp, nchunk),
        in_specs=[tok(0), tok(ngrp), tok(2 * ngrp), lora_tok,
                  par(0), par(ngrp), par(2 * ngrp), lora_par, up, up,
                  par(0), par(0), par(0), par(0), par(0), par(0), par(0),
                  const((2 * LANES, LANES)), const((ch, 3 * ch)), const((2 * ch, 2 * ch)),
                  const((2 * ch, 2 * ch)), const((2 * ch, 2 * ch))],
        out_specs=pl.BlockSpec((batch, ch, wd), lambda h, c: (0, c, h)),
        out_shape=jax.ShapeDtypeStruct((batch, seqlen, width), BF16),
        scratch_shapes=[pltpu.VMEM((batch * hb, LANES, LANES), F32), pltpu.VMEM((SUBLANES, batch * wd), F32)],
        compiler_params=_cparams(("parallel", "arbitrary")),
        name="rwkv7",
    )(p3, p3, p3, p_lora.reshape(batch, seqlen, LANES), row2(shift_mix), row2(shift_mix), row2(shift_mix),
      row2(shift_mix), wup_pad, aup_pad,
      row2(w0), row2(a0), row2(k_k), row2(k_a), row2(r_k), row2(gn_w), row2(gn_b),
      seg, tri, strict, incl, eye)
    return y.reshape(batch * seqlen, width)


def _sb_kernel(q_ref, k_ref, v_ref, uo_ref, o_ref, k2_ref, v2_ref, acc_ref, carry_ref, lb_ref, sums_ref, z_ref):
    qi = pl.program_id(2)
    kb, qb = SB_BLOCK, SB_QBLOCK
    nsub = qb // kb
    npair = q_ref.shape[1] // LANES
    pairs_ = range(npair)

    @pl.when(qi == 0)
    def _():
        lane = lax.broadcasted_iota(jnp.int32, (kb, LANES), 1)
        head0 = lane < HEAD_DIM

        def fill(i, c):
            start = pl.multiple_of(i * kb, kb)
            for p in pairs_:
                for src, dst in ((k_ref, k2_ref), (v_ref, v2_ref)):
                    t = src[pl.ds(start, kb), p * LANES:(p + 1) * LANES].astype(F32)
                    dst[p, i, 0:kb, :] = jnp.where(head0, t, 0.0).astype(BF16)
                    dst[p, i, kb:2 * kb, :] = jnp.where(head0, 0.0, t).astype(BF16)
            return c

        lax.fori_loop(0, k_ref.shape[0] // kb, fill, 0)

    q = [q_ref[:, p * LANES:(p + 1) * LANES] for p in pairs_]
    uo = uo_ref[...]
    acc_ref[...] = jnp.zeros_like(acc_ref)
    carry_ref[...] = jnp.zeros_like(carry_ref)
    tpos = lax.broadcasted_iota(jnp.int32, (qb, kb), 0)
    spos = lax.broadcasted_iota(jnp.int32, (qb, kb), 1)

    def logits(p, sb, zslot):
        base = pl.multiple_of(sb * nsub, nsub)
        keys = k2_ref[p, pl.ds(base, nsub)].reshape(nsub * 2 * kb, LANES)
        z_ref[zslot, p] = _dg(q[p], keys, NT)

    def scores(p, zslot, slot, diagonal):
        for j in range(nsub):
            for h in range(2):
                c = 2 * j + h
                z = z_ref[zslot, p, :, c * kb:(c + 1) * kb]
                nz = -z
                log_keep = jnp.minimum(nz, 0.0) - jnp.log(1.0 + jnp.exp(jnp.minimum(z, nz)))
                if diagonal:
                    causal = (spos + j * kb) < tpos
                    log_keep = jnp.where(causal, log_keep, 0.0)
                    z = jnp.where(causal, z, -jnp.inf)
                hi = log_keep.astype(BF16)
                lo = (log_keep - hi.astype(F32)).astype(BF16)
                lb_ref[slot, p, c] = z
                sums_ref[slot, p, c] = _dg(jnp.concatenate([hi, lo], axis=1), uo, NN)

    def accumulate(p, sb, slot):
        base = pl.multiple_of(sb * nsub, nsub)
        attn = {}
        for h in range(2):
            carry = carry_ref[p, h]
            for j in reversed(range(nsub)):
                s = sums_ref[slot, p, 2 * j + h]
                attn[j, h] = jnp.exp(lb_ref[slot, p, 2 * j + h] + carry + s[:, :kb]).astype(BF16)
                carry = carry + s[:, kb:]
            carry_ref[p, h] = carry
        weights = jnp.concatenate([attn[j, h] for j in range(nsub) for h in range(2)], axis=1)
        values = v2_ref[p, pl.ds(base, nsub)].reshape(nsub * 2 * kb, LANES)
        acc_ref[p] += _dg(weights, values, NN)

    for p in pairs_:
        logits(p, qi, 0)
    for p in pairs_:
        scores(p, 0, 0, True)
    trips = qi // 2

    def body(i, c):
        sb = qi - 1 - 2 * i
        for p in pairs_:
            logits(p, sb, 0)
        for p in pairs_:
            logits(p, sb - 1, 1)
        for p in pairs_:
            accumulate(p, sb + 1, 0)
        for p in pairs_:
            scores(p, 0, 1, False)
        for p in pairs_:
            scores(p, 1, 0, False)
        for p in pairs_:
            accumulate(p, sb, 1)
        return c

    lax.fori_loop(0, trips, body, 0)
    odd = qi - 2 * trips == 1

    @pl.when(odd)
    def _():
        for p in pairs_:
            logits(p, 0, 0)
        for p in pairs_:
            accumulate(p, 1, 0)
        for p in pairs_:
            scores(p, 0, 1, False)
        for p in pairs_:
            accumulate(p, 0, 1)

    @pl.when(jnp.logical_not(odd))
    def _():
        for p in pairs_:
            accumulate(p, 0, 0)

    for p in pairs_:
        o_ref[:, p * LANES:(p + 1) * LANES] = acc_ref[p].astype(o_ref.dtype)


def stick_breaking(p, batch, seqlen, width):
    npair = width // LANES
    kb, qb = SB_BLOCK, SB_QBLOCK
    nq = seqlen // qb
    hp = SB_PAIRS
    ngrp = npair // hp
    wd = hp * LANES
    nsl = 2 * qb // kb
    j = jnp.arange(kb)
    later = (j[:, None] >= j[None, :]).astype(BF16)
    uo = jnp.concatenate([later, jnp.ones((kb, kb), BF16)], axis=1)
    uo = jnp.concatenate([uo, uo], axis=0)
    return pl.pallas_call(
        _sb_kernel,
        grid=(batch, ngrp, nq),
        in_specs=[pl.BlockSpec((qb, wd), lambda b, h, i: (b * nq + i, h)),
                  pl.BlockSpec((seqlen, wd), lambda b, h, i: (b, ngrp + h), pipeline_mode=pl.Buffered(1)),
                  pl.BlockSpec((seqlen, wd), lambda b, h, i: (b, 2 * ngrp + h), pipeline_mode=pl.Buffered(1)),
                  pl.BlockSpec((2 * kb, 2 * kb), lambda b, h, i: (0, 0))],
        out_specs=pl.BlockSpec((qb, wd), lambda b, h, i: (b * nq + i, h)),
        out_shape=jax.ShapeDtypeStruct((batch * seqlen, width), BF16),
        scratch_shapes=[pltpu.VMEM((hp, seqlen // kb, 2 * kb, LANES), BF16),
                        pltpu.VMEM((hp, seqlen // kb, 2 * kb, LANES), BF16),
                        pltpu.VMEM((hp, qb, LANES), F32), pltpu.VMEM((hp, 2, qb, kb), F32),
                        pltpu.VMEM((2, hp, nsl, qb, kb), F32), pltpu.VMEM((2, hp, nsl, qb, 2 * kb), F32),
                        pltpu.VMEM((2, hp, qb, nsl * kb), F32)],
        compiler_params=_cparams(("parallel", "parallel", "arbitrary")),
        name="stick_breaking",
    )(p, p, p, uo)


def _gate_out_kernel(ya_ref, yb_ref, g0_ref, g1_ref, x_ref, w_ref, ng_ref, h_ref, hn_ref, wb_ref):
    @pl.when(pl.program_id(0) == 0)
    def _():
        wb_ref[...] = w_ref[...].astype(BF16)

    gate = jnp.concatenate([g0_ref[...], g1_ref[...]], axis=-1).astype(F32)
    y = jnp.concatenate([ya_ref[...], yb_ref[...]], axis=-1).astype(F32) * (gate * jax.nn.sigmoid(gate))
    h = x_ref[...] + jnp.dot(y.astype(BF16), wb_ref[...], preferred_element_type=F32)
    h_ref[...] = h
    ms = jnp.mean(h * h, axis=-1, keepdims=True)
    hn_ref[...] = ((h * lax.rsqrt(ms + RMS_EPS)) * ng_ref[...]).astype(hn_ref.dtype)


def gate_out(ya, yb, p, gate_col, x, w, next_g, tm=512):
    m, d = x.shape
    half = ya.shape[1]
    tm = min(tm, m)
    assert gate_col % half == 0 and d == 2 * half
    g0 = gate_col // half
    row = lambda n: pl.BlockSpec((tm, n), lambda i: (i, 0))
    return pl.pallas_call(
        _gate_out_kernel,
        grid=(m // tm,),
        in_specs=[row(half), row(half),
                  pl.BlockSpec((tm, half), lambda i: (i, g0)), pl.BlockSpec((tm, half), lambda i: (i, g0 + 1)), row(d),
                  pl.BlockSpec((d, d), lambda i: (0, 0), pipeline_mode=pl.Buffered(1)),
                  pl.BlockSpec((1, d), lambda i: (0, 0))],
        out_specs=[row(d), row(d)],
        out_shape=[jax.ShapeDtypeStruct((m, d), F32), jax.ShapeDtypeStruct((m, d), BF16)],
        scratch_shapes=[pltpu.VMEM((d, d), BF16)],
        compiler_params=_cparams(("arbitrary",)),
        name="gate_out",
    )(ya, yb, p, p, x, w, next_g.reshape(1, d))


def _s5_prep_kernel(lre_ref, lim_ref, ldt_ref, cre_ref, cim_ref, btre_ref, btim_ref,
                    kt_ref, w2_ref, vt2_ref, c2re_ref, c2im_ref, ckre_ref, ckim_ref):
    blk = S5_BLOCK
    h = S5_GROUP
    ns = S5_STATE
    gp = kt_ref.shape[0]
    wl = blk * h
    lre, lim = lre_ref[...], lim_ref[...]
    dt = jnp.exp(ldt_ref[...])
    mag = jnp.exp(lre * dt)
    bre = mag * jnp.cos(lim * dt)
    bim = mag * jnp.sin(lim * dt)
    den = lre * lre + lim * lim
    nre, nim = bre - 1.0, bim
    fre = (nre * lre + nim * lim) / den
    fim = (nim * lre - nre * lim) / den
    cre, cim = cre_ref[...], cim_ref[...]
    ckr, cki = cre * fre - cim * fim, cre * fim + cim * fre
    btre, btim = btre_ref[...], btim_ref[...]
    pr, pi = jnp.ones_like(bre), jnp.zeros_like(bre)
    powers = []
    for tau in range(blk + 1):
        ckre_ref[:, tau * h:(tau + 1) * h, :] = ckr
        ckim_ref[:, tau * h:(tau + 1) * h, :] = cki
        powers.append((pr, pi))
        ckr, cki = ckr * bre - cki * bim, ckr * bim + cki * bre
        pr, pi = pr * bre - pi * bim, pr * bim + pi * bre

    def put(ref, rows, lane0, x):
        r0, r1 = rows
        x = x.reshape(gp // 2, 2, r1 - r0, ns).astype(ref.dtype)
        ref[:, r0:r1, lane0:lane0 + ns] = x[:, 0]
        ref[:, ref.shape[1] // 2 + r0:ref.shape[1] // 2 + r1, lane0 + ns:lane0 + 2 * ns] = x[:, 1]

    w2_ref[...] = jnp.zeros_like(w2_ref)
    vt2_ref[...] = jnp.zeros_like(vt2_ref)
    for i in range(blk):
        pr, pi = powers[blk - 1 - i]
        put(w2_ref, (i * h, (i + 1) * h), 0, pr * btre - pi * btim)
        put(w2_ref, (i * h, (i + 1) * h), 2 * ns, pr * btim + pi * btre)
    put(vt2_ref, (0, wl), 0, ckre_ref[:, h:(blk + 1) * h, :])
    put(vt2_ref, (0, wl), 2 * ns, -ckim_ref[:, h:(blk + 1) * h, :])

    def put_lanes(ref, r0, x):
        x = x.reshape(gp // 2, 2, x.shape[1], ns)
        ref[:, r0:r0 + x.shape[2], 0:ns] = x[:, 0]
        ref[:, r0:r0 + x.shape[2], ns:2 * ns] = x[:, 1]

    row = lax.broadcasted_iota(jnp.int32, (gp, SUBLANES, ns), 1)
    qr, qi = powers[blk]
    pr, pi = qr, qi
    for r in range(SUBLANES):
        if r + 1 in (1, 2, 4):
            lvl = (1, 2, 4).index(r + 1)
            put_lanes(c2re_ref, lvl * SUBLANES, jnp.where(row >= r + 1, pr, 0.0))
            put_lanes(c2im_ref, lvl * SUBLANES, jnp.where(row >= r + 1, pi, 0.0))
        put_lanes(c2re_ref, 3 * SUBLANES + r, pr)
        put_lanes(c2im_ref, 3 * SUBLANES + r, pi)
        pr, pi = pr * qr - pi * qi, pr * qi + pi * qr
    for g in range(gp):
        kt_ref[g] = (_dot(btre[g], ckre_ref[g, 0:blk * h, :], NT) - _dot(btim[g], ckim_ref[g, 0:blk * h, :], NT))


def _s5_kernel(nbatch, p_ref, kt_ref, w_ref, vt_ref, cre_ref, cim_ref, d_ref, o_ref,
               u_ref, gre_ref, gim_ref, xre_ref, xim_ref, yp_ref, stage_ref):
    npair = u_ref.shape[0] // 2
    rows = u_ref.shape[1]
    wlane = u_ref.shape[2]
    blk = S5_BLOCK
    ngrp = 2 * npair
    rc = S5_RELAYOUT_ROWS
    lane_chunk = lax.broadcasted_iota(jnp.int32, (rc, LANES), 1) // S5_GROUP

    def chunk_transpose(arrs):
        arrs = list(arrs)
        s = ngrp // 2
        while s:
            upper = (lane_chunk & s) != 0
            for x in range(ngrp):
                if x & s == 0:
                    ax, ay = arrs[x], arrs[x + s]
                    arrs[x] = jnp.where(upper, pltpu.roll(ay, s * S5_GROUP, 1), ax)
                    arrs[x + s] = jnp.where(upper, ay, pltpu.roll(ax, LANES - s * S5_GROUP, 1))
            s //= 2
        return arrs

    def relayout_in(c, carry):
        t0 = pl.multiple_of(c * (rc * blk), rc * blk)
        r0 = pl.multiple_of(c * rc, rc)
        stage_ref[...] = p_ref[pl.ds(t0, rc * blk), :].astype(F32)
        for half in range(blk // ngrp):
            z = chunk_transpose(stage_ref[pl.ds(ngrp * half + i, rc, stride=blk), :] for i in range(ngrp))
            for g in range(ngrp):
                u_ref[g, pl.ds(r0, rc), half * LANES:(half + 1) * LANES] = z[g].astype(BF16)
        return carry

    lax.fori_loop(0, rows // rc, relayout_in, 0)
    lane = lax.broadcasted_iota(jnp.int32, (S5_GROUP, wlane), 1)

    for g in range(2 * npair):
        kt = kt_ref[g]
        blocks = [kt] + [jnp.where(lane >= i * S5_GROUP, pltpu.roll(kt, i * S5_GROUP, 1), 0.0)
                         for i in range(1, S5_BLOCK)]
        kmat = jnp.concatenate(blocks, axis=0).astype(BF16)
        yp_ref[g] = _dg(u_ref[g], kmat, NN)
    for p in range(npair):
        ucat = jnp.concatenate([u_ref[2 * p], u_ref[2 * p + 1]], axis=1)
        g = _dg(ucat, w_ref[p], NN)
        gre_ref[p] = g[:, :LANES]
        gim_ref[p] = g[:, LANES:]

    def cma(xr, xi, ar, ai, sr, si):
        return xr + (ar * sr - ai * si), xi + (ar * si + ai * sr)

    row = lax.broadcasted_iota(jnp.int32, (SUBLANES, LANES), 0)
    per_batch = rows // nbatch

    def tile(t, carry, base):
        r0 = pl.multiple_of(base + t * SUBLANES, SUBLANES)
        new_carry = []
        for p in range(npair):
            xr = gre_ref[p, pl.ds(r0, SUBLANES), :]
            xi = gim_ref[p, pl.ds(r0, SUBLANES), :]
            for lvl, sh in enumerate((1, 2, 4)):
                ar = cre_ref[p, lvl * SUBLANES:(lvl + 1) * SUBLANES, :]
                ai = cim_ref[p, lvl * SUBLANES:(lvl + 1) * SUBLANES, :]
                xr, xi = cma(xr, xi, ar, ai, pltpu.roll(xr, sh, 0), pltpu.roll(xi, sh, 0))
            ar = cre_ref[p, 3 * SUBLANES:4 * SUBLANES, :]
            ai = cim_ref[p, 3 * SUBLANES:4 * SUBLANES, :]
            cr, ci = carry[2 * p], carry[2 * p + 1]
            xr, xi = cma(xr, xi, ar, ai, cr, ci)
            xre_ref[p, pl.ds(r0, SUBLANES), :] = jnp.where(row == 0, cr, pltpu.roll(xr, 1, 0))
            xim_ref[p, pl.ds(r0, SUBLANES), :] = jnp.where(row == 0, ci, pltpu.roll(xi, 1, 0))
            new_carry.append(jnp.broadcast_to(xr[SUBLANES - 1:SUBLANES, :], (SUBLANES, LANES)))
            new_carry.append(jnp.broadcast_to(xi[SUBLANES - 1:SUBLANES, :], (SUBLANES, LANES)))
        return tuple(new_carry)

    zero = jnp.zeros((SUBLANES, LANES), F32)
    for b in range(nbatch):
        lax.fori_loop(0, per_batch // SUBLANES, functools.partial(tile, base=b * per_batch), (zero,) * (2 * npair))

    for p in range(npair):
        xs = jnp.concatenate([xre_ref[p], xim_ref[p]], axis=1).astype(BF16)
        corr = _dg(xs, vt_ref[p], NT)
        for k in range(2):
            g = 2 * p + k
            y = yp_ref[g] + corr[:, k * wlane:(k + 1) * wlane] + d_ref[g] * u_ref[g].astype(F32)
            yp_ref[g] = jax.nn.gelu(y)

    def relayout_out(c, carry):
        t0 = pl.multiple_of(c * (rc * blk), rc * blk)
        r0 = pl.multiple_of(c * rc, rc)
        for half in range(blk // ngrp):
            yt = chunk_transpose(yp_ref[g, pl.ds(r0, rc), half * LANES:(half + 1) * LANES] for g in range(ngrp))
            for j in range(ngrp):
                stage_ref[pl.ds(ngrp * half + j, rc, stride=blk), :] = yt[j]
        o_ref[pl.ds(t0, rc * blk), :] = stage_ref[...].astype(o_ref.dtype)
        return carry

    lax.fori_loop(0, rows // rc, relayout_out, 0)


def s5_ssm(p, batch, seqlen, lam_re, lam_im, log_dt, b_re, b_im, c_re, c_im, d_skip):
    ngroup, nstate = lam_re.shape
    h, blk = S5_GROUP, S5_BLOCK
    width = ngroup * h
    wl = blk * h
    nb = batch * seqlen // blk
    gp = S5_PREP_GROUPS
    assert nstate == S5_STATE and 2 * nstate == LANES
    assert seqlen % (blk * SUBLANES) == 0 and ngroup % gp == 0 and ngroup % (2 * S5_PAIRS) == 0

    g3 = lambda x: x.reshape(ngroup, 1, nstate)
    bt = lambda x: jnp.swapaxes(x, 1, 2)
    spec1 = pl.BlockSpec((gp, 1, nstate), lambda i: (i, 0, 0))
    spec_c = pl.BlockSpec((gp, h, nstate), lambda i: (i, 0, 0))
    spec_k = pl.BlockSpec((gp, h, wl), lambda i: (i, 0, 0))
    pair = lambda r, c: pl.BlockSpec((gp // 2, r, c), lambda i: (i, 0, 0))
    npairs = ngroup // 2
    kt, w2, vt2, c_re2, c_im2 = pl.pallas_call(
        _s5_prep_kernel,
        grid=(ngroup // gp,),
        in_specs=[spec1, spec1, spec1, spec_c, spec_c, spec_c, spec_c],
        out_specs=[spec_k, pair(2 * wl, 2 * LANES), pair(2 * wl, 2 * LANES),
                   pair(4 * SUBLANES, LANES), pair(4 * SUBLANES, LANES)],
        out_shape=[jax.ShapeDtypeStruct((ngroup, h, wl), F32),
                   jax.ShapeDtypeStruct((npairs, 2 * wl, 2 * LANES), BF16),
                   jax.ShapeDtypeStruct((npairs, 2 * wl, 2 * LANES), BF16),
                   jax.ShapeDtypeStruct((npairs, 4 * SUBLANES, LANES), F32),
                   jax.ShapeDtypeStruct((npairs, 4 * SUBLANES, LANES), F32)],
        scratch_shapes=[pltpu.VMEM((gp, (blk + 1) * h, nstate), F32), pltpu.VMEM((gp, (blk + 1) * h, nstate), F32)],
        compiler_params=_cparams(("parallel",)),
        name="s5_prep",
    )(g3(lam_re), g3(lam_im), jnp.broadcast_to(log_dt[:, None, None], (ngroup, 1, nstate)), c_re, c_im,
      bt(b_re), bt(b_im))

    d_row = jnp.tile(d_skip.reshape(ngroup, 1, h), (1, 1, blk))

    np_ = S5_PAIRS
    gs = 2 * np_
    assert gs * h == LANES and blk % gs == 0 and nb % S5_RELAYOUT_ROWS == 0
    tok = pl.BlockSpec((nb * blk, LANES), lambda i: (0, i))
    grp = lambda r, c: pl.BlockSpec((gs, r, c), lambda i: (i, 0, 0))
    par = lambda r, c: pl.BlockSpec((np_, r, c), lambda i: (i, 0, 0))
    return pl.pallas_call(
        functools.partial(_s5_kernel, batch),
        grid=(ngroup // gs,),
        in_specs=[tok, grp(h, wl), par(2 * wl, 2 * LANES), par(2 * wl, 2 * LANES),
                  par(4 * SUBLANES, LANES), par(4 * SUBLANES, LANES), grp(1, wl)],
        out_specs=tok,
        out_shape=jax.ShapeDtypeStruct((batch * seqlen, width), BF16),
        scratch_shapes=[pltpu.VMEM((gs, nb, wl), BF16)] + [pltpu.VMEM((np_, nb, LANES), F32)] * 4
        + [pltpu.VMEM((gs, nb, wl), F32), pltpu.VMEM((S5_RELAYOUT_ROWS * blk, LANES), F32)],
        compiler_params=_cparams(("parallel",)),
        name="s5_ssm",
    )(p, kt, w2, vt2, c_re2, c_im2, d_row)


def _glu_kernel(y_ref, yj_ref, gj_ref, w_ref, b_ref, o_ref):
    w = w_ref[...].astype(BF16)
    b = b_ref[...]
    for r in range(0, y_ref.shape[0], GLU_SUBTILE):
        rows = slice(r, r + GLU_SUBTILE)
        z = jnp.dot(y_ref[rows, :], w, preferred_element_type=F32) + b
        gate = gj_ref[rows, :].astype(F32)
        yj = yj_ref[rows, :].astype(F32)
        o_ref[rows, :] = (yj * jax.nn.sigmoid(z) * (gate * jax.nn.sigmoid(gate))).astype(o_ref.dtype)


def glu_gate(y, p, w, b, tm=1024, tn=1024):
    m, d = y.shape
    tm = min(tm, m)
    goff = d // tn
    return pl.pallas_call(
        _glu_kernel,
        grid=(m // tm, d // tn),
        in_specs=[pl.BlockSpec((tm, d), lambda i, j: (i, 0)),
                  pl.BlockSpec((tm, tn), lambda i, j: (i, j)),
                  pl.BlockSpec((tm, tn), lambda i, j: (i, goff + j)),
                  pl.BlockSpec((d, tn), lambda i, j: (0, j)),
                  pl.BlockSpec((1, tn), lambda i, j: (0, j))],
        out_specs=pl.BlockSpec((tm, tn), lambda i, j: (i, j)),
        out_shape=jax.ShapeDtypeStruct((m, d), BF16),
        compiler_params=_cparams(("parallel", "arbitrary")),
        name="glu_gate",
    )(y, y, p, w, b.reshape(1, d))


def _final_kernel(a_ref, h_ref, w_ref, g_ref, o_ref, wb_ref):
    @pl.when(pl.program_id(0) == 0)
    def _():
        wb_ref[...] = w_ref[...].astype(BF16)

    h = h_ref[...] + jnp.dot(a_ref[...], wb_ref[...], preferred_element_type=F32)
    ms = jnp.mean(h * h, axis=-1, keepdims=True)
    o_ref[...] = (h * lax.rsqrt(ms + RMS_EPS)) * g_ref[...]


def final_out(a, h, w, g, tm=512):
    m, d = h.shape
    tm = min(tm, m)
    row = pl.BlockSpec((tm, d), lambda i: (i, 0))
    return pl.pallas_call(
        _final_kernel,
        grid=(m // tm,),
        in_specs=[row, row, pl.BlockSpec((d, d), lambda i: (0, 0), pipeline_mode=pl.Buffered(1)),
                  pl.BlockSpec((1, d), lambda i: (0, 0))],
        out_specs=row,
        out_shape=jax.ShapeDtypeStruct((m, d), F32),
        scratch_shapes=[pltpu.VMEM((d, d), BF16)],
        compiler_params=_cparams(("arbitrary",)),
        name="final_out",
    )(a, h, w, g.reshape(1, d))


def kernel(x, norm_g, final_g, ab_w_in, rwkv_shift_mix, rwkv_w_up, rwkv_w0, rwkv_a_up, rwkv_a0, rwkv_k_k, rwkv_k_a, rwkv_r_k, rwkv_gn_w, rwkv_gn_b, ab_w_out, s5_w_in, s5_lam_re, s5_lam_im, s5_log_dt, s5_b_re, s5_b_im, s5_c_re, s5_c_im, s5_d, s5_w_glu, s5_b_glu, s5_w_out):
    batch, seqlen, d = x.shape
    m = batch * seqlen
    rwkv_w = rwkv_w0.shape[1]
    rwkv_proj = 3 * rwkv_w + 2 * LORA_RANK
    sb_w = (ab_w_in.shape[2] - rwkv_proj - d) // 3
    x2 = x.reshape(m, d)

    w_in = ab_w_in[0]
    scale = HEAD_DIM ** -0.5
    col_scale = jnp.concatenate([jnp.full((sb_w,), scale, F32), jnp.ones((2 * sb_w + d,), F32)])
    xn, p_lora = rmsnorm_proj(x2, norm_g[0], w_in, 3 * rwkv_w, 2 * LORA_RANK)
    p_rkv = matmul(xn, w_in, F32, 0, 3 * rwkv_w, name="proj_rkv")
    p_sb = matmul(xn, w_in, BF16, rwkv_proj, 3 * sb_w + d, col_scale, tm=2048, name="proj_sb_gate")
    y_a = rwkv7(p_rkv, p_lora, batch, seqlen, rwkv_shift_mix[0], rwkv_w_up[0], rwkv_w0[0], rwkv_a_up[0], rwkv_a0[0],
                rwkv_k_k[0], rwkv_k_a[0], rwkv_r_k[0], rwkv_gn_w[0], rwkv_gn_b[0])
    y_b = stick_breaking(p_sb, batch, seqlen, sb_w)
    h1, hn1 = gate_out(y_a, y_b, p_sb, 3 * sb_w, x2, ab_w_out[0], norm_g[1])

    p1 = matmul(hn1, s5_w_in[0], BF16, 0, 2 * d, tm=2048, name="proj_s5")
    y_s5 = s5_ssm(p1, batch, seqlen, s5_lam_re[0], s5_lam_im[0], s5_log_dt[0], s5_b_re[0], s5_b_im[0],
                  s5_c_re[0], s5_c_im[0], s5_d[0])
    act = glu_gate(y_s5, p1, s5_w_glu[0], s5_b_glu[0])
    out = final_out(act, h1, s5_w_out[0], final_g)
    return out.reshape(batch, seqlen, d)
```

```python
import functools
import math

import jax
import jax.numpy as jnp
from jax import lax
from jax.experimental import pallas as pl
from jax.experimental.pallas import tpu as pltpu

F32 = jnp.float32
BF16 = jnp.bfloat16

HEAD_DIM = 64
LANES = 128
SUBLANES = 8
LORA_RANK = 64
S5_GROUP = 16
S5_STATE = 64
RMS_EPS = 1e-6
GN_EPS = 64e-5
DECAY_SCALE = math.exp(-0.5)

RWKV_CHUNK = 64
RWKV_PAIRS = 8
SB_BLOCK = 128
SB_QBLOCK = 256
SB_PAIRS = 4
S5_BLOCK = 16
S5_PAIRS = 4
S5_PREP_GROUPS = 16
S5_RELAYOUT_ROWS = 128
GLU_SUBTILE = 128
VMEM_LIMIT = 56 * 1024 * 1024

NN = (((1,), (0,)), ((), ()))
NT = (((1,), (1,)), ((), ()))
TN = (((0,), (0,)), ((), ()))


def _cparams(sem):
    return pltpu.CompilerParams(dimension_semantics=sem, vmem_limit_bytes=VMEM_LIMIT)


def _split(x):
    hi = x.astype(BF16)
    lo = (x - hi.astype(F32)).astype(BF16)
    return hi, lo


def _dg(a, b, dn):
    return lax.dot_general(a, b, dn, preferred_element_type=F32)


def _dot(a, b, dn=NN, passes=3):
    if passes == 1:
        return _dg(a.astype(BF16), b.astype(BF16), dn)
    ah, al = _split(a)
    bh, bl = _split(b)
    return _dg(ah, bh, dn) + (_dg(ah, bl, dn) + _dg(al, bh, dn))


def _norm_proj_kernel(x_ref, g_ref, w_ref, xn_ref, o_ref):
    x = x_ref[...]
    ms = jnp.mean(x * x, axis=-1, keepdims=True)
    xn = ((x * lax.rsqrt(ms + RMS_EPS)) * g_ref[...]).astype(xn_ref.dtype)
    xn_ref[...] = xn
    o_ref[...] = jnp.dot(xn, w_ref[...].astype(BF16), preferred_element_type=F32)


def rmsnorm_proj(x, g, w, col0, n, tm=1024):
    m, d = x.shape
    tm = min(tm, m)
    assert col0 % LANES == 0 and n % LANES == 0
    return pl.pallas_call(
        _norm_proj_kernel,
        grid=(m // tm,),
        in_specs=[pl.BlockSpec((tm, d), lambda i: (i, 0)), pl.BlockSpec((1, d), lambda i: (0, 0)),
                  pl.BlockSpec((pl.Element(d), pl.Element(n)), lambda i: (0, col0))],
        out_specs=[pl.BlockSpec((tm, d), lambda i: (i, 0)), pl.BlockSpec((tm, n), lambda i: (i, 0))],
        out_shape=[jax.ShapeDtypeStruct((m, d), BF16), jax.ShapeDtypeStruct((m, n), F32)],
        compiler_params=_cparams(("parallel",)),
        name="rmsnorm_proj_lora",
    )(x, g.reshape(1, d), w)


def _mm_kernel(a_ref, w_ref, o_ref):
    o_ref[...] = jnp.dot(a_ref[...], w_ref[...].astype(BF16), preferred_element_type=F32).astype(o_ref.dtype)


def _mm_scaled_kernel(a_ref, w_ref, s_ref, o_ref):
    acc = jnp.dot(a_ref[...], w_ref[...].astype(BF16), preferred_element_type=F32)
    o_ref[...] = (acc * s_ref[...]).astype(o_ref.dtype)


def matmul(a, w, out_dtype, col0, n, col_scale=None, tm=1024, tn=1024, name="matmul"):
    m, k = a.shape
    tm = min(tm, m)
    assert m % tm == 0 and n % tn == 0 and col0 % LANES == 0
    in_specs = [pl.BlockSpec((tm, k), lambda i, j: (i, 0)),
                pl.BlockSpec((pl.Element(k), pl.Element(tn)), lambda i, j: (0, pl.multiple_of(col0 + j * tn, LANES)))]
    args = [a, w]
    if col_scale is not None:
        in_specs.append(pl.BlockSpec((1, tn), lambda i, j: (0, j)))
        args.append(col_scale.reshape(1, n))
    return pl.pallas_call(
        _mm_kernel if col_scale is None else _mm_scaled_kernel,
        grid=(m // tm, n // tn),
        in_specs=in_specs,
        out_specs=pl.BlockSpec((tm, tn), lambda i, j: (i, j)),
        out_shape=jax.ShapeDtypeStruct((m, n), out_dtype),
        compiler_params=_cparams(("parallel", "arbitrary")),
        name=name,
    )(*args)


def _rwkv_kernel(r_ref, k_ref, v_ref, lo_ref, mr_ref, mk_ref, mv_ref, mlo_ref, wup_ref, aup_ref,
                 w0_ref, a0_ref, kk_ref, ka_ref, rk_ref, gnw_ref, gnb_ref,
                 seg_ref, tri_ref, strict_ref, incl_ref, eye_ref,
                 y_ref, s_ref, prev_ref):
    c = pl.program_id(1)
    ch = RWKV_CHUNK
    nbatch = r_ref.shape[0]
    npair = nbatch * r_ref.shape[2] // LANES

    def lanes(ref):
        return jnp.concatenate([ref[b] for b in range(nbatch)], axis=1)

    def tiled(ref):
        return jnp.concatenate([ref[...]] * nbatch, axis=1)

    @pl.when(c == 0)
    def _():
        s_ref[...] = jnp.zeros_like(s_ref)
        prev_ref[...] = jnp.zeros_like(prev_ref)

    def token_shift(x, idx, mix):
        row = lax.broadcasted_iota(jnp.int32, x.shape, 0)
        prev = prev_ref[idx:idx + 1, 0:x.shape[1]]
        shifted = jnp.where(row == 0, prev, pltpu.roll(x, 1, 0))
        return x + (shifted - x) * mix

    r_in, k_in, v_in, lo_in = lanes(r_ref), lanes(k_ref), lanes(v_ref), lanes(lo_ref)
    r = token_shift(r_in, 0, tiled(mr_ref))
    k = token_shift(k_in, 1, tiled(mk_ref))
    v = token_shift(v_in, 2, tiled(mv_ref))
    lo = token_shift(lo_in, 3, tiled(mlo_ref))
    prev_ref[0:1, :] = r_in[ch - 1:ch, :]
    prev_ref[1:2, :] = k_in[ch - 1:ch, :]
    prev_ref[2:3, :] = v_in[ch - 1:ch, :]
    prev_ref[3:4, 0:nbatch * LANES] = lo_in[ch - 1:ch, :]

    lo_b = [lo[:, b * LANES:(b + 1) * LANES] for b in range(nbatch)]
    z_w = tiled(w0_ref) + jnp.concatenate([_dot(jnp.tanh(x), wup_ref[...], passes=1) for x in lo_b], axis=1)
    logw = -DECAY_SCALE * jax.nn.sigmoid(z_w)
    a = jax.nn.sigmoid(tiled(a0_ref) + jnp.concatenate([_dot(x, aup_ref[...], passes=1) for x in lo_b], axis=1))

    seg2 = seg_ref[...]

    def head_sum(x):
        tiles = []
        for t in range(npair):
            hi, lo_ = _split(x[:, t * LANES:(t + 1) * LANES])
            tiles.append(_dg(jnp.concatenate([hi, lo_], axis=1), seg2, NN))
        return jnp.concatenate(tiles, axis=1)

    kk = k * tiled(kk_ref)
    kk = kk * lax.rsqrt(jnp.maximum(head_sum(kk * kk), 1e-24))
    k2 = k * (1.0 + (a - 1.0) * tiled(ka_ref))
    ab = kk * a

    l_hi = logw.astype(BF16)
    rem = logw - l_hi.astype(F32)
    l_mid = rem.astype(BF16)
    l_lo = (rem - l_mid.astype(F32)).astype(BF16)
    cum = _dg(tri_ref[...], jnp.concatenate([l_hi, l_mid, l_lo], axis=0), NN)
    cum_last = cum[ch - 1:ch, :]
    e_cum = jnp.exp(cum)
    e_ncum = jnp.exp(-cum)
    e_tail = jnp.exp(cum_last - cum)
    rt = r * e_cum
    kt = k2 * e_ncum
    bt = ab * e_ncum
    at = -kk * jnp.exp(cum - logw)
    khat = k2 * e_tail
    bhat = ab * e_tail
    p_last = e_cum[ch - 1:ch, :]

    lane = lax.broadcasted_iota(jnp.int32, (ch, LANES), 1)
    head0 = lane < HEAD_DIM
    strict = strict_ref[...] > 0.5
    incl = incl_ref[...] > 0.5
    eye = eye_ref[...]

    def stack(x):
        return jnp.concatenate([jnp.where(head0, x, 0.0), jnp.where(head0, 0.0, x)], axis=0)

    def mm(x, w, dn=NN):
        return _dg(x.astype(BF16), w.astype(BF16), dn)

    pairs = range(npair)
    rows = 2 * ch
    sl = [slice(p * LANES, (p + 1) * LANES) for p in pairs]
    at2, rt2, bt2, kt2, v2, khat2, bhat2 = ([stack(t[:, sl[p]]) for p in pairs]
                                            for t in (at, rt, bt, kt, v, khat, bhat))
    gram = []
    for p in pairs:
        lh, ll = _split(jnp.concatenate([at2[p], rt2[p]], axis=0))
        rh, rl = _split(jnp.concatenate([bt2[p], kt2[p]], axis=0))
        gram.append(_dg(jnp.concatenate([lh, ll], axis=1), jnp.concatenate([rh, rh], axis=1), NT))
    a_ab = [jnp.where(strict, g[:rows, :rows], 0.0) for g in gram]
    a_ak = [jnp.where(strict, g[:rows, rows:], 0.0) for g in gram]
    a_rb = [jnp.where(incl, g[rows:, :rows], 0.0) for g in gram]
    a_rk = [jnp.where(incl, g[rows:, rows:], 0.0) for g in gram]
    akv = [mm(a_ak[p], v2[p]) for p in pairs]

    tinv = [eye + a for a in a_ab]
    pw = [mm(a, a) for a in a_ab]
    for _ in range(int(math.log2(ch)) - 2):
        both = [mm(pw[p], jnp.concatenate([pw[p], tinv[p]], axis=1)) for p in pairs]
        pw = [b[:, :rows] for b in both]
        tinv = [tinv[p] + both[p][:, rows:] for p in pairs]
    tinv = [tinv[p] + mm(pw[p], tinv[p]) for p in pairs]

    s = [s_ref[p] for p in pairs]
    tw = [mm(tinv[p], jnp.concatenate([at2[p], akv[p]], axis=1)) for p in pairs]
    ws = [mm(jnp.concatenate([tw[p][:, :LANES], rt2[p]], axis=0), s[p], NT) for p in pairs]
    uv = [jnp.concatenate([ws[p][:rows] + tw[p][:, LANES:], v2[p]], axis=0) for p in pairs]
    y2 = [ws[p][rows:] + mm(jnp.concatenate([a_rb[p], a_rk[p]], axis=1), uv[p]) for p in pairs]
    for p in pairs:
        s_ref[p] = s[p] * p_last[:, sl[p]] + mm(uv[p], jnp.concatenate([bhat2[p], khat2[p]], axis=0), TN)
    y = jnp.concatenate([t[:ch, :] + t[ch:, :] for t in y2], axis=1)

    inv_n = 1.0 / HEAD_DIM
    mu = head_sum(y) * inv_n
    d = y - mu
    var = head_sum(d * d) * inv_n
    yn = d * lax.rsqrt(var + GN_EPS) * tiled(gnw_ref) + tiled(gnb_ref)
    y = (yn + head_sum(r * k2 * tiled(rk_ref)) * v).astype(y_ref.dtype)
    wd = y_ref.shape[2]
    for b in range(nbatch):
        y_ref[b] = y[:, b * wd:(b + 1) * wd]


def rwkv7(p, p_lora, batch, seqlen, shift_mix, w_up, w0, a_up, a0, k_k, k_a, r_k, gn_w, gn_b):
    width = w0.shape[0]
    npair = width // LANES
    ch = RWKV_CHUNK
    nchunk = seqlen // ch
    assert seqlen % ch == 0 and 2 * LORA_RANK == LANES and 2 * HEAD_DIM == LANES
    zeros = jnp.zeros((LORA_RANK, width), F32)
    wup_pad = jnp.concatenate([w_up, zeros], axis=0)
    aup_pad = jnp.concatenate([zeros, a_up], axis=0)
    hb = RWKV_PAIRS
    assert npair % hb == 0
    idx = jnp.arange(LANES)
    seg = (idx[:, None] // HEAD_DIM == idx[None, :] // HEAD_DIM).astype(BF16)
    seg = jnp.concatenate([seg, seg], axis=0)
    t = jnp.arange(ch)
    tri = (t[None, :] <= t[:, None]).astype(BF16)
    tri = jnp.concatenate([tri, tri, tri], axis=1)
    i2 = jnp.arange(2 * ch)
    same = (i2[:, None] // ch) == (i2[None, :] // ch)
    strict = (same & ((i2[None, :] % ch) < (i2[:, None] % ch))).astype(F32)
    incl = (same & ((i2[None, :] % ch) <= (i2[:, None] % ch))).astype(F32)
    eye = jnp.eye(2 * ch, dtype=F32)
    row2 = lambda x: x.reshape(1, -1)

    wd = hb * LANES
    ngrp = npair // hb

    def tok(off):
        return pl.BlockSpec((batch, ch, wd), lambda h, c: (0, c, off + h))

    def par(off):
        return pl.BlockSpec((1, wd), lambda h, c: (0, off + h))

    def const(shape):
        return pl.BlockSpec(shape, lambda h, c: (0, 0))

    up = pl.BlockSpec((LANES, wd), lambda h, c: (0, h))
    lora_tok = pl.BlockSpec((batch, ch, LANES), lambda h, c: (0, c, 0))
    lora_par = pl.BlockSpec((1, LANES), lambda h, c: (0, 3 * npair))
    p3 = p.reshape(batch, seqlen, 3 * width)
    y = pl.pallas_call(
        _rwkv_kernel,
        grid=(ngrp, nchunk),
        in_specs=[tok(0), tok(ngrp), tok(2 * ngrp), lora_tok,
                  par(0), par(ngrp), par(2 * ngrp), lora_par, up, up,
                  par(0), par(0), par(0), par(0), par(0), par(0), par(0),
                  const((2 * LANES, LANES)), const((ch, 3 * ch)), const((2 * ch, 2 * ch)),
                  const((2 * ch, 2 * ch)), const((2 * ch, 2 * ch))],
        out_specs=pl.BlockSpec((batch, ch, wd), lambda h, c: (0, c, h)),
        out_shape=jax.ShapeDtypeStruct((batch, seqlen, width), BF16),
        scratch_shapes=[pltpu.VMEM((batch * hb, LANES, LANES), F32), pltpu.VMEM((SUBLANES, batch * wd), F32)],
        compiler_params=_cparams(("parallel", "arbitrary")),
        name="rwkv7",
    )(p3, p3, p3, p_lora.reshape(batch, seqlen, LANES), row2(shift_mix), row2(shift_mix), row2(shift_mix),
      row2(shift_mix), wup_pad, aup_pad,
      row2(w0), row2(a0), row2(k_k), row2(k_a), row2(r_k), row2(gn_w), row2(gn_b),
      seg, tri, strict, incl, eye)
    return y.reshape(batch * seqlen, width)


def _sb_kernel(q_ref, k_ref, v_ref, uo_ref, o_ref, k2_ref, v2_ref, acc_ref, carry_ref, lb_ref, sums_ref, z_ref):
    qi = pl.program_id(2)
    kb, qb = SB_BLOCK, SB_QBLOCK
    nsub = qb // kb
    npair = q_ref.shape[1] // LANES
    pairs_ = range(npair)

    @pl.when(qi == 0)
    def _():
        lane = lax.broadcasted_iota(jnp.int32, (kb, LANES), 1)
        head0 = lane < HEAD_DIM

        def fill(i, c):
            start = pl.multiple_of(i * kb, kb)
            for p in pairs_:
                for src, dst in ((k_ref, k2_ref), (v_ref, v2_ref)):
                    t = src[pl.ds(start, kb), p * LANES:(p + 1) * LANES].astype(F32)
                    dst[p, i, 0:kb, :] = jnp.where(head0, t, 0.0).astype(BF16)
                    dst[p, i, kb:2 * kb, :] = jnp.where(head0, 0.0, t).astype(BF16)
            return c

        lax.fori_loop(0, k_ref.shape[0] // kb, fill, 0)

    q = [q_ref[:, p * LANES:(p + 1) * LANES] for p in pairs_]
    uo = uo_ref[...]
    acc_ref[...] = jnp.zeros_like(acc_ref)
    carry_ref[...] = jnp.zeros_like(carry_ref)
    tpos = lax.broadcasted_iota(jnp.int32, (qb, kb), 0)
    spos = lax.broadcasted_iota(jnp.int32, (qb, kb), 1)

    def logits(p, sb, zslot):
        base = pl.multiple_of(sb * nsub, nsub)
        keys = k2_ref[p, pl.ds(base, nsub)].reshape(nsub * 2 * kb, LANES)
        z_ref[zslot, p] = _dg(q[p], keys, NT)

    def scores(p, zslot, slot, diagonal):
        for j in range(nsub):
            for h in range(2):
                c = 2 * j + h
                z = z_ref[zslot, p, :, c * kb:(c + 1) * kb]
                nz = -z
                log_keep = jnp.minimum(nz, 0.0) - jnp.log(1.0 + jnp.exp(jnp.minimum(z, nz)))
                if diagonal:
                    causal = (spos + j * kb) < tpos
                    log_keep = jnp.where(causal, log_keep, 0.0)
                    z = jnp.where(causal, z, -jnp.inf)
                hi = log_keep.astype(BF16)
                lo = (log_keep - hi.astype(F32)).astype(BF16)
                lb_ref[slot, p, c] = z
                sums_ref[slot, p, c] = _dg(jnp.concatenate([hi, lo], axis=1), uo, NN)

    def accumulate(p, sb, slot):
        base = pl.multiple_of(sb * nsub, nsub)
        attn = {}
        for h in range(2):
            carry = carry_ref[p, h]
            for j in reversed(range(nsub)):
                s = sums_ref[slot, p, 2 * j + h]
                attn[j, h] = jnp.exp(lb_ref[slot, p, 2 * j + h] + carry + s[:, :kb]).astype(BF16)
                carry = carry + s[:, kb:]
            carry_ref[p, h] = carry
        weights = jnp.concatenate([attn[j, h] for j in range(nsub) for h in range(2)], axis=1)
        values = v2_ref[p, pl.ds(base, nsub)].reshape(nsub * 2 * kb, LANES)
        acc_ref[p] += _dg(weights, values, NN)

    for p in pairs_:
        logits(p, qi, 0)
    for p in pairs_:
        scores(p, 0, 0, True)
    trips = qi // 2

    def body(i, c):
        sb = qi - 1 - 2 * i
        for p in pairs_:
            logits(p, sb, 0)
        for p in pairs_:
            logits(p, sb - 1, 1)
        for p in pairs_:
            accumulate(p, sb + 1, 0)
        for p in pairs_:
            scores(p, 0, 1, False)
        for p in pairs_:
            scores(p, 1, 0, False)
        for p in pairs_:
            accumulate(p, sb, 1)
        return c

    lax.fori_loop(0, trips, body, 0)
    odd = qi - 2 * trips == 1

    @pl.when(odd)
    def _():
        for p in pairs_:
            logits(p, 0, 0)
        for p in pairs_:
            accumulate(p, 1, 0)
        for p in pairs_:
            scores(p, 0, 1, False)
        for p in pairs_:
            accumulate(p, 0, 1)

    @pl.when(jnp.logical_not(odd))
    def _():
        for p in pairs_:
            accumulate(p, 0, 0)

    for p in pairs_:
        o_ref[:, p * LANES:(p + 1) * LANES] = acc_ref[p].astype(o_ref.dtype)


def stick_breaking(p, batch, seqlen, width):
    npair = width // LANES
    kb, qb = SB_BLOCK, SB_QBLOCK
    nq = seqlen // qb
    hp = SB_PAIRS
    ngrp = npair // hp
    wd = hp * LANES
    nsl = 2 * qb // kb
    j = jnp.arange(kb)
    later = (j[:, None] >= j[None, :]).astype(BF16)
    uo = jnp.concatenate([later, jnp.ones((kb, kb), BF16)], axis=1)
    uo = jnp.concatenate([uo, uo], axis=0)
    return pl.pallas_call(
        _sb_kernel,
        grid=(batch, ngrp, nq),
        in_specs=[pl.BlockSpec((qb, wd), lambda b, h, i: (b * nq + i, h)),
                  pl.BlockSpec((seqlen, wd), lambda b, h, i: (b, ngrp + h), pipeline_mode=pl.Buffered(1)),
                  pl.BlockSpec((seqlen, wd), lambda b, h, i: (b, 2 * ngrp + h), pipeline_mode=pl.Buffered(1)),
                  pl.BlockSpec((2 * kb, 2 * kb), lambda b, h, i: (0, 0))],
        out_specs=pl.BlockSpec((qb, wd), lambda b, h, i: (b * nq + i, h)),
        out_shape=jax.ShapeDtypeStruct((batch * seqlen, width), BF16),
        scratch_shapes=[pltpu.VMEM((hp, seqlen // kb, 2 * kb, LANES), BF16),
                        pltpu.VMEM((hp, seqlen // kb, 2 * kb, LANES), BF16),
                        pltpu.VMEM((hp, qb, LANES), F32), pltpu.VMEM((hp, 2, qb, kb), F32),
                        pltpu.VMEM((2, hp, nsl, qb, kb), F32), pltpu.VMEM((2, hp, nsl, qb, 2 * kb), F32),
                        pltpu.VMEM((2, hp, qb, nsl * kb), F32)],
        compiler_params=_cparams(("parallel", "parallel", "arbitrary")),
        name="stick_breaking",
    )(p, p, p, uo)


def _gate_out_kernel(ya_ref, yb_ref, g0_ref, g1_ref, x_ref, w_ref, ng_ref, h_ref, hn_ref, wb_ref):
    @pl.when(pl.program_id(0) == 0)
    def _():
        wb_ref[...] = w_ref[...].astype(BF16)

    gate = jnp.concatenate([g0_ref[...], g1_ref[...]], axis=-1).astype(F32)
    y = jnp.concatenate([ya_ref[...], yb_ref[...]], axis=-1).astype(F32) * (gate * jax.nn.sigmoid(gate))
    h = x_ref[...] + jnp.dot(y.astype(BF16), wb_ref[...], preferred_element_type=F32)
    h_ref[...] = h
    ms = jnp.mean(h * h, axis=-1, keepdims=True)
    hn_ref[...] = ((h * lax.rsqrt(ms + RMS_EPS)) * ng_ref[...]).astype(hn_ref.dtype)


def gate_out(ya, yb, p, gate_col, x, w, next_g, tm=512):
    m, d = x.shape
    half = ya.shape[1]
    tm = min(tm, m)
    assert gate_col % half == 0 and d == 2 * half
    g0 = gate_col // half
    row = lambda n: pl.BlockSpec((tm, n), lambda i: (i, 0))
    return pl.pallas_call(
        _gate_out_kernel,
        grid=(m // tm,),
        in_specs=[row(half), row(half),
                  pl.BlockSpec((tm, half), lambda i: (i, g0)), pl.BlockSpec((tm, half), lambda i: (i, g0 + 1)), row(d),
                  pl.BlockSpec((d, d), lambda i: (0, 0), pipeline_mode=pl.Buffered(1)),
                  pl.BlockSpec((1, d), lambda i: (0, 0))],
        out_specs=[row(d), row(d)],
        out_shape=[jax.ShapeDtypeStruct((m, d), F32), jax.ShapeDtypeStruct((m, d), BF16)],
        scratch_shapes=[pltpu.VMEM((d, d), BF16)],
        compiler_params=_cparams(("arbitrary",)),
        name="gate_out",
    )(ya, yb, p, p, x, w, next_g.reshape(1, d))


def _s5_prep_kernel(lre_ref, lim_ref, ldt_ref, cre_ref, cim_ref, btre_ref, btim_ref,
                    kt_ref, w2_ref, vt2_ref, c2re_ref, c2im_ref, ckre_ref, ckim_ref):
    blk = S5_BLOCK
    h = S5_GROUP
    ns = S5_STATE
    gp = kt_ref.shape[0]
    wl = blk * h
    lre, lim = lre_ref[...], lim_ref[...]
    dt = jnp.exp(ldt_ref[...])
    mag = jnp.exp(lre * dt)
    bre = mag * jnp.cos(lim * dt)
    bim = mag * jnp.sin(lim * dt)
    den = lre * lre + lim * lim
    nre, nim = bre - 1.0, bim
    fre = (nre * lre + nim * lim) / den
    fim = (nim * lre - nre * lim) / den
    cre, cim = cre_ref[...], cim_ref[...]
    ckr, cki = cre * fre - cim * fim, cre * fim + cim * fre
    btre, btim = btre_ref[...], btim_ref[...]
    pr, pi = jnp.ones_like(bre), jnp.zeros_like(bre)
    powers = []
    for tau in range(blk + 1):
        ckre_ref[:, tau * h:(tau + 1) * h, :] = ckr
        ckim_ref[:, tau * h:(tau + 1) * h, :] = cki
        powers.append((pr, pi))
        ckr, cki = ckr * bre - cki * bim, ckr * bim + cki * bre
        pr, pi = pr * bre - pi * bim, pr * bim + pi * bre

    def put(ref, rows, lane0, x):
        r0, r1 = rows
        x = x.reshape(gp // 2, 2, r1 - r0, ns).astype(ref.dtype)
        ref[:, r0:r1, lane0:lane0 + ns] = x[:, 0]
        ref[:, ref.shape[1] // 2 + r0:ref.shape[1] // 2 + r1, lane0 + ns:lane0 + 2 * ns] = x[:, 1]

    w2_ref[...] = jnp.zeros_like(w2_ref)
    vt2_ref[...] = jnp.zeros_like(vt2_ref)
    for i in range(blk):
        pr, pi = powers[blk - 1 - i]
        put(w2_ref, (i * h, (i + 1) * h), 0, pr * btre - pi * btim)
        put(w2_ref, (i * h, (i + 1) * h), 2 * ns, pr * btim + pi * btre)
    put(vt2_ref, (0, wl), 0, ckre_ref[:, h:(blk + 1) * h, :])
    put(vt2_ref, (0, wl), 2 * ns, -ckim_ref[:, h:(blk + 1) * h, :])

    def put_lanes(ref, r0, x):
        x = x.reshape(gp // 2, 2, x.shape[1], ns)
        ref[:, r0:r0 + x.shape[2], 0:ns] = x[:, 0]
        ref[:, r0:r0 + x.shape[2], ns:2 * ns] = x[:, 1]

    row = lax.broadcasted_iota(jnp.int32, (gp, SUBLANES, ns), 1)
    qr, qi = powers[blk]
    pr, pi = qr, qi
    for r in range(SUBLANES):
        if r + 1 in (1, 2, 4):
            lvl = (1, 2, 4).index(r + 1)
            put_lanes(c2re_ref, lvl * SUBLANES, jnp.where(row >= r + 1, pr, 0.0))
            put_lanes(c2im_ref, lvl * SUBLANES, jnp.where(row >= r + 1, pi, 0.0))
        put_lanes(c2re_ref, 3 * SUBLANES + r, pr)
        put_lanes(c2im_ref, 3 * SUBLANES + r, pi)
        pr, pi = pr * qr - pi * qi, pr * qi + pi * qr
    for g in range(gp):
        kt_ref[g] = (_dot(btre[g], ckre_ref[g, 0:blk * h, :], NT) - _dot(btim[g], ckim_ref[g, 0:blk * h, :], NT))


def _s5_kernel(nbatch, p_ref, kt_ref, w_ref, vt_ref, cre_ref, cim_ref, d_ref, o_ref,
               u_ref, gre_ref, gim_ref, xre_ref, xim_ref, yp_ref, stage_ref):
    npair = u_ref.shape[0] // 2
    rows = u_ref.shape[1]
    wlane = u_ref.shape[2]
    blk = S5_BLOCK
    ngrp = 2 * npair
    rc = S5_RELAYOUT_ROWS
    lane_chunk = lax.broadcasted_iota(jnp.int32, (rc, LANES), 1) // S5_GROUP

    def chunk_transpose(arrs):
        arrs = list(arrs)
        s = ngrp // 2
        while s:
            upper = (lane_chunk & s) != 0
            for x in range(ngrp):
                if x & s == 0:
                    ax, ay = arrs[x], arrs[x + s]
                    arrs[x] = jnp.where(upper, pltpu.roll(ay, s * S5_GROUP, 1), ax)
                    arrs[x + s] = jnp.where(upper, ay, pltpu.roll(ax, LANES - s * S5_GROUP, 1))
            s //= 2
        return arrs

    def relayout_in(c, carry):
        t0 = pl.multiple_of(c * (rc * blk), rc * blk)
        r0 = pl.multiple_of(c * rc, rc)
        stage_ref[...] = p_ref[pl.ds(t0, rc * blk), :].astype(F32)
        for half in range(blk // ngrp):
            z = chunk_transpose(stage_ref[pl.ds(ngrp * half + i, rc, stride=blk), :] for i in range(ngrp))
            for g in range(ngrp):
                u_ref[g, pl.ds(r0, rc), half * LANES:(half + 1) * LANES] = z[g].astype(BF16)
        return carry

    lax.fori_loop(0, rows // rc, relayout_in, 0)
    lane = lax.broadcasted_iota(jnp.int32, (S5_GROUP, wlane), 1)

    for g in range(2 * npair):
        kt = kt_ref[g]
        blocks = [kt] + [jnp.where(lane >= i * S5_GROUP, pltpu.roll(kt, i * S5_GROUP, 1), 0.0)
                         for i in range(1, S5_BLOCK)]
        kmat = jnp.concatenate(blocks, axis=0).astype(BF16)
        yp_ref[g] = _dg(u_ref[g], kmat, NN)
    for p in range(npair):
        ucat = jnp.concatenate([u_ref[2 * p], u_ref[2 * p + 1]], axis=1)
        g = _dg(ucat, w_ref[p], NN)
        gre_ref[p] = g[:, :LANES]
        gim_ref[p] = g[:, LANES:]

    def cma(xr, xi, ar, ai, sr, si):
        return xr + (ar * sr - ai * si), xi + (ar * si + ai * sr)

    row = lax.broadcasted_iota(jnp.int32, (SUBLANES, LANES), 0)
    per_batch = rows // nbatch

    def tile(t, carry, base):
        r0 = pl.multiple_of(base + t * SUBLANES, SUBLANES)
        new_carry = []
        for p in range(npair):
            xr = gre_ref[p, pl.ds(r0, SUBLANES), :]
            xi = gim_ref[p, pl.ds(r0, SUBLANES), :]
            for lvl, sh in enumerate((1, 2, 4)):
                ar = cre_ref[p, lvl * SUBLANES:(lvl + 1) * SUBLANES, :]
                ai = cim_ref[p, lvl * SUBLANES:(lvl + 1) * SUBLANES, :]
                xr, xi = cma(xr, xi, ar, ai, pltpu.roll(xr, sh, 0), pltpu.roll(xi, sh, 0))
            ar = cre_ref[p, 3 * SUBLANES:4 * SUBLANES, :]
            ai = cim_ref[p, 3 * SUBLANES:4 * SUBLANES, :]
            cr, ci = carry[2 * p], carry[2 * p + 1]
            xr, xi = cma(xr, xi, ar, ai, cr, ci)
            xre_ref[p, pl.ds(r0, SUBLANES), :] = jnp.where(row == 0, cr, pltpu.roll(xr, 1, 0))
            xim_ref[p, pl.ds(r0, SUBLANES), :] = jnp.where(row == 0, ci, pltpu.roll(xi, 1, 0))
            new_carry.append(jnp.broadcast_to(xr[SUBLANES - 1:SUBLANES, :], (SUBLANES, LANES)))
            new_carry.append(jnp.broadcast_to(xi[SUBLANES - 1:SUBLANES, :], (SUBLANES, LANES)))
        return tuple(new_carry)

    zero = jnp.zeros((SUBLANES, LANES), F32)
    for b in range(nbatch):
        lax.fori_loop(0, per_batch // SUBLANES, functools.partial(tile, base=b * per_batch), (zero,) * (2 * npair))

    for p in range(npair):
        xs = jnp.concatenate([xre_ref[p], xim_ref[p]], axis=1).astype(BF16)
        corr = _dg(xs, vt_ref[p], NT)
        for k in range(2):
            g = 2 * p + k
            y = yp_ref[g] + corr[:, k * wlane:(k + 1) * wlane] + d_ref[g] * u_ref[g].astype(F32)
            yp_ref[g] = jax.nn.gelu(y)

    def relayout_out(c, carry):
        t0 = pl.multiple_of(c * (rc * blk), rc * blk)
        r0 = pl.multiple_of(c * rc, rc)
        for half in range(blk // ngrp):
            yt = chunk_transpose(yp_ref[g, pl.ds(r0, rc), half * LANES:(half + 1) * LANES] for g in range(ngrp))
            for j in range(ngrp):
                stage_ref[pl.ds(ngrp * half + j, rc, stride=blk), :] = yt[j]
        o_ref[pl.ds(t0, rc * blk), :] = stage_ref[...].astype(o_ref.dtype)
        return carry

    lax.fori_loop(0, rows // rc, relayout_out, 0)


def s5_ssm(p, batch, seqlen, lam_re, lam_im, log_dt, b_re, b_im, c_re, c_im, d_skip):
    ngroup, nstate = lam_re.shape
    h, blk = S5_GROUP, S5_BLOCK
    width = ngroup * h
    wl = blk * h
    nb = batch * seqlen // blk
    gp = S5_PREP_GROUPS
    assert nstate == S5_STATE and 2 * nstate == LANES
    assert seqlen % (blk * SUBLANES) == 0 and ngroup % gp == 0 and ngroup % (2 * S5_PAIRS) == 0

    g3 = lambda x: x.reshape(ngroup, 1, nstate)
    bt = lambda x: jnp.swapaxes(x, 1, 2)
    spec1 = pl.BlockSpec((gp, 1, nstate), lambda i: (i, 0, 0))
    spec_c = pl.BlockSpec((gp, h, nstate), lambda i: (i, 0, 0))
    spec_k = pl.BlockSpec((gp, h, wl), lambda i: (i, 0, 0))
    pair = lambda r, c: pl.BlockSpec((gp // 2, r, c), lambda i: (i, 0, 0))
    npairs = ngroup // 2
    kt, w2, vt2, c_re2, c_im2 = pl.pallas_call(
        _s5_prep_kernel,
        grid=(ngroup // gp,),
        in_specs=[spec1, spec1, spec1, spec_c, spec_c, spec_c, spec_c],
        out_specs=[spec_k, pair(2 * wl, 2 * LANES), pair(2 * wl, 2 * LANES),
                   pair(4 * SUBLANES, LANES), pair(4 * SUBLANES, LANES)],
        out_shape=[jax.ShapeDtypeStruct((ngroup, h, wl), F32),
                   jax.ShapeDtypeStruct((npairs, 2 * wl, 2 * LANES), BF16),
                   jax.ShapeDtypeStruct((npairs, 2 * wl, 2 * LANES), BF16),
                   jax.ShapeDtypeStruct((npairs, 4 * SUBLANES, LANES), F32),
                   jax.ShapeDtypeStruct((npairs, 4 * SUBLANES, LANES), F32)],
        scratch_shapes=[pltpu.VMEM((gp, (blk + 1) * h, nstate), F32), pltpu.VMEM((gp, (blk + 1) * h, nstate), F32)],
        compiler_params=_cparams(("parallel",)),
        name="s5_prep",
    )(g3(lam_re), g3(lam_im), jnp.broadcast_to(log_dt[:, None, None], (ngroup, 1, nstate)), c_re, c_im,
      bt(b_re), bt(b_im))

    d_row = jnp.tile(d_skip.reshape(ngroup, 1, h), (1, 1, blk))

    np_ = S5_PAIRS
    gs = 2 * np_
    assert gs * h == LANES and blk % gs == 0 and nb % S5_RELAYOUT_ROWS == 0
    tok = pl.BlockSpec((nb * blk, LANES), lambda i: (0, i))
    grp = lambda r, c: pl.BlockSpec((gs, r, c), lambda i: (i, 0, 0))
    par = lambda r, c: pl.BlockSpec((np_, r, c), lambda i: (i, 0, 0))
    return pl.pallas_call(
        functools.partial(_s5_kernel, batch),
        grid=(ngroup // gs,),
        in_specs=[tok, grp(h, wl), par(2 * wl, 2 * LANES), par(2 * wl, 2 * LANES),
                  par(4 * SUBLANES, LANES), par(4 * SUBLANES, LANES), grp(1, wl)],
        out_specs=tok,
        out_shape=jax.ShapeDtypeStruct((batch * seqlen, width), BF16),
        scratch_shapes=[pltpu.VMEM((gs, nb, wl), BF16)] + [pltpu.VMEM((np_, nb, LANES), F32)] * 4
        + [pltpu.VMEM((gs, nb, wl), F32), pltpu.VMEM((S5_RELAYOUT_ROWS * blk, LANES), F32)],
        compiler_params=_cparams(("parallel",)),
        name="s5_ssm",
    )(p, kt, w2, vt2, c_re2, c_im2, d_row)


def _glu_kernel(y_ref, yj_ref, gj_ref, w_ref, b_ref, o_ref):
    w = w_ref[...].astype(BF16)
    b = b_ref[...]
    for r in range(0, y_ref.shape[0], GLU_SUBTILE):
        rows = slice(r, r + GLU_SUBTILE)
        z = jnp.dot(y_ref[rows, :], w, preferred_element_type=F32) + b
        gate = gj_ref[rows, :].astype(F32)
        yj = yj_ref[rows, :].astype(F32)
        o_ref[rows, :] = (yj * jax.nn.sigmoid(z) * (gate * jax.nn.sigmoid(gate))).astype(o_ref.dtype)


def glu_gate(y, p, w, b, tm=1024, tn=1024):
    m, d = y.shape
    tm = min(tm, m)
    goff = d // tn
    return pl.pallas_call(
        _glu_kernel,
        grid=(m // tm, d // tn),
        in_specs=[pl.BlockSpec((tm, d), lambda i, j: (i, 0)),
                  pl.BlockSpec((tm, tn), lambda i, j: (i, j)),
                  pl.BlockSpec((tm, tn), lambda i, j: (i, goff + j)),
                  pl.BlockSpec((d, tn), lambda i, j: (0, j)),
                  pl.BlockSpec((1, tn), lambda i, j: (0, j))],
        out_specs=pl.BlockSpec((tm, tn), lambda i, j: (i, j)),
        out_shape=jax.ShapeDtypeStruct((m, d), BF16),
        compiler_params=_cparams(("parallel", "arbitrary")),
        name="glu_gate",
    )(y, y, p, w, b.reshape(1, d))


def _final_kernel(a_ref, h_ref, w_ref, g_ref, o_ref, wb_ref):
    @pl.when(pl.program_id(0) == 0)
    def _():
        wb_ref[...] = w_ref[...].astype(BF16)

    h = h_ref[...] + jnp.dot(a_ref[...], wb_ref[...], preferred_element_type=F32)
    ms = jnp.mean(h * h, axis=-1, keepdims=True)
    o_ref[...] = (h * lax.rsqrt(ms + RMS_EPS)) * g_ref[...]


def final_out(a, h, w, g, tm=512):
    m, d = h.shape
    tm = min(tm, m)
    row = pl.BlockSpec((tm, d), lambda i: (i, 0))
    return pl.pallas_call(
        _final_kernel,
        grid=(m // tm,),
        in_specs=[row, row, pl.BlockSpec((d, d), lambda i: (0, 0), pipeline_mode=pl.Buffered(1)),
                  pl.BlockSpec((1, d), lambda i: (0, 0))],
        out_specs=row,
        out_shape=jax.ShapeDtypeStruct((m, d), F32),
        scratch_shapes=[pltpu.VMEM((d, d), BF16)],
        compiler_params=_cparams(("arbitrary",)),
        name="final_out",
    )(a, h, w, g.reshape(1, d))


def kernel(x, norm_g, final_g, ab_w_in, rwkv_shift_mix, rwkv_w_up, rwkv_w0, rwkv_a_up, rwkv_a0, rwkv_k_k, rwkv_k_a, rwkv_r_k, rwkv_gn_w, rwkv_gn_b, ab_w_out, s5_w_in, s5_lam_re, s5_lam_im, s5_log_dt, s5_b_re, s5_b_im, s5_c_re, s5_c_im, s5_d, s5_w_glu, s5_b_glu, s5_w_out):
    batch, seqlen, d = x.shape
    m = batch * seqlen
    rwkv_w = rwkv_w0.shape[1]
    rwkv_proj = 3 * rwkv_w + 2 * LORA_RANK
    sb_w = (ab_w_in.shape[2] - rwkv_proj - d) // 3
    x2 = x.reshape(m, d)

    w_in = ab_w_in[0]
    scale = HEAD_DIM ** -0.5
    col_scale = jnp.concatenate([jnp.full((sb_w,), scale, F32), jnp.ones((2 * sb_w + d,), F32)])
    xn, p_lora = rmsnorm_proj(x2, norm_g[0], w_in, 3 * rwkv_w, 2 * LORA_RANK)
    p_rkv = matmul(xn, w_in, F32, 0, 3 * rwkv_w, name="proj_rkv")
    p_sb = matmul(xn, w_in, BF16, rwkv_proj, 3 * sb_w + d, col_scale, tm=2048, name="proj_sb_gate")
    y_a = rwkv7(p_rkv, p_lora, batch, seqlen, rwkv_shift_mix[0], rwkv_w_up[0], rwkv_w0[0], rwkv_a_up[0], rwkv_a0[0],
                rwkv_k_k[0], rwkv_k_a[0], rwkv_r_k[0], rwkv_gn_w[0], rwkv_gn_b[0])
    y_b = stick_breaking(p_sb, batch, seqlen, sb_w)
    h1, hn1 = gate_out(y_a, y_b, p_sb, 3 * sb_w, x2, ab_w_out[0], norm_g[1])

    p1 = matmul(hn1, s5_w_in[0], BF16, 0, 2 * d, tm=2048, name="proj_s5")
    y_s5 = s5_ssm(p1, batch, seqlen, s5_lam_re[0], s5_lam_im[0], s5_log_dt[0], s5_b_re[0], s5_b_im[0],
                  s5_c_re[0], s5_c_im[0], s5_d[0])
    act = glu_gate(y_s5, p1, s5_w_glu[0], s5_b_glu[0])
    out = final_out(act, h1, s5_w_out[0], final_g)
    return out.reshape(batch, seqlen, d)
```

```python
import functools
import math

import jax
import jax.numpy as jnp
from jax import lax
from jax.experimental import pallas as pl
from jax.experimental.pallas import tpu as pltpu

F32 = jnp.float32
BF16 = jnp.bfloat16

HEAD_DIM = 64
LANES = 128
SUBLANES = 8
LORA_RANK = 64
S5_GROUP = 16
S5_STATE = 64
RMS_EPS = 1e-6
GN_EPS = 64e-5
DECAY_SCALE = math.exp(-0.5)

RWKV_CHUNK = 64
RWKV_PAIRS = 8
SB_BLOCK = 128
SB_QBLOCK = 512
SB_PAIRS = 2
S5_BLOCK = 16
S5_PAIRS = 4
S5_PREP_GROUPS = 16
S5_RELAYOUT_ROWS = 128
GLU_SUBTILE = 128
VMEM_LIMIT = 56 * 1024 * 1024

NN = (((1,), (0,)), ((), ()))
NT = (((1,), (1,)), ((), ()))
TN = (((0,), (0,)), ((), ()))


def _cparams(sem):
    return pltpu.CompilerParams(dimension_semantics=sem, vmem_limit_bytes=VMEM_LIMIT)


def _split(x):
    hi = x.astype(BF16)
    lo = (x - hi.astype(F32)).astype(BF16)
    return hi, lo


def _dg(a, b, dn):
    return lax.dot_general(a, b, dn, preferred_element_type=F32)


def _dot(a, b, dn=NN, passes=3):
    if passes == 1:
        return _dg(a.astype(BF16), b.astype(BF16), dn)
    ah, al = _split(a)
    bh, bl = _split(b)
    return _dg(ah, bh, dn) + (_dg(ah, bl, dn) + _dg(al, bh, dn))


def _norm_proj_kernel(x_ref, g_ref, w_ref, xn_ref, o_ref):
    x = x_ref[...]
    ms = jnp.mean(x * x, axis=-1, keepdims=True)
    xn = ((x * lax.rsqrt(ms + RMS_EPS)) * g_ref[...]).astype(xn_ref.dtype)
    xn_ref[...] = xn
    o_ref[...] = jnp.dot(xn, w_ref[...].astype(BF16), preferred_element_type=F32)


def rmsnorm_proj(x, g, w, col0, n, tm=1024):
    m, d = x.shape
    tm = min(tm, m)
    assert col0 % LANES == 0 and n % LANES == 0
    return pl.pallas_call(
        _norm_proj_kernel,
        grid=(m // tm,),
        in_specs=[pl.BlockSpec((tm, d), lambda i: (i, 0)), pl.BlockSpec((1, d), lambda i: (0, 0)),
                  pl.BlockSpec((pl.Element(d), pl.Element(n)), lambda i: (0, col0))],
        out_specs=[pl.BlockSpec((tm, d), lambda i: (i, 0)), pl.BlockSpec((tm, n), lambda i: (i, 0))],
        out_shape=[jax.ShapeDtypeStruct((m, d), BF16), jax.ShapeDtypeStruct((m, n), F32)],
        compiler_params=_cparams(("parallel",)),
        name="rmsnorm_proj_lora",
    )(x, g.reshape(1, d), w)


def _mm_kernel(a_ref, w_ref, o_ref):
    o_ref[...] = jnp.dot(a_ref[...], w_ref[...].astype(BF16), preferred_element_type=F32).astype(o_ref.dtype)


def _mm_scaled_kernel(a_ref, w_ref, s_ref, o_ref):
    acc = jnp.dot(a_ref[...], w_ref[...].astype(BF16), preferred_element_type=F32)
    o_ref[...] = (acc * s_ref[...]).astype(o_ref.dtype)


def matmul(a, w, out_dtype, col0, n, col_scale=None, tm=1024, tn=1024, name="matmul"):
    m, k = a.shape
    tm = min(tm, m)
    assert m % tm == 0 and n % tn == 0 and col0 % LANES == 0
    in_specs = [pl.BlockSpec((tm, k), lambda i, j: (i, 0)),
                pl.BlockSpec((pl.Element(k), pl.Element(tn)), lambda i, j: (0, pl.multiple_of(col0 + j * tn, LANES)))]
    args = [a, w]
    if col_scale is not None:
        in_specs.append(pl.BlockSpec((1, tn), lambda i, j: (0, j)))
        args.append(col_scale.reshape(1, n))
    return pl.pallas_call(
        _mm_kernel if col_scale is None else _mm_scaled_kernel,
        grid=(m // tm, n // tn),
        in_specs=in_specs,
        out_specs=pl.BlockSpec((tm, tn), lambda i, j: (i, j)),
        out_shape=jax.ShapeDtypeStruct((m, n), out_dtype),
        compiler_params=_cparams(("parallel", "arbitrary")),
        name=name,
    )(*args)


def _rwkv_kernel(r_ref, k_ref, v_ref, lo_ref, mr_ref, mk_ref, mv_ref, mlo_ref, wup_ref, aup_ref,
                 w0_ref, a0_ref, kk_ref, ka_ref, rk_ref, gnw_ref, gnb_ref,
                 seg_ref, tri_ref, strict_ref, incl_ref, eye_ref,
                 y_ref, s_ref, prev_ref):
    c = pl.program_id(1)
    ch = RWKV_CHUNK
    nbatch = r_ref.shape[0]
    npair = nbatch * r_ref.shape[2] // LANES

    def lanes(ref):
        return jnp.concatenate([ref[b] for b in range(nbatch)], axis=1)

    def tiled(ref):
        return jnp.concatenate([ref[...]] * nbatch, axis=1)

    @pl.when(c == 0)
    def _():
        s_ref[...] = jnp.zeros_like(s_ref)
        prev_ref[...] = jnp.zeros_like(prev_ref)

    def token_shift(x, idx, mix):
        row = lax.broadcasted_iota(jnp.int32, x.shape, 0)
        prev = prev_ref[idx:idx + 1, 0:x.shape[1]]
        shifted = jnp.where(row == 0, prev, pltpu.roll(x, 1, 0))
        return x + (shifted - x) * mix

    r_in, k_in, v_in, lo_in = lanes(r_ref), lanes(k_ref), lanes(v_ref), lanes(lo_ref)
    r = token_shift(r_in, 0, tiled(mr_ref))
    k = token_shift(k_in, 1, tiled(mk_ref))
    v = token_shift(v_in, 2, tiled(mv_ref))
    lo = token_shift(lo_in, 3, tiled(mlo_ref))
    prev_ref[0:1, :] = r_in[ch - 1:ch, :]
    prev_ref[1:2, :] = k_in[ch - 1:ch, :]
    prev_ref[2:3, :] = v_in[ch - 1:ch, :]
    prev_ref[3:4, 0:nbatch * LANES] = lo_in[ch - 1:ch, :]

    lo_b = [lo[:, b * LANES:(b + 1) * LANES] for b in range(nbatch)]
    z_w = tiled(w0_ref) + jnp.concatenate([_dot(jnp.tanh(x), wup_ref[...], passes=1) for x in lo_b], axis=1)
    logw = -DECAY_SCALE * jax.nn.sigmoid(z_w)
    a = jax.nn.sigmoid(tiled(a0_ref) + jnp.concatenate([_dot(x, aup_ref[...], passes=1) for x in lo_b], axis=1))

    seg2 = seg_ref[...]

    def head_sum(x):
        tiles = []
        for t in range(npair):
            hi, lo_ = _split(x[:, t * LANES:(t + 1) * LANES])
            tiles.append(_dg(jnp.concatenate([hi, lo_], axis=1), seg2, NN))
        return jnp.concatenate(tiles, axis=1)

    kk = k * tiled(kk_ref)
    kk = kk * lax.rsqrt(jnp.maximum(head_sum(kk * kk), 1e-24))
    k2 = k * (1.0 + (a - 1.0) * tiled(ka_ref))
    ab = kk * a

    l_hi = logw.astype(BF16)
    rem = logw - l_hi.astype(F32)
    l_mid = rem.astype(BF16)
    l_lo = (rem - l_mid.astype(F32)).astype(BF16)
    cum = _dg(tri_ref[...], jnp.concatenate([l_hi, l_mid, l_lo], axis=0), NN)
    cum_last = cum[ch - 1:ch, :]
    e_cum = jnp.exp(cum)
    e_ncum = jnp.exp(-cum)
    e_tail = jnp.exp(cum_last - cum)
    rt = r * e_cum
    kt = k2 * e_ncum
    bt = ab * e_ncum
    at = -kk * jnp.exp(cum - logw)
    khat = k2 * e_tail
    bhat = ab * e_tail
    p_last = e_cum[ch - 1:ch, :]

    lane = lax.broadcasted_iota(jnp.int32, (ch, LANES), 1)
    head0 = lane < HEAD_DIM
    strict = strict_ref[...] > 0.5
    incl = incl_ref[...] > 0.5
    eye = eye_ref[...]

    def stack(x):
        return jnp.concatenate([jnp.where(head0, x, 0.0), jnp.where(head0, 0.0, x)], axis=0)

    def mm(x, w, dn=NN):
        return _dg(x.astype(BF16), w.astype(BF16), dn)

    pairs = range(npair)
    rows = 2 * ch
    sl = [slice(p * LANES, (p + 1) * LANES) for p in pairs]
    at2, rt2, bt2, kt2, v2, khat2, bhat2 = ([stack(t[:, sl[p]]) for p in pairs]
                                            for t in (at, rt, bt, kt, v, khat, bhat))
    gram = []
    for p in pairs:
        lh, ll = _split(jnp.concatenate([at2[p], rt2[p]], axis=0))
        rh, rl = _split(jnp.concatenate([bt2[p], kt2[p]], axis=0))
        gram.append(_dg(jnp.concatenate([lh, ll], axis=1), jnp.concatenate([rh, rh], axis=1), NT))
    a_ab = [jnp.where(strict, g[:rows, :rows], 0.0) for g in gram]
    a_ak = [jnp.where(strict, g[:rows, rows:], 0.0) for g in gram]
    a_rb = [jnp.where(incl, g[rows:, :rows], 0.0) for g in gram]
    a_rk = [jnp.where(incl, g[rows:, rows:], 0.0) for g in gram]
    akv = [mm(a_ak[p], v2[p]) for p in pairs]

    tinv = [eye + a for a in a_ab]
    pw = [mm(a, a) for a in a_ab]
    for _ in range(int(math.log2(ch)) - 2):
        both = [mm(pw[p], jnp.concatenate([pw[p], tinv[p]], axis=1)) for p in pairs]
        pw = [b[:, :rows] for b in both]
        tinv = [tinv[p] + both[p][:, rows:] for p in pairs]
    tinv = [tinv[p] + mm(pw[p], tinv[p]) for p in pairs]

    s = [s_ref[p] for p in pairs]
    tw = [mm(tinv[p], jnp.concatenate([at2[p], akv[p]], axis=1)) for p in pairs]
    ws = [mm(jnp.concatenate([tw[p][:, :LANES], rt2[p]], axis=0), s[p], NT) for p in pairs]
    uv = [jnp.concatenate([ws[p][:rows] + tw[p][:, LANES:], v2[p]], axis=0) for p in pairs]
    y2 = [ws[p][rows:] + mm(jnp.concatenate([a_rb[p], a_rk[p]], axis=1), uv[p]) for p in pairs]
    for p in pairs:
        s_ref[p] = s[p] * p_last[:, sl[p]] + mm(uv[p], jnp.concatenate([bhat2[p], khat2[p]], axis=0), TN)
    y = jnp.concatenate([t[:ch, :] + t[ch:, :] for t in y2], axis=1)

    inv_n = 1.0 / HEAD_DIM
    mu = head_sum(y) * inv_n
    d = y - mu
    var = head_sum(d * d) * inv_n
    yn = d * lax.rsqrt(var + GN_EPS) * tiled(gnw_ref) + tiled(gnb_ref)
    y = (yn + head_sum(r * k2 * tiled(rk_ref)) * v).astype(y_ref.dtype)
    wd = y_ref.shape[2]
    for b in range(nbatch):
        y_ref[b] = y[:, b * wd:(b + 1) * wd]


def rwkv7(p, p_lora, batch, seqlen, shift_mix, w_up, w0, a_up, a0, k_k, k_a, r_k, gn_w, gn_b):
    width = w0.shape[0]
    npair = width // LANES
    ch = RWKV_CHUNK
    nchunk = seqlen // ch
    assert seqlen % ch == 0 and 2 * LORA_RANK == LANES and 2 * HEAD_DIM == LANES
    zeros = jnp.zeros((LORA_RANK, width), F32)
    wup_pad = jnp.concatenate([w_up, zeros], axis=0)
    aup_pad = jnp.concatenate([zeros, a_up], axis=0)
    hb = RWKV_PAIRS
    assert npair % hb == 0
    idx = jnp.arange(LANES)
    seg = (idx[:, None] // HEAD_DIM == idx[None, :] // HEAD_DIM).astype(BF16)
    seg = jnp.concatenate([seg, seg], axis=0)
    t = jnp.arange(ch)
    tri = (t[None, :] <= t[:, None]).astype(BF16)
    tri = jnp.concatenate([tri, tri, tri], axis=1)
    i2 = jnp.arange(2 * ch)
    same = (i2[:, None] // ch) == (i2[None, :] // ch)
    strict = (same & ((i2[None, :] % ch) < (i2[:, None] % ch))).astype(F32)
    incl = (same & ((i2[None, :] % ch) <= (i2[:, None] % ch))).astype(F32)
    eye = jnp.eye(2 * ch, dtype=F32)
    row2 = lambda x: x.reshape(1, -1)

    wd = hb * LANES
    ngrp = npair // hb

    def tok(off):
        return pl.BlockSpec((batch, ch, wd), lambda h, c: (0, c, off + h))

    def par(off):
        return pl.BlockSpec((1, wd), lambda h, c: (0, off + h))

    def const(shape):
        return pl.BlockSpec(shape, lambda h, c: (0, 0))

    up = pl.BlockSpec((LANES, wd), lambda h, c: (0, h))
    lora_tok = pl.BlockSpec((batch, ch, LANES), lambda h, c: (0, c, 0))
    lora_par = pl.BlockSpec((1, LANES), lambda h, c: (0, 3 * npair))
    p3 = p.reshape(batch, seqlen, 3 * width)
    y = pl.pallas_call(
        _rwkv_kernel,
        grid=(ngrp, nchunk),
        in_specs=[tok(0), tok(ngrp), tok(2 * ngrp), lora_tok,
                  par(0), par(ngrp), par(2 * ngrp), lora_par, up, up,
                  par(0), par(0), par(0), par(0), par(0), par(0), par(0),
                  const((2 * LANES, LANES)), const((ch, 3 * ch)), const((2 * ch, 2 * ch)),
                  const((2 * ch, 2 * ch)), const((2 * ch, 2 * ch))],
        out_specs=pl.BlockSpec((batch, ch, wd), lambda h, c: (0, c, h)),
        out_shape=jax.ShapeDtypeStruct((batch, seqlen, width), BF16),
        scratch_shapes=[pltpu.VMEM((batch * hb, LANES, LANES), F32), pltpu.VMEM((SUBLANES, batch * wd), F32)],
        compiler_params=_cparams(("parallel", "arbitrary")),
        name="rwkv7",
    )(p3, p3, p3, p_lora.reshape(batch, seqlen, LANES), row2(shift_mix), row2(shift_mix), row2(shift_mix),
      row2(shift_mix), wup_pad, aup_pad,
      row2(w0), row2(a0), row2(k_k), row2(k_a), row2(r_k), row2(gn_w), row2(gn_b),
      seg, tri, strict, incl, eye)
    return y.reshape(batch * seqlen, width)


def _sb_kernel(q_ref, k_ref, v_ref, uo_ref, o_ref, k2_ref, v2_ref, acc_ref, carry_ref, lb_ref, sums_ref, z_ref):
    qi = pl.program_id(2)
    kb, qb = SB_BLOCK, SB_QBLOCK
    nsub = qb // kb
    npair = q_ref.shape[1] // LANES
    pairs_ = range(npair)

    @pl.when(qi == 0)
    def _():
        lane = lax.broadcasted_iota(jnp.int32, (kb, LANES), 1)
        head0 = lane < HEAD_DIM

        def fill(i, c):
            start = pl.multiple_of(i * kb, kb)
            for p in pairs_:
                for src, dst in ((k_ref, k2_ref), (v_ref, v2_ref)):
                    t = src[pl.ds(start, kb), p * LANES:(p + 1) * LANES].astype(F32)
                    dst[p, i, 0:kb, :] = jnp.where(head0, t, 0.0).astype(BF16)
                    dst[p, i, kb:2 * kb, :] = jnp.where(head0, 0.0, t).astype(BF16)
            return c

        lax.fori_loop(0, k_ref.shape[0] // kb, fill, 0)

    q = [q_ref[:, p * LANES:(p + 1) * LANES] for p in pairs_]
    uo = uo_ref[...]
    acc_ref[...] = jnp.zeros_like(acc_ref)
    carry_ref[...] = jnp.zeros_like(carry_ref)
    tpos = lax.broadcasted_iota(jnp.int32, (qb, kb), 0)
    spos = lax.broadcasted_iota(jnp.int32, (qb, kb), 1)

    def logits(p, sb, zslot):
        base = pl.multiple_of(sb * nsub, nsub)
        keys = k2_ref[p, pl.ds(base, nsub)].reshape(nsub * 2 * kb, LANES)
        z_ref[zslot, p] = _dg(q[p], keys, NT)

    def scores(p, zslot, slot, diagonal):
        for j in range(nsub):
            for h in range(2):
                c = 2 * j + h
                z = z_ref[zslot, p, :, c * kb:(c + 1) * kb]
                nz = -z
                log_keep = jnp.minimum(nz, 0.0) - jnp.log(1.0 + jnp.exp(jnp.minimum(z, nz)))
                if diagonal:
                    causal = (spos + j * kb) < tpos
                    log_keep = jnp.where(causal, log_keep, 0.0)
                    z = jnp.where(causal, z, -jnp.inf)
                hi = log_keep.astype(BF16)
                lo = (log_keep - hi.astype(F32)).astype(BF16)
                lb_ref[slot, p, c] = z
                sums_ref[slot, p, c] = _dg(jnp.concatenate([hi, lo], axis=1), uo, NN)

    def accumulate(p, sb, slot):
        base = pl.multiple_of(sb * nsub, nsub)
        attn = {}
        for h in range(2):
            carry = carry_ref[p, h]
            for j in reversed(range(nsub)):
                s = sums_ref[slot, p, 2 * j + h]
                attn[j, h] = jnp.exp(lb_ref[slot, p, 2 * j + h] + carry + s[:, :kb]).astype(BF16)
                carry = carry + s[:, kb:]
            carry_ref[p, h] = carry
        weights = jnp.concatenate([attn[j, h] for j in range(nsub) for h in range(2)], axis=1)
        values = v2_ref[p, pl.ds(base, nsub)].reshape(nsub * 2 * kb, LANES)
        acc_ref[p] += _dg(weights, values, NN)

    for p in pairs_:
        logits(p, qi, 0)
    for p in pairs_:
        scores(p, 0, 0, True)
    trips = qi // 2

    def body(i, c):
        sb = qi - 1 - 2 * i
        for p in pairs_:
            logits(p, sb, 0)
        for p in pairs_:
            logits(p, sb - 1, 1)
        for p in pairs_:
            accumulate(p, sb + 1, 0)
        for p in pairs_:
            scores(p, 0, 1, False)
        for p in pairs_:
            scores(p, 1, 0, False)
        for p in pairs_:
            accumulate(p, sb, 1)
        return c

    lax.fori_loop(0, trips, body, 0)
    odd = qi - 2 * trips == 1

    @pl.when(odd)
    def _():
        for p in pairs_:
            logits(p, 0, 0)
        for p in pairs_:
            accumulate(p, 1, 0)
        for p in pairs_:
            scores(p, 0, 1, False)
        for p in pairs_:
            accumulate(p, 0, 1)

    @pl.when(jnp.logical_not(odd))
    def _():
        for p in pairs_:
            accumulate(p, 0, 0)

    for p in pairs_:
        o_ref[:, p * LANES:(p + 1) * LANES] = acc_ref[p].astype(o_ref.dtype)


def stick_breaking(p, batch, seqlen, width):
    npair = width // LANES
    kb, qb = SB_BLOCK, SB_QBLOCK
    nq = seqlen // qb
    hp = SB_PAIRS
    ngrp = npair // hp
    wd = hp * LANES
    nsl = 2 * qb // kb
    j = jnp.arange(kb)
    later = (j[:, None] >= j[None, :]).astype(BF16)
    uo = jnp.concatenate([later, jnp.ones((kb, kb), BF16)], axis=1)
    uo = jnp.concatenate([uo, uo], axis=0)
    return pl.pallas_call(
        _sb_kernel,
        grid=(batch, ngrp, nq),
        in_specs=[pl.BlockSpec((qb, wd), lambda b, h, i: (b * nq + i, h)),
                  pl.BlockSpec((seqlen, wd), lambda b, h, i: (b, ngrp + h), pipeline_mode=pl.Buffered(1)),
                  pl.BlockSpec((seqlen, wd), lambda b, h, i: (b, 2 * ngrp + h), pipeline_mode=pl.Buffered(1)),
                  pl.BlockSpec((2 * kb, 2 * kb), lambda b, h, i: (0, 0))],
        out_specs=pl.BlockSpec((qb, wd), lambda b, h, i: (b * nq + i, h)),
        out_shape=jax.ShapeDtypeStruct((batch * seqlen, width), BF16),
        scratch_shapes=[pltpu.VMEM((hp, seqlen // kb, 2 * kb, LANES), BF16),
                        pltpu.VMEM((hp, seqlen // kb, 2 * kb, LANES), BF16),
                        pltpu.VMEM((hp, qb, LANES), F32), pltpu.VMEM((hp, 2, qb, kb), F32),
                        pltpu.VMEM((2, hp, nsl, qb, kb), F32), pltpu.VMEM((2, hp, nsl, qb, 2 * kb), F32),
                        pltpu.VMEM((2, hp, qb, nsl * kb), F32)],
        compiler_params=_cparams(("parallel", "parallel", "arbitrary")),
        name="stick_breaking",
    )(p, p, p, uo)


def _gate_out_kernel(ya_ref, yb_ref, g0_ref, g1_ref, x_ref, w_ref, ng_ref, h_ref, hn_ref, wb_ref):
    @pl.when(pl.program_id(0) == 0)
    def _():
        wb_ref[...] = w_ref[...].astype(BF16)

    gate = jnp.concatenate([g0_ref[...], g1_ref[...]], axis=-1).astype(F32)
    y = jnp.concatenate([ya_ref[...], yb_ref[...]], axis=-1).astype(F32) * (gate * jax.nn.sigmoid(gate))
    h = x_ref[...] + jnp.dot(y.astype(BF16), wb_ref[...], preferred_element_type=F32)
    h_ref[...] = h
    ms = jnp.mean(h * h, axis=-1, keepdims=True)
    hn_ref[...] = ((h * lax.rsqrt(ms + RMS_EPS)) * ng_ref[...]).astype(hn_ref.dtype)


def gate_out(ya, yb, p, gate_col, x, w, next_g, tm=512):
    m, d = x.shape
    half = ya.shape[1]
    tm = min(tm, m)
    assert gate_col % half == 0 and d == 2 * half
    g0 = gate_col // half
    row = lambda n: pl.BlockSpec((tm, n), lambda i: (i, 0))
    return pl.pallas_call(
        _gate_out_kernel,
        grid=(m // tm,),
        in_specs=[row(half), row(half),
                  pl.BlockSpec((tm, half), lambda i: (i, g0)), pl.BlockSpec((tm, half), lambda i: (i, g0 + 1)), row(d),
                  pl.BlockSpec((d, d), lambda i: (0, 0), pipeline_mode=pl.Buffered(1)),
                  pl.BlockSpec((1, d), lambda i: (0, 0))],
        out_specs=[row(d), row(d)],
        out_shape=[jax.ShapeDtypeStruct((m, d), F32), jax.ShapeDtypeStruct((m, d), BF16)],
        scratch_shapes=[pltpu.VMEM((d, d), BF16)],
        compiler_params=_cparams(("arbitrary",)),
        name="gate_out",
    )(ya, yb, p, p, x, w, next_g.reshape(1, d))


def _s5_prep_kernel(lre_ref, lim_ref, ldt_ref, cre_ref, cim_ref, btre_ref, btim_ref,
                    kt_ref, w2_ref, vt2_ref, c2re_ref, c2im_ref, ckre_ref, ckim_ref):
    blk = S5_BLOCK
    h = S5_GROUP
    ns = S5_STATE
    gp = kt_ref.shape[0]
    wl = blk * h
    lre, lim = lre_ref[...], lim_ref[...]
    dt = jnp.exp(ldt_ref[...])
    mag = jnp.exp(lre * dt)
    bre = mag * jnp.cos(lim * dt)
    bim = mag * jnp.sin(lim * dt)
    den = lre * lre + lim * lim
    nre, nim = bre - 1.0, bim
    fre = (nre * lre + nim * lim) / den
    fim = (nim * lre - nre * lim) / den
    cre, cim = cre_ref[...], cim_ref[...]
    ckr, cki = cre * fre - cim * fim, cre * fim + cim * fre
    btre, btim = btre_ref[...], btim_ref[...]
    pr, pi = jnp.ones_like(bre), jnp.zeros_like(bre)
    powers = []
    for tau in range(blk + 1):
        ckre_ref[:, tau * h:(tau + 1) * h, :] = ckr
        ckim_ref[:, tau * h:(tau + 1) * h, :] = cki
        powers.append((pr, pi))
        ckr, cki = ckr * bre - cki * bim, ckr * bim + cki * bre
        pr, pi = pr * bre - pi * bim, pr * bim + pi * bre

    def put(ref, rows, lane0, x):
        r0, r1 = rows
        x = x.reshape(gp // 2, 2, r1 - r0, ns).astype(ref.dtype)
        ref[:, r0:r1, lane0:lane0 + ns] = x[:, 0]
        ref[:, ref.shape[1] // 2 + r0:ref.shape[1] // 2 + r1, lane0 + ns:lane0 + 2 * ns] = x[:, 1]

    w2_ref[...] = jnp.zeros_like(w2_ref)
    vt2_ref[...] = jnp.zeros_like(vt2_ref)
    for i in range(blk):
        pr, pi = powers[blk - 1 - i]
        put(w2_ref, (i * h, (i + 1) * h), 0, pr * btre - pi * btim)
        put(w2_ref, (i * h, (i + 1) * h), 2 * ns, pr * btim + pi * btre)
    put(vt2_ref, (0, wl), 0, ckre_ref[:, h:(blk + 1) * h, :])
    put(vt2_ref, (0, wl), 2 * ns, -ckim_ref[:, h:(blk + 1) * h, :])

    def put_lanes(ref, r0, x):
        x = x.reshape(gp // 2, 2, x.shape[1], ns)
        ref[:, r0:r0 + x.shape[2], 0:ns] = x[:, 0]
        ref[:, r0:r0 + x.shape[2], ns:2 * ns] = x[:, 1]

    row = lax.broadcasted_iota(jnp.int32, (gp, SUBLANES, ns), 1)
    qr, qi = powers[blk]
    pr, pi = qr, qi
    for r in range(SUBLANES):
        if r + 1 in (1, 2, 4):
            lvl = (1, 2, 4).index(r + 1)
            put_lanes(c2re_ref, lvl * SUBLANES, jnp.where(row >= r + 1, pr, 0.0))
            put_lanes(c2im_ref, lvl * SUBLANES, jnp.where(row >= r + 1, pi, 0.0))
        put_lanes(c2re_ref, 3 * SUBLANES + r, pr)
        put_lanes(c2im_ref, 3 * SUBLANES + r, pi)
        pr, pi = pr * qr - pi * qi, pr * qi + pi * qr
    for g in range(gp):
        kt_ref[g] = (_dot(btre[g], ckre_ref[g, 0:blk * h, :], NT) - _dot(btim[g], ckim_ref[g, 0:blk * h, :], NT))


def _s5_kernel(nbatch, p_ref, kt_ref, w_ref, vt_ref, cre_ref, cim_ref, d_ref, o_ref,
               u_ref, gre_ref, gim_ref, xre_ref, xim_ref, yp_ref, stage_ref):
    npair = u_ref.shape[0] // 2
    rows = u_ref.shape[1]
    wlane = u_ref.shape[2]
    blk = S5_BLOCK
    ngrp = 2 * npair
    rc = S5_RELAYOUT_ROWS
    lane_chunk = lax.broadcasted_iota(jnp.int32, (rc, LANES), 1) // S5_GROUP

    def chunk_transpose(arrs):
        arrs = list(arrs)
        s = ngrp // 2
        while s:
            upper = (lane_chunk & s) != 0
            for x in range(ngrp):
                if x & s == 0:
                    ax, ay = arrs[x], arrs[x + s]
                    arrs[x] = jnp.where(upper, pltpu.roll(ay, s * S5_GROUP, 1), ax)
                    arrs[x + s] = jnp.where(upper, ay, pltpu.roll(ax, LANES - s * S5_GROUP, 1))
            s //= 2
        return arrs

    def relayout_in(c, carry):
        t0 = pl.multiple_of(c * (rc * blk), rc * blk)
        r0 = pl.multiple_of(c * rc, rc)
        stage_ref[...] = p_ref[pl.ds(t0, rc * blk), :].astype(F32)
        for half in range(blk // ngrp):
            z = chunk_transpose(stage_ref[pl.ds(ngrp * half + i, rc, stride=blk), :] for i in range(ngrp))
            for g in range(ngrp):
                u_ref[g, pl.ds(r0, rc), half * LANES:(half + 1) * LANES] = z[g].astype(BF16)
        return carry

    lax.fori_loop(0, rows // rc, relayout_in, 0)
    lane = lax.broadcasted_iota(jnp.int32, (S5_GROUP, wlane), 1)

    for g in range(2 * npair):
        kt = kt_ref[g]
        blocks = [kt] + [jnp.where(lane >= i * S5_GROUP, pltpu.roll(kt, i * S5_GROUP, 1), 0.0)
                         for i in range(1, S5_BLOCK)]
        kmat = jnp.concatenate(blocks, axis=0).astype(BF16)
        yp_ref[g] = _dg(u_ref[g], kmat, NN)
    for p in range(npair):
        ucat = jnp.concatenate([u_ref[2 * p], u_ref[2 * p + 1]], axis=1)
        g = _dg(ucat, w_ref[p], NN)
        gre_ref[p] = g[:, :LANES]
        gim_ref[p] = g[:, LANES:]

    def cma(xr, xi, ar, ai, sr, si):
        return xr + (ar * sr - ai * si), xi + (ar * si + ai * sr)

    row = lax.broadcasted_iota(jnp.int32, (SUBLANES, LANES), 0)
    per_batch = rows // nbatch

    def tile(t, carry, base):
        r0 = pl.multiple_of(base + t * SUBLANES, SUBLANES)
        new_carry = []
        for p in range(npair):
            xr = gre_ref[p, pl.ds(r0, SUBLANES), :]
            xi = gim_ref[p, pl.ds(r0, SUBLANES), :]
            for lvl, sh in enumerate((1, 2, 4)):
                ar = cre_ref[p, lvl * SUBLANES:(lvl + 1) * SUBLANES, :]
                ai = cim_ref[p, lvl * SUBLANES:(lvl + 1) * SUBLANES, :]
                xr, xi = cma(xr, xi, ar, ai, pltpu.roll(xr, sh, 0), pltpu.roll(xi, sh, 0))
            ar = cre_ref[p, 3 * SUBLANES:4 * SUBLANES, :]
            ai = cim_ref[p, 3 * SUBLANES:4 * SUBLANES, :]
            cr, ci = carry[2 * p], carry[2 * p + 1]
            xr, xi = cma(xr, xi, ar, ai, cr, ci)
            xre_ref[p, pl.ds(r0, SUBLANES), :] = jnp.where(row == 0, cr, pltpu.roll(xr, 1, 0))
            xim_ref[p, pl.ds(r0, SUBLANES), :] = jnp.where(row == 0, ci, pltpu.roll(xi, 1, 0))
            new_carry.append(jnp.broadcast_to(xr[SUBLANES - 1:SUBLANES, :], (SUBLANES, LANES)))
            new_carry.append(jnp.broadcast_to(xi[SUBLANES - 1:SUBLANES, :], (SUBLANES, LANES)))
        return tuple(new_carry)

    zero = jnp.zeros((SUBLANES, LANES), F32)
    for b in range(nbatch):
        lax.fori_loop(0, per_batch // SUBLANES, functools.partial(tile, base=b * per_batch), (zero,) * (2 * npair))

    for p in range(npair):
        xs = jnp.concatenate([xre_ref[p], xim_ref[p]], axis=1).astype(BF16)
        corr = _dg(xs, vt_ref[p], NT)
        for k in range(2):
            g = 2 * p + k
            y = yp_ref[g] + corr[:, k * wlane:(k + 1) * wlane] + d_ref[g] * u_ref[g].astype(F32)
            yp_ref[g] = jax.nn.gelu(y)

    def relayout_out(c, carry):
        t0 = pl.multiple_of(c * (rc * blk), rc * blk)
        r0 = pl.multiple_of(c * rc, rc)
        for half in range(blk // ngrp):
            yt = chunk_transpose(yp_ref[g, pl.ds(r0, rc), half * LANES:(half + 1) * LANES] for g in range(ngrp))
            for j in range(ngrp):
                stage_ref[pl.ds(ngrp * half + j, rc, stride=blk), :] = yt[j]
        o_ref[pl.ds(t0, rc * blk), :] = stage_ref[...].astype(o_ref.dtype)
        return carry

    lax.fori_loop(0, rows // rc, relayout_out, 0)


def s5_ssm(p, batch, seqlen, lam_re, lam_im, log_dt, b_re, b_im, c_re, c_im, d_skip):
    ngroup, nstate = lam_re.shape
    h, blk = S5_GROUP, S5_BLOCK
    width = ngroup * h
    wl = blk * h
    nb = batch * seqlen // blk
    gp = S5_PREP_GROUPS
    assert nstate == S5_STATE and 2 * nstate == LANES
    assert seqlen % (blk * SUBLANES) == 0 and ngroup % gp == 0 and ngroup % (2 * S5_PAIRS) == 0

    g3 = lambda x: x.reshape(ngroup, 1, nstate)
    bt = lambda x: jnp.swapaxes(x, 1, 2)
    spec1 = pl.BlockSpec((gp, 1, nstate), lambda i: (i, 0, 0))
    spec_c = pl.BlockSpec((gp, h, nstate), lambda i: (i, 0, 0))
    spec_k = pl.BlockSpec((gp, h, wl), lambda i: (i, 0, 0))
    pair = lambda r, c: pl.BlockSpec((gp // 2, r, c), lambda i: (i, 0, 0))
    npairs = ngroup // 2
    kt, w2, vt2, c_re2, c_im2 = pl.pallas_call(
        _s5_prep_kernel,
        grid=(ngroup // gp,),
        in_specs=[spec1, spec1, spec1, spec_c, spec_c, spec_c, spec_c],
        out_specs=[spec_k, pair(2 * wl, 2 * LANES), pair(2 * wl, 2 * LANES),
                   pair(4 * SUBLANES, LANES), pair(4 * SUBLANES, LANES)],
        out_shape=[jax.ShapeDtypeStruct((ngroup, h, wl), F32),
                   jax.ShapeDtypeStruct((npairs, 2 * wl, 2 * LANES), BF16),
                   jax.ShapeDtypeStruct((npairs, 2 * wl, 2 * LANES), BF16),
                   jax.ShapeDtypeStruct((npairs, 4 * SUBLANES, LANES), F32),
                   jax.ShapeDtypeStruct((npairs, 4 * SUBLANES, LANES), F32)],
        scratch_shapes=[pltpu.VMEM((gp, (blk + 1) * h, nstate), F32), pltpu.VMEM((gp, (blk + 1) * h, nstate), F32)],
        compiler_params=_cparams(("parallel",)),
        name="s5_prep",
    )(g3(lam_re), g3(lam_im), jnp.broadcast_to(log_dt[:, None, None], (ngroup, 1, nstate)), c_re, c_im,
      bt(b_re), bt(b_im))

    d_row = jnp.tile(d_skip.reshape(ngroup, 1, h), (1, 1, blk))

    np_ = S5_PAIRS
    gs = 2 * np_
    assert gs * h == LANES and blk % gs == 0 and nb % S5_RELAYOUT_ROWS == 0
    tok = pl.BlockSpec((nb * blk, LANES), lambda i: (0, i))
    grp = lambda r, c: pl.BlockSpec((gs, r, c), lambda i: (i, 0, 0))
    par = lambda r, c: pl.BlockSpec((np_, r, c), lambda i: (i, 0, 0))
    return pl.pallas_call(
        functools.partial(_s5_kernel, batch),
        grid=(ngroup // gs,),
        in_specs=[tok, grp(h, wl), par(2 * wl, 2 * LANES), par(2 * wl, 2 * LANES),
                  par(4 * SUBLANES, LANES), par(4 * SUBLANES, LANES), grp(1, wl)],
        out_specs=tok,
        out_shape=jax.ShapeDtypeStruct((batch * seqlen, width), BF16),
        scratch_shapes=[pltpu.VMEM((gs, nb, wl), BF16)] + [pltpu.VMEM((np_, nb, LANES), F32)] * 4
        + [pltpu.VMEM((gs, nb, wl), F32), pltpu.VMEM((S5_RELAYOUT_ROWS * blk, LANES), F32)],
        compiler_params=_cparams(("parallel",)),
        name="s5_ssm",
    )(p, kt, w2, vt2, c_re2, c_im2, d_row)


def _glu_kernel(y_ref, yj_ref, gj_ref, w_ref, b_ref, o_ref):
    w = w_ref[...].astype(BF16)
    b = b_ref[...]
    for r in range(0, y_ref.shape[0], GLU_SUBTILE):
        rows = slice(r, r + GLU_SUBTILE)
        z = jnp.dot(y_ref[rows, :], w, preferred_element_type=F32) + b
        gate = gj_ref[rows, :].astype(F32)
        yj = yj_ref[rows, :].astype(F32)
        o_ref[rows, :] = (yj * jax.nn.sigmoid(z) * (gate * jax.nn.sigmoid(gate))).astype(o_ref.dtype)


def glu_gate(y, p, w, b, tm=1024, tn=1024):
    m, d = y.shape
    tm = min(tm, m)
    goff = d // tn
    return pl.pallas_call(
        _glu_kernel,
        grid=(m // tm, d // tn),
        in_specs=[pl.BlockSpec((tm, d), lambda i, j: (i, 0)),
                  pl.BlockSpec((tm, tn), lambda i, j: (i, j)),
                  pl.BlockSpec((tm, tn), lambda i, j: (i, goff + j)),
                  pl.BlockSpec((d, tn), lambda i, j: (0, j)),
                  pl.BlockSpec((1, tn), lambda i, j: (0, j))],
        out_specs=pl.BlockSpec((tm, tn), lambda i, j: (i, j)),
        out_shape=jax.ShapeDtypeStruct((m, d), BF16),
        compiler_params=_cparams(("parallel", "arbitrary")),
        name="glu_gate",
    )(y, y, p, w, b.reshape(1, d))


def _final_kernel(a_ref, h_ref, w_ref, g_ref, o_ref, wb_ref):
    @pl.when(pl.program_id(0) == 0)
    def _():
        wb_ref[...] = w_ref[...].astype(BF16)

    h = h_ref[...] + jnp.dot(a_ref[...], wb_ref[...], preferred_element_type=F32)
    ms = jnp.mean(h * h, axis=-1, keepdims=True)
    o_ref[...] = (h * lax.rsqrt(ms + RMS_EPS)) * g_ref[...]


def final_out(a, h, w, g, tm=512):
    m, d = h.shape
    tm = min(tm, m)
    row = pl.BlockSpec((tm, d), lambda i: (i, 0))
    return pl.pallas_call(
        _final_kernel,
        grid=(m // tm,),
        in_specs=[row, row, pl.BlockSpec((d, d), lambda i: (0, 0), pipeline_mode=pl.Buffered(1)),
                  pl.BlockSpec((1, d), lambda i: (0, 0))],
        out_specs=row,
        out_shape=jax.ShapeDtypeStruct((m, d), F32),
        scratch_shapes=[pltpu.VMEM((d, d), BF16)],
        compiler_params=_cparams(("arbitrary",)),
        name="final_out",
    )(a, h, w, g.reshape(1, d))


def kernel(x, norm_g, final_g, ab_w_in, rwkv_shift_mix, rwkv_w_up, rwkv_w0, rwkv_a_up, rwkv_a0, rwkv_k_k, rwkv_k_a, rwkv_r_k, rwkv_gn_w, rwkv_gn_b, ab_w_out, s5_w_in, s5_lam_re, s5_lam_im, s5_log_dt, s5_b_re, s5_b_im, s5_c_re, s5_c_im, s5_d, s5_w_glu, s5_b_glu, s5_w_out):
    batch, seqlen, d = x.shape
    m = batch * seqlen
    rwkv_w = rwkv_w0.shape[1]
    rwkv_proj = 3 * rwkv_w + 2 * LORA_RANK
    sb_w = (ab_w_in.shape[2] - rwkv_proj - d) // 3
    x2 = x.reshape(m, d)

    w_in = ab_w_in[0]
    scale = HEAD_DIM ** -0.5
    col_scale = jnp.concatenate([jnp.full((sb_w,), scale, F32), jnp.ones((2 * sb_w + d,), F32)])
    xn, p_lora = rmsnorm_proj(x2, norm_g[0], w_in, 3 * rwkv_w, 2 * LORA_RANK)
    p_rkv = matmul(xn, w_in, F32, 0, 3 * rwkv_w, name="proj_rkv")
    p_sb = matmul(xn, w_in, BF16, rwkv_proj, 3 * sb_w + d, col_scale, tm=2048, name="proj_sb_gate")
    y_a = rwkv7(p_rkv, p_lora, batch, seqlen, rwkv_shift_mix[0], rwkv_w_up[0], rwkv_w0[0], rwkv_a_up[0], rwkv_a0[0],
                rwkv_k_k[0], rwkv_k_a[0], rwkv_r_k[0], rwkv_gn_w[0], rwkv_gn_b[0])
    y_b = stick_breaking(p_sb, batch, seqlen, sb_w)
    h1, hn1 = gate_out(y_a, y_b, p_sb, 3 * sb_w, x2, ab_w_out[0], norm_g[1])

    p1 = matmul(hn1, s5_w_in[0], BF16, 0, 2 * d, tm=2048, name="proj_s5")
    y_s5 = s5_ssm(p1, batch, seqlen, s5_lam_re[0], s5_lam_im[0], s5_log_dt[0], s5_b_re[0], s5_b_im[0],
                  s5_c_re[0], s5_c_im[0], s5_d[0])
    act = glu_gate(y_s5, p1, s5_w_glu[0], s5_b_glu[0])
    out = final_out(act, h1, s5_w_out[0], final_g)
    return out.reshape(batch, seqlen, d)
```
